```python
import jax, jax.numpy as jnp
from jax import lax
import numpy as np

D_MODEL = 1024
BATCH = 16
SEQ = 256
DEPTH = 1
DEC_BATCH = 4
DEC_SEQ = 4096
PAST_LEN = 256

GRID_W = 64
HEAD_DIM = 64
ATTN_WIDTH = D_MODEL // 2
N_HEADS = ATTN_WIDTH // HEAD_DIM
N_KV_HEADS = max(N_HEADS // 4, 1)
Q_PER_KV = N_HEADS // N_KV_HEADS
KV_WIDTH = N_KV_HEADS * HEAD_DIM
CONV_WIDTH = D_MODEL - ATTN_WIDTH
CONV_K = 3
WINDOW = 128
BLOCK = 128
ROPE_BASE = 10000.0
AXIS_DIM = HEAD_DIM // 2
N_FREQ = AXIS_DIM // 2
N_GROUPS = 4
EXPERTS_PER_GROUP = 8
N_EXPERTS = N_GROUPS * EXPERTS_PER_GROUP
TOP_K = 2
D_EXPERT = D_MODEL // 4
IN_WIDTH = ATTN_WIDTH + 2 * KV_WIDTH + 3 * CONV_WIDTH
EPS = 1e-6
NEG = -1e30
SCALE = HEAD_DIM ** -0.5

kernel_name = "hybrid_swa_shortconv_hmoe_diffusion_step"


def rmsnorm(x, g):
    xf = x.astype(jnp.float32)
    y = xf * lax.rsqrt(jnp.mean(xf * xf, axis=-1, keepdims=True) + EPS)
    return (y * g.astype(jnp.float32)).astype(x.dtype)


def adaln(cvec, w_mod, b_mod):
    m = jax.nn.silu(cvec) @ w_mod + b_mod
    return jnp.split(m[:, None, :], 6, axis=-1)


def project(h, w_in):
    b, n, _ = h.shape
    cuts = np.cumsum([ATTN_WIDTH, KV_WIDTH, KV_WIDTH, CONV_WIDTH, CONV_WIDTH]).tolist()
    q, k, v, bg, cg, xin = jnp.split(h @ w_in, cuts, axis=-1)
    q = q.reshape(b, n, N_HEADS, HEAD_DIM)
    k = k.reshape(b, n, N_KV_HEADS, HEAD_DIM)
    v = v.reshape(b, n, N_KV_HEADS, HEAD_DIM)
    return q, k, v, bg, cg, xin


def short_conv_mixer(bg, cg, xin, conv_w, conv_b):
    u = cg * xin
    n = u.shape[1]
    pad = CONV_K // 2
    up = jnp.pad(u, ((0, 0), (pad, pad), (0, 0)))
    y = sum(up[:, j:j + n] * conv_w[j] for j in range(CONV_K)) + conv_b
    return bg * y


def _rot(x, ang):
    cos = jnp.cos(ang)[None, :, None, :].astype(x.dtype)
    sin = jnp.sin(ang)[None, :, None, :].astype(x.dtype)
    x1, x2 = x[..., :N_FREQ], x[..., N_FREQ:]
    return jnp.concatenate([x1 * cos - x2 * sin, x1 * sin + x2 * cos], axis=-1)


def axial_rope(x):
    n = x.shape[1]
    t = jnp.arange(n)
    inv = ROPE_BASE ** (-jnp.arange(N_FREQ, dtype=jnp.float32) / N_FREQ)
    rows = (t // GRID_W).astype(jnp.float32)
    cols = (t % GRID_W).astype(jnp.float32)
    return jnp.concatenate([_rot(x[..., :AXIS_DIM], rows[:, None] * inv),
                            _rot(x[..., AXIS_DIM:], cols[:, None] * inv)], axis=-1)


def _sink_col(sink, shape):
    sk = sink.astype(jnp.float32).reshape(N_KV_HEADS, Q_PER_KV, 1, 1)
    return jnp.broadcast_to(sk, shape[:-1] + (1,))


def context_attention(q, k, v, sink):
    b, s = q.shape[:2]
    nb = s // BLOCK
    qb = q.reshape(b, nb, BLOCK, N_KV_HEADS, Q_PER_KV, HEAD_DIM).transpose(1, 0, 2, 3, 4, 5)

    def one_block(qblk):
        sc = jnp.einsum('bqkgd,bskd->bkgqs', qblk, k, preferred_element_type=jnp.float32) * SCALE
        sc = jnp.concatenate([sc, _sink_col(sink, sc.shape)], axis=-1)
        p = jax.nn.softmax(sc, axis=-1)[..., :-1].astype(v.dtype)
        return jnp.einsum('bkgqs,bskd->bqkgd', p, v)

    o = lax.map(one_block, qb)
    return o.transpose(1, 0, 2, 3, 4, 5).reshape(b, s, ATTN_WIDTH)


def latent_attention(q, k, v, ck, cv, sink):
    b, n = q.shape[:2]
    nb = n // BLOCK
    qb = q.reshape(b, nb, BLOCK, N_KV_HEADS, Q_PER_KV, HEAD_DIM)

    def band(a):
        ap = jnp.pad(a, ((0, 0), (BLOCK, BLOCK), (0, 0), (0, 0)))
        ap = ap.reshape(b, nb + 2, BLOCK, N_KV_HEADS, HEAD_DIM)
        return jnp.concatenate([ap[:, :-2], ap[:, 1:-1], ap[:, 2:]], axis=2)

    kb, vb = band(k), band(v)
    blk = jnp.arange(nb)[:, None]
    qpos = blk * BLOCK + jnp.arange(BLOCK)[None, :]
    kpos = (blk - 1) * BLOCK + jnp.arange(3 * BLOCK)[None, :]
    mask = ((jnp.abs(qpos[:, :, None] - kpos[:, None, :]) <= WINDOW)
            & (kpos[:, None, :] >= 0) & (kpos[:, None, :] < n))
    s_loc = jnp.einsum('bnqkgd,bnskd->bnkgqs', qb, kb, preferred_element_type=jnp.float32) * SCALE
    s_loc = jnp.where(mask[None, :, None, None], s_loc, NEG)
    s_ctx = jnp.einsum('bnqkgd,bpkd->bnkgqp', qb, ck, preferred_element_type=jnp.float32) * SCALE
    sc = jnp.concatenate([s_loc, s_ctx, _sink_col(sink, s_loc.shape)], axis=-1)
    p = jax.nn.softmax(sc, axis=-1)
    n_loc = 3 * BLOCK
    n_ctx = ck.shape[1]
    p_loc = p[..., :n_loc].astype(v.dtype)
    p_ctx = p[..., n_loc:n_loc + n_ctx].astype(v.dtype)
    o = (jnp.einsum('bnkgqs,bnskd->bnqkgd', p_loc, vb)
         + jnp.einsum('bnkgqp,bpkd->bnqkgd', p_ctx, cv))
    return o.reshape(b, n, ATTN_WIDTH)


def hier_moe(h, w_coarse, b_coarse, w_fine, b_fine, w_gate, w_up, w_down):
    b, n, d = h.shape
    t = h.reshape(b * n, d)
    pg_all = jax.nn.softmax((t @ w_coarse + b_coarse).astype(jnp.float32), axis=-1)
    grp = jnp.argmax(pg_all, axis=-1)
    pg = jnp.max(pg_all, axis=-1)
    fine = (t @ w_fine + b_fine).astype(jnp.float32).reshape(-1, N_GROUPS, EXPERTS_PER_GROUP)
    fine = jnp.take_along_axis(fine, grp[:, None, None], axis=1)[:, 0]
    top_v, top_i = lax.top_k(fine, TOP_K)
    pe = jax.nn.softmax(top_v, axis=-1) * pg[:, None]
    eidx = grp[:, None] * EXPERTS_PER_GROUP + top_i
    gates = jnp.sum(jax.nn.one_hot(eidx, N_EXPERTS, dtype=jnp.float32) * pe[..., None], axis=1).astype(t.dtype)
    a = (jax.nn.silu(jnp.einsum('td,edf->tef', t, w_gate))
         * jnp.einsum('td,edf->tef', t, w_up) * gates[:, :, None])
    y = jnp.einsum('tef,efd->td', a, w_down)
    return y.reshape(b, n, d)


def trunk_layer(x, cvec, ctx_kv, w_mod, b_mod, norm1_g, w_in, conv_w, conv_b, sink,
                attn_out_g, conv_out_g, w_o, norm2_g, w_coarse, b_coarse, w_fine, b_fine,
                w_gate, w_up, w_down):
    sh1, sc1, g1, sh2, sc2, g2 = adaln(cvec, w_mod, b_mod)
    h = rmsnorm(x, norm1_g) * (1 + sc1) + sh1
    q, k, v, bg, cg, xin = project(h, w_in)
    if ctx_kv is None:
        attn = context_attention(q, k, v, sink)
    else:
        attn = latent_attention(axial_rope(q), axial_rope(k), v, ctx_kv[0], ctx_kv[1], sink)
    conv = short_conv_mixer(bg, cg, xin, conv_w, conv_b)
    mixed = jnp.concatenate([rmsnorm(attn, attn_out_g), rmsnorm(conv, conv_out_g)], axis=-1) @ w_o
    x = x + g1 * mixed
    h2 = rmsnorm(x, norm2_g) * (1 + sc2) + sh2
    x = x + g2 * hier_moe(h2, w_coarse, b_coarse, w_fine, b_fine, w_gate, w_up, w_down)
    return x, k, v


def setup_inputs(seed: int = 0) -> dict:
    key = jax.random.key(seed)
    ks = jax.random.split(key, 32)
    f32 = jnp.float32

    def nrm(k, shape, scale):
        return jax.random.normal(k, shape, f32) * scale

    L, D = DEPTH, D_MODEL
    return {
        "x_prompt": nrm(ks[0], (BATCH, SEQ, D), 1.0),
        "x_sample": nrm(ks[1], (DEC_BATCH, DEC_SEQ, D), 1.0),
        "cache_k": nrm(ks[2], (DEC_BATCH, DEPTH, PAST_LEN, N_KV_HEADS, HEAD_DIM), 1.0),
        "cache_v": nrm(ks[3], (DEC_BATCH, DEPTH, PAST_LEN, N_KV_HEADS, HEAD_DIM), 1.0),
        "c": nrm(ks[4], (DEC_BATCH, D), 1.0),
        "c_ctx": nrm(ks[5], (D,), 1.0),
        "w_mod": nrm(ks[6], (L, D, 6 * D), 0.5 * D ** -0.5),
        "b_mod": nrm(ks[7], (L, 6 * D), 0.02),
        "norm1_g": 1.0 + nrm(ks[8], (L, D), 0.02),
        "w_in": nrm(ks[9], (L, D, IN_WIDTH), D ** -0.5),
        "conv_w": nrm(ks[10], (L, CONV_K, CONV_WIDTH), CONV_K ** -0.5),
        "conv_b": nrm(ks[11], (L, CONV_WIDTH), 0.02),
        "sink": nrm(ks[12], (L, N_HEADS), 0.5),
        "attn_out_g": 1.0 + nrm(ks[13], (L, ATTN_WIDTH), 0.02),
        "conv_out_g": 1.0 + nrm(ks[14], (L, CONV_WIDTH), 0.02),
        "w_o": nrm(ks[15], (L, ATTN_WIDTH + CONV_WIDTH, D), (ATTN_WIDTH + CONV_WIDTH) ** -0.5),
        "norm2_g": 1.0 + nrm(ks[16], (L, D), 0.02),
        "w_coarse": nrm(ks[17], (L, D, N_GROUPS), D ** -0.5),
        "b_coarse": nrm(ks[18], (L, N_GROUPS), 0.01),
        "w_fine": nrm(ks[19], (L, D, N_EXPERTS), D ** -0.5),
        "b_fine": nrm(ks[20], (L, N_EXPERTS), 0.01),
        "w_gate": nrm(ks[21], (L, N_EXPERTS, D, D_EXPERT), D ** -0.5),
        "w_up": nrm(ks[22], (L, N_EXPERTS, D, D_EXPERT), D ** -0.5),
        "w_down": nrm(ks[23], (L, N_EXPERTS, D_EXPERT, D), D_EXPERT ** -0.5),
        "final_g": 1.0 + nrm(ks[24], (D,), 0.02),
    }


def reference(x_prompt, x_sample, cache_k, cache_v, c, c_ctx, w_mod, b_mod, norm1_g, w_in,
              conv_w, conv_b, sink, attn_out_g, conv_out_g, w_o, norm2_g, w_coarse, b_coarse,
              w_fine, b_fine, w_gate, w_up, w_down, final_g):
    yp = x_prompt
    new_ks, new_vs = [], []
    for l in range(DEPTH):
        yp, k_l, v_l = trunk_layer(
            yp, c_ctx[None], None, w_mod[l], b_mod[l], norm1_g[l], w_in[l], conv_w[l], conv_b[l],
            sink[l], attn_out_g[l], conv_out_g[l], w_o[l], norm2_g[l], w_coarse[l], b_coarse[l],
            w_fine[l], b_fine[l], w_gate[l], w_up[l], w_down[l])
        new_ks.append(k_l)
        new_vs.append(v_l)
    y_prompt = rmsnorm(yp, final_g)
    new_k = jnp.stack(new_ks, axis=1)
    new_v = jnp.stack(new_vs, axis=1)

    ys = x_sample
    for l in range(DEPTH):
        ys, _, _ = trunk_layer(
            ys, c, (cache_k[:, l], cache_v[:, l]), w_mod[l], b_mod[l], norm1_g[l], w_in[l],
            conv_w[l], conv_b[l], sink[l], attn_out_g[l], conv_out_g[l], w_o[l], norm2_g[l],
            w_coarse[l], b_coarse[l], w_fine[l], b_fine[l], w_gate[l], w_up[l], w_down[l])
    y_sample = rmsnorm(ys, final_g)
    return (y_prompt, y_sample, new_k, new_v)
```

```python
import functools

import jax
import jax.numpy as jnp
import numpy as np
from jax import lax
from jax.experimental import pallas as pl
from jax.experimental.pallas import tpu as pltpu

D_MODEL = 1024
HEAD_DIM = 64
ATTN_WIDTH = 512
N_HEADS = 8
N_KV_HEADS = 2
Q_PER_KV = 4
KV_WIDTH = 128
CONV_WIDTH = 512
CONV_K = 3
WINDOW = 128
GRID_W = 64
ROPE_BASE = 10000.0
N_FREQ = 16
N_GROUPS = 4
EXPERTS_PER_GROUP = 8
N_EXPERTS = 32
D_EXPERT = 256
IN_WIDTH = ATTN_WIDTH + 2 * KV_WIDTH + 3 * CONV_WIDTH
EPS = 1e-6
NEG = -1e30
SCALE = HEAD_DIM ** -0.5

LANES = 128
BF16_SUBLANES = 16
VMEM_LIMIT_BYTES = 48 * 1024 * 1024

SH1, SC1, G1, SH2, SC2, G2 = (i * D_MODEL for i in range(6))
MOD_ROWS = 8

COARSE_LANE0 = N_EXPERTS

f32 = jnp.float32
bf16 = jnp.bfloat16


def _params(*semantics):
    return pltpu.CompilerParams(dimension_semantics=semantics, vmem_limit_bytes=VMEM_LIMIT_BYTES)


def _rms(x):
    return x * lax.rsqrt(jnp.mean(x * x, axis=-1, keepdims=True) + EPS)


def _dot(a, b):
    return jnp.dot(a, b, preferred_element_type=f32)


def _dot_nt(a, b):
    return lax.dot_general(a, b, (((1,), (1,)), ((), ())), preferred_element_type=f32)


def _mod_kernel(cv_ref, w_ref, b_ref, o_ref):
    a = cv_ref[...]
    a = a * jax.nn.sigmoid(a)
    o_ref[...] = jnp.dot(a, w_ref[...], precision=lax.Precision.HIGHEST,
                         preferred_element_type=f32) + b_ref[...]


def _modulation(cvecs, w_mod, b_mod):
    tn = D_MODEL
    return pl.pallas_call(
        _mod_kernel,
        grid=(6 * D_MODEL // tn,),
        in_specs=[pl.BlockSpec((MOD_ROWS, D_MODEL), lambda j: (0, 0)),
                  pl.BlockSpec((D_MODEL, tn), lambda j: (0, j)),
                  pl.BlockSpec((1, tn), lambda j: (0, j))],
        out_specs=pl.BlockSpec((MOD_ROWS, tn), lambda j: (0, j)),
        out_shape=jax.ShapeDtypeStruct((MOD_ROWS, 6 * D_MODEL), f32),
        compiler_params=_params("arbitrary"),
        name="mod",
    )(cvecs, w_mod, b_mod.reshape(1, -1))


def _swap_halves(x):
    lane = lax.broadcasted_iota(jnp.int32, x.shape, 1)
    up = pltpu.roll(x, LANES - N_FREQ, axis=1)
    dn = pltpu.roll(x, N_FREQ, axis=1)
    return jnp.where((lane % (2 * N_FREQ)) < N_FREQ, up, dn)


def _rope(x, cos, sin):
    parts = []
    for c in range(x.shape[1] // LANES):
        xc = x[:, c * LANES:(c + 1) * LANES]
        parts.append(xc * cos + _swap_halves(xc) * sin)
    return parts[0] if len(parts) == 1 else jnp.concatenate(parts, axis=1)


def _inproj_kernel(*refs, rope):
    if rope:
        x_ref, mod_ref, g_ref, w_ref, cos_ref, sin_ref, q_ref, k_ref, v_ref, u_ref, bg_ref = refs
    else:
        x_ref, mod_ref, g_ref, w_ref, q_ref, k_ref, v_ref, u_ref, bg_ref = refs
    h = _rms(x_ref[...]) * g_ref[...]
    h = h * (1.0 + mod_ref[:, SC1:SC1 + D_MODEL]) + mod_ref[:, SH1:SH1 + D_MODEL]
    z = _dot(h.astype(bf16), w_ref[...])
    o = 0
    q = z[:, o:o + ATTN_WIDTH]; o += ATTN_WIDTH
    k = z[:, o:o + KV_WIDTH]; o += KV_WIDTH
    v = z[:, o:o + KV_WIDTH]; o += KV_WIDTH
    bg = z[:, o:o + CONV_WIDTH]; o += CONV_WIDTH
    cg = z[:, o:o + CONV_WIDTH]; o += CONV_WIDTH
    xin = z[:, o:o + CONV_WIDTH]
    if rope:
        q = _rope(q, cos_ref[...], sin_ref[...])
        k = _rope(k, cos_ref[...], sin_ref[...])
    q_ref[...] = q.astype(q_ref.dtype)
    k_ref[...] = k.astype(k_ref.dtype)
    v_ref[...] = v.astype(v_ref.dtype)
    u_ref[...] = (cg * xin).astype(u_ref.dtype)
    bg_ref[...] = bg.astype(bg_ref.dtype)


def _inproj(x, mod3, norm_g, w_in, *, seq, mod_row0, kv_dtype, rope_tabs=None, tm=512):
    t = x.shape[0]
    tiles_per_seq = seq // tm
    row = lambda i: (mod_row0 + (i // tiles_per_seq if mod_row0 else 0), 0, 0)
    in_specs = [pl.BlockSpec((tm, D_MODEL), lambda i: (i, 0)),
                pl.BlockSpec((None, 1, 6 * D_MODEL), row),
                pl.BlockSpec((1, D_MODEL), lambda i: (0, 0)),
                pl.BlockSpec((D_MODEL, IN_WIDTH), lambda i: (0, 0))]
    args = [x, mod3, norm_g, w_in]
    if rope_tabs is not None:
        in_specs += [pl.BlockSpec((tm, LANES), lambda i: (i % tiles_per_seq, 0))] * 2
        args += list(rope_tabs)
    widths = (ATTN_WIDTH, KV_WIDTH, KV_WIDTH, CONV_WIDTH, CONV_WIDTH)
    dtypes = (bf16, kv_dtype, kv_dtype, bf16, bf16)
    return pl.pallas_call(
        functools.partial(_inproj_kernel, rope=rope_tabs is not None),
        grid=(t // tm,),
        in_specs=in_specs,
        out_specs=[pl.BlockSpec((tm, w), lambda i: (i, 0)) for w in widths],
        out_shape=[jax.ShapeDtypeStruct((t, w), d) for w, d in zip(widths, dtypes)],
        compiler_params=_params("arbitrary"),
        name="inproj_rope" if rope_tabs is not None else "inproj",
    )(*args)


def _rope_tables(n):
    t = np.arange(n)
    inv = ROPE_BASE ** (-np.arange(N_FREQ, dtype=np.float32) / N_FREQ)
    rows = (t // GRID_W).astype(np.float32)
    cols = (t % GRID_W).astype(np.float32)
    d = np.arange(LANES) % HEAD_DIM
    pos = np.where(d[None, :] < HEAD_DIM // 2, rows[:, None], cols[:, None])
    ang = jnp.asarray(pos.astype(np.float32) * inv[d % N_FREQ][None, :])
    sign = np.where((d % (2 * N_FREQ)) < N_FREQ, -1.0, 1.0).astype(np.float32)
    return jnp.cos(ang), jnp.sin(ang) * sign[None, :]


def _attend_group(q_rows, sink_col, keys, vals, masks):
    scores = []
    m = sink_col
    for kk, mk in zip(keys, masks):
        s = _dot_nt(q_rows, kk) * SCALE
        if mk is not None:
            s = jnp.where(mk, s, NEG)
        scores.append(s)
        m = jnp.maximum(m, jnp.max(s, axis=-1, keepdims=True))
    den = jnp.exp(sink_col - m)
    acc = None
    for s, vv in zip(scores, vals):
        p = jnp.exp(s - m)
        den = den + jnp.sum(p, axis=-1, keepdims=True)
        pv = _dot(p.astype(bf16), vv)
        acc = pv if acc is None else acc + pv
    return acc / den


def _heads_attention(q, sink_ref, key_sets, val_sets, masks):
    bq = q.shape[0]
    ridx = lax.broadcasted_iota(jnp.int32, (Q_PER_KV * bq, 1), 0)
    outs = []
    for g in range(N_KV_HEADS):
        heads = range(g * Q_PER_KV, (g + 1) * Q_PER_KV)
        q_rows = jnp.concatenate([q[:, h * HEAD_DIM:(h + 1) * HEAD_DIM] for h in heads], axis=0)
        sink_col = jnp.full((Q_PER_KV * bq, 1), sink_ref[g * Q_PER_KV], f32)
        for j in range(1, Q_PER_KV):
            sink_col = jnp.where(ridx >= j * bq, sink_ref[g * Q_PER_KV + j], sink_col)
        lo, hi = g * HEAD_DIM, (g + 1) * HEAD_DIM
        o = _attend_group(q_rows, sink_col, [kk[:, lo:hi] for kk in key_sets],
                          [vv[:, lo:hi] for vv in val_sets], masks)
        outs += [o[j * bq:(j + 1) * bq] for j in range(Q_PER_KV)]
    return jnp.concatenate(outs, axis=1)


def _ctx_attn_kernel(sink_ref, q_ref, k_ref, v_ref, o_ref):
    k = k_ref[...].astype(bf16)
    v = v_ref[...].astype(bf16)
    o_ref[...] = _heads_attention(q_ref[...], sink_ref, [k], [v], [None]).astype(o_ref.dtype)


def _context_attention(q, k, v, sink, *, seq):
    t = q.shape[0]
    return pl.pallas_call(
        _ctx_attn_kernel,
        grid=(t // seq,),
        in_specs=[pl.BlockSpec(memory_space=pltpu.SMEM),
                  pl.BlockSpec((seq, ATTN_WIDTH), lambda b: (b, 0)),
                  pl.BlockSpec((seq, KV_WIDTH), lambda b: (b, 0)),
                  pl.BlockSpec((seq, KV_WIDTH), lambda b: (b, 0))],
        out_specs=pl.BlockSpec((seq, ATTN_WIDTH), lambda b: (b, 0)),
        out_shape=jax.ShapeDtypeStruct((t, ATTN_WIDTH), bf16),
        compiler_params=_params("arbitrary"),
        name="ctx_attn",
    )(sink, q, k, v)


def _lat_attn_kernel(sink_ref, q_ref, k_ref, v_ref, ck_ref, cv_ref, o_ref, *, bq, seq):
    i = pl.program_id(1)
    band = bq + 2 * WINDOW
    start = pl.multiple_of(jnp.clip(i * bq - WINDOW, 0, seq - band), LANES)
    kb = k_ref[pl.ds(start, band), :]
    vb = v_ref[pl.ds(start, band), :]
    shape = (Q_PER_KV * bq, band)
    qpos = i * bq + (lax.broadcasted_iota(jnp.int32, shape, 0) & (bq - 1))
    kpos = start + lax.broadcasted_iota(jnp.int32, shape, 1)
    mask = jnp.abs(qpos - kpos) <= WINDOW
    ck = ck_ref[...].astype(bf16)
    cv = cv_ref[...].astype(bf16)
    o_ref[...] = _heads_attention(q_ref[...], sink_ref, [kb, ck], [vb, cv],
                                  [mask, None]).astype(o_ref.dtype)


def _latent_attention(q, k, v, ck, cv, sink, *, seq, bq=128):
    nb, past = ck.shape[0], ck.shape[1]
    q3, k3, v3 = (a.reshape(nb, seq, a.shape[-1]) for a in (q, k, v))
    out = pl.pallas_call(
        functools.partial(_lat_attn_kernel, bq=bq, seq=seq),
        grid=(nb, seq // bq),
        in_specs=[pl.BlockSpec(memory_space=pltpu.SMEM),
                  pl.BlockSpec((None, bq, ATTN_WIDTH), lambda b, i: (b, i, 0)),
                  pl.BlockSpec((None, seq, KV_WIDTH), lambda b, i: (b, 0, 0)),
                  pl.BlockSpec((None, seq, KV_WIDTH), lambda b, i: (b, 0, 0)),
                  pl.BlockSpec((None, past, KV_WIDTH), lambda b, i: (b, 0, 0)),
                  pl.BlockSpec((None, past, KV_WIDTH), lambda b, i: (b, 0, 0))],
        out_specs=pl.BlockSpec((None, bq, ATTN_WIDTH), lambda b, i: (b, i, 0)),
        out_shape=jax.ShapeDtypeStruct((nb, seq, ATTN_WIDTH), bf16),
        compiler_params=_params("arbitrary", "arbitrary"),
        name="lat_attn",
    )(sink, q3, k3, v3, ck, cv)
    return out.reshape(nb * seq, ATTN_WIDTH)


def _route(logits):
    lane = lax.broadcasted_iota(jnp.int32, logits.shape, 1)
    lane_f = lane.astype(f32)
    big = jnp.float32(LANES)
    is_c = jnp.logical_and(lane >= COARSE_LANE0, lane < COARSE_LANE0 + N_GROUPS)
    lc = jnp.where(is_c, logits, -jnp.inf)
    mc = jnp.max(lc, axis=-1, keepdims=True)
    grp = jnp.min(jnp.where(lc == mc, lane_f, big), axis=-1, keepdims=True) - COARSE_LANE0
    pg = 1.0 / jnp.sum(jnp.exp(lc - mc), axis=-1, keepdims=True)
    in_g = jnp.floor(lane_f * (1.0 / EXPERTS_PER_GROUP)) == grp
    fl = jnp.where(in_g, logits, -jnp.inf)
    m1 = jnp.max(fl, axis=-1, keepdims=True)
    i1 = jnp.min(jnp.where(fl == m1, lane_f, big), axis=-1, keepdims=True)
    fl2 = jnp.where(lane_f == i1, -jnp.inf, fl)
    m2 = jnp.max(fl2, axis=-1, keepdims=True)
    i2 = jnp.min(jnp.where(fl2 == m2, lane_f, big), axis=-1, keepdims=True)
    e2 = jnp.exp(m2 - m1)
    p1 = pg / (1.0 + e2)
    p2 = pg * e2 / (1.0 + e2)
    return jnp.where(lane_f == i1, p1, 0.0) + jnp.where(lane_f == i2, p2, 0.0)


def _post_kernel(x_ref, attn_ref, u_ref, up_ref, un_ref, bg_ref, mod_ref, ag_ref, cg_ref, cw_ref,
                 cb_ref, wo_ref, n2_ref, wr_ref, br_ref, x1_ref, h2_ref, gates_ref, *, tm, seq):
    i = pl.program_id(0)
    u = u_ref[...].astype(f32)
    rows = lax.broadcasted_iota(jnp.int32, (tm, 1), 0)
    spos = (i * tm + rows) % seq
    u_dn = jnp.where(rows == 0, up_ref[...].astype(f32)[BF16_SUBLANES - 1:, :],
                     pltpu.roll(u, 1, axis=0))
    u_dn = jnp.where(spos == 0, 0.0, u_dn)
    u_up = jnp.where(rows == tm - 1, un_ref[...].astype(f32)[0:1, :], pltpu.roll(u, tm - 1, axis=0))
    u_up = jnp.where(spos == seq - 1, 0.0, u_up)
    y = u_dn * cw_ref[0:1, :] + u * cw_ref[1:2, :] + u_up * cw_ref[2:3, :] + cb_ref[...]
    conv = bg_ref[...].astype(f32) * y
    attn_n = _rms(attn_ref[...].astype(f32)) * ag_ref[...]
    conv_n = _rms(conv) * cg_ref[...]
    mixed = (_dot(attn_n.astype(bf16), wo_ref[0:ATTN_WIDTH, :])
             + _dot(conv_n.astype(bf16), wo_ref[ATTN_WIDTH:, :]))
    x1 = x_ref[...] + mod_ref[:, G1:G1 + D_MODEL] * mixed
    x1_ref[...] = x1
    h2 = _rms(x1) * n2_ref[...]
    h2 = h2 * (1.0 + mod_ref[:, SC2:SC2 + D_MODEL]) + mod_ref[:, SH2:SH2 + D_MODEL]
    h2_ref[...] = h2.astype(h2_ref.dtype)
    logits = jnp.dot(h2, wr_ref[...], precision=lax.Precision.HIGHEST,
                     preferred_element_type=f32) + br_ref[...]
    gates_ref[...] = _route(logits)


def _post(x, attn, u, bg, mod3, p, *, seq, mod_row0, tm=256):
    t = x.shape[0]
    tiles_per_seq = seq // tm
    halo = BF16_SUBLANES
    n_halo = t // halo
    row = lambda i: (mod_row0 + (i // tiles_per_seq if mod_row0 else 0), 0, 0)
    tile = lambda w: pl.BlockSpec((tm, w), lambda i: (i, 0))
    full = lambda a: pl.BlockSpec(a.shape, lambda i: (0,) * a.ndim)
    small = [p["attn_out_g"], p["conv_out_g"], p["conv_w"], p["conv_b"], p["w_o"], p["norm2_g"],
             p["w_router"], p["b_router"]]
    return pl.pallas_call(
        functools.partial(_post_kernel, tm=tm, seq=seq),
        grid=(t // tm,),
        in_specs=[tile(D_MODEL), tile(ATTN_WIDTH), tile(CONV_WIDTH),
                  pl.BlockSpec((halo, CONV_WIDTH),
                               lambda i: (jnp.maximum(i * (tm // halo) - 1, 0), 0)),
                  pl.BlockSpec((halo, CONV_WIDTH),
                               lambda i: (jnp.minimum((i + 1) * (tm // halo), n_halo - 1), 0)),
                  tile(CONV_WIDTH),
                  pl.BlockSpec((None, 1, 6 * D_MODEL), row)] + [full(a) for a in small],
        out_specs=[tile(D_MODEL), tile(D_MODEL), tile(LANES)],
        out_shape=[jax.ShapeDtypeStruct((t, D_MODEL), f32),
                   jax.ShapeDtypeStruct((t, D_MODEL), bf16),
                   jax.ShapeDtypeStruct((t, LANES), f32)],
        compiler_params=_params("arbitrary"),
        name="post",
    )(x, attn, u, u, u, bg, mod3, *small)


def _moe_kernel(h_ref, gates_ref, wg_ref, wu_ref, wd_ref, x1_ref, mod_ref, fg_ref, o_ref, acc_ref):
    e = pl.program_id(1)

    @pl.when(e == 0)
    def _():
        acc_ref[...] = jnp.zeros_like(acc_ref)

    h = h_ref[...]
    lane = lax.broadcasted_iota(jnp.int32, gates_ref.shape, 1)
    ge = jnp.sum(jnp.where(lane == e, gates_ref[...], 0.0), axis=-1, keepdims=True)
    g = _dot(h, wg_ref[...].astype(bf16))
    up = _dot(h, wu_ref[...].astype(bf16))
    a = (g * jax.nn.sigmoid(g)) * up * ge
    acc_ref[...] += _dot(a.astype(bf16), wd_ref[...].astype(bf16))

    @pl.when(e == pl.num_programs(1) - 1)
    def _():
        x2 = x1_ref[...] + mod_ref[:, G2:G2 + D_MODEL] * acc_ref[...]
        o_ref[...] = _rms(x2) * fg_ref[...]


def _moe(h2, gates, x1, mod3, w_gate, w_up, w_down, final_g, *, seq, mod_row0, tm=1024):
    t = h2.shape[0]
    tiles_per_seq = max(seq // tm, 1)
    row = lambda i, e: (mod_row0 + (i // tiles_per_seq if mod_row0 else 0), 0, 0)
    tile = lambda w: pl.BlockSpec((tm, w), lambda i, e: (i, 0))
    return pl.pallas_call(
        _moe_kernel,
        grid=(t // tm, N_EXPERTS),
        in_specs=[tile(D_MODEL), tile(LANES),
                  pl.BlockSpec((None, D_MODEL, D_EXPERT), lambda i, e: (e, 0, 0)),
                  pl.BlockSpec((None, D_MODEL, D_EXPERT), lambda i, e: (e, 0, 0)),
                  pl.BlockSpec((None, D_EXPERT, D_MODEL), lambda i, e: (e, 0, 0)),
                  tile(D_MODEL),
                  pl.BlockSpec((None, 1, 6 * D_MODEL), row),
                  pl.BlockSpec((1, D_MODEL), lambda i, e: (0, 0))],
        out_specs=tile(D_MODEL),
        out_shape=jax.ShapeDtypeStruct((t, D_MODEL), f32),
        scratch_shapes=[pltpu.VMEM((tm, D_MODEL), f32)],
        compiler_params=_params("arbitrary", "arbitrary"),
        name="moe",
    )(h2, gates, w_gate, w_up, w_down, x1, mod3, final_g)


def _trunk(x, mod3, p, *, seq, mod_row0, ctx_kv):
    nb = x.shape[0] // seq
    latent = ctx_kv is not None
    q, k, v, u, bg = _inproj(x, mod3, p["norm1_g"], p["w_in"], seq=seq, mod_row0=mod_row0,
                             kv_dtype=bf16 if latent else f32,
                             rope_tabs=_rope_tables(seq) if latent else None)
    if latent:
        attn = _latent_attention(q, k, v, ctx_kv[0], ctx_kv[1], p["sink"], seq=seq)
    else:
        attn = _context_attention(q, k, v, p["sink"], seq=seq)
    x1, h2, gates = _post(x, attn, u, bg, mod3, p, seq=seq, mod_row0=mod_row0)
    y = _moe(h2, gates, x1, mod3, p["w_gate"], p["w_up"], p["w_down"], p["final_g"],
             seq=seq, mod_row0=mod_row0)
    return y.reshape(nb, seq, D_MODEL), k, v


def kernel(x_prompt, x_sample, cache_k, cache_v, c, c_ctx, w_mod, b_mod, norm1_g, w_in, conv_w,
           conv_b, sink, attn_out_g, conv_out_g, w_o, norm2_g, w_coarse, b_coarse, w_fine, b_fine,
           w_gate, w_up, w_down, final_g):
    batch, seq, _ = x_prompt.shape
    dec_batch, dec_seq, _ = x_sample.shape
    past = cache_k.shape[2]
    assert w_mod.shape[0] == 1 and 1 + dec_batch <= MOD_ROWS

    cvecs = jnp.concatenate([c_ctx[None], c, jnp.zeros((MOD_ROWS - 1 - dec_batch, D_MODEL), f32)])
    mod3 = _modulation(cvecs, w_mod[0], b_mod[0]).reshape(MOD_ROWS, 1, 6 * D_MODEL)

    pad = jnp.zeros((D_MODEL, LANES - N_EXPERTS - N_GROUPS), f32)
    p = {
        "norm1_g": norm1_g, "w_in": w_in[0].astype(bf16), "conv_w": conv_w[0], "conv_b": conv_b,
        "sink": sink[0], "attn_out_g": attn_out_g, "conv_out_g": conv_out_g,
        "w_o": w_o[0].astype(bf16), "norm2_g": norm2_g,
        "w_router": jnp.concatenate([w_fine[0], w_coarse[0], pad], axis=1),
        "b_router": jnp.concatenate([b_fine[0], b_coarse[0], pad[0]])[None],
        "w_gate": w_gate[0], "w_up": w_up[0], "w_down": w_down[0], "final_g": final_g[None],
    }

    y_prompt, k_p, v_p = _trunk(x_prompt.reshape(batch * seq, D_MODEL), mod3, p, seq=seq,
                                mod_row0=0, ctx_kv=None)
    new_k = k_p.reshape(batch, 1, seq, N_KV_HEADS, HEAD_DIM)
    new_v = v_p.reshape(batch, 1, seq, N_KV_HEADS, HEAD_DIM)

    ctx_kv = (cache_k[:, 0].reshape(dec_batch, past, KV_WIDTH),
              cache_v[:, 0].reshape(dec_batch, past, KV_WIDTH))
    y_sample, _, _ = _trunk(x_sample.reshape(dec_batch * dec_seq, D_MODEL), mod3, p, seq=dec_seq,
                            mod_row0=1, ctx_kv=ctx_kv)
    return y_prompt, y_sample, new_k, new_v
```

```python
import functools

import jax
import jax.numpy as jnp
import numpy as np
from jax import lax
from jax.experimental import pallas as pl
from jax.experimental.pallas import tpu as pltpu

D_MODEL = 1024
HEAD_DIM = 64
ATTN_WIDTH = 512
N_HEADS = 8
N_KV_HEADS = 2
Q_PER_KV = 4
KV_WIDTH = 128
CONV_WIDTH = 512
CONV_K = 3
WINDOW = 128
GRID_W = 64
ROPE_BASE = 10000.0
N_FREQ = 16
N_GROUPS = 4
EXPERTS_PER_GROUP = 8
N_EXPERTS = 32
TOP_K = 2
D_EXPERT = 256
IN_WIDTH = ATTN_WIDTH + 2 * KV_WIDTH + 3 * CONV_WIDTH
EPS = 1e-6
NEG = -1e30
SCALE = HEAD_DIM ** -0.5

LANES = 128
F32_SUBLANES = 8
BF16_SUBLANES = 16
assert D_MODEL == F32_SUBLANES * LANES

MOE_BLOCK = 2048
MOE_CHUNK = 128
CHUNK_PITCH = MOE_CHUNK + F32_SUBLANES
OFFS_LEN = 40
VMEM_LIMIT_BYTES = 48 * 1024 * 1024

SH1, SC1, G1, SH2, SC2, G2 = (i * D_MODEL for i in range(6))
MOD_ROWS = 8

COARSE_LANE0 = N_EXPERTS

f32 = jnp.float32
bf16 = jnp.bfloat16


def _params(*semantics):
    return pltpu.CompilerParams(dimension_semantics=semantics, vmem_limit_bytes=VMEM_LIMIT_BYTES)


def _rms(x):
    return x * lax.rsqrt(jnp.mean(x * x, axis=-1, keepdims=True) + EPS)


def _dot(a, b):
    return jnp.dot(a, b, preferred_element_type=f32)


def _dot_nt(a, b):
    return lax.dot_general(a, b, (((1,), (1,)), ((), ())), preferred_element_type=f32)


def _mod_kernel(cv_ref, w_ref, b_ref, o_ref):
    a = cv_ref[...]
    a = a * jax.nn.sigmoid(a)
    o_ref[...] = jnp.dot(a, w_ref[...], precision=lax.Precision.HIGHEST,
                         preferred_element_type=f32) + b_ref[...]


def _modulation(cvecs, w_mod, b_mod):
    tn = D_MODEL
    return pl.pallas_call(
        _mod_kernel,
        grid=(6 * D_MODEL // tn,),
        in_specs=[pl.BlockSpec((MOD_ROWS, D_MODEL), lambda j: (0, 0)),
                  pl.BlockSpec((D_MODEL, tn), lambda j: (0, j)),
                  pl.BlockSpec((1, tn), lambda j: (0, j))],
        out_specs=pl.BlockSpec((MOD_ROWS, tn), lambda j: (0, j)),
        out_shape=jax.ShapeDtypeStruct((MOD_ROWS, 6 * D_MODEL), f32),
        compiler_params=_params("arbitrary"),
        name="mod",
    )(cvecs, w_mod, b_mod.reshape(1, -1))


def _swap_halves(x):
    lane = lax.broadcasted_iota(jnp.int32, x.shape, 1)
    up = pltpu.roll(x, LANES - N_FREQ, axis=1)
    dn = pltpu.roll(x, N_FREQ, axis=1)
    return jnp.where((lane % (2 * N_FREQ)) < N_FREQ, up, dn)


def _rope(x, cos, sin):
    parts = []
    for c in range(x.shape[1] // LANES):
        xc = x[:, c * LANES:(c + 1) * LANES]
        parts.append(xc * cos + _swap_halves(xc) * sin)
    return parts[0] if len(parts) == 1 else jnp.concatenate(parts, axis=1)


def _inproj_kernel(*refs, rope):
    if rope:
        x_ref, mod_ref, g_ref, w_ref, cos_ref, sin_ref, q_ref, k_ref, v_ref, u_ref, bg_ref = refs
    else:
        x_ref, mod_ref, g_ref, w_ref, q_ref, k_ref, v_ref, u_ref, bg_ref = refs
    h = _rms(x_ref[...]) * g_ref[...]
    h = h * (1.0 + mod_ref[:, SC1:SC1 + D_MODEL]) + mod_ref[:, SH1:SH1 + D_MODEL]
    z = _dot(h.astype(bf16), w_ref[...])
    o = 0
    q = z[:, o:o + ATTN_WIDTH]; o += ATTN_WIDTH
    k = z[:, o:o + KV_WIDTH]; o += KV_WIDTH
    v = z[:, o:o + KV_WIDTH]; o += KV_WIDTH
    bg = z[:, o:o + CONV_WIDTH]; o += CONV_WIDTH
    cg = z[:, o:o + CONV_WIDTH]; o += CONV_WIDTH
    xin = z[:, o:o + CONV_WIDTH]
    if rope:
        q = _rope(q, cos_ref[...], sin_ref[...])
        k = _rope(k, cos_ref[...], sin_ref[...])
    q_ref[...] = q.astype(q_ref.dtype)
    k_ref[...] = k.astype(k_ref.dtype)
    v_ref[...] = v.astype(v_ref.dtype)
    u_ref[...] = (cg * xin).astype(u_ref.dtype)
    bg_ref[...] = bg.astype(bg_ref.dtype)


def _inproj(x, mod3, norm_g, w_in, *, seq, mod_row0, kv_dtype, rope_tabs=None, tm=512):
    t = x.shape[0]
    tiles_per_seq = seq // tm
    row = lambda i: (mod_row0 + (i // tiles_per_seq if mod_row0 else 0), 0, 0)
    in_specs = [pl.BlockSpec((tm, D_MODEL), lambda i: (i, 0)),
                pl.BlockSpec((None, 1, 6 * D_MODEL), row),
                pl.BlockSpec((1, D_MODEL), lambda i: (0, 0)),
                pl.BlockSpec((D_MODEL, IN_WIDTH), lambda i: (0, 0))]
    args = [x, mod3, norm_g, w_in]
    if rope_tabs is not None:
        in_specs += [pl.BlockSpec((tm, LANES), lambda i: (i % tiles_per_seq, 0))] * 2
        args += list(rope_tabs)
    widths = (ATTN_WIDTH, KV_WIDTH, KV_WIDTH, CONV_WIDTH, CONV_WIDTH)
    dtypes = (bf16, kv_dtype, kv_dtype, bf16, bf16)
    return pl.pallas_call(
        functools.partial(_inproj_kernel, rope=rope_tabs is not None),
        grid=(t // tm,),
        in_specs=in_specs,
        out_specs=[pl.BlockSpec((tm, w), lambda i: (i, 0)) for w in widths],
        out_shape=[jax.ShapeDtypeStruct((t, w), d) for w, d in zip(widths, dtypes)],
        compiler_params=_params("arbitrary"),
        name="inproj_rope" if rope_tabs is not None else "inproj",
    )(*args)


def _rope_tables(n):
    t = np.arange(n)
    inv = ROPE_BASE ** (-np.arange(N_FREQ, dtype=np.float32) / N_FREQ)
    rows = (t // GRID_W).astype(np.float32)
    cols = (t % GRID_W).astype(np.float32)
    d = np.arange(LANES) % HEAD_DIM
    pos = np.where(d[None, :] < HEAD_DIM // 2, rows[:, None], cols[:, None])
    ang = jnp.asarray(pos.astype(np.float32) * inv[d % N_FREQ][None, :])
    sign = np.where((d % (2 * N_FREQ)) < N_FREQ, -1.0, 1.0).astype(np.float32)
    return jnp.cos(ang), jnp.sin(ang) * sign[None, :]


def _attend_group(q_rows, sink_col, keys, vals, masks):
    scores = []
    m = sink_col
    for kk, mk in zip(keys, masks):
        s = _dot_nt(q_rows, kk) * SCALE
        if mk is not None:
            s = jnp.where(mk, s, NEG)
        scores.append(s)
        m = jnp.maximum(m, jnp.max(s, axis=-1, keepdims=True))
    den = jnp.exp(sink_col - m)
    acc = None
    for s, vv in zip(scores, vals):
        p = jnp.exp(s - m)
        den = den + jnp.sum(p, axis=-1, keepdims=True)
        pv = _dot(p.astype(bf16), vv)
        acc = pv if acc is None else acc + pv
    return acc / den


def _heads_attention(q, sink_ref, key_sets, val_sets, masks):
    bq = q.shape[0]
    ridx = lax.broadcasted_iota(jnp.int32, (Q_PER_KV * bq, 1), 0)
    outs = []
    for g in range(N_KV_HEADS):
        heads = range(g * Q_PER_KV, (g + 1) * Q_PER_KV)
        q_rows = jnp.concatenate([q[:, h * HEAD_DIM:(h + 1) * HEAD_DIM] for h in heads], axis=0)
        sink_col = jnp.full((Q_PER_KV * bq, 1), sink_ref[g * Q_PER_KV], f32)
        for j in range(1, Q_PER_KV):
            sink_col = jnp.where(ridx >= j * bq, sink_ref[g * Q_PER_KV + j], sink_col)
        lo, hi = g * HEAD_DIM, (g + 1) * HEAD_DIM
        o = _attend_group(q_rows, sink_col, [kk[:, lo:hi] for kk in key_sets],
                          [vv[:, lo:hi] for vv in val_sets], masks)
        outs += [o[j * bq:(j + 1) * bq] for j in range(Q_PER_KV)]
    return jnp.concatenate(outs, axis=1)


def _ctx_attn_kernel(sink_ref, q_ref, k_ref, v_ref, o_ref):
    k = k_ref[...].astype(bf16)
    v = v_ref[...].astype(bf16)
    o_ref[...] = _heads_attention(q_ref[...], sink_ref, [k], [v], [None]).astype(o_ref.dtype)


def _context_attention(q, k, v, sink, *, seq):
    t = q.shape[0]
    return pl.pallas_call(
        _ctx_attn_kernel,
        grid=(t // seq,),
        in_specs=[pl.BlockSpec(memory_space=pltpu.SMEM),
                  pl.BlockSpec((seq, ATTN_WIDTH), lambda b: (b, 0)),
                  pl.BlockSpec((seq, KV_WIDTH), lambda b: (b, 0)),
                  pl.BlockSpec((seq, KV_WIDTH), lambda b: (b, 0))],
        out_specs=pl.BlockSpec((seq, ATTN_WIDTH), lambda b: (b, 0)),
        out_shape=jax.ShapeDtypeStruct((t, ATTN_WIDTH), bf16),
        compiler_params=_params("arbitrary"),
        name="ctx_attn",
    )(sink, q, k, v)


def _lat_attn_kernel(sink_ref, q_ref, k_ref, v_ref, ck_ref, cv_ref, o_ref, *, bq, seq):
    i = pl.program_id(1)
    band = bq + 2 * WINDOW
    start = pl.multiple_of(jnp.clip(i * bq - WINDOW, 0, seq - band), LANES)
    kb = k_ref[pl.ds(start, band), :]
    vb = v_ref[pl.ds(start, band), :]
    shape = (Q_PER_KV * bq, band)
    qpos = i * bq + (lax.broadcasted_iota(jnp.int32, shape, 0) & (bq - 1))
    kpos = start + lax.broadcasted_iota(jnp.int32, shape, 1)
    mask = jnp.abs(qpos - kpos) <= WINDOW
    ck = ck_ref[...].astype(bf16)
    cv = cv_ref[...].astype(bf16)
    o_ref[...] = _heads_attention(q_ref[...], sink_ref, [kb, ck], [vb, cv],
                                  [mask, None]).astype(o_ref.dtype)


def _latent_attention(q, k, v, ck, cv, sink, *, seq, bq=128):
    nb, past = ck.shape[0], ck.shape[1]
    q3, k3, v3 = (a.reshape(nb, seq, a.shape[-1]) for a in (q, k, v))
    out = pl.pallas_call(
        functools.partial(_lat_attn_kernel, bq=bq, seq=seq),
        grid=(nb, seq // bq),
        in_specs=[pl.BlockSpec(memory_space=pltpu.SMEM),
                  pl.BlockSpec((None, bq, ATTN_WIDTH), lambda b, i: (b, i, 0)),
                  pl.BlockSpec((None, seq, KV_WIDTH), lambda b, i: (b, 0, 0)),
                  pl.BlockSpec((None, seq, KV_WIDTH), lambda b, i: (b, 0, 0)),
                  pl.BlockSpec((None, past, KV_WIDTH), lambda b, i: (b, 0, 0)),
                  pl.BlockSpec((None, past, KV_WIDTH), lambda b, i: (b, 0, 0))],
        out_specs=pl.BlockSpec((None, bq, ATTN_WIDTH), lambda b, i: (b, i, 0)),
        out_shape=jax.ShapeDtypeStruct((nb, seq, ATTN_WIDTH), bf16),
        compiler_params=_params("arbitrary", "arbitrary"),
        name="lat_attn",
    )(sink, q3, k3, v3, ck, cv)
    return out.reshape(nb * seq, ATTN_WIDTH)


def _route(logits):
    lane = lax.broadcasted_iota(jnp.int32, logits.shape, 1)
    lane_f = lane.astype(f32)
    big = jnp.float32(LANES)
    is_c = jnp.logical_and(lane >= COARSE_LANE0, lane < COARSE_LANE0 + N_GROUPS)
    lc = jnp.where(is_c, logits, -jnp.inf)
    mc = jnp.max(lc, axis=-1, keepdims=True)
    grp = jnp.min(jnp.where(lc == mc, lane_f, big), axis=-1, keepdims=True) - COARSE_LANE0
    pg = 1.0 / jnp.sum(jnp.exp(lc - mc), axis=-1, keepdims=True)
    in_g = jnp.floor(lane_f * (1.0 / EXPERTS_PER_GROUP)) == grp
    fl = jnp.where(in_g, logits, -jnp.inf)
    m1 = jnp.max(fl, axis=-1, keepdims=True)
    i1 = jnp.min(jnp.where(fl == m1, lane_f, big), axis=-1, keepdims=True)
    fl2 = jnp.where(lane_f == i1, -jnp.inf, fl)
    m2 = jnp.max(fl2, axis=-1, keepdims=True)
    i2 = jnp.min(jnp.where(fl2 == m2, lane_f, big), axis=-1, keepdims=True)
    e2 = jnp.exp(m2 - m1)
    p1 = pg / (1.0 + e2)
    p2 = pg * e2 / (1.0 + e2)
    packed = jnp.where(lane == 0, i1, jnp.where(lane == 1, i2, jnp.where(lane == 2, p1, p2)))
    return jnp.where(lane < 4, packed, 0.0)


def _post_kernel(x_ref, attn_ref, u_ref, up_ref, un_ref, bg_ref, mod_ref, ag_ref, cg_ref, cw_ref,
                 cb_ref, wo_ref, n2_ref, wr_ref, br_ref, x1_ref, h2_ref, route_ref, *, tm, seq):
    i = pl.program_id(0)
    u = u_ref[...].astype(f32)
    rows = lax.broadcasted_iota(jnp.int32, (tm, 1), 0)
    spos = (i * tm + rows) % seq
    u_dn = jnp.where(rows == 0, up_ref[...].astype(f32)[BF16_SUBLANES - 1:, :],
                     pltpu.roll(u, 1, axis=0))
    u_dn = jnp.where(spos == 0, 0.0, u_dn)
    u_up = jnp.where(rows == tm - 1, un_ref[...].astype(f32)[0:1, :], pltpu.roll(u, tm - 1, axis=0))
    u_up = jnp.where(spos == seq - 1, 0.0, u_up)
    y = u_dn * cw_ref[0:1, :] + u * cw_ref[1:2, :] + u_up * cw_ref[2:3, :] + cb_ref[...]
    conv = bg_ref[...].astype(f32) * y
    attn_n = _rms(attn_ref[...].astype(f32)) * ag_ref[...]
    conv_n = _rms(conv) * cg_ref[...]
    mixed = (_dot(attn_n.astype(bf16), wo_ref[0:ATTN_WIDTH, :])
             + _dot(conv_n.astype(bf16), wo_ref[ATTN_WIDTH:, :]))
    x1 = x_ref[...] + mod_ref[:, G1:G1 + D_MODEL] * mixed
    x1_ref[...] = x1
    h2 = _rms(x1) * n2_ref[...]
    h2 = h2 * (1.0 + mod_ref[:, SC2:SC2 + D_MODEL]) + mod_ref[:, SH2:SH2 + D_MODEL]
    for c in range(D_MODEL // LANES):
        h2_ref[pl.ds(c, tm, stride=F32_SUBLANES), :] = h2[:, c * LANES:(c + 1) * LANES]
    logits = jnp.dot(h2, wr_ref[...], precision=lax.Precision.HIGHEST,
                     preferred_element_type=f32) + br_ref[...]
    route_ref[...] = _route(logits)


def _post(x, attn, u, bg, mod3, p, *, seq, mod_row0, tm=256):
    t = x.shape[0]
    tiles_per_seq = seq // tm
    halo = BF16_SUBLANES
    n_halo = t // halo
    row = lambda i: (mod_row0 + (i // tiles_per_seq if mod_row0 else 0), 0, 0)
    tile = lambda w: pl.BlockSpec((tm, w), lambda i: (i, 0))
    full = lambda a: pl.BlockSpec(a.shape, lambda i: (0,) * a.ndim)
    small = [p["attn_out_g"], p["conv_out_g"], p["conv_w"], p["conv_b"], p["w_o"], p["norm2_g"],
             p["w_router"], p["b_router"]]
    return pl.pallas_call(
        functools.partial(_post_kernel, tm=tm, seq=seq),
        grid=(t // tm,),
        in_specs=[tile(D_MODEL), tile(ATTN_WIDTH), tile(CONV_WIDTH),
                  pl.BlockSpec((halo, CONV_WIDTH),
                               lambda i: (jnp.maximum(i * (tm // halo) - 1, 0), 0)),
                  pl.BlockSpec((halo, CONV_WIDTH),
                               lambda i: (jnp.minimum((i + 1) * (tm // halo), n_halo - 1), 0)),
                  tile(CONV_WIDTH),
                  pl.BlockSpec((None, 1, 6 * D_MODEL), row)] + [full(a) for a in small],
        out_specs=[tile(D_MODEL), pl.BlockSpec((tm * F32_SUBLANES, LANES), lambda i: (i, 0)),
                   tile(LANES)],
        out_shape=[jax.ShapeDtypeStruct((t, D_MODEL), f32),
                   jax.ShapeDtypeStruct((t * F32_SUBLANES, LANES), f32),
                   jax.ShapeDtypeStruct((t, LANES), f32)],
        compiler_params=_params("arbitrary"),
        name="post",
    )(x, attn, u, u, u, bg, mod3, *small)


def _dispatch_tables(route, blk):
    nblk = route.shape[0] // blk
    flat = lambda a: a.reshape(nblk, blk, TOP_K).transpose(0, 2, 1).reshape(nblk, TOP_K * blk)
    experts = flat(route[:, 0:TOP_K].astype(jnp.int32))
    weights = flat(route[:, TOP_K:2 * TOP_K])
    order = jnp.argsort(experts, axis=1, stable=True).astype(jnp.int32)
    w_sorted = jnp.take_along_axis(weights, order, axis=1)
    bounds = jnp.arange(OFFS_LEN, dtype=jnp.int32)
    offs = jnp.sum(experts[:, :, None] < bounds[None, None, :], axis=1, dtype=jnp.int32)
    return order[:, None, :], w_sorted[:, None, :], offs[:, None, :]


def _moe_kernel(order_ref, wts_ref, offs_ref, h_ref, wg_ref, wu_ref, wd_ref, y_ref,
                ytmp_ref, xg_ref, og_ref, *, blk):
    e = pl.program_id(1)
    lo = offs_ref[0, e]
    hi = offs_ref[0, e + 1]
    n_chunks = lax.shift_right_logical(hi - lo + (MOE_CHUNK - 1), MOE_CHUNK.bit_length() - 1)
    n_col = D_MODEL // LANES

    def chunk(ci, carry):
        base = lo + ci * MOE_CHUNK
        last = hi - 1
        for r in range(MOE_CHUNK):
            a = order_ref[0, jnp.minimum(base + r, last)]
            src = pl.multiple_of((a & (blk - 1)) * F32_SUBLANES, F32_SUBLANES)
            xg_ref[pl.ds(r, F32_SUBLANES, stride=CHUNK_PITCH), :] = h_ref[pl.ds(src, F32_SUBLANES), :]
        x = jnp.concatenate([xg_ref[c * CHUNK_PITCH:c * CHUNK_PITCH + MOE_CHUNK, :]
                             for c in range(n_col)], axis=1).astype(bf16)
        g = _dot(x, wg_ref[...])
        up = _dot(x, wu_ref[...])
        act = (g * jax.nn.sigmoid(g)) * up
        out = _dot(act.astype(bf16), wd_ref[...])
        for c in range(n_col):
            og_ref[c * CHUNK_PITCH:c * CHUNK_PITCH + MOE_CHUNK, :] = out[:, c * LANES:(c + 1) * LANES]
        for r in range(MOE_CHUNK):
            s = jnp.minimum(base + r, last)
            dst = pl.multiple_of(order_ref[0, s] * F32_SUBLANES, F32_SUBLANES)
            ytmp_ref[pl.ds(dst, F32_SUBLANES), :] = (
                og_ref[pl.ds(r, F32_SUBLANES, stride=CHUNK_PITCH), :] * wts_ref[0, s])
        return carry

    lax.fori_loop(0, n_chunks, chunk, 0)

    @pl.when(e == pl.num_programs(1) - 1)
    def _():
        for c in range(n_col):
            acc = ytmp_ref[pl.ds(c, blk, stride=F32_SUBLANES), :]
            for k in range(1, TOP_K):
                acc = acc + ytmp_ref[pl.ds(k * blk * F32_SUBLANES + c, blk, stride=F32_SUBLANES), :]
            y_ref[:, c * LANES:(c + 1) * LANES] = acc.astype(y_ref.dtype)


def _final_kernel(x1_ref, y_ref, mod_ref, fg_ref, o_ref):
    x2 = x1_ref[...] + mod_ref[:, G2:G2 + D_MODEL] * y_ref[...].astype(f32)
    o_ref[...] = _rms(x2) * fg_ref[...]


def _moe(h2_tiles, route, w_gate, w_up, w_down, *, blk=MOE_BLOCK):
    t = route.shape[0]
    order, w_sorted, offs = _dispatch_tables(route, blk)
    smem = lambda n: pl.BlockSpec((None, 1, n), lambda b, e: (b, 0, 0), memory_space=pltpu.SMEM)
    stage = pltpu.VMEM((D_MODEL // LANES * CHUNK_PITCH, LANES), f32)
    return pl.pallas_call(
        functools.partial(_moe_kernel, blk=blk),
        grid=(t // blk, N_EXPERTS),
        in_specs=[smem(TOP_K * blk), smem(TOP_K * blk), smem(OFFS_LEN),
                  pl.BlockSpec((blk * F32_SUBLANES, LANES), lambda b, e: (b, 0)),
                  pl.BlockSpec((None, D_MODEL, D_EXPERT), lambda b, e: (e, 0, 0)),
                  pl.BlockSpec((None, D_MODEL, D_EXPERT), lambda b, e: (e, 0, 0)),
                  pl.BlockSpec((None, D_EXPERT, D_MODEL), lambda b, e: (e, 0, 0))],
        out_specs=pl.BlockSpec((blk, D_MODEL), lambda b, e: (b, 0)),
        out_shape=jax.ShapeDtypeStruct((t, D_MODEL), bf16),
        scratch_shapes=[pltpu.VMEM((TOP_K * blk * F32_SUBLANES, LANES), f32), stage, stage],
        compiler_params=_params("arbitrary", "arbitrary"),
        name="moe",
    )(order, w_sorted, offs, h2_tiles, w_gate, w_up, w_down)


def _final(x1, y, mod3, final_g, *, seq, mod_row0, tm=512):
    t = x1.shape[0]
    tiles_per_seq = max(seq // tm, 1)
    row = lambda i: (mod_row0 + (i // tiles_per_seq if mod_row0 else 0), 0, 0)
    tile = pl.BlockSpec((tm, D_MODEL), lambda i: (i, 0))
    return pl.pallas_call(
        _final_kernel,
        grid=(t // tm,),
        in_specs=[tile, tile, pl.BlockSpec((None, 1, 6 * D_MODEL), row),
                  pl.BlockSpec((1, D_MODEL), lambda i: (0, 0))],
        out_specs=tile,
        out_shape=jax.ShapeDtypeStruct((t, D_MODEL), f32),
        compiler_params=_params("arbitrary"),
        name="final",
    )(x1, y, mod3, final_g)


def _trunk(x, mod3, p, *, seq, mod_row0, ctx_kv):
    nb = x.shape[0] // seq
    latent = ctx_kv is not None
    q, k, v, u, bg = _inproj(x, mod3, p["norm1_g"], p["w_in"], seq=seq, mod_row0=mod_row0,
                             kv_dtype=bf16 if latent else f32,
                             rope_tabs=_rope_tables(seq) if latent else None)
    if latent:
        attn = _latent_attention(q, k, v, ctx_kv[0], ctx_kv[1], p["sink"], seq=seq)
    else:
        attn = _context_attention(q, k, v, p["sink"], seq=seq)
    x1, h2_tiles, route = _post(x, attn, u, bg, mod3, p, seq=seq, mod_row0=mod_row0)
    y = _moe(h2_tiles, route, p["w_gate"], p["w_up"], p["w_down"])
    out = _final(x1, y, mod3, p["final_g"], seq=seq, mod_row0=mod_row0)
    return out.reshape(nb, seq, D_MODEL), k, v


def kernel(x_prompt, x_sample, cache_k, cache_v, c, c_ctx, w_mod, b_mod, norm1_g, w_in, conv_w,
           conv_b, sink, attn_out_g, conv_out_g, w_o, norm2_g, w_coarse, b_coarse, w_fine, b_fine,
           w_gate, w_up, w_down, final_g):
    batch, seq, _ = x_prompt.shape
    dec_batch, dec_seq, _ = x_sample.shape
    past = cache_k.shape[2]
    assert w_mod.shape[0] == 1 and 1 + dec_batch <= MOD_ROWS

    cvecs = jnp.concatenate([c_ctx[None], c, jnp.zeros((MOD_ROWS - 1 - dec_batch, D_MODEL), f32)])
    mod3 = _modulation(cvecs, w_mod[0], b_mod[0]).reshape(MOD_ROWS, 1, 6 * D_MODEL)

    pad = jnp.zeros((D_MODEL, LANES - N_EXPERTS - N_GROUPS), f32)
    p = {
        "norm1_g": norm1_g, "w_in": w_in[0].astype(bf16), "conv_w": conv_w[0], "conv_b": conv_b,
        "sink": sink[0], "attn_out_g": attn_out_g, "conv_out_g": conv_out_g,
        "w_o": w_o[0].astype(bf16), "norm2_g": norm2_g,
        "w_router": jnp.concatenate([w_fine[0], w_coarse[0], pad], axis=1),
        "b_router": jnp.concatenate([b_fine[0], b_coarse[0], pad[0]])[None],
        "w_gate": w_gate[0].astype(bf16), "w_up": w_up[0].astype(bf16),
        "w_down": w_down[0].astype(bf16), "final_g": final_g[None],
    }

    y_prompt, k_p, v_p = _trunk(x_prompt.reshape(batch * seq, D_MODEL), mod3, p, seq=seq,
                                mod_row0=0, ctx_kv=None)
    new_k = k_p.reshape(batch, 1, seq, N_KV_HEADS, HEAD_DIM)
    new_v = v_p.reshape(batch, 1, seq, N_KV_HEADS, HEAD_DIM)

    ctx_kv = (cache_k[:, 0].reshape(dec_batch, past, KV_WIDTH),
              cache_v[:, 0].reshape(dec_batch, past, KV_WIDTH))
    y_sample, _, _ = _trunk(x_sample.reshape(dec_batch * dec_seq, D_MODEL), mod3, p, seq=dec_seq,
                            mod_row0=1, ctx_kv=ctx_kv)
    return y_prompt, y_sample, new_k, new_v
```

```python
import functools

import jax
import jax.numpy as jnp
import numpy as np
from jax import lax
from jax.experimental import pallas as pl
from jax.experimental.pallas import tpu as pltpu

D_MODEL = 1024
HEAD_DIM = 64
ATTN_WIDTH = 512
N_HEADS = 8
N_KV_HEADS = 2
Q_PER_KV = 4
KV_WIDTH = 128
CONV_WIDTH = 512
CONV_K = 3
WINDOW = 128
GRID_W = 64
ROPE_BASE = 10000.0
N_FREQ = 16
N_GROUPS = 4
EXPERTS_PER_GROUP = 8
N_EXPERTS = 32
TOP_K = 2
D_EXPERT = 256
IN_WIDTH = ATTN_WIDTH + 2 * KV_WIDTH + 3 * CONV_WIDTH
EPS = 1e-6
NEG = -1e30
SCALE = HEAD_DIM ** -0.5

LANES = 128
F32_SUBLANES = 8
BF16_SUBLANES = 16
assert D_MODEL == F32_SUBLANES * LANES

MOE_BLOCK = 2048
MOE_CHUNK = 128
CHUNK_PITCH = MOE_CHUNK + F32_SUBLANES
EXPERTS_PER_STEP = 4
MOE_VMEM_LIMIT_BYTES = 60 * 1024 * 1024
VMEM_LIMIT_BYTES = 48 * 1024 * 1024

SH1, SC1, G1, SH2, SC2, G2 = (i * D_MODEL for i in range(6))
MOD_ROWS = 8

COARSE_LANE0 = N_EXPERTS

f32 = jnp.float32
bf16 = jnp.bfloat16


def _params(*semantics, vmem=VMEM_LIMIT_BYTES):
    return pltpu.CompilerParams(dimension_semantics=semantics, vmem_limit_bytes=vmem)


def _rms(x):
    return x * lax.rsqrt(jnp.mean(x * x, axis=-1, keepdims=True) + EPS)


def _dot(a, b):
    return jnp.dot(a, b, preferred_element_type=f32)


def _dot_nt(a, b):
    return lax.dot_general(a, b, (((1,), (1,)), ((), ())), preferred_element_type=f32)


def _mod_kernel(cv_ref, w_ref, b_ref, o_ref):
    a = cv_ref[...]
    a = a * jax.nn.sigmoid(a)
    o_ref[...] = jnp.dot(a, w_ref[...], precision=lax.Precision.HIGHEST,
                         preferred_element_type=f32) + b_ref[...]


def _modulation(cvecs, w_mod, b_mod):
    tn = D_MODEL
    return pl.pallas_call(
        _mod_kernel,
        grid=(6 * D_MODEL // tn,),
        in_specs=[pl.BlockSpec((MOD_ROWS, D_MODEL), lambda j: (0, 0)),
                  pl.BlockSpec((D_MODEL, tn), lambda j: (0, j)),
                  pl.BlockSpec((1, tn), lambda j: (0, j))],
        out_specs=pl.BlockSpec((MOD_ROWS, tn), lambda j: (0, j)),
        out_shape=jax.ShapeDtypeStruct((MOD_ROWS, 6 * D_MODEL), f32),
        compiler_params=_params("arbitrary"),
        name="mod",
    )(cvecs, w_mod, b_mod.reshape(1, -1))


def _swap_halves(x):
    lane = lax.broadcasted_iota(jnp.int32, x.shape, 1)
    up = pltpu.roll(x, LANES - N_FREQ, axis=1)
    dn = pltpu.roll(x, N_FREQ, axis=1)
    return jnp.where((lane % (2 * N_FREQ)) < N_FREQ, up, dn)


def _rope(x, cos, sin):
    parts = []
    for c in range(x.shape[1] // LANES):
        xc = x[:, c * LANES:(c + 1) * LANES]
        parts.append(xc * cos + _swap_halves(xc) * sin)
    return parts[0] if len(parts) == 1 else jnp.concatenate(parts, axis=1)


def _inproj_kernel(*refs, rope):
    if rope:
        x_ref, mod_ref, g_ref, w_ref, cos_ref, sin_ref, q_ref, k_ref, v_ref, u_ref, bg_ref = refs
    else:
        x_ref, mod_ref, g_ref, w_ref, q_ref, k_ref, v_ref, u_ref, bg_ref = refs
    h = _rms(x_ref[...]) * g_ref[...]
    h = h * (1.0 + mod_ref[:, SC1:SC1 + D_MODEL]) + mod_ref[:, SH1:SH1 + D_MODEL]
    z = _dot(h.astype(bf16), w_ref[...])
    o = 0
    q = z[:, o:o + ATTN_WIDTH]; o += ATTN_WIDTH
    k = z[:, o:o + KV_WIDTH]; o += KV_WIDTH
    v = z[:, o:o + KV_WIDTH]; o += KV_WIDTH
    bg = z[:, o:o + CONV_WIDTH]; o += CONV_WIDTH
    cg = z[:, o:o + CONV_WIDTH]; o += CONV_WIDTH
    xin = z[:, o:o + CONV_WIDTH]
    if rope:
        q = _rope(q, cos_ref[...], sin_ref[...])
        k = _rope(k, cos_ref[...], sin_ref[...])
    q_ref[...] = q.astype(q_ref.dtype)
    k_ref[...] = k.astype(k_ref.dtype)
    v_ref[...] = v.astype(v_ref.dtype)
    u_ref[...] = (cg * xin).astype(u_ref.dtype)
    bg_ref[...] = bg.astype(bg_ref.dtype)


def _inproj(x, mod3, norm_g, w_in, *, seq, mod_row0, kv_dtype, rope_tabs=None, tm=512):
    t = x.shape[0]
    tiles_per_seq = seq // tm
    row = lambda i: (mod_row0 + (i // tiles_per_seq if mod_row0 else 0), 0, 0)
    in_specs = [pl.BlockSpec((tm, D_MODEL), lambda i: (i, 0)),
                pl.BlockSpec((None, 1, 6 * D_MODEL), row),
                pl.BlockSpec((1, D_MODEL), lambda i: (0, 0)),
                pl.BlockSpec((D_MODEL, IN_WIDTH), lambda i: (0, 0))]
    args = [x, mod3, norm_g, w_in]
    if rope_tabs is not None:
        in_specs += [pl.BlockSpec((tm, LANES), lambda i: (i % tiles_per_seq, 0))] * 2
        args += list(rope_tabs)
    widths = (ATTN_WIDTH, KV_WIDTH, KV_WIDTH, CONV_WIDTH, CONV_WIDTH)
    dtypes = (bf16, kv_dtype, kv_dtype, bf16, bf16)
    return pl.pallas_call(
        functools.partial(_inproj_kernel, rope=rope_tabs is not None),
        grid=(t // tm,),
        in_specs=in_specs,
        out_specs=[pl.BlockSpec((tm, w), lambda i: (i, 0)) for w in widths],
        out_shape=[jax.ShapeDtypeStruct((t, w), d) for w, d in zip(widths, dtypes)],
        compiler_params=_params("arbitrary"),
        name="inproj_rope" if rope_tabs is not None else "inproj",
    )(*args)


def _rope_tables(n):
    t = np.arange(n)
    inv = ROPE_BASE ** (-np.arange(N_FREQ, dtype=np.float32) / N_FREQ)
    rows = (t // GRID_W).astype(np.float32)
    cols = (t % GRID_W).astype(np.float32)
    d = np.arange(LANES) % HEAD_DIM
    pos = np.where(d[None, :] < HEAD_DIM // 2, rows[:, None], cols[:, None])
    ang = jnp.asarray(pos.astype(np.float32) * inv[d % N_FREQ][None, :])
    sign = np.where((d % (2 * N_FREQ)) < N_FREQ, -1.0, 1.0).astype(np.float32)
    return jnp.cos(ang), jnp.sin(ang) * sign[None, :]


def _attend_group(q_rows, sink_col, keys, vals, masks):
    scores = []
    m = sink_col
    for kk, mk in zip(keys, masks):
        s = _dot_nt(q_rows, kk) * SCALE
        if mk is not None:
            s = jnp.where(mk, s, NEG)
        scores.append(s)
        m = jnp.maximum(m, jnp.max(s, axis=-1, keepdims=True))
    den = jnp.exp(sink_col - m)
    acc = None
    for s, vv in zip(scores, vals):
        p = jnp.exp(s - m)
        den = den + jnp.sum(p, axis=-1, keepdims=True)
        pv = _dot(p.astype(bf16), vv)
        acc = pv if acc is None else acc + pv
    return acc / den


def _heads_attention(q, sink_ref, key_sets, val_sets, masks):
    bq = q.shape[0]
    ridx = lax.broadcasted_iota(jnp.int32, (Q_PER_KV * bq, 1), 0)
    outs = []
    for g in range(N_KV_HEADS):
        heads = range(g * Q_PER_KV, (g + 1) * Q_PER_KV)
        q_rows = jnp.concatenate([q[:, h * HEAD_DIM:(h + 1) * HEAD_DIM] for h in heads], axis=0)
        sink_col = jnp.full((Q_PER_KV * bq, 1), sink_ref[g * Q_PER_KV], f32)
        for j in range(1, Q_PER_KV):
            sink_col = jnp.where(ridx >= j * bq, sink_ref[g * Q_PER_KV + j], sink_col)
        lo, hi = g * HEAD_DIM, (g + 1) * HEAD_DIM
        o = _attend_group(q_rows, sink_col, [kk[:, lo:hi] for kk in key_sets],
                          [vv[:, lo:hi] for vv in val_sets], masks)
        outs += [o[j * bq:(j + 1) * bq] for j in range(Q_PER_KV)]
    return jnp.concatenate(outs, axis=1)


def _ctx_attn_kernel(sink_ref, q_ref, k_ref, v_ref, o_ref):
    k = k_ref[...].astype(bf16)
    v = v_ref[...].astype(bf16)
    o_ref[...] = _heads_attention(q_ref[...], sink_ref, [k], [v], [None]).astype(o_ref.dtype)


def _context_attention(q, k, v, sink, *, seq):
    t = q.shape[0]
    return pl.pallas_call(
        _ctx_attn_kernel,
        grid=(t // seq,),
        in_specs=[pl.BlockSpec(memory_space=pltpu.SMEM),
                  pl.BlockSpec((seq, ATTN_WIDTH), lambda b: (b, 0)),
                  pl.BlockSpec((seq, KV_WIDTH), lambda b: (b, 0)),
                  pl.BlockSpec((seq, KV_WIDTH), lambda b: (b, 0))],
        out_specs=pl.BlockSpec((seq, ATTN_WIDTH), lambda b: (b, 0)),
        out_shape=jax.ShapeDtypeStruct((t, ATTN_WIDTH), bf16),
        compiler_params=_params("arbitrary"),
        name="ctx_attn",
    )(sink, q, k, v)


def _lat_attn_kernel(sink_ref, q_ref, k_ref, v_ref, ck_ref, cv_ref, o_ref, *, bq, seq):
    i = pl.program_id(1)
    band = bq + 2 * WINDOW
    start = pl.multiple_of(jnp.clip(i * bq - WINDOW, 0, seq - band), LANES)
    kb = k_ref[pl.ds(start, band), :]
    vb = v_ref[pl.ds(start, band), :]
    shape = (Q_PER_KV * bq, band)
    qpos = i * bq + (lax.broadcasted_iota(jnp.int32, shape, 0) & (bq - 1))
    kpos = start + lax.broadcasted_iota(jnp.int32, shape, 1)
    mask = jnp.abs(qpos - kpos) <= WINDOW
    ck = ck_ref[...].astype(bf16)
    cv = cv_ref[...].astype(bf16)
    o_ref[...] = _heads_attention(q_ref[...], sink_ref, [kb, ck], [vb, cv],
                                  [mask, None]).astype(o_ref.dtype)


def _latent_attention(q, k, v, ck, cv, sink, *, seq, bq=128):
    nb, past = ck.shape[0], ck.shape[1]
    q3, k3, v3 = (a.reshape(nb, seq, a.shape[-1]) for a in (q, k, v))
    out = pl.pallas_call(
        functools.partial(_lat_attn_kernel, bq=bq, seq=seq),
        grid=(nb, seq // bq),
        in_specs=[pl.BlockSpec(memory_space=pltpu.SMEM),
                  pl.BlockSpec((None, bq, ATTN_WIDTH), lambda b, i: (b, i, 0)),
                  pl.BlockSpec((None, seq, KV_WIDTH), lambda b, i: (b, 0, 0)),
                  pl.BlockSpec((None, seq, KV_WIDTH), lambda b, i: (b, 0, 0)),
                  pl.BlockSpec((None, past, KV_WIDTH), lambda b, i: (b, 0, 0)),
                  pl.BlockSpec((None, past, KV_WIDTH), lambda b, i: (b, 0, 0))],
        out_specs=pl.BlockSpec((None, bq, ATTN_WIDTH), lambda b, i: (b, i, 0)),
        out_shape=jax.ShapeDtypeStruct((nb, seq, ATTN_WIDTH), bf16),
        compiler_params=_params("arbitrary", "arbitrary"),
        name="lat_attn",
    )(sink, q3, k3, v3, ck, cv)
    return out.reshape(nb * seq, ATTN_WIDTH)


def _route(logits):
    lane = lax.broadcasted_iota(jnp.int32, logits.shape, 1)
    lane_f = lane.astype(f32)
    big = jnp.float32(LANES)
    is_c = jnp.logical_and(lane >= COARSE_LANE0, lane < COARSE_LANE0 + N_GROUPS)
    lc = jnp.where(is_c, logits, -jnp.inf)
    mc = jnp.max(lc, axis=-1, keepdims=True)
    grp = jnp.min(jnp.where(lc == mc, lane_f, big), axis=-1, keepdims=True) - COARSE_LANE0
    pg = 1.0 / jnp.sum(jnp.exp(lc - mc), axis=-1, keepdims=True)
    in_g = jnp.floor(lane_f * (1.0 / EXPERTS_PER_GROUP)) == grp
    fl = jnp.where(in_g, logits, -jnp.inf)
    m1 = jnp.max(fl, axis=-1, keepdims=True)
    i1 = jnp.min(jnp.where(fl == m1, lane_f, big), axis=-1, keepdims=True)
    fl2 = jnp.where(lane_f == i1, -jnp.inf, fl)
    m2 = jnp.max(fl2, axis=-1, keepdims=True)
    i2 = jnp.min(jnp.where(fl2 == m2, lane_f, big), axis=-1, keepdims=True)
    e2 = jnp.exp(m2 - m1)
    p1 = pg / (1.0 + e2)
    p2 = pg * e2 / (1.0 + e2)
    packed = jnp.where(lane == 0, i1, jnp.where(lane == 1, i2, jnp.where(lane == 2, p1, p2)))
    return jnp.where(lane < 4, packed, 0.0)


def _post_kernel(x_ref, attn_ref, u_ref, up_ref, un_ref, bg_ref, mod_ref, ag_ref, cg_ref, cw_ref,
                 cb_ref, wo_ref, n2_ref, wr_ref, br_ref, x1_ref, h2_ref, route_ref, *, tm, seq):
    i = pl.program_id(0)
    u = u_ref[...].astype(f32)
    rows = lax.broadcasted_iota(jnp.int32, (tm, 1), 0)
    spos = (i * tm + rows) % seq
    u_dn = jnp.where(rows == 0, up_ref[...].astype(f32)[BF16_SUBLANES - 1:, :],
                     pltpu.roll(u, 1, axis=0))
    u_dn = jnp.where(spos == 0, 0.0, u_dn)
    u_up = jnp.where(rows == tm - 1, un_ref[...].astype(f32)[0:1, :], pltpu.roll(u, tm - 1, axis=0))
    u_up = jnp.where(spos == seq - 1, 0.0, u_up)
    y = u_dn * cw_ref[0:1, :] + u * cw_ref[1:2, :] + u_up * cw_ref[2:3, :] + cb_ref[...]
    conv = bg_ref[...].astype(f32) * y
    attn_n = _rms(attn_ref[...].astype(f32)) * ag_ref[...]
    conv_n = _rms(conv) * cg_ref[...]
    mixed = (_dot(attn_n.astype(bf16), wo_ref[0:ATTN_WIDTH, :])
             + _dot(conv_n.astype(bf16), wo_ref[ATTN_WIDTH:, :]))
    x1 = x_ref[...] + mod_ref[:, G1:G1 + D_MODEL] * mixed
    x1_ref[...] = x1
    h2 = _rms(x1) * n2_ref[...]
    h2 = h2 * (1.0 + mod_ref[:, SC2:SC2 + D_MODEL]) + mod_ref[:, SH2:SH2 + D_MODEL]
    for c in range(D_MODEL // LANES):
        h2_ref[pl.ds(c, tm, stride=F32_SUBLANES), :] = h2[:, c * LANES:(c + 1) * LANES]
    logits = jnp.dot(h2, wr_ref[...], precision=lax.Precision.HIGHEST,
                     preferred_element_type=f32) + br_ref[...]
    route_ref[...] = _route(logits)


def _post(x, attn, u, bg, mod3, p, *, seq, mod_row0, tm=256):
    t = x.shape[0]
    tiles_per_seq = seq // tm
    halo = BF16_SUBLANES
    n_halo = t // halo
    row = lambda i: (mod_row0 + (i // tiles_per_seq if mod_row0 else 0), 0, 0)
    tile = lambda w: pl.BlockSpec((tm, w), lambda i: (i, 0))
    full = lambda a: pl.BlockSpec(a.shape, lambda i: (0,) * a.ndim)
    small = [p["attn_out_g"], p["conv_out_g"], p["conv_w"], p["conv_b"], p["w_o"], p["norm2_g"],
             p["w_router"], p["b_router"]]
    return pl.pallas_call(
        functools.partial(_post_kernel, tm=tm, seq=seq),
        grid=(t // tm,),
        in_specs=[tile(D_MODEL), tile(ATTN_WIDTH), tile(CONV_WIDTH),
                  pl.BlockSpec((halo, CONV_WIDTH),
                               lambda i: (jnp.maximum(i * (tm // halo) - 1, 0), 0)),
                  pl.BlockSpec((halo, CONV_WIDTH),
                               lambda i: (jnp.minimum((i + 1) * (tm // halo), n_halo - 1), 0)),
                  tile(CONV_WIDTH),
                  pl.BlockSpec((None, 1, 6 * D_MODEL), row)] + [full(a) for a in small],
        out_specs=[tile(D_MODEL), pl.BlockSpec((tm * F32_SUBLANES, LANES), lambda i: (i, 0)),
                   tile(LANES)],
        out_shape=[jax.ShapeDtypeStruct((t, D_MODEL), f32),
                   jax.ShapeDtypeStruct((t * F32_SUBLANES, LANES), f32),
                   jax.ShapeDtypeStruct((t, LANES), f32)],
        compiler_params=_params("arbitrary"),
        name="post",
    )(x, attn, u, u, u, bg, mod3, *small)


def _max_chunks(blk):
    n = TOP_K * blk // MOE_CHUNK + EXPERTS_PER_STEP
    return n + n % 2


def _dispatch_tables(route, blk):
    nblk = route.shape[0] // blk
    n_slots = TOP_K * blk
    flat = lambda a: a.reshape(nblk, blk, TOP_K).transpose(0, 2, 1).reshape(nblk, n_slots)
    experts = flat(route[:, 0:TOP_K].astype(jnp.int32))
    weights = flat(route[:, TOP_K:2 * TOP_K])
    order = jnp.argsort(experts, axis=1, stable=True).astype(jnp.int32)
    w_sorted = jnp.take_along_axis(weights, order, axis=1)
    bounds = jnp.arange(N_EXPERTS + 1, dtype=jnp.int32)
    offs = jnp.sum(experts[:, :, None] < bounds[None, None, :], axis=1, dtype=jnp.int32)
    tail = ((0, 0), (0, MOE_CHUNK))
    src_rows = jnp.pad((order & (blk - 1)) * F32_SUBLANES, tail)
    dst_rows = jnp.pad(order * F32_SUBLANES, tail, constant_values=n_slots * F32_SUBLANES)
    w_sorted = jnp.pad(w_sorted, tail)

    eg, n_steps, max_c = EXPERTS_PER_STEP, N_EXPERTS // EXPERTS_PER_STEP, _max_chunks(blk)
    lo = offs[:, :N_EXPERTS].reshape(nblk, n_steps, eg)
    n_ch = (offs[:, 1:] - offs[:, :N_EXPERTS] + MOE_CHUNK - 1) // MOE_CHUNK
    n_ch = n_ch.reshape(nblk, n_steps, eg)
    cum = jnp.cumsum(n_ch, axis=2)
    total = cum[..., -1]
    pos = jnp.arange(max_c, dtype=jnp.int32)
    j = jnp.sum(cum[:, :, None, :] <= pos[None, None, :, None], axis=-1, dtype=jnp.int32)
    j = jnp.minimum(j, eg - 1)
    first = jnp.take_along_axis(cum - n_ch, j, axis=2)
    base = jnp.take_along_axis(lo, j, axis=2) + (pos - first) * MOE_CHUNK
    valid = pos < total[..., None]
    base = jnp.where(valid, base, n_slots).reshape(nblk, 1, n_steps * max_c)
    j = jnp.where(valid, j, 0).reshape(nblk, 1, n_steps * max_c)
    n_pairs = ((total + 1) // 2)[:, None, :]
    return src_rows[:, None, :], dst_rows[:, None, :], w_sorted[:, None, :], j, base, n_pairs


def _moe_kernel(src_ref, dst_ref, wts_ref, cj_ref, cbase_g_ref, cbase_s_ref, npairs_ref,
                h_ref, wg_ref, wu_ref, wd_ref, y_ref, ytmp_ref, *stage_refs, blk):
    n_col = D_MODEL // LANES
    step = pl.program_id(1)
    max_c = _max_chunks(blk)

    def gather(base, xg_ref):
        for r in range(MOE_CHUNK):
            src = pl.multiple_of(src_ref[0, base + r], F32_SUBLANES)
            xg_ref[pl.ds(r, F32_SUBLANES, stride=CHUNK_PITCH), :] = h_ref[pl.ds(src, F32_SUBLANES), :]

    def experts_mlp(j, xg_ref, og_ref):
        x = jnp.concatenate([xg_ref[c * CHUNK_PITCH:c * CHUNK_PITCH + MOE_CHUNK, :]
                             for c in range(n_col)], axis=1).astype(bf16)
        g = _dot(x, wg_ref[j])
        up = _dot(x, wu_ref[j])
        act = (g * jax.nn.sigmoid(g)) * up
        out = _dot(act.astype(bf16), wd_ref[j])
        for c in range(n_col):
            og_ref[c * CHUNK_PITCH:c * CHUNK_PITCH + MOE_CHUNK, :] = out[:, c * LANES:(c + 1) * LANES]

    def scatter(base, og_ref):
        for r in range(MOE_CHUNK):
            dst = pl.multiple_of(dst_ref[0, base + r], F32_SUBLANES)
            ytmp_ref[pl.ds(dst, F32_SUBLANES), :] = (
                og_ref[pl.ds(r, F32_SUBLANES, stride=CHUNK_PITCH), :] * wts_ref[0, base + r])

    def pair(pi, carry):
        chunks = [(step * max_c + 2 * pi + half, stage_refs[2 * half], stage_refs[2 * half + 1])
                  for half in range(2)]
        for p, xg_ref, _ in chunks:
            gather(cbase_g_ref[0, p], xg_ref)
        for p, xg_ref, og_ref in chunks:
            experts_mlp(cj_ref[0, p], xg_ref, og_ref)
        for p, _, og_ref in chunks:
            scatter(cbase_s_ref[0, p], og_ref)
        return carry

    lax.fori_loop(0, npairs_ref[0, step], pair, 0)

    @pl.when(pl.program_id(1) == pl.num_programs(1) - 1)
    def _():
        for c in range(n_col):
            acc = ytmp_ref[pl.ds(c, blk, stride=F32_SUBLANES), :]
            for k in range(1, TOP_K):
                acc = acc + ytmp_ref[pl.ds(k * blk * F32_SUBLANES + c, blk, stride=F32_SUBLANES), :]
            y_ref[:, c * LANES:(c + 1) * LANES] = acc.astype(y_ref.dtype)


def _final_kernel(x1_ref, y_ref, mod_ref, fg_ref, o_ref):
    x2 = x1_ref[...] + mod_ref[:, G2:G2 + D_MODEL] * y_ref[...].astype(f32)
    o_ref[...] = _rms(x2) * fg_ref[...]


def _moe(h2_tiles, route, w_gate, w_up, w_down, *, blk=MOE_BLOCK):
    t = route.shape[0]
    src_rows, dst_rows, w_sorted, cj, cbase, n_pairs = _dispatch_tables(route, blk)
    n_slots = TOP_K * blk
    eg, n_steps = EXPERTS_PER_STEP, N_EXPERTS // EXPERTS_PER_STEP
    smem = lambda n: pl.BlockSpec((None, 1, n), lambda b, g: (b, 0, 0), memory_space=pltpu.SMEM)
    stage = pltpu.VMEM((D_MODEL // LANES * CHUNK_PITCH, LANES), f32)
    return pl.pallas_call(
        functools.partial(_moe_kernel, blk=blk),
        grid=(t // blk, n_steps),
        in_specs=[smem(n_slots + MOE_CHUNK)] * 3 + [smem(n_steps * _max_chunks(blk))] * 3 + [
                  smem(n_steps),
                  pl.BlockSpec((blk * F32_SUBLANES, LANES), lambda b, g: (b, 0)),
                  pl.BlockSpec((eg, D_MODEL, D_EXPERT), lambda b, g: (g, 0, 0)),
                  pl.BlockSpec((eg, D_MODEL, D_EXPERT), lambda b, g: (g, 0, 0)),
                  pl.BlockSpec((eg, D_EXPERT, D_MODEL), lambda b, g: (g, 0, 0))],
        out_specs=pl.BlockSpec((blk, D_MODEL), lambda b, g: (b, 0)),
        out_shape=jax.ShapeDtypeStruct((t, D_MODEL), bf16),
        scratch_shapes=[pltpu.VMEM(((n_slots + 1) * F32_SUBLANES, LANES), f32)] + [stage] * 4,
        compiler_params=_params("arbitrary", "arbitrary", vmem=MOE_VMEM_LIMIT_BYTES),
        name="moe",
    )(src_rows, dst_rows, w_sorted, cj, cbase, cbase, n_pairs, h2_tiles, w_gate, w_up, w_down)


def _final(x1, y, mod3, final_g, *, seq, mod_row0, tm=512):
    t = x1.shape[0]
    tiles_per_seq = max(seq // tm, 1)
    row = lambda i: (mod_row0 + (i // tiles_per_seq if mod_row0 else 0), 0, 0)
    tile = pl.BlockSpec((tm, D_MODEL), lambda i: (i, 0))
    return pl.pallas_call(
        _final_kernel,
        grid=(t // tm,),
        in_specs=[tile, tile, pl.BlockSpec((None, 1, 6 * D_MODEL), row),
                  pl.BlockSpec((1, D_MODEL), lambda i: (0, 0))],
        out_specs=tile,
        out_shape=jax.ShapeDtypeStruct((t, D_MODEL), f32),
        compiler_params=_params("arbitrary"),
        name="final",
    )(x1, y, mod3, final_g)


def _trunk(x, mod3, p, *, seq, mod_row0, ctx_kv):
    nb = x.shape[0] // seq
    latent = ctx_kv is not None
    q, k, v, u, bg = _inproj(x, mod3, p["norm1_g"], p["w_in"], seq=seq, mod_row0=mod_row0,
                             kv_dtype=bf16 if latent else f32,
                             rope_tabs=_rope_tables(seq) if latent else None)
    if latent:
        attn = _latent_attention(q, k, v, ctx_kv[0], ctx_kv[1], p["sink"], seq=seq)
    else:
        attn = _context_attention(q, k, v, p["sink"], seq=seq)
    x1, h2_tiles, route = _post(x, attn, u, bg, mod3, p, seq=seq, mod_row0=mod_row0)
    y = _moe(h2_tiles, route, p["w_gate"], p["w_up"], p["w_down"])
    out = _final(x1, y, mod3, p["final_g"], seq=seq, mod_row0=mod_row0)
    return out.reshape(nb, seq, D_MODEL), k, v


def kernel(x_prompt, x_sample, cache_k, cache_v, c, c_ctx, w_mod, b_mod, norm1_g, w_in, conv_w,
           conv_b, sink, attn_out_g, conv_out_g, w_o, norm2_g, w_coarse, b_coarse, w_fine, b_fine,
           w_gate, w_up, w_down, final_g):
    batch, seq, _ = x_prompt.shape
    dec_batch, dec_seq, _ = x_sample.shape
    past = cache_k.shape[2]
    assert w_mod.shape[0] == 1 and 1 + dec_batch <= MOD_ROWS

    cvecs = jnp.concatenate([c_ctx[None], c, jnp.zeros((MOD_ROWS - 1 - dec_batch, D_MODEL), f32)])
    mod3 = _modulation(cvecs, w_mod[0], b_mod[0]).reshape(MOD_ROWS, 1, 6 * D_MODEL)

    pad = jnp.zeros((D_MODEL, LANES - N_EXPERTS - N_GROUPS), f32)
    p = {
        "norm1_g": norm1_g, "w_in": w_in[0].astype(bf16), "conv_w": conv_w[0], "conv_b": conv_b,
        "sink": sink[0], "attn_out_g": attn_out_g, "conv_out_g": conv_out_g,
        "w_o": w_o[0].astype(bf16), "norm2_g": norm2_g,
        "w_router": jnp.concatenate([w_fine[0], w_coarse[0], pad], axis=1),
        "b_router": jnp.concatenate([b_fine[0], b_coarse[0], pad[0]])[None],
        "w_gate": w_gate[0].astype(bf16), "w_up": w_up[0].astype(bf16),
        "w_down": w_down[0].astype(bf16), "final_g": final_g[None],
    }

    y_prompt, k_p, v_p = _trunk(x_prompt.reshape(batch * seq, D_MODEL), mod3, p, seq=seq,
                                mod_row0=0, ctx_kv=None)
    new_k = k_p.reshape(batch, 1, seq, N_KV_HEADS, HEAD_DIM)
    new_v = v_p.reshape(batch, 1, seq, N_KV_HEADS, HEAD_DIM)

    ctx_kv = (cache_k[:, 0].reshape(dec_batch, past, KV_WIDTH),
              cache_v[:, 0].reshape(dec_batch, past, KV_WIDTH))
    y_sample, _, _ = _trunk(x_sample.reshape(dec_batch * dec_seq, D_MODEL), mod3, p, seq=dec_seq,
                            mod_row0=1, ctx_kv=ctx_kv)
    return y_prompt, y_sample, new_k, new_v
```

```python
import functools

import jax
import jax.numpy as jnp
import numpy as np
from jax import lax
from jax.experimental import pallas as pl
from jax.experimental.pallas import tpu as pltpu

D_MODEL = 1024
HEAD_DIM = 64
ATTN_WIDTH = 512
N_HEADS = 8
N_KV_HEADS = 2
Q_PER_KV = 4
KV_WIDTH = 128
CONV_WIDTH = 512
CONV_K = 3
WINDOW = 128
GRID_W = 64
ROPE_BASE = 10000.0
N_FREQ = 16
N_GROUPS = 4
EXPERTS_PER_GROUP = 8
N_EXPERTS = 32
TOP_K = 2
D_EXPERT = 256
IN_WIDTH = ATTN_WIDTH + 2 * KV_WIDTH + 3 * CONV_WIDTH
EPS = 1e-6
NEG = -1e30
SCALE = HEAD_DIM ** -0.5

LANES = 128
F32_SUBLANES = 8
BF16_SUBLANES = 16
assert D_MODEL == F32_SUBLANES * LANES

MOE_BLOCK = 2048
MOE_CHUNK = 128
CHUNK_PITCH = MOE_CHUNK + F32_SUBLANES
EXPERTS_PER_STEP = 4
MOE_VMEM_LIMIT_BYTES = 60 * 1024 * 1024
VMEM_LIMIT_BYTES = 48 * 1024 * 1024

SH1, SC1, G1, SH2, SC2, G2 = (i * D_MODEL for i in range(6))
MOD_ROWS = 8

COARSE_LANE0 = N_EXPERTS

f32 = jnp.float32
bf16 = jnp.bfloat16


def _params(*semantics, vmem=VMEM_LIMIT_BYTES):
    return pltpu.CompilerParams(dimension_semantics=semantics, vmem_limit_bytes=vmem)


def _rms(x):
    return x * lax.rsqrt(jnp.mean(x * x, axis=-1, keepdims=True) + EPS)


def _dot(a, b):
    return jnp.dot(a, b, preferred_element_type=f32)


def _split_bf16(w):
    hi = w.astype(bf16)
    lo = (w - hi.astype(f32)).astype(bf16)
    return jnp.concatenate([hi, lo], axis=1)


def _dot_nt(a, b):
    return lax.dot_general(a, b, (((1,), (1,)), ((), ())), preferred_element_type=f32)


def _mod_kernel(cv_ref, w_ref, b_ref, o_ref):
    a = cv_ref[...]
    a = a * jax.nn.sigmoid(a)
    o_ref[...] = jnp.dot(a, w_ref[...], precision=lax.Precision.HIGHEST,
                         preferred_element_type=f32) + b_ref[...]


def _modulation(cvecs, w_mod, b_mod):
    tn = D_MODEL
    return pl.pallas_call(
        _mod_kernel,
        grid=(6 * D_MODEL // tn,),
        in_specs=[pl.BlockSpec((MOD_ROWS, D_MODEL), lambda j: (0, 0)),
                  pl.BlockSpec((D_MODEL, tn), lambda j: (0, j)),
                  pl.BlockSpec((1, tn), lambda j: (0, j))],
        out_specs=pl.BlockSpec((MOD_ROWS, tn), lambda j: (0, j)),
        out_shape=jax.ShapeDtypeStruct((MOD_ROWS, 6 * D_MODEL), f32),
        compiler_params=_params("arbitrary"),
        name="mod",
    )(cvecs, w_mod, b_mod.reshape(1, -1))


def _swap_halves(x):
    lane = lax.broadcasted_iota(jnp.int32, x.shape, 1)
    up = pltpu.roll(x, LANES - N_FREQ, axis=1)
    dn = pltpu.roll(x, N_FREQ, axis=1)
    return jnp.where((lane % (2 * N_FREQ)) < N_FREQ, up, dn)


def _rope(x, cos, sin):
    parts = []
    for c in range(x.shape[1] // LANES):
        xc = x[:, c * LANES:(c + 1) * LANES]
        parts.append(xc * cos + _swap_halves(xc) * sin)
    return parts[0] if len(parts) == 1 else jnp.concatenate(parts, axis=1)


def _inproj_kernel(*refs, rope):
    if rope:
        x_ref, mod_ref, g_ref, w_ref, cos_ref, sin_ref, q_ref, k_ref, v_ref, u_ref, bg_ref = refs
    else:
        x_ref, mod_ref, g_ref, w_ref, q_ref, k_ref, v_ref, u_ref, bg_ref = refs
    h = _rms(x_ref[...]) * g_ref[...]
    h = h * (1.0 + mod_ref[:, SC1:SC1 + D_MODEL]) + mod_ref[:, SH1:SH1 + D_MODEL]
    z = _dot(h.astype(bf16), w_ref[...])
    o = 0
    q = z[:, o:o + ATTN_WIDTH]; o += ATTN_WIDTH
    k = z[:, o:o + KV_WIDTH]; o += KV_WIDTH
    v = z[:, o:o + KV_WIDTH]; o += KV_WIDTH
    bg = z[:, o:o + CONV_WIDTH]; o += CONV_WIDTH
    cg = z[:, o:o + CONV_WIDTH]; o += CONV_WIDTH
    xin = z[:, o:o + CONV_WIDTH]
    if rope:
        q = _rope(q, cos_ref[...], sin_ref[...])
        k = _rope(k, cos_ref[...], sin_ref[...])
    q_ref[...] = q.astype(q_ref.dtype)
    k_ref[...] = k.astype(k_ref.dtype)
    v_ref[...] = v.astype(v_ref.dtype)
    u_ref[...] = (cg * xin).astype(u_ref.dtype)
    bg_ref[...] = bg.astype(bg_ref.dtype)


def _inproj(x, mod3, norm_g, w_in, *, seq, mod_row0, kv_dtype, rope_tabs=None, tm=512):
    t = x.shape[0]
    tiles_per_seq = seq // tm
    row = lambda i: (mod_row0 + (i // tiles_per_seq if mod_row0 else 0), 0, 0)
    in_specs = [pl.BlockSpec((tm, D_MODEL), lambda i: (i, 0)),
                pl.BlockSpec((None, 1, 6 * D_MODEL), row),
                pl.BlockSpec((1, D_MODEL), lambda i: (0, 0)),
                pl.BlockSpec((D_MODEL, IN_WIDTH), lambda i: (0, 0))]
    args = [x, mod3, norm_g, w_in]
    if rope_tabs is not None:
        in_specs += [pl.BlockSpec((tm, LANES), lambda i: (i % tiles_per_seq, 0))] * 2
        args += list(rope_tabs)
    widths = (ATTN_WIDTH, KV_WIDTH, KV_WIDTH, CONV_WIDTH, CONV_WIDTH)
    dtypes = (bf16, kv_dtype, kv_dtype, bf16, bf16)
    return pl.pallas_call(
        functools.partial(_inproj_kernel, rope=rope_tabs is not None),
        grid=(t // tm,),
        in_specs=in_specs,
        out_specs=[pl.BlockSpec((tm, w), lambda i: (i, 0)) for w in widths],
        out_shape=[jax.ShapeDtypeStruct((t, w), d) for w, d in zip(widths, dtypes)],
        compiler_params=_params("arbitrary"),
        name="inproj_rope" if rope_tabs is not None else "inproj",
    )(*args)


def _rope_tables(n):
    t = np.arange(n)
    inv = ROPE_BASE ** (-np.arange(N_FREQ, dtype=np.float32) / N_FREQ)
    rows = (t // GRID_W).astype(np.float32)
    cols = (t % GRID_W).astype(np.float32)
    d = np.arange(LANES) % HEAD_DIM
    pos = np.where(d[None, :] < HEAD_DIM // 2, rows[:, None], cols[:, None])
    ang = jnp.asarray(pos.astype(np.float32) * inv[d % N_FREQ][None, :])
    sign = np.where((d % (2 * N_FREQ)) < N_FREQ, -1.0, 1.0).astype(np.float32)
    return jnp.cos(ang), jnp.sin(ang) * sign[None, :]


def _attend_group(q_rows, sink_col, keys, vals, masks):
    scores = []
    m = sink_col
    for kk, mk in zip(keys, masks):
        s = _dot_nt(q_rows, kk) * SCALE
        if mk is not None:
            s = jnp.where(mk, s, NEG)
        scores.append(s)
        m = jnp.maximum(m, jnp.max(s, axis=-1, keepdims=True))
    den = jnp.exp(sink_col - m)
    acc = None
    for s, vv in zip(scores, vals):
        p = jnp.exp(s - m)
        den = den + jnp.sum(p, axis=-1, keepdims=True)
        pv = _dot(p.astype(bf16), vv)
        acc = pv if acc is None else acc + pv
    return acc / den


def _heads_attention(q, sink_ref, key_sets, val_sets, masks):
    bq = q.shape[0]
    ridx = lax.broadcasted_iota(jnp.int32, (Q_PER_KV * bq, 1), 0)
    outs = []
    for g in range(N_KV_HEADS):
        heads = range(g * Q_PER_KV, (g + 1) * Q_PER_KV)
        q_rows = jnp.concatenate([q[:, h * HEAD_DIM:(h + 1) * HEAD_DIM] for h in heads], axis=0)
        sink_col = jnp.full((Q_PER_KV * bq, 1), sink_ref[g * Q_PER_KV], f32)
        for j in range(1, Q_PER_KV):
            sink_col = jnp.where(ridx >= j * bq, sink_ref[g * Q_PER_KV + j], sink_col)
        lo, hi = g * HEAD_DIM, (g + 1) * HEAD_DIM
        o = _attend_group(q_rows, sink_col, [kk[:, lo:hi] for kk in key_sets],
                          [vv[:, lo:hi] for vv in val_sets], masks)
        outs += [o[j * bq:(j + 1) * bq] for j in range(Q_PER_KV)]
    return jnp.concatenate(outs, axis=1)


def _ctx_attn_kernel(sink_ref, q_ref, k_ref, v_ref, o_ref):
    k = k_ref[...].astype(bf16)
    v = v_ref[...].astype(bf16)
    o_ref[...] = _heads_attention(q_ref[...], sink_ref, [k], [v], [None]).astype(o_ref.dtype)


def _context_attention(q, k, v, sink, *, seq):
    t = q.shape[0]
    return pl.pallas_call(
        _ctx_attn_kernel,
        grid=(t // seq,),
        in_specs=[pl.BlockSpec(memory_space=pltpu.SMEM),
                  pl.BlockSpec((seq, ATTN_WIDTH), lambda b: (b, 0)),
                  pl.BlockSpec((seq, KV_WIDTH), lambda b: (b, 0)),
                  pl.BlockSpec((seq, KV_WIDTH), lambda b: (b, 0))],
        out_specs=pl.BlockSpec((seq, ATTN_WIDTH), lambda b: (b, 0)),
        out_shape=jax.ShapeDtypeStruct((t, ATTN_WIDTH), bf16),
        compiler_params=_params("arbitrary"),
        name="ctx_attn",
    )(sink, q, k, v)


def _lat_attn_kernel(sink_ref, q_ref, k_ref, v_ref, ck_ref, cv_ref, o_ref, *, bq, seq):
    i = pl.program_id(1)
    band = bq + 2 * WINDOW
    start = pl.multiple_of(jnp.clip(i * bq - WINDOW, 0, seq - band), LANES)
    kb = k_ref[pl.ds(start, band), :]
    vb = v_ref[pl.ds(start, band), :]
    shape = (Q_PER_KV * bq, band)
    qpos = i * bq + (lax.broadcasted_iota(jnp.int32, shape, 0) & (bq - 1))
    kpos = start + lax.broadcasted_iota(jnp.int32, shape, 1)
    mask = jnp.abs(qpos - kpos) <= WINDOW
    ck = ck_ref[...].astype(bf16)
    cv = cv_ref[...].astype(bf16)
    o_ref[...] = _heads_attention(q_ref[...], sink_ref, [kb, ck], [vb, cv],
                                  [mask, None]).astype(o_ref.dtype)


def _latent_attention(q, k, v, ck, cv, sink, *, seq, bq=128):
    nb, past = ck.shape[0], ck.shape[1]
    q3, k3, v3 = (a.reshape(nb, seq, a.shape[-1]) for a in (q, k, v))
    out = pl.pallas_call(
        functools.partial(_lat_attn_kernel, bq=bq, seq=seq),
        grid=(nb, seq // bq),
        in_specs=[pl.BlockSpec(memory_space=pltpu.SMEM),
                  pl.BlockSpec((None, bq, ATTN_WIDTH), lambda b, i: (b, i, 0)),
                  pl.BlockSpec((None, seq, KV_WIDTH), lambda b, i: (b, 0, 0)),
                  pl.BlockSpec((None, seq, KV_WIDTH), lambda b, i: (b, 0, 0)),
                  pl.BlockSpec((None, past, KV_WIDTH), lambda b, i: (b, 0, 0)),
                  pl.BlockSpec((None, past, KV_WIDTH), lambda b, i: (b, 0, 0))],
        out_specs=pl.BlockSpec((None, bq, ATTN_WIDTH), lambda b, i: (b, i, 0)),
        out_shape=jax.ShapeDtypeStruct((nb, seq, ATTN_WIDTH), bf16),
        compiler_params=_params("arbitrary", "arbitrary"),
        name="lat_attn",
    )(sink, q3, k3, v3, ck, cv)
    return out.reshape(nb * seq, ATTN_WIDTH)


def _route(logits):
    lane = lax.broadcasted_iota(jnp.int32, logits.shape, 1)
    lane_f = lane.astype(f32)
    big = jnp.float32(LANES)
    is_c = jnp.logical_and(lane >= COARSE_LANE0, lane < COARSE_LANE0 + N_GROUPS)
    lc = jnp.where(is_c, logits, -jnp.inf)
    mc = jnp.max(lc, axis=-1, keepdims=True)
    grp = jnp.min(jnp.where(lc == mc, lane_f, big), axis=-1, keepdims=True) - COARSE_LANE0
    pg = 1.0 / jnp.sum(jnp.exp(lc - mc), axis=-1, keepdims=True)
    in_g = jnp.floor(lane_f * (1.0 / EXPERTS_PER_GROUP)) == grp
    fl = jnp.where(in_g, logits, -jnp.inf)
    m1 = jnp.max(fl, axis=-1, keepdims=True)
    i1 = jnp.min(jnp.where(fl == m1, lane_f, big), axis=-1, keepdims=True)
    fl2 = jnp.where(lane_f == i1, -jnp.inf, fl)
    m2 = jnp.max(fl2, axis=-1, keepdims=True)
    i2 = jnp.min(jnp.where(fl2 == m2, lane_f, big), axis=-1, keepdims=True)
    e2 = jnp.exp(m2 - m1)
    p1 = pg / (1.0 + e2)
    p2 = pg * e2 / (1.0 + e2)
    packed = jnp.where(lane == 0, i1, jnp.where(lane == 1, i2, jnp.where(lane == 2, p1, p2)))
    return jnp.where(lane < 4, packed, 0.0)


def _post_kernel(x_ref, attn_ref, u_ref, up_ref, un_ref, bg_ref, mod_ref, ag_ref, cg_ref, cw_ref,
                 cb_ref, wo_ref, n2_ref, wr_ref, br_ref, x1_ref, h2_ref, route_ref, *, tm, seq):
    i = pl.program_id(0)
    u = u_ref[...].astype(f32)
    rows = lax.broadcasted_iota(jnp.int32, (tm, 1), 0)
    spos = (i * tm + rows) % seq
    u_dn = jnp.where(rows == 0, up_ref[...].astype(f32)[BF16_SUBLANES - 1:, :],
                     pltpu.roll(u, 1, axis=0))
    u_dn = jnp.where(spos == 0, 0.0, u_dn)
    u_up = jnp.where(rows == tm - 1, un_ref[...].astype(f32)[0:1, :], pltpu.roll(u, tm - 1, axis=0))
    u_up = jnp.where(spos == seq - 1, 0.0, u_up)
    y = u_dn * cw_ref[0:1, :] + u * cw_ref[1:2, :] + u_up * cw_ref[2:3, :] + cb_ref[...]
    conv = bg_ref[...].astype(f32) * y
    attn_n = _rms(attn_ref[...].astype(f32)) * ag_ref[...]
    conv_n = _rms(conv) * cg_ref[...]
    mixed = (_dot(attn_n.astype(bf16), wo_ref[0:ATTN_WIDTH, :])
             + _dot(conv_n.astype(bf16), wo_ref[ATTN_WIDTH:, :]))
    x1 = x_ref[...] + mod_ref[:, G1:G1 + D_MODEL] * mixed
    x1_ref[...] = x1
    h2 = _rms(x1) * n2_ref[...]
    h2 = h2 * (1.0 + mod_ref[:, SC2:SC2 + D_MODEL]) + mod_ref[:, SH2:SH2 + D_MODEL]
    for c in range(D_MODEL // LANES):
        h2_ref[pl.ds(c, tm, stride=F32_SUBLANES), :] = h2[:, c * LANES:(c + 1) * LANES]
    h2_hi = h2.astype(bf16)
    h2_lo = (h2 - h2_hi.astype(f32)).astype(bf16)
    both = _dot(h2_hi, wr_ref[...])
    logits = (both[:, :LANES] + both[:, LANES:] + _dot(h2_lo, wr_ref[:, :LANES])) + br_ref[...]
    route_ref[...] = _route(logits)


def _post(x, attn, u, bg, mod3, p, *, seq, mod_row0, tm=256):
    t = x.shape[0]
    tiles_per_seq = seq // tm
    halo = BF16_SUBLANES
    n_halo = t // halo
    row = lambda i: (mod_row0 + (i // tiles_per_seq if mod_row0 else 0), 0, 0)
    tile = lambda w: pl.BlockSpec((tm, w), lambda i: (i, 0))
    full = lambda a: pl.BlockSpec(a.shape, lambda i: (0,) * a.ndim)
    small = [p["attn_out_g"], p["conv_out_g"], p["conv_w"], p["conv_b"], p["w_o"], p["norm2_g"],
             p["w_router"], p["b_router"]]
    return pl.pallas_call(
        functools.partial(_post_kernel, tm=tm, seq=seq),
        grid=(t // tm,),
        in_specs=[tile(D_MODEL), tile(ATTN_WIDTH), tile(CONV_WIDTH),
                  pl.BlockSpec((halo, CONV_WIDTH),
                               lambda i: (jnp.maximum(i * (tm // halo) - 1, 0), 0)),
                  pl.BlockSpec((halo, CONV_WIDTH),
                               lambda i: (jnp.minimum((i + 1) * (tm // halo), n_halo - 1), 0)),
                  tile(CONV_WIDTH),
                  pl.BlockSpec((None, 1, 6 * D_MODEL), row)] + [full(a) for a in small],
        out_specs=[tile(D_MODEL), pl.BlockSpec((tm * F32_SUBLANES, LANES), lambda i: (i, 0)),
                   tile(LANES)],
        out_shape=[jax.ShapeDtypeStruct((t, D_MODEL), f32),
                   jax.ShapeDtypeStruct((t * F32_SUBLANES, LANES), f32),
                   jax.ShapeDtypeStruct((t, LANES), f32)],
        compiler_params=_params("arbitrary"),
        name="post",
    )(x, attn, u, u, u, bg, mod3, *small)


def _max_chunks(blk):
    n = TOP_K * blk // MOE_CHUNK + EXPERTS_PER_STEP
    return n + n % 2


def _dispatch_tables(route, blk):
    nblk = route.shape[0] // blk
    n_slots = TOP_K * blk
    flat = lambda a: a.reshape(nblk, blk, TOP_K).transpose(0, 2, 1).reshape(nblk, n_slots)
    experts = flat(route[:, 0:TOP_K].astype(jnp.int32))
    weights = flat(route[:, TOP_K:2 * TOP_K])
    order = jnp.argsort(experts, axis=1, stable=True).astype(jnp.int32)
    w_sorted = jnp.take_along_axis(weights, order, axis=1)
    bounds = jnp.arange(N_EXPERTS + 1, dtype=jnp.int32)
    offs = jnp.sum(experts[:, :, None] < bounds[None, None, :], axis=1, dtype=jnp.int32)
    tail = ((0, 0), (0, MOE_CHUNK))
    src_rows = jnp.pad((order & (blk - 1)) * F32_SUBLANES, tail)
    dst_rows = jnp.pad(order * F32_SUBLANES, tail, constant_values=n_slots * F32_SUBLANES)
    w_sorted = jnp.pad(w_sorted, tail)

    eg, n_steps, max_c = EXPERTS_PER_STEP, N_EXPERTS // EXPERTS_PER_STEP, _max_chunks(blk)
    lo = offs[:, :N_EXPERTS].reshape(nblk, n_steps, eg)
    n_ch = (offs[:, 1:] - offs[:, :N_EXPERTS] + MOE_CHUNK - 1) // MOE_CHUNK
    n_ch = n_ch.reshape(nblk, n_steps, eg)
    cum = jnp.cumsum(n_ch, axis=2)
    total = cum[..., -1]
    pos = jnp.arange(max_c, dtype=jnp.int32)
    j = jnp.sum(cum[:, :, None, :] <= pos[None, None, :, None], axis=-1, dtype=jnp.int32)
    j = jnp.minimum(j, eg - 1)
    first = jnp.take_along_axis(cum - n_ch, j, axis=2)
    base = jnp.take_along_axis(lo, j, axis=2) + (pos - first) * MOE_CHUNK
    valid = pos < total[..., None]
    base = jnp.where(valid, base, n_slots).reshape(nblk, 1, n_steps * max_c)
    j = jnp.where(valid, j, 0).reshape(nblk, 1, n_steps * max_c)
    n_pairs = ((total + 1) // 2)[:, None, :]
    return src_rows[:, None, :], dst_rows[:, None, :], w_sorted[:, None, :], j, base, n_pairs


def _moe_kernel(src_ref, dst_ref, wts_ref, cj_ref, cbase_g_ref, cbase_s_ref, npairs_ref,
                h_ref, wg_ref, wu_ref, wd_ref, y_ref, ytmp_ref, *stage_refs, blk):
    n_col = D_MODEL // LANES
    step = pl.program_id(1)
    max_c = _max_chunks(blk)

    def gather(base, xg_ref):
        for r in range(MOE_CHUNK):
            src = pl.multiple_of(src_ref[0, base + r], F32_SUBLANES)
            xg_ref[pl.ds(r, F32_SUBLANES, stride=CHUNK_PITCH), :] = h_ref[pl.ds(src, F32_SUBLANES), :]

    def experts_mlp(j, xg_ref, og_ref):
        x = jnp.concatenate([xg_ref[c * CHUNK_PITCH:c * CHUNK_PITCH + MOE_CHUNK, :]
                             for c in range(n_col)], axis=1).astype(bf16)
        g = _dot(x, wg_ref[j])
        up = _dot(x, wu_ref[j])
        act = (g * jax.nn.sigmoid(g)) * up
        out = _dot(act.astype(bf16), wd_ref[j])
        for c in range(n_col):
            og_ref[c * CHUNK_PITCH:c * CHUNK_PITCH + MOE_CHUNK, :] = out[:, c * LANES:(c + 1) * LANES]

    def scatter(base, og_ref):
        for r in range(MOE_CHUNK):
            dst = pl.multiple_of(dst_ref[0, base + r], F32_SUBLANES)
            ytmp_ref[pl.ds(dst, F32_SUBLANES), :] = (
                og_ref[pl.ds(r, F32_SUBLANES, stride=CHUNK_PITCH), :] * wts_ref[0, base + r])

    def pair(pi, carry):
        chunks = [(step * max_c + 2 * pi + half, stage_refs[2 * half], stage_refs[2 * half + 1])
                  for half in range(2)]
        for p, xg_ref, _ in chunks:
            gather(cbase_g_ref[0, p], xg_ref)
        for p, xg_ref, og_ref in chunks:
            experts_mlp(cj_ref[0, p], xg_ref, og_ref)
        for p, _, og_ref in chunks:
            scatter(cbase_s_ref[0, p], og_ref)
        return carry

    lax.fori_loop(0, npairs_ref[0, step], pair, 0)

    @pl.when(pl.program_id(1) == pl.num_programs(1) - 1)
    def _():
        for c in range(n_col):
            acc = ytmp_ref[pl.ds(c, blk, stride=F32_SUBLANES), :]
            for k in range(1, TOP_K):
                acc = acc + ytmp_ref[pl.ds(k * blk * F32_SUBLANES + c, blk, stride=F32_SUBLANES), :]
            y_ref[:, c * LANES:(c + 1) * LANES] = acc.astype(y_ref.dtype)


def _final_kernel(x1_ref, y_ref, mod_ref, fg_ref, o_ref):
    x2 = x1_ref[...] + mod_ref[:, G2:G2 + D_MODEL] * y_ref[...].astype(f32)
    o_ref[...] = _rms(x2) * fg_ref[...]


def _moe(h2_tiles, route, w_gate, w_up, w_down, *, blk=MOE_BLOCK):
    t = route.shape[0]
    src_rows, dst_rows, w_sorted, cj, cbase, n_pairs = _dispatch_tables(route, blk)
    n_slots = TOP_K * blk
    eg, n_steps = EXPERTS_PER_STEP, N_EXPERTS // EXPERTS_PER_STEP
    smem = lambda n: pl.BlockSpec((None, 1, n), lambda b, g: (b, 0, 0), memory_space=pltpu.SMEM)
    stage = pltpu.VMEM((D_MODEL // LANES * CHUNK_PITCH, LANES), f32)
    return pl.pallas_call(
        functools.partial(_moe_kernel, blk=blk),
        grid=(t // blk, n_steps),
        in_specs=[smem(n_slots + MOE_CHUNK)] * 3 + [smem(n_steps * _max_chunks(blk))] * 3 + [
                  smem(n_steps),
                  pl.BlockSpec((blk * F32_SUBLANES, LANES), lambda b, g: (b, 0)),
                  pl.BlockSpec((eg, D_MODEL, D_EXPERT), lambda b, g: (g, 0, 0)),
                  pl.BlockSpec((eg, D_MODEL, D_EXPERT), lambda b, g: (g, 0, 0)),
                  pl.BlockSpec((eg, D_EXPERT, D_MODEL), lambda b, g: (g, 0, 0))],
        out_specs=pl.BlockSpec((blk, D_MODEL), lambda b, g: (b, 0)),
        out_shape=jax.ShapeDtypeStruct((t, D_MODEL), bf16),
        scratch_shapes=[pltpu.VMEM(((n_slots + 1) * F32_SUBLANES, LANES), f32)] + [stage] * 4,
        compiler_params=_params("arbitrary", "arbitrary", vmem=MOE_VMEM_LIMIT_BYTES),
        name="moe",
    )(src_rows, dst_rows, w_sorted, cj, cbase, cbase, n_pairs, h2_tiles, w_gate, w_up, w_down)


def _final(x1, y, mod3, final_g, *, seq, mod_row0, tm=512):
    t = x1.shape[0]
    tiles_per_seq = max(seq // tm, 1)
    row = lambda i: (mod_row0 + (i // tiles_per_seq if mod_row0 else 0), 0, 0)
    tile = pl.BlockSpec((tm, D_MODEL), lambda i: (i, 0))
    return pl.pallas_call(
        _final_kernel,
        grid=(t // tm,),
        in_specs=[tile, tile, pl.BlockSpec((None, 1, 6 * D_MODEL), row),
                  pl.BlockSpec((1, D_MODEL), lambda i: (0, 0))],
        out_specs=tile,
        out_shape=jax.ShapeDtypeStruct((t, D_MODEL), f32),
        compiler_params=_params("arbitrary"),
        name="final",
    )(x1, y, mod3, final_g)


def _trunk(x, mod3, p, *, seq, mod_row0, ctx_kv):
    nb = x.shape[0] // seq
    latent = ctx_kv is not None
    q, k, v, u, bg = _inproj(x, mod3, p["norm1_g"], p["w_in"], seq=seq, mod_row0=mod_row0,
                             kv_dtype=bf16 if latent else f32,
                             rope_tabs=_rope_tables(seq) if latent else None)
    if latent:
        attn = _latent_attention(q, k, v, ctx_kv[0], ctx_kv[1], p["sink"], seq=seq)
    else:
        attn = _context_attention(q, k, v, p["sink"], seq=seq)
    x1, h2_tiles, route = _post(x, attn, u, bg, mod3, p, seq=seq, mod_row0=mod_row0)
    y = _moe(h2_tiles, route, p["w_gate"], p["w_up"], p["w_down"])
    out = _final(x1, y, mod3, p["final_g"], seq=seq, mod_row0=mod_row0)
    return out.reshape(nb, seq, D_MODEL), k, v


def kernel(x_prompt, x_sample, cache_k, cache_v, c, c_ctx, w_mod, b_mod, norm1_g, w_in, conv_w,
           conv_b, sink, attn_out_g, conv_out_g, w_o, norm2_g, w_coarse, b_coarse, w_fine, b_fine,
           w_gate, w_up, w_down, final_g):
    batch, seq, _ = x_prompt.shape
    dec_batch, dec_seq, _ = x_sample.shape
    past = cache_k.shape[2]
    assert w_mod.shape[0] == 1 and 1 + dec_batch <= MOD_ROWS

    cvecs = jnp.concatenate([c_ctx[None], c, jnp.zeros((MOD_ROWS - 1 - dec_batch, D_MODEL), f32)])
    mod3 = _modulation(cvecs, w_mod[0], b_mod[0]).reshape(MOD_ROWS, 1, 6 * D_MODEL)

    pad = jnp.zeros((D_MODEL, LANES - N_EXPERTS - N_GROUPS), f32)
    p = {
        "norm1_g": norm1_g, "w_in": w_in[0].astype(bf16), "conv_w": conv_w[0], "conv_b": conv_b,
        "sink": sink[0], "attn_out_g": attn_out_g, "conv_out_g": conv_out_g,
        "w_o": w_o[0].astype(bf16), "norm2_g": norm2_g,
        "w_router": _split_bf16(jnp.concatenate([w_fine[0], w_coarse[0], pad], axis=1)),
        "b_router": jnp.concatenate([b_fine[0], b_coarse[0], pad[0]])[None],
        "w_gate": w_gate[0].astype(bf16), "w_up": w_up[0].astype(bf16),
        "w_down": w_down[0].astype(bf16), "final_g": final_g[None],
    }

    y_prompt, k_p, v_p = _trunk(x_prompt.reshape(batch * seq, D_MODEL), mod3, p, seq=seq,
                                mod_row0=0, ctx_kv=None)
    new_k = k_p.reshape(batch, 1, seq, N_KV_HEADS, HEAD_DIM)
    new_v = v_p.reshape(batch, 1, seq, N_KV_HEADS, HEAD_DIM)

    ctx_kv = (cache_k[:, 0].reshape(dec_batch, past, KV_WIDTH),
              cache_v[:, 0].reshape(dec_batch, past, KV_WIDTH))
    y_sample, _, _ = _trunk(x_sample.reshape(dec_batch * dec_seq, D_MODEL), mod3, p, seq=dec_seq,
                            mod_row0=1, ctx_kv=ctx_kv)
    return y_prompt, y_sample, new_k, new_v
```

```python
import functools

import jax
import jax.numpy as jnp
import numpy as np
from jax import lax
from jax.experimental import pallas as pl
from jax.experimental.pallas import tpu as pltpu

D_MODEL = 1024
HEAD_DIM = 64
ATTN_WIDTH = 512
N_HEADS = 8
N_KV_HEADS = 2
Q_PER_KV = 4
KV_WIDTH = 128
CONV_WIDTH = 512
CONV_K = 3
WINDOW = 128
GRID_W = 64
ROPE_BASE = 10000.0
N_FREQ = 16
N_GROUPS = 4
EXPERTS_PER_GROUP = 8
N_EXPERTS = 32
TOP_K = 2
D_EXPERT = 256
IN_WIDTH = ATTN_WIDTH + 2 * KV_WIDTH + 3 * CONV_WIDTH
EPS = 1e-6
NEG = -1e30
SCALE = HEAD_DIM ** -0.5

LANES = 128
F32_SUBLANES = 8
BF16_SUBLANES = 16
assert D_MODEL == F32_SUBLANES * LANES

MOE_BLOCK = 2048
MOE_CHUNK = 160
CHUNK_PITCH = MOE_CHUNK + F32_SUBLANES
EXPERTS_PER_STEP = 4
MOE_VMEM_LIMIT_BYTES = 60 * 1024 * 1024
VMEM_LIMIT_BYTES = 48 * 1024 * 1024

SH1, SC1, G1, SH2, SC2, G2 = (i * D_MODEL for i in range(6))
MOD_ROWS = 8

COARSE_LANE0 = N_EXPERTS

f32 = jnp.float32
bf16 = jnp.bfloat16


def _params(*semantics, vmem=VMEM_LIMIT_BYTES):
    return pltpu.CompilerParams(dimension_semantics=semantics, vmem_limit_bytes=vmem)


def _rms(x):
    return x * lax.rsqrt(jnp.mean(x * x, axis=-1, keepdims=True) + EPS)


def _dot(a, b):
    return jnp.dot(a, b, preferred_element_type=f32)


def _split_bf16(w):
    hi = w.astype(bf16)
    lo = (w - hi.astype(f32)).astype(bf16)
    return jnp.concatenate([hi, lo], axis=1)


def _dot_nt(a, b):
    return lax.dot_general(a, b, (((1,), (1,)), ((), ())), preferred_element_type=f32)


def _mod_kernel(cv_ref, w_ref, b_ref, o_ref):
    a = cv_ref[...]
    a = a * jax.nn.sigmoid(a)
    o_ref[...] = jnp.dot(a, w_ref[...], precision=lax.Precision.HIGHEST,
                         preferred_element_type=f32) + b_ref[...]


def _modulation(cvecs, w_mod, b_mod):
    tn = D_MODEL
    return pl.pallas_call(
        _mod_kernel,
        grid=(6 * D_MODEL // tn,),
        in_specs=[pl.BlockSpec((MOD_ROWS, D_MODEL), lambda j: (0, 0)),
                  pl.BlockSpec((D_MODEL, tn), lambda j: (0, j)),
                  pl.BlockSpec((1, tn), lambda j: (0, j))],
        out_specs=pl.BlockSpec((MOD_ROWS, tn), lambda j: (0, j)),
        out_shape=jax.ShapeDtypeStruct((MOD_ROWS, 6 * D_MODEL), f32),
        compiler_params=_params("arbitrary"),
        name="mod",
    )(cvecs, w_mod, b_mod.reshape(1, -1))


def _swap_halves(x):
    lane = lax.broadcasted_iota(jnp.int32, x.shape, 1)
    up = pltpu.roll(x, LANES - N_FREQ, axis=1)
    dn = pltpu.roll(x, N_FREQ, axis=1)
    return jnp.where((lane % (2 * N_FREQ)) < N_FREQ, up, dn)


def _rope(x, cos, sin):
    parts = []
    for c in range(x.shape[1] // LANES):
        xc = x[:, c * LANES:(c + 1) * LANES]
        parts.append(xc * cos + _swap_halves(xc) * sin)
    return parts[0] if len(parts) == 1 else jnp.concatenate(parts, axis=1)


def _inproj_kernel(*refs, rope):
    if rope:
        x_ref, mod_ref, g_ref, w_ref, cos_ref, sin_ref, q_ref, k_ref, v_ref, u_ref, bg_ref = refs
    else:
        x_ref, mod_ref, g_ref, w_ref, q_ref, k_ref, v_ref, u_ref, bg_ref = refs
    h = _rms(x_ref[...]) * g_ref[...]
    h = h * (1.0 + mod_ref[:, SC1:SC1 + D_MODEL]) + mod_ref[:, SH1:SH1 + D_MODEL]
    z = _dot(h.astype(bf16), w_ref[...])
    o = 0
    q = z[:, o:o + ATTN_WIDTH]; o += ATTN_WIDTH
    k = z[:, o:o + KV_WIDTH]; o += KV_WIDTH
    v = z[:, o:o + KV_WIDTH]; o += KV_WIDTH
    bg = z[:, o:o + CONV_WIDTH]; o += CONV_WIDTH
    cg = z[:, o:o + CONV_WIDTH]; o += CONV_WIDTH
    xin = z[:, o:o + CONV_WIDTH]
    if rope:
        q = _rope(q, cos_ref[...], sin_ref[...])
        k = _rope(k, cos_ref[...], sin_ref[...])
    q_ref[...] = q.astype(q_ref.dtype)
    k_ref[...] = k.astype(k_ref.dtype)
    v_ref[...] = v.astype(v_ref.dtype)
    u_ref[...] = (cg * xin).astype(u_ref.dtype)
    bg_ref[...] = bg.astype(bg_ref.dtype)


def _inproj(x, mod3, norm_g, w_in, *, seq, mod_row0, kv_dtype, rope_tabs=None, tm=512):
    t = x.shape[0]
    tiles_per_seq = seq // tm
    row = lambda i: (mod_row0 + (i // tiles_per_seq if mod_row0 else 0), 0, 0)
    in_specs = [pl.BlockSpec((tm, D_MODEL), lambda i: (i, 0)),
                pl.BlockSpec((None, 1, 6 * D_MODEL), row),
                pl.BlockSpec((1, D_MODEL), lambda i: (0, 0)),
                pl.BlockSpec((D_MODEL, IN_WIDTH), lambda i: (0, 0))]
    args = [x, mod3, norm_g, w_in]
    if rope_tabs is not None:
        in_specs += [pl.BlockSpec((tm, LANES), lambda i: (i % tiles_per_seq, 0))] * 2
        args += list(rope_tabs)
    widths = (ATTN_WIDTH, KV_WIDTH, KV_WIDTH, CONV_WIDTH, CONV_WIDTH)
    dtypes = (bf16, kv_dtype, kv_dtype, bf16, bf16)
    return pl.pallas_call(
        functools.partial(_inproj_kernel, rope=rope_tabs is not None),
        grid=(t // tm,),
        in_specs=in_specs,
        out_specs=[pl.BlockSpec((tm, w), lambda i: (i, 0)) for w in widths],
        out_shape=[jax.ShapeDtypeStruct((t, w), d) for w, d in zip(widths, dtypes)],
        compiler_params=_params("arbitrary"),
        name="inproj_rope" if rope_tabs is not None else "inproj",
    )(*args)


def _rope_tables(n):
    t = np.arange(n)
    inv = ROPE_BASE ** (-np.arange(N_FREQ, dtype=np.float32) / N_FREQ)
    rows = (t // GRID_W).astype(np.float32)
    cols = (t % GRID_W).astype(np.float32)
    d = np.arange(LANES) % HEAD_DIM
    pos = np.where(d[None, :] < HEAD_DIM // 2, rows[:, None], cols[:, None])
    ang = jnp.asarray(pos.astype(np.float32) * inv[d % N_FREQ][None, :])
    sign = np.where((d % (2 * N_FREQ)) < N_FREQ, -1.0, 1.0).astype(np.float32)
    return jnp.cos(ang), jnp.sin(ang) * sign[None, :]


def _attend_group(q_rows, sink_col, keys, vals, masks):
    scores = []
    m = sink_col
    for kk, mk in zip(keys, masks):
        s = _dot_nt(q_rows, kk) * SCALE
        if mk is not None:
            s = jnp.where(mk, s, NEG)
        scores.append(s)
        m = jnp.maximum(m, jnp.max(s, axis=-1, keepdims=True))
    den = jnp.exp(sink_col - m)
    acc = None
    for s, vv in zip(scores, vals):
        p = jnp.exp(s - m)
        den = den + jnp.sum(p, axis=-1, keepdims=True)
        pv = _dot(p.astype(bf16), vv)
        acc = pv if acc is None else acc + pv
    return acc / den


def _heads_attention(q, sink_ref, key_sets, val_sets, masks):
    bq = q.shape[0]
    ridx = lax.broadcasted_iota(jnp.int32, (Q_PER_KV * bq, 1), 0)
    outs = []
    for g in range(N_KV_HEADS):
        heads = range(g * Q_PER_KV, (g + 1) * Q_PER_KV)
        q_rows = jnp.concatenate([q[:, h * HEAD_DIM:(h + 1) * HEAD_DIM] for h in heads], axis=0)
        sink_col = jnp.full((Q_PER_KV * bq, 1), sink_ref[g * Q_PER_KV], f32)
        for j in range(1, Q_PER_KV):
            sink_col = jnp.where(ridx >= j * bq, sink_ref[g * Q_PER_KV + j], sink_col)
        lo, hi = g * HEAD_DIM, (g + 1) * HEAD_DIM
        o = _attend_group(q_rows, sink_col, [kk[:, lo:hi] for kk in key_sets],
                          [vv[:, lo:hi] for vv in val_sets], masks)
        outs += [o[j * bq:(j + 1) * bq] for j in range(Q_PER_KV)]
    return jnp.concatenate(outs, axis=1)


def _ctx_attn_kernel(sink_ref, q_ref, k_ref, v_ref, o_ref):
    k = k_ref[...].astype(bf16)
    v = v_ref[...].astype(bf16)
    o_ref[...] = _heads_attention(q_ref[...], sink_ref, [k], [v], [None]).astype(o_ref.dtype)


def _context_attention(q, k, v, sink, *, seq):
    t = q.shape[0]
    return pl.pallas_call(
        _ctx_attn_kernel,
        grid=(t // seq,),
        in_specs=[pl.BlockSpec(memory_space=pltpu.SMEM),
                  pl.BlockSpec((seq, ATTN_WIDTH), lambda b: (b, 0)),
                  pl.BlockSpec((seq, KV_WIDTH), lambda b: (b, 0)),
                  pl.BlockSpec((seq, KV_WIDTH), lambda b: (b, 0))],
        out_specs=pl.BlockSpec((seq, ATTN_WIDTH), lambda b: (b, 0)),
        out_shape=jax.ShapeDtypeStruct((t, ATTN_WIDTH), bf16),
        compiler_params=_params("arbitrary"),
        name="ctx_attn",
    )(sink, q, k, v)


def _lat_attn_kernel(sink_ref, q_ref, k_ref, v_ref, ck_ref, cv_ref, o_ref, *, bq, seq):
    i = pl.program_id(1)
    band = bq + 2 * WINDOW
    start = pl.multiple_of(jnp.clip(i * bq - WINDOW, 0, seq - band), LANES)
    kb = k_ref[pl.ds(start, band), :]
    vb = v_ref[pl.ds(start, band), :]
    shape = (Q_PER_KV * bq, band)
    qpos = i * bq + (lax.broadcasted_iota(jnp.int32, shape, 0) & (bq - 1))
    kpos = start + lax.broadcasted_iota(jnp.int32, shape, 1)
    mask = jnp.abs(qpos - kpos) <= WINDOW
    ck = ck_ref[...].astype(bf16)
    cv = cv_ref[...].astype(bf16)
    o_ref[...] = _heads_attention(q_ref[...], sink_ref, [kb, ck], [vb, cv],
                                  [mask, None]).astype(o_ref.dtype)


def _latent_attention(q, k, v, ck, cv, sink, *, seq, bq=128):
    nb, past = ck.shape[0], ck.shape[1]
    q3, k3, v3 = (a.reshape(nb, seq, a.shape[-1]) for a in (q, k, v))
    out = pl.pallas_call(
        functools.partial(_lat_attn_kernel, bq=bq, seq=seq),
        grid=(nb, seq // bq),
        in_specs=[pl.BlockSpec(memory_space=pltpu.SMEM),
                  pl.BlockSpec((None, bq, ATTN_WIDTH), lambda b, i: (b, i, 0)),
                  pl.BlockSpec((None, seq, KV_WIDTH), lambda b, i: (b, 0, 0)),
                  pl.BlockSpec((None, seq, KV_WIDTH), lambda b, i: (b, 0, 0)),
                  pl.BlockSpec((None, past, KV_WIDTH), lambda b, i: (b, 0, 0)),
                  pl.BlockSpec((None, past, KV_WIDTH), lambda b, i: (b, 0, 0))],
        out_specs=pl.BlockSpec((None, bq, ATTN_WIDTH), lambda b, i: (b, i, 0)),
        out_shape=jax.ShapeDtypeStruct((nb, seq, ATTN_WIDTH), bf16),
        compiler_params=_params("arbitrary", "arbitrary"),
        name="lat_attn",
    )(sink, q3, k3, v3, ck, cv)
    return out.reshape(nb * seq, ATTN_WIDTH)


def _route(logits):
    lane = lax.broadcasted_iota(jnp.int32, logits.shape, 1)
    lane_f = lane.astype(f32)
    big = jnp.float32(LANES)
    is_c = jnp.logical_and(lane >= COARSE_LANE0, lane < COARSE_LANE0 + N_GROUPS)
    lc = jnp.where(is_c, logits, -jnp.inf)
    mc = jnp.max(lc, axis=-1, keepdims=True)
    grp = jnp.min(jnp.where(lc == mc, lane_f, big), axis=-1, keepdims=True) - COARSE_LANE0
    pg = 1.0 / jnp.sum(jnp.exp(lc - mc), axis=-1, keepdims=True)
    in_g = jnp.floor(lane_f * (1.0 / EXPERTS_PER_GROUP)) == grp
    fl = jnp.where(in_g, logits, -jnp.inf)
    m1 = jnp.max(fl, axis=-1, keepdims=True)
    i1 = jnp.min(jnp.where(fl == m1, lane_f, big), axis=-1, keepdims=True)
    fl2 = jnp.where(lane_f == i1, -jnp.inf, fl)
    m2 = jnp.max(fl2, axis=-1, keepdims=True)
    i2 = jnp.min(jnp.where(fl2 == m2, lane_f, big), axis=-1, keepdims=True)
    e2 = jnp.exp(m2 - m1)
    p1 = pg / (1.0 + e2)
    p2 = pg * e2 / (1.0 + e2)
    packed = jnp.where(lane == 0, i1, jnp.where(lane == 1, i2, jnp.where(lane == 2, p1, p2)))
    return jnp.where(lane < 4, packed, 0.0)


def _post_kernel(x_ref, attn_ref, u_ref, up_ref, un_ref, bg_ref, mod_ref, ag_ref, cg_ref, cw_ref,
                 cb_ref, wo_ref, n2_ref, wr_ref, br_ref, x1_ref, h2_ref, route_ref, *, tm, seq):
    i = pl.program_id(0)
    u = u_ref[...].astype(f32)
    rows = lax.broadcasted_iota(jnp.int32, (tm, 1), 0)
    spos = (i * tm + rows) % seq
    u_dn = jnp.where(rows == 0, up_ref[...].astype(f32)[BF16_SUBLANES - 1:, :],
                     pltpu.roll(u, 1, axis=0))
    u_dn = jnp.where(spos == 0, 0.0, u_dn)
    u_up = jnp.where(rows == tm - 1, un_ref[...].astype(f32)[0:1, :], pltpu.roll(u, tm - 1, axis=0))
    u_up = jnp.where(spos == seq - 1, 0.0, u_up)
    y = u_dn * cw_ref[0:1, :] + u * cw_ref[1:2, :] + u_up * cw_ref[2:3, :] + cb_ref[...]
    conv = bg_ref[...].astype(f32) * y
    attn_n = _rms(attn_ref[...].astype(f32)) * ag_ref[...]
    conv_n = _rms(conv) * cg_ref[...]
    mixed = (_dot(attn_n.astype(bf16), wo_ref[0:ATTN_WIDTH, :])
             + _dot(conv_n.astype(bf16), wo_ref[ATTN_WIDTH:, :]))
    x1 = x_ref[...] + mod_ref[:, G1:G1 + D_MODEL] * mixed
    x1_ref[...] = x1
    h2 = _rms(x1) * n2_ref[...]
    h2 = h2 * (1.0 + mod_ref[:, SC2:SC2 + D_MODEL]) + mod_ref[:, SH2:SH2 + D_MODEL]
    for c in range(D_MODEL // LANES):
        h2_ref[pl.ds(c, tm, stride=F32_SUBLANES), :] = h2[:, c * LANES:(c + 1) * LANES]
    h2_hi = h2.astype(bf16)
    h2_lo = (h2 - h2_hi.astype(f32)).astype(bf16)
    both = _dot(h2_hi, wr_ref[...])
    logits = (both[:, :LANES] + both[:, LANES:] + _dot(h2_lo, wr_ref[:, :LANES])) + br_ref[...]
    route_ref[...] = _route(logits)


def _post(x, attn, u, bg, mod3, p, *, seq, mod_row0, tm=256):
    t = x.shape[0]
    tiles_per_seq = seq // tm
    halo = BF16_SUBLANES
    n_halo = t // halo
    row = lambda i: (mod_row0 + (i // tiles_per_seq if mod_row0 else 0), 0, 0)
    tile = lambda w: pl.BlockSpec((tm, w), lambda i: (i, 0))
    full = lambda a: pl.BlockSpec(a.shape, lambda i: (0,) * a.ndim)
    small = [p["attn_out_g"], p["conv_out_g"], p["conv_w"], p["conv_b"], p["w_o"], p["norm2_g"],
             p["w_router"], p["b_router"]]
    return pl.pallas_call(
        functools.partial(_post_kernel, tm=tm, seq=seq),
        grid=(t // tm,),
        in_specs=[tile(D_MODEL), tile(ATTN_WIDTH), tile(CONV_WIDTH),
                  pl.BlockSpec((halo, CONV_WIDTH),
                               lambda i: (jnp.maximum(i * (tm // halo) - 1, 0), 0)),
                  pl.BlockSpec((halo, CONV_WIDTH),
                               lambda i: (jnp.minimum((i + 1) * (tm // halo), n_halo - 1), 0)),
                  tile(CONV_WIDTH),
                  pl.BlockSpec((None, 1, 6 * D_MODEL), row)] + [full(a) for a in small],
        out_specs=[tile(D_MODEL), pl.BlockSpec((tm * F32_SUBLANES, LANES), lambda i: (i, 0)),
                   tile(LANES)],
        out_shape=[jax.ShapeDtypeStruct((t, D_MODEL), f32),
                   jax.ShapeDtypeStruct((t * F32_SUBLANES, LANES), f32),
                   jax.ShapeDtypeStruct((t, LANES), f32)],
        compiler_params=_params("arbitrary"),
        name="post",
    )(x, attn, u, u, u, bg, mod3, *small)


def _max_chunks(blk):
    n = -(-TOP_K * blk // MOE_CHUNK) + EXPERTS_PER_STEP
    return n + n % 2


def _dispatch_tables(route, blk):
    nblk = route.shape[0] // blk
    n_slots = TOP_K * blk
    flat = lambda a: a.reshape(nblk, blk, TOP_K).transpose(0, 2, 1).reshape(nblk, n_slots)
    experts = flat(route[:, 0:TOP_K].astype(jnp.int32))
    weights = flat(route[:, TOP_K:2 * TOP_K])
    order = jnp.argsort(experts, axis=1, stable=True).astype(jnp.int32)
    w_sorted = jnp.take_along_axis(weights, order, axis=1)
    bounds = jnp.arange(N_EXPERTS + 1, dtype=jnp.int32)
    offs = jnp.sum(experts[:, :, None] < bounds[None, None, :], axis=1, dtype=jnp.int32)
    tail = ((0, 0), (0, MOE_CHUNK))
    src_rows = jnp.pad((order & (blk - 1)) * F32_SUBLANES, tail)
    dst_rows = jnp.pad(order * F32_SUBLANES, tail, constant_values=n_slots * F32_SUBLANES)
    w_sorted = jnp.pad(w_sorted, tail)

    eg, n_steps, max_c = EXPERTS_PER_STEP, N_EXPERTS // EXPERTS_PER_STEP, _max_chunks(blk)
    lo = offs[:, :N_EXPERTS].reshape(nblk, n_steps, eg)
    n_ch = (offs[:, 1:] - offs[:, :N_EXPERTS] + MOE_CHUNK - 1) // MOE_CHUNK
    n_ch = n_ch.reshape(nblk, n_steps, eg)
    cum = jnp.cumsum(n_ch, axis=2)
    total = cum[..., -1]
    pos = jnp.arange(max_c, dtype=jnp.int32)
    j = jnp.sum(cum[:, :, None, :] <= pos[None, None, :, None], axis=-1, dtype=jnp.int32)
    j = jnp.minimum(j, eg - 1)
    first = jnp.take_along_axis(cum - n_ch, j, axis=2)
    base = jnp.take_along_axis(lo, j, axis=2) + (pos - first) * MOE_CHUNK
    valid = pos < total[..., None]
    base = jnp.where(valid, base, n_slots).reshape(nblk, 1, n_steps * max_c)
    j = jnp.where(valid, j, 0).reshape(nblk, 1, n_steps * max_c)
    n_pairs = ((total + 1) // 2)[:, None, :]
    return src_rows[:, None, :], dst_rows[:, None, :], w_sorted[:, None, :], j, base, n_pairs


def _moe_kernel(src_ref, dst_ref, wts_ref, cj_ref, cbase_g_ref, cbase_s_ref, npairs_ref,
                h_ref, wg_ref, wu_ref, wd_ref, y_ref, ytmp_ref, *stage_refs, blk):
    n_col = D_MODEL // LANES
    step = pl.program_id(1)
    max_c = _max_chunks(blk)

    def gather(base, xg_ref):
        for r in range(MOE_CHUNK):
            src = pl.multiple_of(src_ref[0, base + r], F32_SUBLANES)
            xg_ref[pl.ds(r, F32_SUBLANES, stride=CHUNK_PITCH), :] = h_ref[pl.ds(src, F32_SUBLANES), :]

    def experts_mlp(j, xg_ref, og_ref):
        x = jnp.concatenate([xg_ref[c * CHUNK_PITCH:c * CHUNK_PITCH + MOE_CHUNK, :]
                             for c in range(n_col)], axis=1).astype(bf16)
        g = _dot(x, wg_ref[j])
        up = _dot(x, wu_ref[j])
        act = (g * jax.nn.sigmoid(g)) * up
        out = _dot(act.astype(bf16), wd_ref[j])
        for c in range(n_col):
            og_ref[c * CHUNK_PITCH:c * CHUNK_PITCH + MOE_CHUNK, :] = out[:, c * LANES:(c + 1) * LANES]

    def scatter(base, og_ref):
        for r in range(MOE_CHUNK):
            dst = pl.multiple_of(dst_ref[0, base + r], F32_SUBLANES)
            ytmp_ref[pl.ds(dst, F32_SUBLANES), :] = (
                og_ref[pl.ds(r, F32_SUBLANES, stride=CHUNK_PITCH), :] * wts_ref[0, base + r])

    def pair(pi, carry):
        chunks = [(step * max_c + 2 * pi + half, stage_refs[2 * half], stage_refs[2 * half + 1])
                  for half in range(2)]
        for p, xg_ref, _ in chunks:
            gather(cbase_g_ref[0, p], xg_ref)
        for p, xg_ref, og_ref in chunks:
            experts_mlp(cj_ref[0, p], xg_ref, og_ref)
        for p, _, og_ref in chunks:
            scatter(cbase_s_ref[0, p], og_ref)
        return carry

    lax.fori_loop(0, npairs_ref[0, step], pair, 0)

    @pl.when(pl.program_id(1) == pl.num_programs(1) - 1)
    def _():
        for c in range(n_col):
            acc = ytmp_ref[pl.ds(c, blk, stride=F32_SUBLANES), :]
            for k in range(1, TOP_K):
                acc = acc + ytmp_ref[pl.ds(k * blk * F32_SUBLANES + c, blk, stride=F32_SUBLANES), :]
            y_ref[:, c * LANES:(c + 1) * LANES] = acc.astype(y_ref.dtype)


def _final_kernel(x1_ref, y_ref, mod_ref, fg_ref, o_ref):
    x2 = x1_ref[...] + mod_ref[:, G2:G2 + D_MODEL] * y_ref[...].astype(f32)
    o_ref[...] = _rms(x2) * fg_ref[...]


def _moe(h2_tiles, route, w_gate, w_up, w_down, *, blk=MOE_BLOCK):
    t = route.shape[0]
    src_rows, dst_rows, w_sorted, cj, cbase, n_pairs = _dispatch_tables(route, blk)
    n_slots = TOP_K * blk
    eg, n_steps = EXPERTS_PER_STEP, N_EXPERTS // EXPERTS_PER_STEP
    smem = lambda n: pl.BlockSpec((None, 1, n), lambda b, g: (b, 0, 0), memory_space=pltpu.SMEM)
    stage = pltpu.VMEM((D_MODEL // LANES * CHUNK_PITCH, LANES), f32)
    return pl.pallas_call(
        functools.partial(_moe_kernel, blk=blk),
        grid=(t // blk, n_steps),
        in_specs=[smem(n_slots + MOE_CHUNK)] * 3 + [smem(n_steps * _max_chunks(blk))] * 3 + [
                  smem(n_steps),
                  pl.BlockSpec((blk * F32_SUBLANES, LANES), lambda b, g: (b, 0)),
                  pl.BlockSpec((eg, D_MODEL, D_EXPERT), lambda b, g: (g, 0, 0)),
                  pl.BlockSpec((eg, D_MODEL, D_EXPERT), lambda b, g: (g, 0, 0)),
                  pl.BlockSpec((eg, D_EXPERT, D_MODEL), lambda b, g: (g, 0, 0))],
        out_specs=pl.BlockSpec((blk, D_MODEL), lambda b, g: (b, 0)),
        out_shape=jax.ShapeDtypeStruct((t, D_MODEL), bf16),
        scratch_shapes=[pltpu.VMEM(((n_slots + 1) * F32_SUBLANES, LANES), f32)] + [stage] * 4,
        compiler_params=_params("arbitrary", "arbitrary", vmem=MOE_VMEM_LIMIT_BYTES),
        name="moe",
    )(src_rows, dst_rows, w_sorted, cj, cbase, cbase, n_pairs, h2_tiles, w_gate, w_up, w_down)


def _final(x1, y, mod3, final_g, *, seq, mod_row0, tm=512):
    t = x1.shape[0]
    tiles_per_seq = max(seq // tm, 1)
    row = lambda i: (mod_row0 + (i // tiles_per_seq if mod_row0 else 0), 0, 0)
    tile = pl.BlockSpec((tm, D_MODEL), lambda i: (i, 0))
    return pl.pallas_call(
        _final_kernel,
        grid=(t // tm,),
        in_specs=[tile, tile, pl.BlockSpec((None, 1, 6 * D_MODEL), row),
                  pl.BlockSpec((1, D_MODEL), lambda i: (0, 0))],
        out_specs=tile,
        out_shape=jax.ShapeDtypeStruct((t, D_MODEL), f32),
        compiler_params=_params("arbitrary"),
        name="final",
    )(x1, y, mod3, final_g)


def _trunk(x, mod3, p, *, seq, mod_row0, ctx_kv):
    nb = x.shape[0] // seq
    latent = ctx_kv is not None
    q, k, v, u, bg = _inproj(x, mod3, p["norm1_g"], p["w_in"], seq=seq, mod_row0=mod_row0,
                             kv_dtype=bf16 if latent else f32,
                             rope_tabs=_rope_tables(seq) if latent else None)
    if latent:
        attn = _latent_attention(q, k, v, ctx_kv[0], ctx_kv[1], p["sink"], seq=seq)
    else:
        attn = _context_attention(q, k, v, p["sink"], seq=seq)
    x1, h2_tiles, route = _post(x, attn, u, bg, mod3, p, seq=seq, mod_row0=mod_row0)
    y = _moe(h2_tiles, route, p["w_gate"], p["w_up"], p["w_down"])
    out = _final(x1, y, mod3, p["final_g"], seq=seq, mod_row0=mod_row0)
    return out.reshape(nb, seq, D_MODEL), k, v


def kernel(x_prompt, x_sample, cache_k, cache_v, c, c_ctx, w_mod, b_mod, norm1_g, w_in, conv_w,
           conv_b, sink, attn_out_g, conv_out_g, w_o, norm2_g, w_coarse, b_coarse, w_fine, b_fine,
           w_gate, w_up, w_down, final_g):
    batch, seq, _ = x_prompt.shape
    dec_batch, dec_seq, _ = x_sample.shape
    past = cache_k.shape[2]
    assert w_mod.shape[0] == 1 and 1 + dec_batch <= MOD_ROWS

    cvecs = jnp.concatenate([c_ctx[None], c, jnp.zeros((MOD_ROWS - 1 - dec_batch, D_MODEL), f32)])
    mod3 = _modulation(cvecs, w_mod[0], b_mod[0]).reshape(MOD_ROWS, 1, 6 * D_MODEL)

    pad = jnp.zeros((D_MODEL, LANES - N_EXPERTS - N_GROUPS), f32)
    p = {
        "norm1_g": norm1_g, "w_in": w_in[0].astype(bf16), "conv_w": conv_w[0], "conv_b": conv_b,
        "sink": sink[0], "attn_out_g": attn_out_g, "conv_out_g": conv_out_g,
        "w_o": w_o[0].astype(bf16), "norm2_g": norm2_g,
        "w_router": _split_bf16(jnp.concatenate([w_fine[0], w_coarse[0], pad], axis=1)),
        "b_router": jnp.concatenate([b_fine[0], b_coarse[0], pad[0]])[None],
        "w_gate": w_gate[0].astype(bf16), "w_up": w_up[0].astype(bf16),
        "w_down": w_down[0].astype(bf16), "final_g": final_g[None],
    }

    y_prompt, k_p, v_p = _trunk(x_prompt.reshape(batch * seq, D_MODEL), mod3, p, seq=seq,
                                mod_row0=0, ctx_kv=None)
    new_k = k_p.reshape(batch, 1, seq, N_KV_HEADS, HEAD_DIM)
    new_v = v_p.reshape(batch, 1, seq, N_KV_HEADS, HEAD_DIM)

    ctx_kv = (cache_k[:, 0].reshape(dec_batch, past, KV_WIDTH),
              cache_v[:, 0].reshape(dec_batch, past, KV_WIDTH))
    y_sample, _, _ = _trunk(x_sample.reshape(dec_batch * dec_seq, D_MODEL), mod3, p, seq=dec_seq,
                            mod_row0=1, ctx_kv=ctx_kv)
    return y_prompt, y_sample, new_k, new_v
```

```python
import functools

import jax
import jax.numpy as jnp
import numpy as np
from jax import lax
from jax.experimental import pallas as pl
from jax.experimental.pallas import tpu as pltpu

D_MODEL = 1024
HEAD_DIM = 64
ATTN_WIDTH = 512
N_HEADS = 8
N_KV_HEADS = 2
Q_PER_KV = 4
KV_WIDTH = 128
CONV_WIDTH = 512
CONV_K = 3
WINDOW = 128
GRID_W = 64
ROPE_BASE = 10000.0
N_FREQ = 16
N_GROUPS = 4
EXPERTS_PER_GROUP = 8
N_EXPERTS = 32
TOP_K = 2
D_EXPERT = 256
IN_WIDTH = ATTN_WIDTH + 2 * KV_WIDTH + 3 * CONV_WIDTH
EPS = 1e-6
NEG = -1e30
SCALE = HEAD_DIM ** -0.5
LOG2_E = 1.4426950408889634

LANES = 128
F32_SUBLANES = 8
BF16_SUBLANES = 16
assert D_MODEL == F32_SUBLANES * LANES

MOE_BLOCK = 2048
MOE_CHUNK = 160
CHUNK_PITCH = MOE_CHUNK + F32_SUBLANES
EXPERTS_PER_STEP = 4
MOE_VMEM_LIMIT_BYTES = 60 * 1024 * 1024
VMEM_LIMIT_BYTES = 48 * 1024 * 1024

SH1, SC1, G1, SH2, SC2, G2 = (i * D_MODEL for i in range(6))
MOD_ROWS = 8

COARSE_LANE0 = N_EXPERTS

f32 = jnp.float32
bf16 = jnp.bfloat16


def _params(*semantics, vmem=VMEM_LIMIT_BYTES):
    return pltpu.CompilerParams(dimension_semantics=semantics, vmem_limit_bytes=vmem)


def _rms(x):
    return x * lax.rsqrt(jnp.mean(x * x, axis=-1, keepdims=True) + EPS)


def _dot(a, b):
    return jnp.dot(a, b, preferred_element_type=f32)


def _split_bf16(w):
    hi = w.astype(bf16)
    lo = (w - hi.astype(f32)).astype(bf16)
    return jnp.concatenate([hi, lo], axis=1)


def _dot_nt(a, b):
    return lax.dot_general(a, b, (((1,), (1,)), ((), ())), preferred_element_type=f32)


def _mod_kernel(cv_ref, w_ref, b_ref, o_ref):
    a = cv_ref[...]
    a = a * jax.nn.sigmoid(a)
    o_ref[...] = jnp.dot(a, w_ref[...], precision=lax.Precision.HIGHEST,
                         preferred_element_type=f32) + b_ref[...]


def _modulation(cvecs, w_mod, b_mod):
    tn = D_MODEL
    return pl.pallas_call(
        _mod_kernel,
        grid=(6 * D_MODEL // tn,),
        in_specs=[pl.BlockSpec((MOD_ROWS, D_MODEL), lambda j: (0, 0)),
                  pl.BlockSpec((D_MODEL, tn), lambda j: (0, j)),
                  pl.BlockSpec((1, tn), lambda j: (0, j))],
        out_specs=pl.BlockSpec((MOD_ROWS, tn), lambda j: (0, j)),
        out_shape=jax.ShapeDtypeStruct((MOD_ROWS, 6 * D_MODEL), f32),
        compiler_params=_params("arbitrary"),
        name="mod",
    )(cvecs, w_mod, b_mod.reshape(1, -1))


def _swap_halves(x):
    lane = lax.broadcasted_iota(jnp.int32, x.shape, 1)
    up = pltpu.roll(x, LANES - N_FREQ, axis=1)
    dn = pltpu.roll(x, N_FREQ, axis=1)
    return jnp.where((lane % (2 * N_FREQ)) < N_FREQ, up, dn)


def _rope(x, cos, sin):
    parts = []
    for c in range(x.shape[1] // LANES):
        xc = x[:, c * LANES:(c + 1) * LANES]
        parts.append(xc * cos + _swap_halves(xc) * sin)
    return parts[0] if len(parts) == 1 else jnp.concatenate(parts, axis=1)


def _inproj_kernel(*refs, rope):
    if rope:
        x_ref, mod_ref, g_ref, w_ref, cos_ref, sin_ref, q_ref, k_ref, v_ref, u_ref, bg_ref = refs
    else:
        x_ref, mod_ref, g_ref, w_ref, q_ref, k_ref, v_ref, u_ref, bg_ref = refs
    h = _rms(x_ref[...]) * g_ref[...]
    h = h * (1.0 + mod_ref[:, SC1:SC1 + D_MODEL]) + mod_ref[:, SH1:SH1 + D_MODEL]
    z = _dot(h.astype(bf16), w_ref[...])
    o = 0
    q = z[:, o:o + ATTN_WIDTH]; o += ATTN_WIDTH
    k = z[:, o:o + KV_WIDTH]; o += KV_WIDTH
    v = z[:, o:o + KV_WIDTH]; o += KV_WIDTH
    bg = z[:, o:o + CONV_WIDTH]; o += CONV_WIDTH
    cg = z[:, o:o + CONV_WIDTH]; o += CONV_WIDTH
    xin = z[:, o:o + CONV_WIDTH]
    if rope:
        q = _rope(q, cos_ref[...], sin_ref[...])
        k = _rope(k, cos_ref[...], sin_ref[...])
    q_ref[...] = q.astype(q_ref.dtype)
    k_ref[...] = k.astype(k_ref.dtype)
    v_ref[...] = (v.T if rope else v).astype(v_ref.dtype)
    u_ref[...] = (cg * xin).astype(u_ref.dtype)
    bg_ref[...] = bg.astype(bg_ref.dtype)


def _inproj(x, mod3, norm_g, w_in, *, seq, mod_row0, kv_dtype, rope_tabs=None, tm=512):
    t = x.shape[0]
    tiles_per_seq = seq // tm
    row = lambda i: (mod_row0 + (i // tiles_per_seq if mod_row0 else 0), 0, 0)
    in_specs = [pl.BlockSpec((tm, D_MODEL), lambda i: (i, 0)),
                pl.BlockSpec((None, 1, 6 * D_MODEL), row),
                pl.BlockSpec((1, D_MODEL), lambda i: (0, 0)),
                pl.BlockSpec((D_MODEL, IN_WIDTH), lambda i: (0, 0))]
    args = [x, mod3, norm_g, w_in]
    if rope_tabs is not None:
        in_specs += [pl.BlockSpec((tm, LANES), lambda i: (i % tiles_per_seq, 0))] * 2
        args += list(rope_tabs)
    widths = (ATTN_WIDTH, KV_WIDTH, KV_WIDTH, CONV_WIDTH, CONV_WIDTH)
    dtypes = (bf16, kv_dtype, kv_dtype, bf16, bf16)
    out_specs = [pl.BlockSpec((tm, w), lambda i: (i, 0)) for w in widths]
    out_shape = [jax.ShapeDtypeStruct((t, w), d) for w, d in zip(widths, dtypes)]
    if rope_tabs is not None:
        out_specs[2] = pl.BlockSpec((None, KV_WIDTH, tm),
                                    lambda i: (i // tiles_per_seq, 0, i % tiles_per_seq))
        out_shape[2] = jax.ShapeDtypeStruct((t // seq, KV_WIDTH, seq), kv_dtype)
    return pl.pallas_call(
        functools.partial(_inproj_kernel, rope=rope_tabs is not None),
        grid=(t // tm,),
        in_specs=in_specs,
        out_specs=out_specs,
        out_shape=out_shape,
        compiler_params=_params("arbitrary"),
        name="inproj_rope" if rope_tabs is not None else "inproj",
    )(*args)


def _rope_tables(n):
    t = np.arange(n)
    inv = ROPE_BASE ** (-np.arange(N_FREQ, dtype=np.float32) / N_FREQ)
    rows = (t // GRID_W).astype(np.float32)
    cols = (t % GRID_W).astype(np.float32)
    d = np.arange(LANES) % HEAD_DIM
    pos = np.where(d[None, :] < HEAD_DIM // 2, rows[:, None], cols[:, None])
    ang = jnp.asarray(pos.astype(np.float32) * inv[d % N_FREQ][None, :])
    sign = np.where((d % (2 * N_FREQ)) < N_FREQ, -1.0, 1.0).astype(np.float32)
    return jnp.cos(ang), jnp.sin(ang) * sign[None, :]


def _attend_group(q_rows, sink_col, keys, vals, masks):
    scores = []
    m = sink_col
    for kk, mk in zip(keys, masks):
        s = _dot_nt(q_rows, kk) * SCALE
        if mk is not None:
            s = jnp.where(mk, s, NEG)
        scores.append(s)
        m = jnp.maximum(m, jnp.max(s, axis=-1, keepdims=True))
    den = jnp.exp(sink_col - m)
    acc = None
    for s, vv in zip(scores, vals):
        p = jnp.exp(s - m)
        den = den + jnp.sum(p, axis=-1, keepdims=True)
        pv = _dot(p.astype(bf16), vv)
        acc = pv if acc is None else acc + pv
    return acc / den


def _heads_attention(q, sink_ref, key_sets, val_sets, masks):
    bq = q.shape[0]
    ridx = lax.broadcasted_iota(jnp.int32, (Q_PER_KV * bq, 1), 0)
    outs = []
    for g in range(N_KV_HEADS):
        heads = range(g * Q_PER_KV, (g + 1) * Q_PER_KV)
        q_rows = jnp.concatenate([q[:, h * HEAD_DIM:(h + 1) * HEAD_DIM] for h in heads], axis=0)
        sink_col = jnp.full((Q_PER_KV * bq, 1), sink_ref[g * Q_PER_KV], f32)
        for j in range(1, Q_PER_KV):
            sink_col = jnp.where(ridx >= j * bq, sink_ref[g * Q_PER_KV + j], sink_col)
        lo, hi = g * HEAD_DIM, (g + 1) * HEAD_DIM
        o = _attend_group(q_rows, sink_col, [kk[:, lo:hi] for kk in key_sets],
                          [vv[:, lo:hi] for vv in val_sets], masks)
        outs += [o[j * bq:(j + 1) * bq] for j in range(Q_PER_KV)]
    return jnp.concatenate(outs, axis=1)


def _ctx_attn_kernel(sink_ref, q_ref, k_ref, v_ref, o_ref):
    k = k_ref[...].astype(bf16)
    v = v_ref[...].astype(bf16)
    o_ref[...] = _heads_attention(q_ref[...], sink_ref, [k], [v], [None]).astype(o_ref.dtype)


def _context_attention(q, k, v, sink, *, seq):
    t = q.shape[0]
    return pl.pallas_call(
        _ctx_attn_kernel,
        grid=(t // seq,),
        in_specs=[pl.BlockSpec(memory_space=pltpu.SMEM),
                  pl.BlockSpec((seq, ATTN_WIDTH), lambda b: (b, 0)),
                  pl.BlockSpec((seq, KV_WIDTH), lambda b: (b, 0)),
                  pl.BlockSpec((seq, KV_WIDTH), lambda b: (b, 0))],
        out_specs=pl.BlockSpec((seq, ATTN_WIDTH), lambda b: (b, 0)),
        out_shape=jax.ShapeDtypeStruct((t, ATTN_WIDTH), bf16),
        compiler_params=_params("arbitrary"),
        name="ctx_attn",
    )(sink, q, k, v)


def _lat_attn_kernel(sink_ref, q_ref, k_ref, vt_ref, ck_ref, cv_ref, o_ref, *, bq, seq):
    band = bq + 2 * WINDOW
    n_col = Q_PER_KV * bq
    ck = ck_ref[...].astype(bf16)
    cvt = cv_ref[...].T.astype(bf16)
    delta = ((lax.broadcasted_iota(jnp.int32, (band, n_col), 1) & (bq - 1))
             - lax.broadcasted_iota(jnp.int32, (band, n_col), 0))
    col_head = lax.broadcasted_iota(jnp.int32, (1, n_col), 1) // bq
    sink_rows = []
    for g in range(N_KV_HEADS):
        sink_row = jnp.full((1, n_col), sink_ref[g * Q_PER_KV], f32)
        for j in range(1, Q_PER_KV):
            sink_row = jnp.where(col_head == j, sink_ref[g * Q_PER_KV + j], sink_row)
        sink_rows.append(sink_row * LOG2_E)
    ones_rows = jnp.ones((BF16_SUBLANES, 1), bf16)
    cvt_aug = [jnp.concatenate([cvt[g * HEAD_DIM:(g + 1) * HEAD_DIM, :],
                                jnp.broadcast_to(ones_rows, (BF16_SUBLANES, cvt.shape[1]))], axis=0)
               for g in range(N_KV_HEADS)]

    n_sub = q_ref.shape[0] // bq
    subs = []
    for sb in range(n_sub):
        i = pl.program_id(1) * n_sub + sb
        start = pl.multiple_of(jnp.clip(i * bq - WINDOW, 0, seq - band), LANES)
        shifted = delta + (i * bq - start + WINDOW)
        subs.append(dict(
            kb=k_ref[pl.ds(start, band), :],
            vbt=vt_ref[:, pl.ds(start, band)],
            qt=(q_ref[sb * bq:(sb + 1) * bq, :].astype(f32) * (SCALE * LOG2_E)).T.astype(bf16),
            mask=shifted.astype(jnp.uint32) <= 2 * WINDOW))

    def scores(sub, g):
        lo, hi = g * HEAD_DIM, (g + 1) * HEAD_DIM
        heads = range(g * Q_PER_KV, (g + 1) * Q_PER_KV)
        qt_g = jnp.concatenate([sub["qt"][h * HEAD_DIM:(h + 1) * HEAD_DIM, :] for h in heads],
                               axis=1)
        s_loc = jnp.where(sub["mask"], _dot(sub["kb"][:, lo:hi], qt_g), NEG)
        s_ctx = _dot(ck[:, lo:hi], qt_g)
        return s_loc, s_ctx

    def weighted_values(sub, g, s_loc, s_ctx):
        lo, hi = g * HEAD_DIM, (g + 1) * HEAD_DIM
        m = jnp.maximum(jnp.maximum(jnp.max(s_loc, axis=0, keepdims=True),
                                    jnp.max(s_ctx, axis=0, keepdims=True)), sink_rows[g])
        p_loc = jnp.exp2(s_loc - m).astype(bf16)
        p_ctx = jnp.exp2(s_ctx - m).astype(bf16)
        vbt_aug = jnp.concatenate(
            [sub["vbt"][lo:hi, :], jnp.broadcast_to(ones_rows, (BF16_SUBLANES, band))], axis=0)
        o_aug = _dot(vbt_aug, p_loc) + _dot(cvt_aug[g], p_ctx)
        den = o_aug[HEAD_DIM:HEAD_DIM + 1, :] + jnp.exp2(sink_rows[g] - m)
        return o_aug[:HEAD_DIM, :] / den

    passes = [(sb, g) for sb in range(n_sub) for g in range(N_KV_HEADS)]
    out_t = {}
    pending = scores(subs[0], 0)
    for n, (sb, g) in enumerate(passes):
        nxt = scores(subs[passes[n + 1][0]], passes[n + 1][1]) if n + 1 < len(passes) else None
        out_t[sb, g] = weighted_values(subs[sb], g, *pending)
        pending = nxt
    for sb in range(n_sub):
        blocks = [jnp.concatenate([out_t[sb, g][:, j * bq:(j + 1) * bq]
                                   for g in range(N_KV_HEADS)], axis=0).T
                  for j in range(Q_PER_KV)]
        o_ref[sb * bq:(sb + 1) * bq, :] = jnp.concatenate(blocks, axis=1).astype(o_ref.dtype)


LATENT_HEAD_PERM = np.array([(Q_PER_KV * g + j) * HEAD_DIM + d for j in range(Q_PER_KV)
                             for g in range(N_KV_HEADS) for d in range(HEAD_DIM)])


def _latent_attention(q, k, vt, ck, cv, sink, *, seq, bq=128, sub_blocks=4):
    nb, past = ck.shape[0], ck.shape[1]
    assert bq == LANES and N_KV_HEADS * HEAD_DIM == LANES
    q3, k3 = (a.reshape(nb, seq, a.shape[-1]) for a in (q, k))
    tq = bq * sub_blocks
    out = pl.pallas_call(
        functools.partial(_lat_attn_kernel, bq=bq, seq=seq),
        grid=(nb, seq // tq),
        in_specs=[pl.BlockSpec(memory_space=pltpu.SMEM),
                  pl.BlockSpec((None, tq, ATTN_WIDTH), lambda b, i: (b, i, 0)),
                  pl.BlockSpec((None, seq, KV_WIDTH), lambda b, i: (b, 0, 0)),
                  pl.BlockSpec((None, KV_WIDTH, seq), lambda b, i: (b, 0, 0)),
                  pl.BlockSpec((None, past, KV_WIDTH), lambda b, i: (b, 0, 0)),
                  pl.BlockSpec((None, past, KV_WIDTH), lambda b, i: (b, 0, 0))],
        out_specs=pl.BlockSpec((None, tq, ATTN_WIDTH), lambda b, i: (b, i, 0)),
        out_shape=jax.ShapeDtypeStruct((nb, seq, ATTN_WIDTH), bf16),
        compiler_params=_params("arbitrary", "arbitrary"),
        name="lat_attn",
    )(sink, q3, k3, vt, ck, cv)
    return out.reshape(nb * seq, ATTN_WIDTH)


def _route(logits):
    lane = lax.broadcasted_iota(jnp.int32, logits.shape, 1)
    lane_f = lane.astype(f32)
    big = jnp.float32(LANES)
    is_c = jnp.logical_and(lane >= COARSE_LANE0, lane < COARSE_LANE0 + N_GROUPS)
    lc = jnp.where(is_c, logits, -jnp.inf)
    mc = jnp.max(lc, axis=-1, keepdims=True)
    grp = jnp.min(jnp.where(lc == mc, lane_f, big), axis=-1, keepdims=True) - COARSE_LANE0
    pg = 1.0 / jnp.sum(jnp.exp(lc - mc), axis=-1, keepdims=True)
    in_g = jnp.floor(lane_f * (1.0 / EXPERTS_PER_GROUP)) == grp
    fl = jnp.where(in_g, logits, -jnp.inf)
    m1 = jnp.max(fl, axis=-1, keepdims=True)
    i1 = jnp.min(jnp.where(fl == m1, lane_f, big), axis=-1, keepdims=True)
    fl2 = jnp.where(lane_f == i1, -jnp.inf, fl)
    m2 = jnp.max(fl2, axis=-1, keepdims=True)
    i2 = jnp.min(jnp.where(fl2 == m2, lane_f, big), axis=-1, keepdims=True)
    e2 = jnp.exp(m2 - m1)
    p1 = pg / (1.0 + e2)
    p2 = pg * e2 / (1.0 + e2)
    packed = jnp.where(lane == 0, i1, jnp.where(lane == 1, i2, jnp.where(lane == 2, p1, p2)))
    return jnp.where(lane < 4, packed, 0.0)


def _post_kernel(x_ref, attn_ref, u_ref, up_ref, un_ref, bg_ref, mod_ref, ag_ref, cg_ref, cw_ref,
                 cb_ref, wo_ref, n2_ref, wr_ref, br_ref, x1_ref, h2_ref, route_ref, *, tm, seq):
    i = pl.program_id(0)
    u = u_ref[...].astype(f32)
    rows = lax.broadcasted_iota(jnp.int32, (tm, 1), 0)
    spos = (i * tm + rows) % seq
    u_dn = jnp.where(rows == 0, up_ref[...].astype(f32)[BF16_SUBLANES - 1:, :],
                     pltpu.roll(u, 1, axis=0))
    u_dn = jnp.where(spos == 0, 0.0, u_dn)
    u_up = jnp.where(rows == tm - 1, un_ref[...].astype(f32)[0:1, :], pltpu.roll(u, tm - 1, axis=0))
    u_up = jnp.where(spos == seq - 1, 0.0, u_up)
    y = u_dn * cw_ref[0:1, :] + u * cw_ref[1:2, :] + u_up * cw_ref[2:3, :] + cb_ref[...]
    conv = bg_ref[...].astype(f32) * y
    attn_n = _rms(attn_ref[...].astype(f32)) * ag_ref[...]
    conv_n = _rms(conv) * cg_ref[...]
    mixed = (_dot(attn_n.astype(bf16), wo_ref[0:ATTN_WIDTH, :])
             + _dot(conv_n.astype(bf16), wo_ref[ATTN_WIDTH:, :]))
    x1 = x_ref[...] + mod_ref[:, G1:G1 + D_MODEL] * mixed
    x1_ref[...] = x1
    h2 = _rms(x1) * n2_ref[...]
    h2 = h2 * (1.0 + mod_ref[:, SC2:SC2 + D_MODEL]) + mod_ref[:, SH2:SH2 + D_MODEL]
    for c in range(D_MODEL // LANES):
        h2_ref[pl.ds(c, tm, stride=F32_SUBLANES), :] = h2[:, c * LANES:(c + 1) * LANES]
    h2_hi = h2.astype(bf16)
    h2_lo = (h2 - h2_hi.astype(f32)).astype(bf16)
    both = _dot(h2_hi, wr_ref[...])
    logits = (both[:, :LANES] + both[:, LANES:] + _dot(h2_lo, wr_ref[:, :LANES])) + br_ref[...]
    route_ref[...] = _route(logits)


def _post(x, attn, u, bg, mod3, p, *, seq, mod_row0, tm=256):
    t = x.shape[0]
    tiles_per_seq = seq // tm
    halo = BF16_SUBLANES
    n_halo = t // halo
    row = lambda i: (mod_row0 + (i // tiles_per_seq if mod_row0 else 0), 0, 0)
    tile = lambda w: pl.BlockSpec((tm, w), lambda i: (i, 0))
    full = lambda a: pl.BlockSpec(a.shape, lambda i: (0,) * a.ndim)
    small = [p["attn_out_g"], p["conv_out_g"], p["conv_w"], p["conv_b"], p["w_o"], p["norm2_g"],
             p["w_router"], p["b_router"]]
    return pl.pallas_call(
        functools.partial(_post_kernel, tm=tm, seq=seq),
        grid=(t // tm,),
        in_specs=[tile(D_MODEL), tile(ATTN_WIDTH), tile(CONV_WIDTH),
                  pl.BlockSpec((halo, CONV_WIDTH),
                               lambda i: (jnp.maximum(i * (tm // halo) - 1, 0), 0)),
                  pl.BlockSpec((halo, CONV_WIDTH),
                               lambda i: (jnp.minimum((i + 1) * (tm // halo), n_halo - 1), 0)),
                  tile(CONV_WIDTH),
                  pl.BlockSpec((None, 1, 6 * D_MODEL), row)] + [full(a) for a in small],
        out_specs=[tile(D_MODEL), pl.BlockSpec((tm * F32_SUBLANES, LANES), lambda i: (i, 0)),
                   tile(LANES)],
        out_shape=[jax.ShapeDtypeStruct((t, D_MODEL), f32),
                   jax.ShapeDtypeStruct((t * F32_SUBLANES, LANES), f32),
                   jax.ShapeDtypeStruct((t, LANES), f32)],
        compiler_params=_params("arbitrary"),
        name="post",
    )(x, attn, u, u, u, bg, mod3, *small)


def _max_chunks(blk):
    n = -(-TOP_K * blk // MOE_CHUNK) + EXPERTS_PER_STEP
    return n + n % 2


def _dispatch_tables(route, blk):
    nblk = route.shape[0] // blk
    n_slots = TOP_K * blk
    flat = lambda a: a.reshape(nblk, blk, TOP_K).transpose(0, 2, 1).reshape(nblk, n_slots)
    experts = flat(route[:, 0:TOP_K].astype(jnp.int32))
    weights = flat(route[:, TOP_K:2 * TOP_K])
    order = jnp.argsort(experts, axis=1, stable=True).astype(jnp.int32)
    w_sorted = jnp.take_along_axis(weights, order, axis=1)
    bounds = jnp.arange(N_EXPERTS + 1, dtype=jnp.int32)
    offs = jnp.sum(experts[:, :, None] < bounds[None, None, :], axis=1, dtype=jnp.int32)
    tail = ((0, 0), (0, MOE_CHUNK))
    src_rows = jnp.pad((order & (blk - 1)) * F32_SUBLANES, tail)
    dst_rows = jnp.pad(order * F32_SUBLANES, tail, constant_values=n_slots * F32_SUBLANES)
    w_sorted = jnp.pad(w_sorted, tail)

    eg, n_steps, max_c = EXPERTS_PER_STEP, N_EXPERTS // EXPERTS_PER_STEP, _max_chunks(blk)
    lo = offs[:, :N_EXPERTS].reshape(nblk, n_steps, eg)
    n_ch = (offs[:, 1:] - offs[:, :N_EXPERTS] + MOE_CHUNK - 1) // MOE_CHUNK
    n_ch = n_ch.reshape(nblk, n_steps, eg)
    cum = jnp.cumsum(n_ch, axis=2)
    total = cum[..., -1]
    pos = jnp.arange(max_c, dtype=jnp.int32)
    j = jnp.sum(cum[:, :, None, :] <= pos[None, None, :, None], axis=-1, dtype=jnp.int32)
    j = jnp.minimum(j, eg - 1)
    first = jnp.take_along_axis(cum - n_ch, j, axis=2)
    base = jnp.take_along_axis(lo, j, axis=2) + (pos - first) * MOE_CHUNK
    valid = pos < total[..., None]
    base = jnp.where(valid, base, n_slots).reshape(nblk, 1, n_steps * max_c)
    j = jnp.where(valid, j, 0).reshape(nblk, 1, n_steps * max_c)
    n_pairs = ((total + 1) // 2)[:, None, :]
    return src_rows[:, None, :], dst_rows[:, None, :], w_sorted[:, None, :], j, base, n_pairs


def _moe_kernel(src_ref, dst_ref, wts_ref, cj_ref, cbase_g_ref, cbase_s_ref, npairs_ref,
                h_ref, wg_ref, wu_ref, wd_ref, y_ref, ytmp_ref, *stage_refs, blk):
    n_col = D_MODEL // LANES
    step = pl.program_id(1)
    max_c = _max_chunks(blk)

    def gather(base, xg_ref):
        for r in range(MOE_CHUNK):
            src = pl.multiple_of(src_ref[0, base + r], F32_SUBLANES)
            xg_ref[pl.ds(r, F32_SUBLANES, stride=CHUNK_PITCH), :] = h_ref[pl.ds(src, F32_SUBLANES), :]

    def experts_mlp(j, xg_ref, og_ref):
        x = jnp.concatenate([xg_ref[c * CHUNK_PITCH:c * CHUNK_PITCH + MOE_CHUNK, :]
                             for c in range(n_col)], axis=1).astype(bf16)
        g = _dot(x, wg_ref[j])
        up = _dot(x, wu_ref[j])
        act = (g * jax.nn.sigmoid(g)) * up
        out = _dot(act.astype(bf16), wd_ref[j])
        for c in range(n_col):
            og_ref[c * CHUNK_PITCH:c * CHUNK_PITCH + MOE_CHUNK, :] = out[:, c * LANES:(c + 1) * LANES]

    def scatter(base, og_ref):
        for r in range(MOE_CHUNK):
            dst = pl.multiple_of(dst_ref[0, base + r], F32_SUBLANES)
            ytmp_ref[pl.ds(dst, F32_SUBLANES), :] = (
                og_ref[pl.ds(r, F32_SUBLANES, stride=CHUNK_PITCH), :] * wts_ref[0, base + r])

    def pair(pi, carry):
        chunks = [(step * max_c + 2 * pi + half, stage_refs[2 * half], stage_refs[2 * half + 1])
                  for half in range(2)]
        for p, xg_ref, _ in chunks:
            gather(cbase_g_ref[0, p], xg_ref)
        for p, xg_ref, og_ref in chunks:
            experts_mlp(cj_ref[0, p], xg_ref, og_ref)
        for p, _, og_ref in chunks:
            scatter(cbase_s_ref[0, p], og_ref)
        return carry

    lax.fori_loop(0, npairs_ref[0, step], pair, 0)

    @pl.when(pl.program_id(1) == pl.num_programs(1) - 1)
    def _():
        for c in range(n_col):
            acc = ytmp_ref[pl.ds(c, blk, stride=F32_SUBLANES), :]
            for k in range(1, TOP_K):
                acc = acc + ytmp_ref[pl.ds(k * blk * F32_SUBLANES + c, blk, stride=F32_SUBLANES), :]
            y_ref[:, c * LANES:(c + 1) * LANES] = acc.astype(y_ref.dtype)


def _final_kernel(x1_ref, y_ref, mod_ref, fg_ref, o_ref):
    x2 = x1_ref[...] + mod_ref[:, G2:G2 + D_MODEL] * y_ref[...].astype(f32)
    o_ref[...] = _rms(x2) * fg_ref[...]


def _moe(h2_tiles, route, w_gate, w_up, w_down, *, blk=MOE_BLOCK):
    t = route.shape[0]
    src_rows, dst_rows, w_sorted, cj, cbase, n_pairs = _dispatch_tables(route, blk)
    n_slots = TOP_K * blk
    eg, n_steps = EXPERTS_PER_STEP, N_EXPERTS // EXPERTS_PER_STEP
    smem = lambda n: pl.BlockSpec((None, 1, n), lambda b, g: (b, 0, 0), memory_space=pltpu.SMEM)
    stage = pltpu.VMEM((D_MODEL // LANES * CHUNK_PITCH, LANES), f32)
    return pl.pallas_call(
        functools.partial(_moe_kernel, blk=blk),
        grid=(t // blk, n_steps),
        in_specs=[smem(n_slots + MOE_CHUNK)] * 3 + [smem(n_steps * _max_chunks(blk))] * 3 + [
                  smem(n_steps),
                  pl.BlockSpec((blk * F32_SUBLANES, LANES), lambda b, g: (b, 0)),
                  pl.BlockSpec((eg, D_MODEL, D_EXPERT), lambda b, g: (g, 0, 0)),
                  pl.BlockSpec((eg, D_MODEL, D_EXPERT), lambda b, g: (g, 0, 0)),
                  pl.BlockSpec((eg, D_EXPERT, D_MODEL), lambda b, g: (g, 0, 0))],
        out_specs=pl.BlockSpec((blk, D_MODEL), lambda b, g: (b, 0)),
        out_shape=jax.ShapeDtypeStruct((t, D_MODEL), bf16),
        scratch_shapes=[pltpu.VMEM(((n_slots + 1) * F32_SUBLANES, LANES), f32)] + [stage] * 4,
        compiler_params=_params("arbitrary", "arbitrary", vmem=MOE_VMEM_LIMIT_BYTES),
        name="moe",
    )(src_rows, dst_rows, w_sorted, cj, cbase, cbase, n_pairs, h2_tiles, w_gate, w_up, w_down)


def _final(x1, y, mod3, final_g, *, seq, mod_row0, tm=512):
    t = x1.shape[0]
    tiles_per_seq = max(seq // tm, 1)
    row = lambda i: (mod_row0 + (i // tiles_per_seq if mod_row0 else 0), 0, 0)
    tile = pl.BlockSpec((tm, D_MODEL), lambda i: (i, 0))
    return pl.pallas_call(
        _final_kernel,
        grid=(t // tm,),
        in_specs=[tile, tile, pl.BlockSpec((None, 1, 6 * D_MODEL), row),
                  pl.BlockSpec((1, D_MODEL), lambda i: (0, 0))],
        out_specs=tile,
        out_shape=jax.ShapeDtypeStruct((t, D_MODEL), f32),
        compiler_params=_params("arbitrary"),
        name="final",
    )(x1, y, mod3, final_g)


def _trunk(x, mod3, p, *, seq, mod_row0, ctx_kv):
    nb = x.shape[0] // seq
    latent = ctx_kv is not None
    q, k, v, u, bg = _inproj(x, mod3, p["norm1_g"], p["w_in"], seq=seq, mod_row0=mod_row0,
                             kv_dtype=bf16 if latent else f32,
                             rope_tabs=_rope_tables(seq) if latent else None)
    if latent:
        attn = _latent_attention(q, k, v, ctx_kv[0], ctx_kv[1], p["sink"], seq=seq)
        rows = np.concatenate([LATENT_HEAD_PERM, np.arange(ATTN_WIDTH, ATTN_WIDTH + CONV_WIDTH)])
        p = dict(p, attn_out_g=p["attn_out_g"][:, LATENT_HEAD_PERM], w_o=p["w_o"][rows])
    else:
        attn = _context_attention(q, k, v, p["sink"], seq=seq)
    x1, h2_tiles, route = _post(x, attn, u, bg, mod3, p, seq=seq, mod_row0=mod_row0)
    y = _moe(h2_tiles, route, p["w_gate"], p["w_up"], p["w_down"])
    out = _final(x1, y, mod3, p["final_g"], seq=seq, mod_row0=mod_row0)
    return out.reshape(nb, seq, D_MODEL), k, v


def kernel(x_prompt, x_sample, cache_k, cache_v, c, c_ctx, w_mod, b_mod, norm1_g, w_in, conv_w,
           conv_b, sink, attn_out_g, conv_out_g, w_o, norm2_g, w_coarse, b_coarse, w_fine, b_fine,
           w_gate, w_up, w_down, final_g):
    batch, seq, _ = x_prompt.shape
    dec_batch, dec_seq, _ = x_sample.shape
    past = cache_k.shape[2]
    assert w_mod.shape[0] == 1 and 1 + dec_batch <= MOD_ROWS

    cvecs = jnp.concatenate([c_ctx[None], c, jnp.zeros((MOD_ROWS - 1 - dec_batch, D_MODEL), f32)])
    mod3 = _modulation(cvecs, w_mod[0], b_mod[0]).reshape(MOD_ROWS, 1, 6 * D_MODEL)

    pad = jnp.zeros((D_MODEL, LANES - N_EXPERTS - N_GROUPS), f32)
    p = {
        "norm1_g": norm1_g, "w_in": w_in[0].astype(bf16), "conv_w": conv_w[0], "conv_b": conv_b,
        "sink": sink[0], "attn_out_g": attn_out_g, "conv_out_g": conv_out_g,
        "w_o": w_o[0].astype(bf16), "norm2_g": norm2_g,
        "w_router": _split_bf16(jnp.concatenate([w_fine[0], w_coarse[0], pad], axis=1)),
        "b_router": jnp.concatenate([b_fine[0], b_coarse[0], pad[0]])[None],
        "w_gate": w_gate[0].astype(bf16), "w_up": w_up[0].astype(bf16),
        "w_down": w_down[0].astype(bf16), "final_g": final_g[None],
    }

    y_prompt, k_p, v_p = _trunk(x_prompt.reshape(batch * seq, D_MODEL), mod3, p, seq=seq,
                                mod_row0=0, ctx_kv=None)
    new_k = k_p.reshape(batch, 1, seq, N_KV_HEADS, HEAD_DIM)
    new_v = v_p.reshape(batch, 1, seq, N_KV_HEADS, HEAD_DIM)

    ctx_kv = (cache_k[:, 0].reshape(dec_batch, past, KV_WIDTH),
              cache_v[:, 0].reshape(dec_batch, past, KV_WIDTH))
    y_sample, _, _ = _trunk(x_sample.reshape(dec_batch * dec_seq, D_MODEL), mod3, p, seq=dec_seq,
                            mod_row0=1, ctx_kv=ctx_kv)
    return y_prompt, y_sample, new_k, new_v
```

```python
import functools

import jax
import jax.numpy as jnp
import numpy as np
from jax import lax
from jax.experimental import pallas as pl
from jax.experimental.pallas import tpu as pltpu

D_MODEL = 1024
HEAD_DIM = 64
ATTN_WIDTH = 512
N_HEADS = 8
N_KV_HEADS = 2
Q_PER_KV = 4
KV_WIDTH = 128
CONV_WIDTH = 512
CONV_K = 3
WINDOW = 128
GRID_W = 64
ROPE_BASE = 10000.0
N_FREQ = 16
N_GROUPS = 4
EXPERTS_PER_GROUP = 8
N_EXPERTS = 32
TOP_K = 2
D_EXPERT = 256
IN_WIDTH = ATTN_WIDTH + 2 * KV_WIDTH + 3 * CONV_WIDTH
EPS = 1e-6
NEG = -1e30
SCALE = HEAD_DIM ** -0.5
LOG2_E = 1.4426950408889634

LANES = 128
F32_SUBLANES = 8
BF16_SUBLANES = 16
assert D_MODEL == F32_SUBLANES * LANES

MOE_BLOCK = 2048
MOE_CHUNK = 160
CHUNK_PITCH = MOE_CHUNK + F32_SUBLANES
EXPERTS_PER_STEP = 4
MOE_VMEM_LIMIT_BYTES = 60 * 1024 * 1024
VMEM_LIMIT_BYTES = 48 * 1024 * 1024

SH1, SC1, G1, SH2, SC2, G2 = (i * D_MODEL for i in range(6))
MOD_ROWS = 8

COARSE_LANE0 = N_EXPERTS

f32 = jnp.float32
bf16 = jnp.bfloat16


def _params(*semantics, vmem=VMEM_LIMIT_BYTES):
    return pltpu.CompilerParams(dimension_semantics=semantics, vmem_limit_bytes=vmem)


def _rms(x):
    return x * lax.rsqrt(jnp.mean(x * x, axis=-1, keepdims=True) + EPS)


def _dot(a, b):
    return jnp.dot(a, b, preferred_element_type=f32)


def _split_bf16(w):
    hi = w.astype(bf16)
    lo = (w - hi.astype(f32)).astype(bf16)
    return jnp.concatenate([hi, lo], axis=1)


def _dot_nt(a, b):
    return lax.dot_general(a, b, (((1,), (1,)), ((), ())), preferred_element_type=f32)


def _mod_kernel(cv_ref, w_ref, b_ref, o_ref):
    a = cv_ref[...]
    a = a * jax.nn.sigmoid(a)
    o_ref[...] = jnp.dot(a, w_ref[...], precision=lax.Precision.HIGHEST,
                         preferred_element_type=f32) + b_ref[...]


def _modulation(cvecs, w_mod, b_mod):
    tn = D_MODEL
    return pl.pallas_call(
        _mod_kernel,
        grid=(6 * D_MODEL // tn,),
        in_specs=[pl.BlockSpec((MOD_ROWS, D_MODEL), lambda j: (0, 0)),
                  pl.BlockSpec((D_MODEL, tn), lambda j: (0, j)),
                  pl.BlockSpec((1, tn), lambda j: (0, j))],
        out_specs=pl.BlockSpec((MOD_ROWS, tn), lambda j: (0, j)),
        out_shape=jax.ShapeDtypeStruct((MOD_ROWS, 6 * D_MODEL), f32),
        compiler_params=_params("arbitrary"),
        name="mod",
    )(cvecs, w_mod, b_mod.reshape(1, -1))


def _swap_halves(x):
    lane = lax.broadcasted_iota(jnp.int32, x.shape, 1)
    up = pltpu.roll(x, LANES - N_FREQ, axis=1)
    dn = pltpu.roll(x, N_FREQ, axis=1)
    return jnp.where((lane % (2 * N_FREQ)) < N_FREQ, up, dn)


def _rope(x, cos, sin):
    parts = []
    for c in range(x.shape[1] // LANES):
        xc = x[:, c * LANES:(c + 1) * LANES]
        parts.append(xc * cos + _swap_halves(xc) * sin)
    return parts[0] if len(parts) == 1 else jnp.concatenate(parts, axis=1)


def _inproj_kernel(*refs, rope):
    if rope:
        x_ref, mod_ref, g_ref, w_ref, cos_ref, sin_ref, q_ref, k_ref, v_ref, u_ref, bg_ref = refs
    else:
        x_ref, mod_ref, g_ref, w_ref, q_ref, k_ref, v_ref, u_ref, bg_ref = refs
    h = _rms(x_ref[...]) * g_ref[...]
    h = h * (1.0 + mod_ref[:, SC1:SC1 + D_MODEL]) + mod_ref[:, SH1:SH1 + D_MODEL]
    z = _dot(h.astype(bf16), w_ref[...])
    o = 0
    q = z[:, o:o + ATTN_WIDTH]; o += ATTN_WIDTH
    k = z[:, o:o + KV_WIDTH]; o += KV_WIDTH
    v = z[:, o:o + KV_WIDTH]; o += KV_WIDTH
    bg = z[:, o:o + CONV_WIDTH]; o += CONV_WIDTH
    cg = z[:, o:o + CONV_WIDTH]; o += CONV_WIDTH
    xin = z[:, o:o + CONV_WIDTH]
    if rope:
        q = _rope(q, cos_ref[...], sin_ref[...])
        k = _rope(k, cos_ref[...], sin_ref[...])
    q_ref[...] = q.astype(q_ref.dtype)
    k_ref[...] = k.astype(k_ref.dtype)
    v_ref[...] = (v.T if rope else v).astype(v_ref.dtype)
    u_ref[...] = (cg * xin).astype(u_ref.dtype)
    bg_ref[...] = bg.astype(bg_ref.dtype)


def _inproj(x, mod3, norm_g, w_in, *, seq, mod_row0, kv_dtype, rope_tabs=None, tm=512):
    t = x.shape[0]
    tiles_per_seq = seq // tm
    row = lambda i: (mod_row0 + (i // tiles_per_seq if mod_row0 else 0), 0, 0)
    in_specs = [pl.BlockSpec((tm, D_MODEL), lambda i: (i, 0)),
                pl.BlockSpec((None, 1, 6 * D_MODEL), row),
                pl.BlockSpec((1, D_MODEL), lambda i: (0, 0)),
                pl.BlockSpec((D_MODEL, IN_WIDTH), lambda i: (0, 0))]
    args = [x, mod3, norm_g, w_in]
    if rope_tabs is not None:
        in_specs += [pl.BlockSpec((tm, LANES), lambda i: (i % tiles_per_seq, 0))] * 2
        args += list(rope_tabs)
    widths = (ATTN_WIDTH, KV_WIDTH, KV_WIDTH, CONV_WIDTH, CONV_WIDTH)
    dtypes = (bf16, kv_dtype, kv_dtype, bf16, bf16)
    out_specs = [pl.BlockSpec((tm, w), lambda i: (i, 0)) for w in widths]
    out_shape = [jax.ShapeDtypeStruct((t, w), d) for w, d in zip(widths, dtypes)]
    if rope_tabs is not None:
        out_specs[2] = pl.BlockSpec((None, KV_WIDTH, tm),
                                    lambda i: (i // tiles_per_seq, 0, i % tiles_per_seq))
        out_shape[2] = jax.ShapeDtypeStruct((t // seq, KV_WIDTH, seq), kv_dtype)
    return pl.pallas_call(
        functools.partial(_inproj_kernel, rope=rope_tabs is not None),
        grid=(t // tm,),
        in_specs=in_specs,
        out_specs=out_specs,
        out_shape=out_shape,
        compiler_params=_params("arbitrary"),
        name="inproj_rope" if rope_tabs is not None else "inproj",
    )(*args)


def _rope_tables(n):
    t = np.arange(n)
    inv = ROPE_BASE ** (-np.arange(N_FREQ, dtype=np.float32) / N_FREQ)
    rows = (t // GRID_W).astype(np.float32)
    cols = (t % GRID_W).astype(np.float32)
    d = np.arange(LANES) % HEAD_DIM
    pos = np.where(d[None, :] < HEAD_DIM // 2, rows[:, None], cols[:, None])
    ang = jnp.asarray(pos.astype(np.float32) * inv[d % N_FREQ][None, :])
    sign = np.where((d % (2 * N_FREQ)) < N_FREQ, -1.0, 1.0).astype(np.float32)
    return jnp.cos(ang), jnp.sin(ang) * sign[None, :]


def _attend_group(q_rows, sink_col, keys, vals, masks):
    scores = []
    m = sink_col
    for kk, mk in zip(keys, masks):
        s = _dot_nt(q_rows, kk) * SCALE
        if mk is not None:
            s = jnp.where(mk, s, NEG)
        scores.append(s)
        m = jnp.maximum(m, jnp.max(s, axis=-1, keepdims=True))
    den = jnp.exp(sink_col - m)
    acc = None
    for s, vv in zip(scores, vals):
        p = jnp.exp(s - m)
        den = den + jnp.sum(p, axis=-1, keepdims=True)
        pv = _dot(p.astype(bf16), vv)
        acc = pv if acc is None else acc + pv
    return acc / den


def _heads_attention(q, sink_ref, key_sets, val_sets, masks):
    bq = q.shape[0]
    ridx = lax.broadcasted_iota(jnp.int32, (Q_PER_KV * bq, 1), 0)
    outs = []
    for g in range(N_KV_HEADS):
        heads = range(g * Q_PER_KV, (g + 1) * Q_PER_KV)
        q_rows = jnp.concatenate([q[:, h * HEAD_DIM:(h + 1) * HEAD_DIM] for h in heads], axis=0)
        sink_col = jnp.full((Q_PER_KV * bq, 1), sink_ref[g * Q_PER_KV], f32)
        for j in range(1, Q_PER_KV):
            sink_col = jnp.where(ridx >= j * bq, sink_ref[g * Q_PER_KV + j], sink_col)
        lo, hi = g * HEAD_DIM, (g + 1) * HEAD_DIM
        o = _attend_group(q_rows, sink_col, [kk[:, lo:hi] for kk in key_sets],
                          [vv[:, lo:hi] for vv in val_sets], masks)
        outs += [o[j * bq:(j + 1) * bq] for j in range(Q_PER_KV)]
    return jnp.concatenate(outs, axis=1)


def _ctx_attn_kernel(sink_ref, q_ref, k_ref, v_ref, o_ref):
    k = k_ref[...].astype(bf16)
    v = v_ref[...].astype(bf16)
    o_ref[...] = _heads_attention(q_ref[...], sink_ref, [k], [v], [None]).astype(o_ref.dtype)


def _context_attention(q, k, v, sink, *, seq):
    t = q.shape[0]
    return pl.pallas_call(
        _ctx_attn_kernel,
        grid=(t // seq,),
        in_specs=[pl.BlockSpec(memory_space=pltpu.SMEM),
                  pl.BlockSpec((seq, ATTN_WIDTH), lambda b: (b, 0)),
                  pl.BlockSpec((seq, KV_WIDTH), lambda b: (b, 0)),
                  pl.BlockSpec((seq, KV_WIDTH), lambda b: (b, 0))],
        out_specs=pl.BlockSpec((seq, ATTN_WIDTH), lambda b: (b, 0)),
        out_shape=jax.ShapeDtypeStruct((t, ATTN_WIDTH), bf16),
        compiler_params=_params("arbitrary"),
        name="ctx_attn",
    )(sink, q, k, v)


def _lat_attn_kernel(sink_ref, q_ref, k_ref, vt_ref, ck_ref, cv_ref, o_ref, *, bq, seq):
    band = bq + 2 * WINDOW
    n_col = Q_PER_KV * bq
    ck = ck_ref[...].astype(bf16)
    cvt = cv_ref[...].T.astype(bf16)
    delta = ((lax.broadcasted_iota(jnp.int32, (band, n_col), 1) & (bq - 1))
             - lax.broadcasted_iota(jnp.int32, (band, n_col), 0))
    col_head = lax.broadcasted_iota(jnp.int32, (1, n_col), 1) // bq
    sink_rows = []
    for g in range(N_KV_HEADS):
        sink_row = jnp.full((1, n_col), sink_ref[g * Q_PER_KV], f32)
        for j in range(1, Q_PER_KV):
            sink_row = jnp.where(col_head == j, sink_ref[g * Q_PER_KV + j], sink_row)
        sink_rows.append(sink_row * LOG2_E)
    ones_rows = jnp.ones((BF16_SUBLANES, 1), bf16)
    cvt_aug = [jnp.concatenate([cvt[g * HEAD_DIM:(g + 1) * HEAD_DIM, :],
                                jnp.broadcast_to(ones_rows, (BF16_SUBLANES, cvt.shape[1]))], axis=0)
               for g in range(N_KV_HEADS)]

    n_sub = q_ref.shape[0] // bq
    subs = []
    for sb in range(n_sub):
        i = pl.program_id(1) * n_sub + sb
        start = pl.multiple_of(jnp.clip(i * bq - WINDOW, 0, seq - band), LANES)
        shifted = delta + (i * bq - start + WINDOW)
        subs.append(dict(
            kb=k_ref[pl.ds(start, band), :],
            vbt=vt_ref[:, pl.ds(start, band)],
            qt=(q_ref[sb * bq:(sb + 1) * bq, :].astype(f32) * (SCALE * LOG2_E)).T.astype(bf16),
            mask=shifted.astype(jnp.uint32) <= 2 * WINDOW))

    def scores(sub, g):
        lo, hi = g * HEAD_DIM, (g + 1) * HEAD_DIM
        heads = range(g * Q_PER_KV, (g + 1) * Q_PER_KV)
        qt_g = jnp.concatenate([sub["qt"][h * HEAD_DIM:(h + 1) * HEAD_DIM, :] for h in heads],
                               axis=1)
        s_loc = jnp.where(sub["mask"], _dot(sub["kb"][:, lo:hi], qt_g), NEG)
        s_ctx = _dot(ck[:, lo:hi], qt_g)
        return s_loc, s_ctx

    def weighted_values(sub, g, s_loc, s_ctx):
        lo, hi = g * HEAD_DIM, (g + 1) * HEAD_DIM
        m = jnp.maximum(jnp.maximum(jnp.max(s_loc, axis=0, keepdims=True),
                                    jnp.max(s_ctx, axis=0, keepdims=True)), sink_rows[g])
        p_loc = jnp.exp2(s_loc - m).astype(bf16)
        p_ctx = jnp.exp2(s_ctx - m).astype(bf16)
        vbt_aug = jnp.concatenate(
            [sub["vbt"][lo:hi, :], jnp.broadcast_to(ones_rows, (BF16_SUBLANES, band))], axis=0)
        o_aug = _dot(vbt_aug, p_loc) + _dot(cvt_aug[g], p_ctx)
        den = o_aug[HEAD_DIM:HEAD_DIM + 1, :] + jnp.exp2(sink_rows[g] - m)
        return o_aug[:HEAD_DIM, :] / den

    passes = [(sb, g) for sb in range(n_sub) for g in range(N_KV_HEADS)]
    out_t = {}
    pending = scores(subs[0], 0)
    for n, (sb, g) in enumerate(passes):
        nxt = scores(subs[passes[n + 1][0]], passes[n + 1][1]) if n + 1 < len(passes) else None
        out_t[sb, g] = weighted_values(subs[sb], g, *pending)
        pending = nxt
    for sb in range(n_sub):
        blocks = [jnp.concatenate([out_t[sb, g][:, j * bq:(j + 1) * bq]
                                   for g in range(N_KV_HEADS)], axis=0).T
                  for j in range(Q_PER_KV)]
        o_ref[sb * bq:(sb + 1) * bq, :] = jnp.concatenate(blocks, axis=1).astype(o_ref.dtype)


LATENT_HEAD_PERM = np.array([(Q_PER_KV * g + j) * HEAD_DIM + d for j in range(Q_PER_KV)
                             for g in range(N_KV_HEADS) for d in range(HEAD_DIM)])


def _latent_attention(q, k, vt, ck, cv, sink, *, seq, bq=128, sub_blocks=4):
    nb, past = ck.shape[0], ck.shape[1]
    assert bq == LANES and N_KV_HEADS * HEAD_DIM == LANES
    q3, k3 = (a.reshape(nb, seq, a.shape[-1]) for a in (q, k))
    tq = bq * sub_blocks
    out = pl.pallas_call(
        functools.partial(_lat_attn_kernel, bq=bq, seq=seq),
        grid=(nb, seq // tq),
        in_specs=[pl.BlockSpec(memory_space=pltpu.SMEM),
                  pl.BlockSpec((None, tq, ATTN_WIDTH), lambda b, i: (b, i, 0)),
                  pl.BlockSpec((None, seq, KV_WIDTH), lambda b, i: (b, 0, 0)),
                  pl.BlockSpec((None, KV_WIDTH, seq), lambda b, i: (b, 0, 0)),
                  pl.BlockSpec((None, past, KV_WIDTH), lambda b, i: (b, 0, 0)),
                  pl.BlockSpec((None, past, KV_WIDTH), lambda b, i: (b, 0, 0))],
        out_specs=pl.BlockSpec((None, tq, ATTN_WIDTH), lambda b, i: (b, i, 0)),
        out_shape=jax.ShapeDtypeStruct((nb, seq, ATTN_WIDTH), bf16),
        compiler_params=_params("arbitrary", "arbitrary"),
        name="lat_attn",
    )(sink, q3, k3, vt, ck, cv)
    return out.reshape(nb * seq, ATTN_WIDTH)


def _route(logits):
    lane = lax.broadcasted_iota(jnp.int32, logits.shape, 1)
    lane_f = lane.astype(f32)
    big = jnp.float32(LANES)
    is_c = jnp.logical_and(lane >= COARSE_LANE0, lane < COARSE_LANE0 + N_GROUPS)
    lc = jnp.where(is_c, logits, -jnp.inf)
    mc = jnp.max(lc, axis=-1, keepdims=True)
    grp = jnp.min(jnp.where(lc == mc, lane_f, big), axis=-1, keepdims=True) - COARSE_LANE0
    pg = 1.0 / jnp.sum(jnp.exp(lc - mc), axis=-1, keepdims=True)
    in_g = jnp.floor(lane_f * (1.0 / EXPERTS_PER_GROUP)) == grp
    fl = jnp.where(in_g, logits, -jnp.inf)
    m1 = jnp.max(fl, axis=-1, keepdims=True)
    i1 = jnp.min(jnp.where(fl == m1, lane_f, big), axis=-1, keepdims=True)
    fl2 = jnp.where(lane_f == i1, -jnp.inf, fl)
    m2 = jnp.max(fl2, axis=-1, keepdims=True)
    i2 = jnp.min(jnp.where(fl2 == m2, lane_f, big), axis=-1, keepdims=True)
    e2 = jnp.exp(m2 - m1)
    p1 = pg / (1.0 + e2)
    p2 = pg * e2 / (1.0 + e2)
    packed = jnp.where(lane == 0, i1, jnp.where(lane == 1, i2, jnp.where(lane == 2, p1, p2)))
    return jnp.where(lane < 4, packed, 0.0)


def _post_kernel(x_ref, attn_ref, u_ref, up_ref, un_ref, bg_ref, mod_ref, ag_ref, cg_ref, cw_ref,
                 cb_ref, wo_ref, n2_ref, wr_ref, br_ref, x1_ref, h2_ref, route_ref, *, tm, seq):
    i = pl.program_id(0)
    u = u_ref[...].astype(f32)
    rows = lax.broadcasted_iota(jnp.int32, (tm, 1), 0)
    spos = (i * tm + rows) % seq
    u_dn = jnp.where(rows == 0, up_ref[...].astype(f32)[BF16_SUBLANES - 1:, :],
                     pltpu.roll(u, 1, axis=0))
    u_dn = jnp.where(spos == 0, 0.0, u_dn)
    u_up = jnp.where(rows == tm - 1, un_ref[...].astype(f32)[0:1, :], pltpu.roll(u, tm - 1, axis=0))
    u_up = jnp.where(spos == seq - 1, 0.0, u_up)
    y = u_dn * cw_ref[0:1, :] + u * cw_ref[1:2, :] + u_up * cw_ref[2:3, :] + cb_ref[...]
    conv = bg_ref[...].astype(f32) * y
    attn_n = _rms(attn_ref[...].astype(f32)) * ag_ref[...]
    conv_n = _rms(conv) * cg_ref[...]
    mixed = (_dot(attn_n.astype(bf16), wo_ref[0:ATTN_WIDTH, :])
             + _dot(conv_n.astype(bf16), wo_ref[ATTN_WIDTH:, :]))
    x1 = x_ref[...] + mod_ref[:, G1:G1 + D_MODEL] * mixed
    x1_ref[...] = x1
    h2 = _rms(x1) * n2_ref[...]
    h2 = h2 * (1.0 + mod_ref[:, SC2:SC2 + D_MODEL]) + mod_ref[:, SH2:SH2 + D_MODEL]
    for c in range(D_MODEL // LANES):
        h2_ref[pl.ds(c, tm, stride=F32_SUBLANES), :] = h2[:, c * LANES:(c + 1) * LANES]
    h2_hi = h2.astype(bf16)
    h2_lo = (h2 - h2_hi.astype(f32)).astype(bf16)
    both = _dot(h2_hi, wr_ref[...])
    logits = (both[:, :LANES] + both[:, LANES:] + _dot(h2_lo, wr_ref[:, :LANES])) + br_ref[...]
    route_ref[...] = _route(logits)


def _post(x, attn, u, bg, mod3, p, *, seq, mod_row0, tm=256):
    t = x.shape[0]
    tiles_per_seq = seq // tm
    halo = BF16_SUBLANES
    n_halo = t // halo
    row = lambda i: (mod_row0 + (i // tiles_per_seq if mod_row0 else 0), 0, 0)
    tile = lambda w: pl.BlockSpec((tm, w), lambda i: (i, 0))
    full = lambda a: pl.BlockSpec(a.shape, lambda i: (0,) * a.ndim)
    small = [p["attn_out_g"], p["conv_out_g"], p["conv_w"], p["conv_b"], p["w_o"], p["norm2_g"],
             p["w_router"], p["b_router"]]
    return pl.pallas_call(
        functools.partial(_post_kernel, tm=tm, seq=seq),
        grid=(t // tm,),
        in_specs=[tile(D_MODEL), tile(ATTN_WIDTH), tile(CONV_WIDTH),
                  pl.BlockSpec((halo, CONV_WIDTH),
                               lambda i: (jnp.maximum(i * (tm // halo) - 1, 0), 0)),
                  pl.BlockSpec((halo, CONV_WIDTH),
                               lambda i: (jnp.minimum((i + 1) * (tm // halo), n_halo - 1), 0)),
                  tile(CONV_WIDTH),
                  pl.BlockSpec((None, 1, 6 * D_MODEL), row)] + [full(a) for a in small],
        out_specs=[tile(D_MODEL), pl.BlockSpec((tm * F32_SUBLANES, LANES), lambda i: (i, 0)),
                   tile(LANES)],
        out_shape=[jax.ShapeDtypeStruct((t, D_MODEL), f32),
                   jax.ShapeDtypeStruct((t * F32_SUBLANES, LANES), f32),
                   jax.ShapeDtypeStruct((t, LANES), f32)],
        compiler_params=_params("arbitrary"),
        name="post",
    )(x, attn, u, u, u, bg, mod3, *small)


def _max_chunks(blk):
    n = -(-TOP_K * blk // MOE_CHUNK) + EXPERTS_PER_STEP
    return n + n % 2


def _dispatch_tables(route, blk):
    nblk = route.shape[0] // blk
    n_slots = TOP_K * blk
    flat = lambda a: a.reshape(nblk, blk, TOP_K).transpose(0, 2, 1).reshape(nblk, n_slots)
    experts = flat(route[:, 0:TOP_K].astype(jnp.int32))
    weights = flat(route[:, TOP_K:2 * TOP_K])
    slot_ids = lax.broadcasted_iota(jnp.int32, experts.shape, 1)
    _, order, w_sorted = lax.sort((experts, slot_ids, weights), dimension=1, num_keys=1)
    bounds = jnp.arange(N_EXPERTS + 1, dtype=jnp.int32)
    offs = jnp.sum(experts[:, :, None] < bounds[None, None, :], axis=1, dtype=jnp.int32)
    tail = ((0, 0), (0, MOE_CHUNK))
    src_rows = jnp.pad((order & (blk - 1)) * F32_SUBLANES, tail)
    dst_rows = jnp.pad(order * F32_SUBLANES, tail, constant_values=n_slots * F32_SUBLANES)
    w_sorted = jnp.pad(w_sorted, tail)

    eg, n_steps, max_c = EXPERTS_PER_STEP, N_EXPERTS // EXPERTS_PER_STEP, _max_chunks(blk)
    lo = offs[:, :N_EXPERTS].reshape(nblk, n_steps, eg)
    n_ch = (offs[:, 1:] - offs[:, :N_EXPERTS] + MOE_CHUNK - 1) // MOE_CHUNK
    n_ch = n_ch.reshape(nblk, n_steps, eg)
    cum = jnp.cumsum(n_ch, axis=2)
    total = cum[..., -1]
    pos = jnp.arange(max_c, dtype=jnp.int32)
    j = jnp.sum(cum[:, :, None, :] <= pos[None, None, :, None], axis=-1, dtype=jnp.int32)
    j = jnp.minimum(j, eg - 1)
    hit = j[..., None] == jnp.arange(eg, dtype=jnp.int32)
    pick = lambda a: jnp.sum(jnp.where(hit, a[:, :, None, :], 0), axis=-1)
    base = pick(lo) + (pos - pick(cum - n_ch)) * MOE_CHUNK
    valid = pos < total[..., None]
    base = jnp.where(valid, base, n_slots).reshape(nblk, 1, n_steps * max_c)
    j = jnp.where(valid, j, 0).reshape(nblk, 1, n_steps * max_c)
    n_pairs = ((total + 1) // 2)[:, None, :]
    return src_rows[:, None, :], dst_rows[:, None, :], w_sorted[:, None, :], j, base, n_pairs


def _moe_kernel(src_ref, dst_ref, wts_ref, cj_ref, cbase_g_ref, cbase_s_ref, npairs_ref,
                h_ref, wg_ref, wu_ref, wd_ref, y_ref, ytmp_ref, *stage_refs, blk):
    n_col = D_MODEL // LANES
    step = pl.program_id(1)
    max_c = _max_chunks(blk)

    def gather(base, xg_ref):
        for r in range(MOE_CHUNK):
            src = pl.multiple_of(src_ref[0, base + r], F32_SUBLANES)
            xg_ref[pl.ds(r, F32_SUBLANES, stride=CHUNK_PITCH), :] = h_ref[pl.ds(src, F32_SUBLANES), :]

    def experts_mlp(j, xg_ref, og_ref):
        x = jnp.concatenate([xg_ref[c * CHUNK_PITCH:c * CHUNK_PITCH + MOE_CHUNK, :]
                             for c in range(n_col)], axis=1).astype(bf16)
        g = _dot(x, wg_ref[j])
        up = _dot(x, wu_ref[j])
        act = (g * jax.nn.sigmoid(g)) * up
        out = _dot(act.astype(bf16), wd_ref[j])
        for c in range(n_col):
            og_ref[c * CHUNK_PITCH:c * CHUNK_PITCH + MOE_CHUNK, :] = out[:, c * LANES:(c + 1) * LANES]

    def scatter(base, og_ref):
        for r in range(MOE_CHUNK):
            dst = pl.multiple_of(dst_ref[0, base + r], F32_SUBLANES)
            ytmp_ref[pl.ds(dst, F32_SUBLANES), :] = (
                og_ref[pl.ds(r, F32_SUBLANES, stride=CHUNK_PITCH), :] * wts_ref[0, base + r])

    def pair(pi, carry):
        chunks = [(step * max_c + 2 * pi + half, stage_refs[2 * half], stage_refs[2 * half + 1])
                  for half in range(2)]
        for p, xg_ref, _ in chunks:
            gather(cbase_g_ref[0, p], xg_ref)
        for p, xg_ref, og_ref in chunks:
            experts_mlp(cj_ref[0, p], xg_ref, og_ref)
        for p, _, og_ref in chunks:
            scatter(cbase_s_ref[0, p], og_ref)
        return carry

    lax.fori_loop(0, npairs_ref[0, step], pair, 0)

    @pl.when(pl.program_id(1) == pl.num_programs(1) - 1)
    def _():
        n = blk * F32_SUBLANES
        for k in range(1, TOP_K):
            ytmp_ref[0:n, :] = ytmp_ref[0:n, :] + ytmp_ref[k * n:(k + 1) * n, :]
        for c in range(n_col):
            y_ref[:, c * LANES:(c + 1) * LANES] = (
                ytmp_ref[pl.ds(c, blk, stride=F32_SUBLANES), :].astype(y_ref.dtype))


def _final_kernel(x1_ref, y_ref, mod_ref, fg_ref, o_ref):
    x2 = x1_ref[...] + mod_ref[:, G2:G2 + D_MODEL] * y_ref[...].astype(f32)
    o_ref[...] = _rms(x2) * fg_ref[...]


def _moe(h2_tiles, tables, w_gate, w_up, w_down, *, blk=MOE_BLOCK):
    t = h2_tiles.shape[0] // F32_SUBLANES
    src_rows, dst_rows, w_sorted, cj, cbase, n_pairs = tables
    n_slots = TOP_K * blk
    eg, n_steps = EXPERTS_PER_STEP, N_EXPERTS // EXPERTS_PER_STEP
    smem = lambda n: pl.BlockSpec((None, 1, n), lambda b, g: (b, 0, 0), memory_space=pltpu.SMEM)
    stage = pltpu.VMEM((D_MODEL // LANES * CHUNK_PITCH, LANES), f32)
    return pl.pallas_call(
        functools.partial(_moe_kernel, blk=blk),
        grid=(t // blk, n_steps),
        in_specs=[smem(n_slots + MOE_CHUNK)] * 3 + [smem(n_steps * _max_chunks(blk))] * 3 + [
                  smem(n_steps),
                  pl.BlockSpec((blk * F32_SUBLANES, LANES), lambda b, g: (b, 0)),
                  pl.BlockSpec((eg, D_MODEL, D_EXPERT), lambda b, g: (g, 0, 0)),
                  pl.BlockSpec((eg, D_MODEL, D_EXPERT), lambda b, g: (g, 0, 0)),
                  pl.BlockSpec((eg, D_EXPERT, D_MODEL), lambda b, g: (g, 0, 0))],
        out_specs=pl.BlockSpec((blk, D_MODEL), lambda b, g: (b, 0)),
        out_shape=jax.ShapeDtypeStruct((t, D_MODEL), bf16),
        scratch_shapes=[pltpu.VMEM(((n_slots + 1) * F32_SUBLANES, LANES), f32)] + [stage] * 4,
        compiler_params=_params("arbitrary", "arbitrary", vmem=MOE_VMEM_LIMIT_BYTES),
        name="moe",
    )(src_rows, dst_rows, w_sorted, cj, cbase, cbase, n_pairs, h2_tiles, w_gate, w_up, w_down)


def _final(x1, y, mod3, final_g, *, seq, mod_row0, tm=512):
    t = x1.shape[0]
    tiles_per_seq = max(seq // tm, 1)
    row = lambda i: (mod_row0 + (i // tiles_per_seq if mod_row0 else 0), 0, 0)
    tile = pl.BlockSpec((tm, D_MODEL), lambda i: (i, 0))
    return pl.pallas_call(
        _final_kernel,
        grid=(t // tm,),
        in_specs=[tile, tile, pl.BlockSpec((None, 1, 6 * D_MODEL), row),
                  pl.BlockSpec((1, D_MODEL), lambda i: (0, 0))],
        out_specs=tile,
        out_shape=jax.ShapeDtypeStruct((t, D_MODEL), f32),
        compiler_params=_params("arbitrary"),
        name="final",
    )(x1, y, mod3, final_g)


def _mixers(x, mod3, p, *, seq, mod_row0, ctx_kv):
    latent = ctx_kv is not None
    q, k, v, u, bg = _inproj(x, mod3, p["norm1_g"], p["w_in"], seq=seq, mod_row0=mod_row0,
                             kv_dtype=bf16 if latent else f32,
                             rope_tabs=_rope_tables(seq) if latent else None)
    if latent:
        attn = _latent_attention(q, k, v, ctx_kv[0], ctx_kv[1], p["sink"], seq=seq)
        rows = np.concatenate([LATENT_HEAD_PERM, np.arange(ATTN_WIDTH, ATTN_WIDTH + CONV_WIDTH)])
        p = dict(p, attn_out_g=p["attn_out_g"][:, LATENT_HEAD_PERM], w_o=p["w_o"][rows])
    else:
        attn = _context_attention(q, k, v, p["sink"], seq=seq)
    x1, h2_tiles, route = _post(x, attn, u, bg, mod3, p, seq=seq, mod_row0=mod_row0)
    return x1, h2_tiles, route[:, :2 * TOP_K], k, v


def _experts(x1, h2_tiles, tables, mod3, p, *, seq, mod_row0):
    y = _moe(h2_tiles, tables, p["w_gate"], p["w_up"], p["w_down"])
    out = _final(x1, y, mod3, p["final_g"], seq=seq, mod_row0=mod_row0)
    return out.reshape(-1, seq, D_MODEL)


def kernel(x_prompt, x_sample, cache_k, cache_v, c, c_ctx, w_mod, b_mod, norm1_g, w_in, conv_w,
           conv_b, sink, attn_out_g, conv_out_g, w_o, norm2_g, w_coarse, b_coarse, w_fine, b_fine,
           w_gate, w_up, w_down, final_g):
    batch, seq, _ = x_prompt.shape
    dec_batch, dec_seq, _ = x_sample.shape
    past = cache_k.shape[2]
    assert w_mod.shape[0] == 1 and 1 + dec_batch <= MOD_ROWS

    cvecs = jnp.concatenate([c_ctx[None], c, jnp.zeros((MOD_ROWS - 1 - dec_batch, D_MODEL), f32)])
    mod3 = _modulation(cvecs, w_mod[0], b_mod[0]).reshape(MOD_ROWS, 1, 6 * D_MODEL)

    pad = jnp.zeros((D_MODEL, LANES - N_EXPERTS - N_GROUPS), f32)
    p = {
        "norm1_g": norm1_g, "w_in": w_in[0].astype(bf16), "conv_w": conv_w[0], "conv_b": conv_b,
        "sink": sink[0], "attn_out_g": attn_out_g, "conv_out_g": conv_out_g,
        "w_o": w_o[0].astype(bf16), "norm2_g": norm2_g,
        "w_router": _split_bf16(jnp.concatenate([w_fine[0], w_coarse[0], pad], axis=1)),
        "b_router": jnp.concatenate([b_fine[0], b_coarse[0], pad[0]])[None],
        "w_gate": w_gate[0].astype(bf16), "w_up": w_up[0].astype(bf16),
        "w_down": w_down[0].astype(bf16), "final_g": final_g[None],
    }

    x1_p, h2_p, route_p, k_p, v_p = _mixers(x_prompt.reshape(batch * seq, D_MODEL), mod3, p,
                                            seq=seq, mod_row0=0, ctx_kv=None)
    new_k = k_p.reshape(batch, 1, seq, N_KV_HEADS, HEAD_DIM)
    new_v = v_p.reshape(batch, 1, seq, N_KV_HEADS, HEAD_DIM)

    ctx_kv = (cache_k[:, 0].reshape(dec_batch, past, KV_WIDTH),
              cache_v[:, 0].reshape(dec_batch, past, KV_WIDTH))
    x1_s, h2_s, route_s, _, _ = _mixers(x_sample.reshape(dec_batch * dec_seq, D_MODEL), mod3, p,
                                        seq=dec_seq, mod_row0=1, ctx_kv=ctx_kv)

    n_blk_p = batch * seq // MOE_BLOCK
    tables = _dispatch_tables(jnp.concatenate([route_p, route_s]), MOE_BLOCK)
    y_prompt = _experts(x1_p, h2_p, [t[:n_blk_p] for t in tables], mod3, p, seq=seq, mod_row0=0)
    y_sample = _experts(x1_s, h2_s, [t[n_blk_p:] for t in tables], mod3, p, seq=dec_seq,
                        mod_row0=1)
    return y_prompt, y_sample, new_k, new_v
```

```python
import functools

import jax
import jax.numpy as jnp
import numpy as np
from jax import lax
from jax.experimental import pallas as pl
from jax.experimental.pallas import tpu as pltpu

D_MODEL = 1024
HEAD_DIM = 64
ATTN_WIDTH = 512
N_HEADS = 8
N_KV_HEADS = 2
Q_PER_KV = 4
KV_WIDTH = 128
CONV_WIDTH = 512
CONV_K = 3
WINDOW = 128
GRID_W = 64
ROPE_BASE = 10000.0
N_FREQ = 16
N_GROUPS = 4
EXPERTS_PER_GROUP = 8
N_EXPERTS = 32
TOP_K = 2
D_EXPERT = 256
IN_WIDTH = ATTN_WIDTH + 2 * KV_WIDTH + 3 * CONV_WIDTH
EPS = 1e-6
NEG = -1e30
SCALE = HEAD_DIM ** -0.5
LOG2_E = 1.4426950408889634

LANES = 128
F32_SUBLANES = 8
BF16_SUBLANES = 16
assert D_MODEL == F32_SUBLANES * LANES

MOE_BLOCK = 2048
MOE_CHUNK = 160
CHUNK_PITCH = MOE_CHUNK + F32_SUBLANES
EXPERTS_PER_STEP = 4
FINAL_ROWS = 256
MOE_VMEM_LIMIT_BYTES = 60 * 1024 * 1024
VMEM_LIMIT_BYTES = 48 * 1024 * 1024

SH1, SC1, G1, SH2, SC2, G2 = (i * D_MODEL for i in range(6))
MOD_ROWS = 8

COARSE_LANE0 = N_EXPERTS

f32 = jnp.float32
bf16 = jnp.bfloat16


def _params(*semantics, vmem=VMEM_LIMIT_BYTES):
    return pltpu.CompilerParams(dimension_semantics=semantics, vmem_limit_bytes=vmem)


def _rms(x):
    return x * lax.rsqrt(jnp.mean(x * x, axis=-1, keepdims=True) + EPS)


def _dot(a, b):
    return jnp.dot(a, b, preferred_element_type=f32)


def _split_bf16(w):
    hi = w.astype(bf16)
    lo = (w - hi.astype(f32)).astype(bf16)
    return jnp.concatenate([hi, lo], axis=1)


def _dot_nt(a, b):
    return lax.dot_general(a, b, (((1,), (1,)), ((), ())), preferred_element_type=f32)


def _mod_kernel(cv_ref, w_ref, b_ref, o_ref):
    a = cv_ref[...]
    a = a * jax.nn.sigmoid(a)
    o_ref[...] = jnp.dot(a, w_ref[...], precision=lax.Precision.HIGHEST,
                         preferred_element_type=f32) + b_ref[...]


def _modulation(cvecs, w_mod, b_mod):
    tn = D_MODEL
    return pl.pallas_call(
        _mod_kernel,
        grid=(6 * D_MODEL // tn,),
        in_specs=[pl.BlockSpec((MOD_ROWS, D_MODEL), lambda j: (0, 0)),
                  pl.BlockSpec((D_MODEL, tn), lambda j: (0, j)),
                  pl.BlockSpec((1, tn), lambda j: (0, j))],
        out_specs=pl.BlockSpec((MOD_ROWS, tn), lambda j: (0, j)),
        out_shape=jax.ShapeDtypeStruct((MOD_ROWS, 6 * D_MODEL), f32),
        compiler_params=_params("arbitrary"),
        name="mod",
    )(cvecs, w_mod, b_mod.reshape(1, -1))


def _swap_halves(x):
    lane = lax.broadcasted_iota(jnp.int32, x.shape, 1)
    up = pltpu.roll(x, LANES - N_FREQ, axis=1)
    dn = pltpu.roll(x, N_FREQ, axis=1)
    return jnp.where((lane % (2 * N_FREQ)) < N_FREQ, up, dn)


def _rope(x, cos, sin):
    parts = []
    for c in range(x.shape[1] // LANES):
        xc = x[:, c * LANES:(c + 1) * LANES]
        parts.append(xc * cos + _swap_halves(xc) * sin)
    return parts[0] if len(parts) == 1 else jnp.concatenate(parts, axis=1)


def _inproj_kernel(*refs, rope):
    if rope:
        x_ref, mod_ref, g_ref, w_ref, cos_ref, sin_ref, q_ref, k_ref, v_ref, u_ref, bg_ref = refs
    else:
        x_ref, mod_ref, g_ref, w_ref, q_ref, k_ref, v_ref, u_ref, bg_ref = refs
    h = _rms(x_ref[...]) * g_ref[...]
    h = h * (1.0 + mod_ref[:, SC1:SC1 + D_MODEL]) + mod_ref[:, SH1:SH1 + D_MODEL]
    z = _dot(h.astype(bf16), w_ref[...])
    o = 0
    q = z[:, o:o + ATTN_WIDTH]; o += ATTN_WIDTH
    k = z[:, o:o + KV_WIDTH]; o += KV_WIDTH
    v = z[:, o:o + KV_WIDTH]; o += KV_WIDTH
    bg = z[:, o:o + CONV_WIDTH]; o += CONV_WIDTH
    cg = z[:, o:o + CONV_WIDTH]; o += CONV_WIDTH
    xin = z[:, o:o + CONV_WIDTH]
    if rope:
        q = _rope(q, cos_ref[...], sin_ref[...])
        k = _rope(k, cos_ref[...], sin_ref[...])
    q_ref[...] = q.astype(q_ref.dtype)
    k_ref[...] = k.astype(k_ref.dtype)
    v_ref[...] = (v.T if rope else v).astype(v_ref.dtype)
    u_ref[...] = (cg * xin).astype(u_ref.dtype)
    bg_ref[...] = bg.astype(bg_ref.dtype)


def _inproj(x, mod3, norm_g, w_in, *, seq, mod_row0, kv_dtype, rope_tabs=None, tm=512):
    t = x.shape[0]
    tiles_per_seq = seq // tm
    row = lambda i: (mod_row0 + (i // tiles_per_seq if mod_row0 else 0), 0, 0)
    in_specs = [pl.BlockSpec((tm, D_MODEL), lambda i: (i, 0)),
                pl.BlockSpec((None, 1, 6 * D_MODEL), row),
                pl.BlockSpec((1, D_MODEL), lambda i: (0, 0)),
                pl.BlockSpec((D_MODEL, IN_WIDTH), lambda i: (0, 0))]
    args = [x, mod3, norm_g, w_in]
    if rope_tabs is not None:
        in_specs += [pl.BlockSpec((tm, LANES), lambda i: (i % tiles_per_seq, 0))] * 2
        args += list(rope_tabs)
    widths = (ATTN_WIDTH, KV_WIDTH, KV_WIDTH, CONV_WIDTH, CONV_WIDTH)
    dtypes = (bf16, kv_dtype, kv_dtype, bf16, bf16)
    out_specs = [pl.BlockSpec((tm, w), lambda i: (i, 0)) for w in widths]
    out_shape = [jax.ShapeDtypeStruct((t, w), d) for w, d in zip(widths, dtypes)]
    if rope_tabs is not None:
        out_specs[2] = pl.BlockSpec((None, KV_WIDTH, tm),
                                    lambda i: (i // tiles_per_seq, 0, i % tiles_per_seq))
        out_shape[2] = jax.ShapeDtypeStruct((t // seq, KV_WIDTH, seq), kv_dtype)
    return pl.pallas_call(
        functools.partial(_inproj_kernel, rope=rope_tabs is not None),
        grid=(t // tm,),
        in_specs=in_specs,
        out_specs=out_specs,
        out_shape=out_shape,
        compiler_params=_params("arbitrary"),
        name="inproj_rope" if rope_tabs is not None else "inproj",
    )(*args)


def _rope_tables(n):
    t = np.arange(n)
    inv = ROPE_BASE ** (-np.arange(N_FREQ, dtype=np.float32) / N_FREQ)
    rows = (t // GRID_W).astype(np.float32)
    cols = (t % GRID_W).astype(np.float32)
    d = np.arange(LANES) % HEAD_DIM
    pos = np.where(d[None, :] < HEAD_DIM // 2, rows[:, None], cols[:, None])
    ang = jnp.asarray(pos.astype(np.float32) * inv[d % N_FREQ][None, :])
    sign = np.where((d % (2 * N_FREQ)) < N_FREQ, -1.0, 1.0).astype(np.float32)
    return jnp.cos(ang), jnp.sin(ang) * sign[None, :]


def _attend_group(q_rows, sink_col, keys, vals, masks):
    scores = []
    m = sink_col
    for kk, mk in zip(keys, masks):
        s = _dot_nt(q_rows, kk) * SCALE
        if mk is not None:
            s = jnp.where(mk, s, NEG)
        scores.append(s)
        m = jnp.maximum(m, jnp.max(s, axis=-1, keepdims=True))
    den = jnp.exp(sink_col - m)
    acc = None
    for s, vv in zip(scores, vals):
        p = jnp.exp(s - m)
        den = den + jnp.sum(p, axis=-1, keepdims=True)
        pv = _dot(p.astype(bf16), vv)
        acc = pv if acc is None else acc + pv
    return acc / den


def _heads_attention(q, sink_ref, key_sets, val_sets, masks):
    bq = q.shape[0]
    ridx = lax.broadcasted_iota(jnp.int32, (Q_PER_KV * bq, 1), 0)
    outs = []
    for g in range(N_KV_HEADS):
        heads = range(g * Q_PER_KV, (g + 1) * Q_PER_KV)
        q_rows = jnp.concatenate([q[:, h * HEAD_DIM:(h + 1) * HEAD_DIM] for h in heads], axis=0)
        sink_col = jnp.full((Q_PER_KV * bq, 1), sink_ref[g * Q_PER_KV], f32)
        for j in range(1, Q_PER_KV):
            sink_col = jnp.where(ridx >= j * bq, sink_ref[g * Q_PER_KV + j], sink_col)
        lo, hi = g * HEAD_DIM, (g + 1) * HEAD_DIM
        o = _attend_group(q_rows, sink_col, [kk[:, lo:hi] for kk in key_sets],
                          [vv[:, lo:hi] for vv in val_sets], masks)
        outs += [o[j * bq:(j + 1) * bq] for j in range(Q_PER_KV)]
    return jnp.concatenate(outs, axis=1)


def _ctx_attn_kernel(sink_ref, q_ref, k_ref, v_ref, o_ref):
    k = k_ref[...].astype(bf16)
    v = v_ref[...].astype(bf16)
    o_ref[...] = _heads_attention(q_ref[...], sink_ref, [k], [v], [None]).astype(o_ref.dtype)


def _context_attention(q, k, v, sink, *, seq):
    t = q.shape[0]
    return pl.pallas_call(
        _ctx_attn_kernel,
        grid=(t // seq,),
        in_specs=[pl.BlockSpec(memory_space=pltpu.SMEM),
                  pl.BlockSpec((seq, ATTN_WIDTH), lambda b: (b, 0)),
                  pl.BlockSpec((seq, KV_WIDTH), lambda b: (b, 0)),
                  pl.BlockSpec((seq, KV_WIDTH), lambda b: (b, 0))],
        out_specs=pl.BlockSpec((seq, ATTN_WIDTH), lambda b: (b, 0)),
        out_shape=jax.ShapeDtypeStruct((t, ATTN_WIDTH), bf16),
        compiler_params=_params("arbitrary"),
        name="ctx_attn",
    )(sink, q, k, v)


def _lat_attn_kernel(sink_ref, q_ref, k_ref, vt_ref, ck_ref, cv_ref, o_ref, *, bq, seq):
    band = bq + 2 * WINDOW
    n_col = Q_PER_KV * bq
    ck = ck_ref[...].astype(bf16)
    cvt = cv_ref[...].T.astype(bf16)
    delta = ((lax.broadcasted_iota(jnp.int32, (band, n_col), 1) & (bq - 1))
             - lax.broadcasted_iota(jnp.int32, (band, n_col), 0))
    col_head = lax.broadcasted_iota(jnp.int32, (1, n_col), 1) // bq
    sink_rows = []
    for g in range(N_KV_HEADS):
        sink_row = jnp.full((1, n_col), sink_ref[g * Q_PER_KV], f32)
        for j in range(1, Q_PER_KV):
            sink_row = jnp.where(col_head == j, sink_ref[g * Q_PER_KV + j], sink_row)
        sink_rows.append(sink_row * LOG2_E)
    ones_rows = jnp.ones((BF16_SUBLANES, 1), bf16)
    cvt_aug = [jnp.concatenate([cvt[g * HEAD_DIM:(g + 1) * HEAD_DIM, :],
                                jnp.broadcast_to(ones_rows, (BF16_SUBLANES, cvt.shape[1]))], axis=0)
               for g in range(N_KV_HEADS)]

    n_sub = q_ref.shape[0] // bq
    subs = []
    for sb in range(n_sub):
        i = pl.program_id(1) * n_sub + sb
        start = pl.multiple_of(jnp.clip(i * bq - WINDOW, 0, seq - band), LANES)
        shifted = delta + (i * bq - start + WINDOW)
        subs.append(dict(
            kb=k_ref[pl.ds(start, band), :],
            vbt=vt_ref[:, pl.ds(start, band)],
            qt=(q_ref[sb * bq:(sb + 1) * bq, :].astype(f32) * (SCALE * LOG2_E)).T.astype(bf16),
            mask=shifted.astype(jnp.uint32) <= 2 * WINDOW))

    def scores(sub, g):
        lo, hi = g * HEAD_DIM, (g + 1) * HEAD_DIM
        heads = range(g * Q_PER_KV, (g + 1) * Q_PER_KV)
        qt_g = jnp.concatenate([sub["qt"][h * HEAD_DIM:(h + 1) * HEAD_DIM, :] for h in heads],
                               axis=1)
        s_loc = jnp.where(sub["mask"], _dot(sub["kb"][:, lo:hi], qt_g), NEG)
        s_ctx = _dot(ck[:, lo:hi], qt_g)
        return s_loc, s_ctx

    def weighted_values(sub, g, s_loc, s_ctx):
        lo, hi = g * HEAD_DIM, (g + 1) * HEAD_DIM
        m = jnp.maximum(jnp.maximum(jnp.max(s_loc, axis=0, keepdims=True),
                                    jnp.max(s_ctx, axis=0, keepdims=True)), sink_rows[g])
        p_loc = jnp.exp2(s_loc - m).astype(bf16)
        p_ctx = jnp.exp2(s_ctx - m).astype(bf16)
        vbt_aug = jnp.concatenate(
            [sub["vbt"][lo:hi, :], jnp.broadcast_to(ones_rows, (BF16_SUBLANES, band))], axis=0)
        o_aug = _dot(vbt_aug, p_loc) + _dot(cvt_aug[g], p_ctx)
        den = o_aug[HEAD_DIM:HEAD_DIM + 1, :] + jnp.exp2(sink_rows[g] - m)
        return o_aug[:HEAD_DIM, :] / den

    passes = [(sb, g) for sb in range(n_sub) for g in range(N_KV_HEADS)]
    out_t = {}
    pending = scores(subs[0], 0)
    for n, (sb, g) in enumerate(passes):
        nxt = scores(subs[passes[n + 1][0]], passes[n + 1][1]) if n + 1 < len(passes) else None
        out_t[sb, g] = weighted_values(subs[sb], g, *pending)
        pending = nxt
    for sb in range(n_sub):
        blocks = [jnp.concatenate([out_t[sb, g][:, j * bq:(j + 1) * bq]
                                   for g in range(N_KV_HEADS)], axis=0).T
                  for j in range(Q_PER_KV)]
        o_ref[sb * bq:(sb + 1) * bq, :] = jnp.concatenate(blocks, axis=1).astype(o_ref.dtype)


LATENT_HEAD_PERM = np.array([(Q_PER_KV * g + j) * HEAD_DIM + d for j in range(Q_PER_KV)
                             for g in range(N_KV_HEADS) for d in range(HEAD_DIM)])


def _latent_attention(q, k, vt, ck, cv, sink, *, seq, bq=128, sub_blocks=4):
    nb, past = ck.shape[0], ck.shape[1]
    assert bq == LANES and N_KV_HEADS * HEAD_DIM == LANES
    q3, k3 = (a.reshape(nb, seq, a.shape[-1]) for a in (q, k))
    tq = bq * sub_blocks
    out = pl.pallas_call(
        functools.partial(_lat_attn_kernel, bq=bq, seq=seq),
        grid=(nb, seq // tq),
        in_specs=[pl.BlockSpec(memory_space=pltpu.SMEM),
                  pl.BlockSpec((None, tq, ATTN_WIDTH), lambda b, i: (b, i, 0)),
                  pl.BlockSpec((None, seq, KV_WIDTH), lambda b, i: (b, 0, 0)),
                  pl.BlockSpec((None, KV_WIDTH, seq), lambda b, i: (b, 0, 0)),
                  pl.BlockSpec((None, past, KV_WIDTH), lambda b, i: (b, 0, 0)),
                  pl.BlockSpec((None, past, KV_WIDTH), lambda b, i: (b, 0, 0))],
        out_specs=pl.BlockSpec((None, tq, ATTN_WIDTH), lambda b, i: (b, i, 0)),
        out_shape=jax.ShapeDtypeStruct((nb, seq, ATTN_WIDTH), bf16),
        compiler_params=_params("arbitrary", "arbitrary"),
        name="lat_attn",
    )(sink, q3, k3, vt, ck, cv)
    return out.reshape(nb * seq, ATTN_WIDTH)


def _route(logits):
    lane = lax.broadcasted_iota(jnp.int32, logits.shape, 1)
    lane_f = lane.astype(f32)
    big = jnp.float32(LANES)
    is_c = jnp.logical_and(lane >= COARSE_LANE0, lane < COARSE_LANE0 + N_GROUPS)
    lc = jnp.where(is_c, logits, -jnp.inf)
    mc = jnp.max(lc, axis=-1, keepdims=True)
    grp = jnp.min(jnp.where(lc == mc, lane_f, big), axis=-1, keepdims=True) - COARSE_LANE0
    pg = 1.0 / jnp.sum(jnp.exp(lc - mc), axis=-1, keepdims=True)
    in_g = jnp.floor(lane_f * (1.0 / EXPERTS_PER_GROUP)) == grp
    fl = jnp.where(in_g, logits, -jnp.inf)
    m1 = jnp.max(fl, axis=-1, keepdims=True)
    i1 = jnp.min(jnp.where(fl == m1, lane_f, big), axis=-1, keepdims=True)
    fl2 = jnp.where(lane_f == i1, -jnp.inf, fl)
    m2 = jnp.max(fl2, axis=-1, keepdims=True)
    i2 = jnp.min(jnp.where(fl2 == m2, lane_f, big), axis=-1, keepdims=True)
    e2 = jnp.exp(m2 - m1)
    p1 = pg / (1.0 + e2)
    p2 = pg * e2 / (1.0 + e2)
    packed = jnp.where(lane == 0, i1, jnp.where(lane == 1, i2, jnp.where(lane == 2, p1, p2)))
    return jnp.where(lane < 4, packed, 0.0)


def _post_kernel(x_ref, attn_ref, u_ref, up_ref, un_ref, bg_ref, mod_ref, ag_ref, cg_ref, cw_ref,
                 cb_ref, wo_ref, n2_ref, wr_ref, br_ref, x1_ref, h2_ref, route_ref, *, tm, seq):
    i = pl.program_id(0)
    u = u_ref[...].astype(f32)
    rows = lax.broadcasted_iota(jnp.int32, (tm, 1), 0)
    spos = (i * tm + rows) % seq
    u_dn = jnp.where(rows == 0, up_ref[...].astype(f32)[BF16_SUBLANES - 1:, :],
                     pltpu.roll(u, 1, axis=0))
    u_dn = jnp.where(spos == 0, 0.0, u_dn)
    u_up = jnp.where(rows == tm - 1, un_ref[...].astype(f32)[0:1, :], pltpu.roll(u, tm - 1, axis=0))
    u_up = jnp.where(spos == seq - 1, 0.0, u_up)
    y = u_dn * cw_ref[0:1, :] + u * cw_ref[1:2, :] + u_up * cw_ref[2:3, :] + cb_ref[...]
    conv = bg_ref[...].astype(f32) * y
    attn_n = _rms(attn_ref[...].astype(f32)) * ag_ref[...]
    conv_n = _rms(conv) * cg_ref[...]
    mixed = (_dot(attn_n.astype(bf16), wo_ref[0:ATTN_WIDTH, :])
             + _dot(conv_n.astype(bf16), wo_ref[ATTN_WIDTH:, :]))
    x1 = x_ref[...] + mod_ref[:, G1:G1 + D_MODEL] * mixed
    x1_ref[...] = x1
    h2 = _rms(x1) * n2_ref[...]
    h2 = h2 * (1.0 + mod_ref[:, SC2:SC2 + D_MODEL]) + mod_ref[:, SH2:SH2 + D_MODEL]
    for c in range(D_MODEL // LANES):
        h2_ref[pl.ds(c, tm, stride=F32_SUBLANES), :] = h2[:, c * LANES:(c + 1) * LANES]
    h2_hi = h2.astype(bf16)
    h2_lo = (h2 - h2_hi.astype(f32)).astype(bf16)
    both = _dot(h2_hi, wr_ref[...])
    logits = (both[:, :LANES] + both[:, LANES:] + _dot(h2_lo, wr_ref[:, :LANES])) + br_ref[...]
    route_ref[...] = _route(logits)


def _post(x, attn, u, bg, mod3, p, *, seq, mod_row0, tm=256):
    t = x.shape[0]
    tiles_per_seq = seq // tm
    halo = BF16_SUBLANES
    n_halo = t // halo
    row = lambda i: (mod_row0 + (i // tiles_per_seq if mod_row0 else 0), 0, 0)
    tile = lambda w: pl.BlockSpec((tm, w), lambda i: (i, 0))
    full = lambda a: pl.BlockSpec(a.shape, lambda i: (0,) * a.ndim)
    small = [p["attn_out_g"], p["conv_out_g"], p["conv_w"], p["conv_b"], p["w_o"], p["norm2_g"],
             p["w_router"], p["b_router"]]
    return pl.pallas_call(
        functools.partial(_post_kernel, tm=tm, seq=seq),
        grid=(t // tm,),
        in_specs=[tile(D_MODEL), tile(ATTN_WIDTH), tile(CONV_WIDTH),
                  pl.BlockSpec((halo, CONV_WIDTH),
                               lambda i: (jnp.maximum(i * (tm // halo) - 1, 0), 0)),
                  pl.BlockSpec((halo, CONV_WIDTH),
                               lambda i: (jnp.minimum((i + 1) * (tm // halo), n_halo - 1), 0)),
                  tile(CONV_WIDTH),
                  pl.BlockSpec((None, 1, 6 * D_MODEL), row)] + [full(a) for a in small],
        out_specs=[tile(D_MODEL), pl.BlockSpec((tm * F32_SUBLANES, LANES), lambda i: (i, 0)),
                   tile(LANES)],
        out_shape=[jax.ShapeDtypeStruct((t, D_MODEL), f32),
                   jax.ShapeDtypeStruct((t * F32_SUBLANES, LANES), f32),
                   jax.ShapeDtypeStruct((t, LANES), f32)],
        compiler_params=_params("arbitrary"),
        name="post",
    )(x, attn, u, u, u, bg, mod3, *small)


def _max_chunks(blk):
    n = -(-TOP_K * blk // MOE_CHUNK) + EXPERTS_PER_STEP
    return n + n % 2


def _dispatch_tables(route, blk):
    nblk = route.shape[0] // blk
    n_slots = TOP_K * blk
    flat = lambda a: a.reshape(nblk, blk, TOP_K).transpose(0, 2, 1).reshape(nblk, n_slots)
    experts = flat(route[:, 0:TOP_K].astype(jnp.int32))
    weights = flat(route[:, TOP_K:2 * TOP_K])
    slot_ids = lax.broadcasted_iota(jnp.int32, experts.shape, 1)
    _, order, w_sorted = lax.sort((experts, slot_ids, weights), dimension=1, num_keys=1)
    bounds = jnp.arange(N_EXPERTS + 1, dtype=jnp.int32)
    offs = jnp.sum(experts[:, :, None] < bounds[None, None, :], axis=1, dtype=jnp.int32)
    tail = ((0, 0), (0, MOE_CHUNK))
    src_rows = jnp.pad((order & (blk - 1)) * F32_SUBLANES, tail)
    dst_rows = jnp.pad(order * F32_SUBLANES, tail, constant_values=n_slots * F32_SUBLANES)
    w_sorted = jnp.pad(w_sorted, tail)

    eg, n_steps, max_c = EXPERTS_PER_STEP, N_EXPERTS // EXPERTS_PER_STEP, _max_chunks(blk)
    lo = offs[:, :N_EXPERTS].reshape(nblk, n_steps, eg)
    n_ch = (offs[:, 1:] - offs[:, :N_EXPERTS] + MOE_CHUNK - 1) // MOE_CHUNK
    n_ch = n_ch.reshape(nblk, n_steps, eg)
    cum = jnp.cumsum(n_ch, axis=2)
    total = cum[..., -1]
    pos = jnp.arange(max_c, dtype=jnp.int32)
    j = jnp.sum(cum[:, :, None, :] <= pos[None, None, :, None], axis=-1, dtype=jnp.int32)
    j = jnp.minimum(j, eg - 1)
    hit = j[..., None] == jnp.arange(eg, dtype=jnp.int32)
    pick = lambda a: jnp.sum(jnp.where(hit, a[:, :, None, :], 0), axis=-1)
    base = pick(lo) + (pos - pick(cum - n_ch)) * MOE_CHUNK
    valid = pos < total[..., None]
    base = jnp.where(valid, base, n_slots).reshape(nblk, 1, n_steps * max_c)
    j = jnp.where(valid, j, 0).reshape(nblk, 1, n_steps * max_c)
    n_pairs = ((total + 1) // 2)[:, None, :]
    return src_rows[:, None, :], dst_rows[:, None, :], w_sorted[:, None, :], j, base, n_pairs


def _moe_kernel(src_ref, dst_ref, wts_ref, cj_ref, cbase_g_ref, cbase_s_ref, npairs_ref,
                h_ref, wg_ref, wu_ref, wd_ref, mod_ref, fg_ref, x1_hbm, out_hbm,
                ytmp_ref, xg0_ref, og0_ref, xg1_ref, og1_ref, xbuf_ref, obuf_ref, sem_in, sem_out,
                *, blk):
    n_col = D_MODEL // LANES
    blk_id, step = pl.program_id(0), pl.program_id(1)
    last_step = step == pl.num_programs(1) - 1
    max_c = _max_chunks(blk)
    stage_refs = (xg0_ref, og0_ref, xg1_ref, og1_ref)
    n_sub = blk // FINAL_ROWS

    def x1_copy(s):
        rows = pl.ds(blk_id * blk + s * FINAL_ROWS, FINAL_ROWS)
        return pltpu.make_async_copy(x1_hbm.at[rows, :], xbuf_ref.at[s % 2], sem_in.at[s % 2])

    def out_copy(s):
        rows = pl.ds(blk_id * blk + s * FINAL_ROWS, FINAL_ROWS)
        return pltpu.make_async_copy(obuf_ref.at[s % 2], out_hbm.at[rows, :], sem_out.at[s % 2])

    @pl.when(last_step)
    def _():
        x1_copy(0).start()

    def gather(base, xg_ref):
        for r in range(MOE_CHUNK):
            src = pl.multiple_of(src_ref[0, base + r], F32_SUBLANES)
            xg_ref[pl.ds(r, F32_SUBLANES, stride=CHUNK_PITCH), :] = h_ref[pl.ds(src, F32_SUBLANES), :]

    def experts_mlp(j, xg_ref, og_ref):
        x = jnp.concatenate([xg_ref[c * CHUNK_PITCH:c * CHUNK_PITCH + MOE_CHUNK, :]
                             for c in range(n_col)], axis=1).astype(bf16)
        g = _dot(x, wg_ref[j])
        up = _dot(x, wu_ref[j])
        act = (g * jax.nn.sigmoid(g)) * up
        out = _dot(act.astype(bf16), wd_ref[j])
        for c in range(n_col):
            og_ref[c * CHUNK_PITCH:c * CHUNK_PITCH + MOE_CHUNK, :] = out[:, c * LANES:(c + 1) * LANES]

    def scatter(base, og_ref):
        for r in range(MOE_CHUNK):
            dst = pl.multiple_of(dst_ref[0, base + r], F32_SUBLANES)
            ytmp_ref[pl.ds(dst, F32_SUBLANES), :] = (
                og_ref[pl.ds(r, F32_SUBLANES, stride=CHUNK_PITCH), :] * wts_ref[0, base + r])

    def pair(pi, carry):
        chunks = [(step * max_c + 2 * pi + half, stage_refs[2 * half], stage_refs[2 * half + 1])
                  for half in range(2)]
        for p, xg_ref, _ in chunks:
            gather(cbase_g_ref[0, p], xg_ref)
        for p, xg_ref, og_ref in chunks:
            experts_mlp(cj_ref[0, p], xg_ref, og_ref)
        for p, _, og_ref in chunks:
            scatter(cbase_s_ref[0, p], og_ref)
        return carry

    lax.fori_loop(0, npairs_ref[0, step], pair, 0)

    @pl.when(last_step)
    def _():
        n = blk * F32_SUBLANES
        for k in range(1, TOP_K):
            ytmp_ref[0:n, :] = ytmp_ref[0:n, :] + ytmp_ref[k * n:(k + 1) * n, :]
        gate2 = mod_ref[:, G2:G2 + D_MODEL]
        for s in range(n_sub):
            if s + 1 < n_sub:
                x1_copy(s + 1).start()
            x1_copy(s).wait()
            if s >= 2:
                out_copy(s - 2).wait()
            first = s * FINAL_ROWS * F32_SUBLANES
            y = jnp.concatenate([ytmp_ref[pl.ds(first + c, FINAL_ROWS, stride=F32_SUBLANES), :]
                                 for c in range(n_col)], axis=1)
            x2 = xbuf_ref[s % 2] + gate2 * y
            obuf_ref[s % 2] = _rms(x2) * fg_ref[...]
            out_copy(s).start()
        for s in range(max(n_sub - 2, 0), n_sub):
            out_copy(s).wait()


def _moe(h2_tiles, tables, x1, mod3, final_g, w_gate, w_up, w_down, *, seq, mod_row0,
         blk=MOE_BLOCK):
    t = h2_tiles.shape[0] // F32_SUBLANES
    src_rows, dst_rows, w_sorted, cj, cbase, n_pairs = tables
    n_slots = TOP_K * blk
    eg, n_steps = EXPERTS_PER_STEP, N_EXPERTS // EXPERTS_PER_STEP
    blocks_per_seq = max(seq // blk, 1)
    row = lambda b, g: (mod_row0 + (b // blocks_per_seq if mod_row0 else 0), 0, 0)
    smem = lambda n: pl.BlockSpec((None, 1, n), lambda b, g: (b, 0, 0), memory_space=pltpu.SMEM)
    stage = pltpu.VMEM((D_MODEL // LANES * CHUNK_PITCH, LANES), f32)
    final_buf = pltpu.VMEM((2, FINAL_ROWS, D_MODEL), f32)
    return pl.pallas_call(
        functools.partial(_moe_kernel, blk=blk),
        grid=(t // blk, n_steps),
        in_specs=[smem(n_slots + MOE_CHUNK)] * 3 + [smem(n_steps * _max_chunks(blk))] * 3 + [
                  smem(n_steps),
                  pl.BlockSpec((blk * F32_SUBLANES, LANES), lambda b, g: (b, 0)),
                  pl.BlockSpec((eg, D_MODEL, D_EXPERT), lambda b, g: (g, 0, 0)),
                  pl.BlockSpec((eg, D_MODEL, D_EXPERT), lambda b, g: (g, 0, 0)),
                  pl.BlockSpec((eg, D_EXPERT, D_MODEL), lambda b, g: (g, 0, 0)),
                  pl.BlockSpec((None, 1, 6 * D_MODEL), row),
                  pl.BlockSpec((1, D_MODEL), lambda b, g: (0, 0)),
                  pl.BlockSpec(memory_space=pl.ANY)],
        out_specs=pl.BlockSpec(memory_space=pl.ANY),
        out_shape=jax.ShapeDtypeStruct((t, D_MODEL), f32),
        scratch_shapes=[pltpu.VMEM(((n_slots + 1) * F32_SUBLANES, LANES), f32)] + [stage] * 4 + [
                        final_buf, final_buf,
                        pltpu.SemaphoreType.DMA((2,)), pltpu.SemaphoreType.DMA((2,))],
        compiler_params=_params("arbitrary", "arbitrary", vmem=MOE_VMEM_LIMIT_BYTES),
        name="moe",
    )(src_rows, dst_rows, w_sorted, cj, cbase, cbase, n_pairs, h2_tiles, w_gate, w_up, w_down,
      mod3, final_g, x1)


def _mixers(x, mod3, p, *, seq, mod_row0, ctx_kv):
    latent = ctx_kv is not None
    q, k, v, u, bg = _inproj(x, mod3, p["norm1_g"], p["w_in"], seq=seq, mod_row0=mod_row0,
                             kv_dtype=bf16 if latent else f32,
                             rope_tabs=_rope_tables(seq) if latent else None)
    if latent:
        attn = _latent_attention(q, k, v, ctx_kv[0], ctx_kv[1], p["sink"], seq=seq)
        rows = np.concatenate([LATENT_HEAD_PERM, np.arange(ATTN_WIDTH, ATTN_WIDTH + CONV_WIDTH)])
        p = dict(p, attn_out_g=p["attn_out_g"][:, LATENT_HEAD_PERM], w_o=p["w_o"][rows])
    else:
        attn = _context_attention(q, k, v, p["sink"], seq=seq)
    x1, h2_tiles, route = _post(x, attn, u, bg, mod3, p, seq=seq, mod_row0=mod_row0)
    return x1, h2_tiles, route[:, :2 * TOP_K], k, v


def _experts(x1, h2_tiles, tables, mod3, p, *, seq, mod_row0):
    out = _moe(h2_tiles, tables, x1, mod3, p["final_g"], p["w_gate"], p["w_up"], p["w_down"],
               seq=seq, mod_row0=mod_row0)
    return out.reshape(-1, seq, D_MODEL)


def kernel(x_prompt, x_sample, cache_k, cache_v, c, c_ctx, w_mod, b_mod, norm1_g, w_in, conv_w,
           conv_b, sink, attn_out_g, conv_out_g, w_o, norm2_g, w_coarse, b_coarse, w_fine, b_fine,
           w_gate, w_up, w_down, final_g):
    batch, seq, _ = x_prompt.shape
    dec_batch, dec_seq, _ = x_sample.shape
    past = cache_k.shape[2]
    assert w_mod.shape[0] == 1 and 1 + dec_batch <= MOD_ROWS

    cvecs = jnp.concatenate([c_ctx[None], c, jnp.zeros((MOD_ROWS - 1 - dec_batch, D_MODEL), f32)])
    mod3 = _modulation(cvecs, w_mod[0], b_mod[0]).reshape(MOD_ROWS, 1, 6 * D_MODEL)

    pad = jnp.zeros((D_MODEL, LANES - N_EXPERTS - N_GROUPS), f32)
    p = {
        "norm1_g": norm1_g, "w_in": w_in[0].astype(bf16), "conv_w": conv_w[0], "conv_b": conv_b,
        "sink": sink[0], "attn_out_g": attn_out_g, "conv_out_g": conv_out_g,
        "w_o": w_o[0].astype(bf16), "norm2_g": norm2_g,
        "w_router": _split_bf16(jnp.concatenate([w_fine[0], w_coarse[0], pad], axis=1)),
        "b_router": jnp.concatenate([b_fine[0], b_coarse[0], pad[0]])[None],
        "w_gate": w_gate[0].astype(bf16), "w_up": w_up[0].astype(bf16),
        "w_down": w_down[0].astype(bf16), "final_g": final_g[None],
    }

    x1_p, h2_p, route_p, k_p, v_p = _mixers(x_prompt.reshape(batch * seq, D_MODEL), mod3, p,
                                            seq=seq, mod_row0=0, ctx_kv=None)
    new_k = k_p.reshape(batch, 1, seq, N_KV_HEADS, HEAD_DIM)
    new_v = v_p.reshape(batch, 1, seq, N_KV_HEADS, HEAD_DIM)

    ctx_kv = (cache_k[:, 0].reshape(dec_batch, past, KV_WIDTH),
              cache_v[:, 0].reshape(dec_batch, past, KV_WIDTH))
    x1_s, h2_s, route_s, _, _ = _mixers(x_sample.reshape(dec_batch * dec_seq, D_MODEL), mod3, p,
                                        seq=dec_seq, mod_row0=1, ctx_kv=ctx_kv)

    n_blk_p = batch * seq // MOE_BLOCK
    tables = _dispatch_tables(jnp.concatenate([route_p, route_s]), MOE_BLOCK)
    y_prompt = _experts(x1_p, h2_p, [t[:n_blk_p] for t in tables], mod3, p, seq=seq, mod_row0=0)
    y_sample = _experts(x1_s, h2_s, [t[n_blk_p:] for t in tables], mod3, p, seq=dec_seq,
                        mod_row0=1)
    return y_prompt, y_sample, new_k, new_v
```

```python
import functools

import jax
import jax.numpy as jnp
import numpy as np
from jax import lax
from jax.experimental import pallas as pl
from jax.experimental.pallas import tpu as pltpu

D_MODEL = 1024
HEAD_DIM = 64
ATTN_WIDTH = 512
N_HEADS = 8
N_KV_HEADS = 2
Q_PER_KV = 4
KV_WIDTH = 128
CONV_WIDTH = 512
CONV_K = 3
WINDOW = 128
GRID_W = 64
ROPE_BASE = 10000.0
N_FREQ = 16
N_GROUPS = 4
EXPERTS_PER_GROUP = 8
N_EXPERTS = 32
TOP_K = 2
D_EXPERT = 256
IN_WIDTH = ATTN_WIDTH + 2 * KV_WIDTH + 3 * CONV_WIDTH
EPS = 1e-6
NEG = -1e30
SCALE = HEAD_DIM ** -0.5
LOG2_E = 1.4426950408889634

LANES = 128
F32_SUBLANES = 8
BF16_SUBLANES = 16
assert D_MODEL == F32_SUBLANES * LANES

MOE_BLOCK = 2048
MOE_CHUNK = 160
CHUNK_PITCH = MOE_CHUNK + F32_SUBLANES
EXPERTS_PER_STEP = 4
FINAL_ROWS = 256
FINAL_SLOTS = 4
MOE_VMEM_LIMIT_BYTES = 60 * 1024 * 1024
VMEM_LIMIT_BYTES = 48 * 1024 * 1024

SH1, SC1, G1, SH2, SC2, G2 = (i * D_MODEL for i in range(6))
MOD_ROWS = 8

COARSE_LANE0 = N_EXPERTS

f32 = jnp.float32
bf16 = jnp.bfloat16


def _params(*semantics, vmem=VMEM_LIMIT_BYTES):
    return pltpu.CompilerParams(dimension_semantics=semantics, vmem_limit_bytes=vmem)


def _rms(x):
    return x * lax.rsqrt(jnp.mean(x * x, axis=-1, keepdims=True) + EPS)


def _dot(a, b):
    return jnp.dot(a, b, preferred_element_type=f32)


def _split_bf16(w):
    hi = w.astype(bf16)
    lo = (w - hi.astype(f32)).astype(bf16)
    return jnp.concatenate([hi, lo], axis=1)


def _dot_nt(a, b):
    return lax.dot_general(a, b, (((1,), (1,)), ((), ())), preferred_element_type=f32)


def _mod_kernel(cv_ref, w_ref, b_ref, o_ref):
    a = cv_ref[...]
    a = a * jax.nn.sigmoid(a)
    o_ref[...] = jnp.dot(a, w_ref[...], precision=lax.Precision.HIGHEST,
                         preferred_element_type=f32) + b_ref[...]


def _modulation(cvecs, w_mod, b_mod):
    tn = D_MODEL
    return pl.pallas_call(
        _mod_kernel,
        grid=(6 * D_MODEL // tn,),
        in_specs=[pl.BlockSpec((MOD_ROWS, D_MODEL), lambda j: (0, 0)),
                  pl.BlockSpec((D_MODEL, tn), lambda j: (0, j)),
                  pl.BlockSpec((1, tn), lambda j: (0, j))],
        out_specs=pl.BlockSpec((MOD_ROWS, tn), lambda j: (0, j)),
        out_shape=jax.ShapeDtypeStruct((MOD_ROWS, 6 * D_MODEL), f32),
        compiler_params=_params("arbitrary"),
        name="mod",
    )(cvecs, w_mod, b_mod.reshape(1, -1))


def _swap_halves(x):
    lane = lax.broadcasted_iota(jnp.int32, x.shape, 1)
    up = pltpu.roll(x, LANES - N_FREQ, axis=1)
    dn = pltpu.roll(x, N_FREQ, axis=1)
    return jnp.where((lane % (2 * N_FREQ)) < N_FREQ, up, dn)


def _rope(x, cos, sin):
    parts = []
    for c in range(x.shape[1] // LANES):
        xc = x[:, c * LANES:(c + 1) * LANES]
        parts.append(xc * cos + _swap_halves(xc) * sin)
    return parts[0] if len(parts) == 1 else jnp.concatenate(parts, axis=1)


def _inproj_kernel(*refs, rope):
    if rope:
        x_ref, mod_ref, g_ref, w_ref, cos_ref, sin_ref, q_ref, k_ref, v_ref, u_ref, bg_ref = refs
    else:
        x_ref, mod_ref, g_ref, w_ref, q_ref, k_ref, v_ref, u_ref, bg_ref = refs
    h = _rms(x_ref[...]) * g_ref[...]
    h = h * (1.0 + mod_ref[:, SC1:SC1 + D_MODEL]) + mod_ref[:, SH1:SH1 + D_MODEL]
    z = _dot(h.astype(bf16), w_ref[...])
    o = 0
    q = z[:, o:o + ATTN_WIDTH]; o += ATTN_WIDTH
    k = z[:, o:o + KV_WIDTH]; o += KV_WIDTH
    v = z[:, o:o + KV_WIDTH]; o += KV_WIDTH
    bg = z[:, o:o + CONV_WIDTH]; o += CONV_WIDTH
    cg = z[:, o:o + CONV_WIDTH]; o += CONV_WIDTH
    xin = z[:, o:o + CONV_WIDTH]
    if rope:
        q = _rope(q, cos_ref[...], sin_ref[...])
        k = _rope(k, cos_ref[...], sin_ref[...])
    q_ref[...] = q.astype(q_ref.dtype)
    k_ref[...] = k.astype(k_ref.dtype)
    v_ref[...] = (v.T if rope else v).astype(v_ref.dtype)
    u_ref[...] = (cg * xin).astype(u_ref.dtype)
    bg_ref[...] = bg.astype(bg_ref.dtype)


def _inproj(x, mod3, norm_g, w_in, *, seq, mod_row0, kv_dtype, rope_tabs=None, tm=512):
    t = x.shape[0]
    tiles_per_seq = seq // tm
    row = lambda i: (mod_row0 + (i // tiles_per_seq if mod_row0 else 0), 0, 0)
    in_specs = [pl.BlockSpec((tm, D_MODEL), lambda i: (i, 0)),
                pl.BlockSpec((None, 1, 6 * D_MODEL), row),
                pl.BlockSpec((1, D_MODEL), lambda i: (0, 0)),
                pl.BlockSpec((D_MODEL, IN_WIDTH), lambda i: (0, 0))]
    args = [x, mod3, norm_g, w_in]
    if rope_tabs is not None:
        in_specs += [pl.BlockSpec((tm, LANES), lambda i: (i % tiles_per_seq, 0))] * 2
        args += list(rope_tabs)
    widths = (ATTN_WIDTH, KV_WIDTH, KV_WIDTH, CONV_WIDTH, CONV_WIDTH)
    dtypes = (bf16, kv_dtype, kv_dtype, bf16, bf16)
    out_specs = [pl.BlockSpec((tm, w), lambda i: (i, 0)) for w in widths]
    out_shape = [jax.ShapeDtypeStruct((t, w), d) for w, d in zip(widths, dtypes)]
    if rope_tabs is not None:
        out_specs[2] = pl.BlockSpec((None, KV_WIDTH, tm),
                                    lambda i: (i // tiles_per_seq, 0, i % tiles_per_seq))
        out_shape[2] = jax.ShapeDtypeStruct((t // seq, KV_WIDTH, seq), kv_dtype)
    return pl.pallas_call(
        functools.partial(_inproj_kernel, rope=rope_tabs is not None),
        grid=(t // tm,),
        in_specs=in_specs,
        out_specs=out_specs,
        out_shape=out_shape,
        compiler_params=_params("arbitrary"),
        name="inproj_rope" if rope_tabs is not None else "inproj",
    )(*args)


def _rope_tables(n):
    t = np.arange(n)
    inv = ROPE_BASE ** (-np.arange(N_FREQ, dtype=np.float32) / N_FREQ)
    rows = (t // GRID_W).astype(np.float32)
    cols = (t % GRID_W).astype(np.float32)
    d = np.arange(LANES) % HEAD_DIM
    pos = np.where(d[None, :] < HEAD_DIM // 2, rows[:, None], cols[:, None])
    ang = jnp.asarray(pos.astype(np.float32) * inv[d % N_FREQ][None, :])
    sign = np.where((d % (2 * N_FREQ)) < N_FREQ, -1.0, 1.0).astype(np.float32)
    return jnp.cos(ang), jnp.sin(ang) * sign[None, :]


def _attend_group(q_rows, sink_col, keys, vals, masks):
    scores = []
    m = sink_col
    for kk, mk in zip(keys, masks):
        s = _dot_nt(q_rows, kk) * SCALE
        if mk is not None:
            s = jnp.where(mk, s, NEG)
        scores.append(s)
        m = jnp.maximum(m, jnp.max(s, axis=-1, keepdims=True))
    den = jnp.exp(sink_col - m)
    acc = None
    for s, vv in zip(scores, vals):
        p = jnp.exp(s - m)
        den = den + jnp.sum(p, axis=-1, keepdims=True)
        pv = _dot(p.astype(bf16), vv)
        acc = pv if acc is None else acc + pv
    return acc / den


def _heads_attention(q, sink_ref, key_sets, val_sets, masks):
    bq = q.shape[0]
    ridx = lax.broadcasted_iota(jnp.int32, (Q_PER_KV * bq, 1), 0)
    outs = []
    for g in range(N_KV_HEADS):
        heads = range(g * Q_PER_KV, (g + 1) * Q_PER_KV)
        q_rows = jnp.concatenate([q[:, h * HEAD_DIM:(h + 1) * HEAD_DIM] for h in heads], axis=0)
        sink_col = jnp.full((Q_PER_KV * bq, 1), sink_ref[g * Q_PER_KV], f32)
        for j in range(1, Q_PER_KV):
            sink_col = jnp.where(ridx >= j * bq, sink_ref[g * Q_PER_KV + j], sink_col)
        lo, hi = g * HEAD_DIM, (g + 1) * HEAD_DIM
        o = _attend_group(q_rows, sink_col, [kk[:, lo:hi] for kk in key_sets],
                          [vv[:, lo:hi] for vv in val_sets], masks)
        outs += [o[j * bq:(j + 1) * bq] for j in range(Q_PER_KV)]
    return jnp.concatenate(outs, axis=1)


def _ctx_attn_kernel(sink_ref, q_ref, k_ref, v_ref, o_ref):
    k = k_ref[...].astype(bf16)
    v = v_ref[...].astype(bf16)
    o_ref[...] = _heads_attention(q_ref[...], sink_ref, [k], [v], [None]).astype(o_ref.dtype)


def _context_attention(q, k, v, sink, *, seq):
    t = q.shape[0]
    return pl.pallas_call(
        _ctx_attn_kernel,
        grid=(t // seq,),
        in_specs=[pl.BlockSpec(memory_space=pltpu.SMEM),
                  pl.BlockSpec((seq, ATTN_WIDTH), lambda b: (b, 0)),
                  pl.BlockSpec((seq, KV_WIDTH), lambda b: (b, 0)),
                  pl.BlockSpec((seq, KV_WIDTH), lambda b: (b, 0))],
        out_specs=pl.BlockSpec((seq, ATTN_WIDTH), lambda b: (b, 0)),
        out_shape=jax.ShapeDtypeStruct((t, ATTN_WIDTH), bf16),
        compiler_params=_params("arbitrary"),
        name="ctx_attn",
    )(sink, q, k, v)


def _lat_attn_kernel(sink_ref, q_ref, k_ref, vt_ref, ck_ref, cv_ref, o_ref, *, bq, seq):
    band = bq + 2 * WINDOW
    n_col = Q_PER_KV * bq
    ck = ck_ref[...].astype(bf16)
    cvt = cv_ref[...].T.astype(bf16)
    delta = ((lax.broadcasted_iota(jnp.int32, (band, n_col), 1) & (bq - 1))
             - lax.broadcasted_iota(jnp.int32, (band, n_col), 0))
    col_head = lax.broadcasted_iota(jnp.int32, (1, n_col), 1) // bq
    sink_rows = []
    for g in range(N_KV_HEADS):
        sink_row = jnp.full((1, n_col), sink_ref[g * Q_PER_KV], f32)
        for j in range(1, Q_PER_KV):
            sink_row = jnp.where(col_head == j, sink_ref[g * Q_PER_KV + j], sink_row)
        sink_rows.append(sink_row * LOG2_E)
    ones_rows = jnp.ones((BF16_SUBLANES, 1), bf16)
    cvt_aug = [jnp.concatenate([cvt[g * HEAD_DIM:(g + 1) * HEAD_DIM, :],
                                jnp.broadcast_to(ones_rows, (BF16_SUBLANES, cvt.shape[1]))], axis=0)
               for g in range(N_KV_HEADS)]

    n_sub = q_ref.shape[0] // bq
    subs = []
    for sb in range(n_sub):
        i = pl.program_id(1) * n_sub + sb
        start = pl.multiple_of(jnp.clip(i * bq - WINDOW, 0, seq - band), LANES)
        shifted = delta + (i * bq - start + WINDOW)
        subs.append(dict(
            kb=k_ref[pl.ds(start, band), :],
            vbt=vt_ref[:, pl.ds(start, band)],
            qt=(q_ref[sb * bq:(sb + 1) * bq, :].astype(f32) * (SCALE * LOG2_E)).T.astype(bf16),
            mask=shifted.astype(jnp.uint32) <= 2 * WINDOW))

    def scores(sub, g):
        lo, hi = g * HEAD_DIM, (g + 1) * HEAD_DIM
        heads = range(g * Q_PER_KV, (g + 1) * Q_PER_KV)
        qt_g = jnp.concatenate([sub["qt"][h * HEAD_DIM:(h + 1) * HEAD_DIM, :] for h in heads],
                               axis=1)
        s_loc = jnp.where(sub["mask"], _dot(sub["kb"][:, lo:hi], qt_g), NEG)
        s_ctx = _dot(ck[:, lo:hi], qt_g)
        return s_loc, s_ctx

    def weighted_values(sub, g, s_loc, s_ctx):
        lo, hi = g * HEAD_DIM, (g + 1) * HEAD_DIM
        m = jnp.maximum(jnp.maximum(jnp.max(s_loc, axis=0, keepdims=True),
                                    jnp.max(s_ctx, axis=0, keepdims=True)), sink_rows[g])
        p_loc = jnp.exp2(s_loc - m).astype(bf16)
        p_ctx = jnp.exp2(s_ctx - m).astype(bf16)
        vbt_aug = jnp.concatenate(
            [sub["vbt"][lo:hi, :], jnp.broadcast_to(ones_rows, (BF16_SUBLANES, band))], axis=0)
        o_aug = _dot(vbt_aug, p_loc) + _dot(cvt_aug[g], p_ctx)
        den = o_aug[HEAD_DIM:HEAD_DIM + 1, :] + jnp.exp2(sink_rows[g] - m)
        return o_aug[:HEAD_DIM, :] / den

    passes = [(sb, g) for sb in range(n_sub) for g in range(N_KV_HEADS)]
    out_t = {}
    pending = scores(subs[0], 0)
    for n, (sb, g) in enumerate(passes):
        nxt = scores(subs[passes[n + 1][0]], passes[n + 1][1]) if n + 1 < len(passes) else None
        out_t[sb, g] = weighted_values(subs[sb], g, *pending)
        pending = nxt
    for sb in range(n_sub):
        blocks = [jnp.concatenate([out_t[sb, g][:, j * bq:(j + 1) * bq]
                                   for g in range(N_KV_HEADS)], axis=0).T
                  for j in range(Q_PER_KV)]
        o_ref[sb * bq:(sb + 1) * bq, :] = jnp.concatenate(blocks, axis=1).astype(o_ref.dtype)


LATENT_HEAD_PERM = np.array([(Q_PER_KV * g + j) * HEAD_DIM + d for j in range(Q_PER_KV)
                             for g in range(N_KV_HEADS) for d in range(HEAD_DIM)])


def _latent_attention(q, k, vt, ck, cv, sink, *, seq, bq=128, sub_blocks=4):
    nb, past = ck.shape[0], ck.shape[1]
    assert bq == LANES and N_KV_HEADS * HEAD_DIM == LANES
    q3, k3 = (a.reshape(nb, seq, a.shape[-1]) for a in (q, k))
    tq = bq * sub_blocks
    out = pl.pallas_call(
        functools.partial(_lat_attn_kernel, bq=bq, seq=seq),
        grid=(nb, seq // tq),
        in_specs=[pl.BlockSpec(memory_space=pltpu.SMEM),
                  pl.BlockSpec((None, tq, ATTN_WIDTH), lambda b, i: (b, i, 0)),
                  pl.BlockSpec((None, seq, KV_WIDTH), lambda b, i: (b, 0, 0)),
                  pl.BlockSpec((None, KV_WIDTH, seq), lambda b, i: (b, 0, 0)),
                  pl.BlockSpec((None, past, KV_WIDTH), lambda b, i: (b, 0, 0)),
                  pl.BlockSpec((None, past, KV_WIDTH), lambda b, i: (b, 0, 0))],
        out_specs=pl.BlockSpec((None, tq, ATTN_WIDTH), lambda b, i: (b, i, 0)),
        out_shape=jax.ShapeDtypeStruct((nb, seq, ATTN_WIDTH), bf16),
        compiler_params=_params("arbitrary", "arbitrary"),
        name="lat_attn",
    )(sink, q3, k3, vt, ck, cv)
    return out.reshape(nb * seq, ATTN_WIDTH)


def _route(logits):
    lane = lax.broadcasted_iota(jnp.int32, logits.shape, 1)
    lane_f = lane.astype(f32)
    big = jnp.float32(LANES)
    is_c = jnp.logical_and(lane >= COARSE_LANE0, lane < COARSE_LANE0 + N_GROUPS)
    lc = jnp.where(is_c, logits, -jnp.inf)
    mc = jnp.max(lc, axis=-1, keepdims=True)
    grp = jnp.min(jnp.where(lc == mc, lane_f, big), axis=-1, keepdims=True) - COARSE_LANE0
    pg = 1.0 / jnp.sum(jnp.exp(lc - mc), axis=-1, keepdims=True)
    in_g = jnp.floor(lane_f * (1.0 / EXPERTS_PER_GROUP)) == grp
    fl = jnp.where(in_g, logits, -jnp.inf)
    m1 = jnp.max(fl, axis=-1, keepdims=True)
    i1 = jnp.min(jnp.where(fl == m1, lane_f, big), axis=-1, keepdims=True)
    fl2 = jnp.where(lane_f == i1, -jnp.inf, fl)
    m2 = jnp.max(fl2, axis=-1, keepdims=True)
    i2 = jnp.min(jnp.where(fl2 == m2, lane_f, big), axis=-1, keepdims=True)
    e2 = jnp.exp(m2 - m1)
    p1 = pg / (1.0 + e2)
    p2 = pg * e2 / (1.0 + e2)
    packed = jnp.where(lane == 0, i1, jnp.where(lane == 1, i2, jnp.where(lane == 2, p1, p2)))
    return jnp.where(lane < 4, packed, 0.0)


def _post_kernel(x_ref, attn_ref, u_ref, up_ref, un_ref, bg_ref, mod_ref, ag_ref, cg_ref, cw_ref,
                 cb_ref, wo_ref, n2_ref, wr_ref, br_ref, x1_ref, h2_ref, route_ref, *, tm, seq):
    i = pl.program_id(0)
    u = u_ref[...].astype(f32)
    rows = lax.broadcasted_iota(jnp.int32, (tm, 1), 0)
    spos = (i * tm + rows) % seq
    u_dn = jnp.where(rows == 0, up_ref[...].astype(f32)[BF16_SUBLANES - 1:, :],
                     pltpu.roll(u, 1, axis=0))
    u_dn = jnp.where(spos == 0, 0.0, u_dn)
    u_up = jnp.where(rows == tm - 1, un_ref[...].astype(f32)[0:1, :], pltpu.roll(u, tm - 1, axis=0))
    u_up = jnp.where(spos == seq - 1, 0.0, u_up)
    y = u_dn * cw_ref[0:1, :] + u * cw_ref[1:2, :] + u_up * cw_ref[2:3, :] + cb_ref[...]
    conv = bg_ref[...].astype(f32) * y
    attn_n = _rms(attn_ref[...].astype(f32)) * ag_ref[...]
    conv_n = _rms(conv) * cg_ref[...]
    mixed = (_dot(attn_n.astype(bf16), wo_ref[0:ATTN_WIDTH, :])
             + _dot(conv_n.astype(bf16), wo_ref[ATTN_WIDTH:, :]))
    x1 = x_ref[...] + mod_ref[:, G1:G1 + D_MODEL] * mixed
    x1_ref[...] = x1
    h2 = _rms(x1) * n2_ref[...]
    h2 = h2 * (1.0 + mod_ref[:, SC2:SC2 + D_MODEL]) + mod_ref[:, SH2:SH2 + D_MODEL]
    for c in range(D_MODEL // LANES):
        h2_ref[pl.ds(c, tm, stride=F32_SUBLANES), :] = h2[:, c * LANES:(c + 1) * LANES]
    h2_hi = h2.astype(bf16)
    h2_lo = (h2 - h2_hi.astype(f32)).astype(bf16)
    both = _dot(h2_hi, wr_ref[...])
    logits = (both[:, :LANES] + both[:, LANES:] + _dot(h2_lo, wr_ref[:, :LANES])) + br_ref[...]
    route_ref[...] = _route(logits)


def _post(x, attn, u, bg, mod3, p, *, seq, mod_row0, tm=256):
    t = x.shape[0]
    tiles_per_seq = seq // tm
    halo = BF16_SUBLANES
    n_halo = t // halo
    row = lambda i: (mod_row0 + (i // tiles_per_seq if mod_row0 else 0), 0, 0)
    tile = lambda w: pl.BlockSpec((tm, w), lambda i: (i, 0))
    full = lambda a: pl.BlockSpec(a.shape, lambda i: (0,) * a.ndim)
    small = [p["attn_out_g"], p["conv_out_g"], p["conv_w"], p["conv_b"], p["w_o"], p["norm2_g"],
             p["w_router"], p["b_router"]]
    return pl.pallas_call(
        functools.partial(_post_kernel, tm=tm, seq=seq),
        grid=(t // tm,),
        in_specs=[tile(D_MODEL), tile(ATTN_WIDTH), tile(CONV_WIDTH),
                  pl.BlockSpec((halo, CONV_WIDTH),
                               lambda i: (jnp.maximum(i * (tm // halo) - 1, 0), 0)),
                  pl.BlockSpec((halo, CONV_WIDTH),
                               lambda i: (jnp.minimum((i + 1) * (tm // halo), n_halo - 1), 0)),
                  tile(CONV_WIDTH),
                  pl.BlockSpec((None, 1, 6 * D_MODEL), row)] + [full(a) for a in small],
        out_specs=[tile(D_MODEL), pl.BlockSpec((tm * F32_SUBLANES, LANES), lambda i: (i, 0)),
                   tile(LANES)],
        out_shape=[jax.ShapeDtypeStruct((t, D_MODEL), f32),
                   jax.ShapeDtypeStruct((t * F32_SUBLANES, LANES), f32),
                   jax.ShapeDtypeStruct((t, LANES), f32)],
        compiler_params=_params("arbitrary"),
        name="post",
    )(x, attn, u, u, u, bg, mod3, *small)


def _max_chunks(blk):
    n = -(-TOP_K * blk // MOE_CHUNK) + EXPERTS_PER_STEP
    return n + n % 2


def _dispatch_tables(route, blk):
    nblk = route.shape[0] // blk
    n_slots = TOP_K * blk
    flat = lambda a: a.reshape(nblk, blk, TOP_K).transpose(0, 2, 1).reshape(nblk, n_slots)
    experts = flat(route[:, 0:TOP_K].astype(jnp.int32))
    weights = flat(route[:, TOP_K:2 * TOP_K])
    slot_ids = lax.broadcasted_iota(jnp.int32, experts.shape, 1)
    _, order, w_sorted = lax.sort((experts, slot_ids, weights), dimension=1, num_keys=1)
    bounds = jnp.arange(N_EXPERTS + 1, dtype=jnp.int32)
    offs = jnp.sum(experts[:, :, None] < bounds[None, None, :], axis=1, dtype=jnp.int32)
    tail = ((0, 0), (0, MOE_CHUNK))
    src_rows = jnp.pad((order & (blk - 1)) * F32_SUBLANES, tail)
    dst_rows = jnp.pad(order * F32_SUBLANES, tail, constant_values=n_slots * F32_SUBLANES)
    w_sorted = jnp.pad(w_sorted, tail)

    eg, n_steps, max_c = EXPERTS_PER_STEP, N_EXPERTS // EXPERTS_PER_STEP, _max_chunks(blk)
    lo = offs[:, :N_EXPERTS].reshape(nblk, n_steps, eg)
    n_ch = (offs[:, 1:] - offs[:, :N_EXPERTS] + MOE_CHUNK - 1) // MOE_CHUNK
    n_ch = n_ch.reshape(nblk, n_steps, eg)
    cum = jnp.cumsum(n_ch, axis=2)
    total = cum[..., -1]
    pos = jnp.arange(max_c, dtype=jnp.int32)
    j = jnp.sum(cum[:, :, None, :] <= pos[None, None, :, None], axis=-1, dtype=jnp.int32)
    j = jnp.minimum(j, eg - 1)
    hit = j[..., None] == jnp.arange(eg, dtype=jnp.int32)
    pick = lambda a: jnp.sum(jnp.where(hit, a[:, :, None, :], 0), axis=-1)
    base = pick(lo) + (pos - pick(cum - n_ch)) * MOE_CHUNK
    valid = pos < total[..., None]
    base = jnp.where(valid, base, n_slots).reshape(nblk, 1, n_steps * max_c)
    j = jnp.where(valid, j, 0).reshape(nblk, 1, n_steps * max_c)
    n_pairs = ((total + 1) // 2)[:, None, :]
    return src_rows[:, None, :], dst_rows[:, None, :], w_sorted[:, None, :], j, base, n_pairs


def _moe_kernel(src_ref, dst_ref, wts_ref, cj_ref, cbase_g_ref, cbase_s_ref, npairs_ref,
                h_ref, wg_ref, wu_ref, wd_ref, mod_ref, fg_ref, x1_hbm, out_hbm,
                ytmp_ref, xg0_ref, og0_ref, xg1_ref, og1_ref, xbuf_ref, obuf_ref, sem_in, sem_out,
                *, blk):
    n_col = D_MODEL // LANES
    blk_id, step = pl.program_id(0), pl.program_id(1)
    last_step = step == pl.num_programs(1) - 1
    max_c = _max_chunks(blk)
    stage_refs = (xg0_ref, og0_ref, xg1_ref, og1_ref)
    n_sub = blk // FINAL_ROWS

    def x1_copy(s):
        rows = pl.ds(blk_id * blk + s * FINAL_ROWS, FINAL_ROWS)
        slot = s % FINAL_SLOTS
        return pltpu.make_async_copy(x1_hbm.at[rows, :], xbuf_ref.at[slot], sem_in.at[slot])

    def out_copy(s):
        rows = pl.ds(blk_id * blk + s * FINAL_ROWS, FINAL_ROWS)
        slot = s % FINAL_SLOTS
        return pltpu.make_async_copy(obuf_ref.at[slot], out_hbm.at[rows, :], sem_out.at[slot])

    @pl.when(last_step)
    def _():
        for s in range(min(FINAL_SLOTS, n_sub)):
            x1_copy(s).start()

    def gather(base, xg_ref):
        for r in range(MOE_CHUNK):
            src = pl.multiple_of(src_ref[0, base + r], F32_SUBLANES)
            xg_ref[pl.ds(r, F32_SUBLANES, stride=CHUNK_PITCH), :] = h_ref[pl.ds(src, F32_SUBLANES), :]

    def experts_mlp(j, xg_ref, og_ref):
        x = jnp.concatenate([xg_ref[c * CHUNK_PITCH:c * CHUNK_PITCH + MOE_CHUNK, :]
                             for c in range(n_col)], axis=1).astype(bf16)
        g = _dot(x, wg_ref[j])
        up = _dot(x, wu_ref[j])
        act = (g * jax.nn.sigmoid(g)) * up
        out = _dot(act.astype(bf16), wd_ref[j])
        for c in range(n_col):
            og_ref[c * CHUNK_PITCH:c * CHUNK_PITCH + MOE_CHUNK, :] = out[:, c * LANES:(c + 1) * LANES]

    def scatter(base, og_ref):
        for r in range(MOE_CHUNK):
            dst = pl.multiple_of(dst_ref[0, base + r], F32_SUBLANES)
            ytmp_ref[pl.ds(dst, F32_SUBLANES), :] = (
                og_ref[pl.ds(r, F32_SUBLANES, stride=CHUNK_PITCH), :] * wts_ref[0, base + r])

    def pair(pi, carry):
        chunks = [(step * max_c + 2 * pi + half, stage_refs[2 * half], stage_refs[2 * half + 1])
                  for half in range(2)]
        for p, xg_ref, _ in chunks:
            gather(cbase_g_ref[0, p], xg_ref)
        for p, xg_ref, og_ref in chunks:
            experts_mlp(cj_ref[0, p], xg_ref, og_ref)
        for p, _, og_ref in chunks:
            scatter(cbase_s_ref[0, p], og_ref)
        return carry

    lax.fori_loop(0, npairs_ref[0, step], pair, 0)

    @pl.when(last_step)
    def _():
        n = blk * F32_SUBLANES
        for k in range(1, TOP_K):
            ytmp_ref[0:n, :] = ytmp_ref[0:n, :] + ytmp_ref[k * n:(k + 1) * n, :]
        gate2 = mod_ref[:, G2:G2 + D_MODEL]
        for s in range(n_sub):
            slot = s % FINAL_SLOTS
            x1_copy(s).wait()
            if s >= FINAL_SLOTS:
                out_copy(s - FINAL_SLOTS).wait()
            first = s * FINAL_ROWS * F32_SUBLANES
            y = jnp.concatenate([ytmp_ref[pl.ds(first + c, FINAL_ROWS, stride=F32_SUBLANES), :]
                                 for c in range(n_col)], axis=1)
            x2 = xbuf_ref[slot] + gate2 * y
            obuf_ref[slot] = _rms(x2) * fg_ref[...]
            out_copy(s).start()
            if s + FINAL_SLOTS < n_sub:
                x1_copy(s + FINAL_SLOTS).start()
        for s in range(max(n_sub - FINAL_SLOTS, 0), n_sub):
            out_copy(s).wait()


def _moe(h2_tiles, tables, x1, mod3, final_g, w_gate, w_up, w_down, *, seq, mod_row0,
         blk=MOE_BLOCK):
    t = h2_tiles.shape[0] // F32_SUBLANES
    src_rows, dst_rows, w_sorted, cj, cbase, n_pairs = tables
    n_slots = TOP_K * blk
    eg, n_steps = EXPERTS_PER_STEP, N_EXPERTS // EXPERTS_PER_STEP
    blocks_per_seq = max(seq // blk, 1)
    row = lambda b, g: (mod_row0 + (b // blocks_per_seq if mod_row0 else 0), 0, 0)
    smem = lambda n: pl.BlockSpec((None, 1, n), lambda b, g: (b, 0, 0), memory_space=pltpu.SMEM)
    stage = pltpu.VMEM((D_MODEL // LANES * CHUNK_PITCH, LANES), f32)
    final_buf = pltpu.VMEM((FINAL_SLOTS, FINAL_ROWS, D_MODEL), f32)
    return pl.pallas_call(
        functools.partial(_moe_kernel, blk=blk),
        grid=(t // blk, n_steps),
        in_specs=[smem(n_slots + MOE_CHUNK)] * 3 + [smem(n_steps * _max_chunks(blk))] * 3 + [
                  smem(n_steps),
                  pl.BlockSpec((blk * F32_SUBLANES, LANES), lambda b, g: (b, 0)),
                  pl.BlockSpec((eg, D_MODEL, D_EXPERT), lambda b, g: (g, 0, 0)),
                  pl.BlockSpec((eg, D_MODEL, D_EXPERT), lambda b, g: (g, 0, 0)),
                  pl.BlockSpec((eg, D_EXPERT, D_MODEL), lambda b, g: (g, 0, 0)),
                  pl.BlockSpec((None, 1, 6 * D_MODEL), row),
                  pl.BlockSpec((1, D_MODEL), lambda b, g: (0, 0)),
                  pl.BlockSpec(memory_space=pl.ANY)],
        out_specs=pl.BlockSpec(memory_space=pl.ANY),
        out_shape=jax.ShapeDtypeStruct((t, D_MODEL), f32),
        scratch_shapes=[pltpu.VMEM(((n_slots + 1) * F32_SUBLANES, LANES), f32)] + [stage] * 4 + [
                        final_buf, final_buf,
                        pltpu.SemaphoreType.DMA((FINAL_SLOTS,)),
                        pltpu.SemaphoreType.DMA((FINAL_SLOTS,))],
        compiler_params=_params("arbitrary", "arbitrary", vmem=MOE_VMEM_LIMIT_BYTES),
        name="moe",
    )(src_rows, dst_rows, w_sorted, cj, cbase, cbase, n_pairs, h2_tiles, w_gate, w_up, w_down,
      mod3, final_g, x1)


def _mixers(x, mod3, p, *, seq, mod_row0, ctx_kv):
    latent = ctx_kv is not None
    q, k, v, u, bg = _inproj(x, mod3, p["norm1_g"], p["w_in"], seq=seq, mod_row0=mod_row0,
                             kv_dtype=bf16 if latent else f32,
                             rope_tabs=_rope_tables(seq) if latent else None)
    if latent:
        attn = _latent_attention(q, k, v, ctx_kv[0], ctx_kv[1], p["sink"], seq=seq)
        rows = np.concatenate([LATENT_HEAD_PERM, np.arange(ATTN_WIDTH, ATTN_WIDTH + CONV_WIDTH)])
        p = dict(p, attn_out_g=p["attn_out_g"][:, LATENT_HEAD_PERM], w_o=p["w_o"][rows])
    else:
        attn = _context_attention(q, k, v, p["sink"], seq=seq)
    x1, h2_tiles, route = _post(x, attn, u, bg, mod3, p, seq=seq, mod_row0=mod_row0)
    return x1, h2_tiles, route[:, :2 * TOP_K], k, v


def _experts(x1, h2_tiles, tables, mod3, p, *, seq, mod_row0):
    out = _moe(h2_tiles, tables, x1, mod3, p["final_g"], p["w_gate"], p["w_up"], p["w_down"],
               seq=seq, mod_row0=mod_row0)
    return out.reshape(-1, seq, D_MODEL)


def kernel(x_prompt, x_sample, cache_k, cache_v, c, c_ctx, w_mod, b_mod, norm1_g, w_in, conv_w,
           conv_b, sink, attn_out_g, conv_out_g, w_o, norm2_g, w_coarse, b_coarse, w_fine, b_fine,
           w_gate, w_up, w_down, final_g):
    batch, seq, _ = x_prompt.shape
    dec_batch, dec_seq, _ = x_sample.shape
    past = cache_k.shape[2]
    assert w_mod.shape[0] == 1 and 1 + dec_batch <= MOD_ROWS

    cvecs = jnp.concatenate([c_ctx[None], c, jnp.zeros((MOD_ROWS - 1 - dec_batch, D_MODEL), f32)])
    mod3 = _modulation(cvecs, w_mod[0], b_mod[0]).reshape(MOD_ROWS, 1, 6 * D_MODEL)

    pad = jnp.zeros((D_MODEL, LANES - N_EXPERTS - N_GROUPS), f32)
    p = {
        "norm1_g": norm1_g, "w_in": w_in[0].astype(bf16), "conv_w": conv_w[0], "conv_b": conv_b,
        "sink": sink[0], "attn_out_g": attn_out_g, "conv_out_g": conv_out_g,
        "w_o": w_o[0].astype(bf16), "norm2_g": norm2_g,
        "w_router": _split_bf16(jnp.concatenate([w_fine[0], w_coarse[0], pad], axis=1)),
        "b_router": jnp.concatenate([b_fine[0], b_coarse[0], pad[0]])[None],
        "w_gate": w_gate[0].astype(bf16), "w_up": w_up[0].astype(bf16),
        "w_down": w_down[0].astype(bf16), "final_g": final_g[None],
    }

    x1_p, h2_p, route_p, k_p, v_p = _mixers(x_prompt.reshape(batch * seq, D_MODEL), mod3, p,
                                            seq=seq, mod_row0=0, ctx_kv=None)
    new_k = k_p.reshape(batch, 1, seq, N_KV_HEADS, HEAD_DIM)
    new_v = v_p.reshape(batch, 1, seq, N_KV_HEADS, HEAD_DIM)

    ctx_kv = (cache_k[:, 0].reshape(dec_batch, past, KV_WIDTH),
              cache_v[:, 0].reshape(dec_batch, past, KV_WIDTH))
    x1_s, h2_s, route_s, _, _ = _mixers(x_sample.reshape(dec_batch * dec_seq, D_MODEL), mod3, p,
                                        seq=dec_seq, mod_row0=1, ctx_kv=ctx_kv)

    n_blk_p = batch * seq // MOE_BLOCK
    tables = _dispatch_tables(jnp.concatenate([route_p, route_s]), MOE_BLOCK)
    y_prompt = _experts(x1_p, h2_p, [t[:n_blk_p] for t in tables], mod3, p, seq=seq, mod_row0=0)
    y_sample = _experts(x1_s, h2_s, [t[n_blk_p:] for t in tables], mod3, p, seq=dec_seq,
                        mod_row0=1)
    return y_prompt, y_sample, new_k, new_v
```

```python
import functools

import jax
import jax.numpy as jnp
import numpy as np
from jax import lax
from jax.experimental import pallas as pl
from jax.experimental.pallas import tpu as pltpu

D_MODEL = 1024
HEAD_DIM = 64
ATTN_WIDTH = 512
N_HEADS = 8
N_KV_HEADS = 2
Q_PER_KV = 4
KV_WIDTH = 128
CONV_WIDTH = 512
CONV_K = 3
WINDOW = 128
GRID_W = 64
ROPE_BASE = 10000.0
N_FREQ = 16
N_GROUPS = 4
EXPERTS_PER_GROUP = 8
N_EXPERTS = 32
TOP_K = 2
D_EXPERT = 256
IN_WIDTH = ATTN_WIDTH + 2 * KV_WIDTH + 3 * CONV_WIDTH
EPS = 1e-6
NEG = -1e30
SCALE = HEAD_DIM ** -0.5
LOG2_E = 1.4426950408889634

LANES = 128
F32_SUBLANES = 8
BF16_SUBLANES = 16
assert D_MODEL == F32_SUBLANES * LANES

INPROJ_ROWS = 512
MOE_BLOCK = 2048
MOE_CHUNK = 160
CHUNK_PITCH = MOE_CHUNK + F32_SUBLANES
EXPERTS_PER_STEP = 4
FINAL_ROWS = 256
FINAL_SLOTS = 4
MOE_VMEM_LIMIT_BYTES = 60 * 1024 * 1024
VMEM_LIMIT_BYTES = 48 * 1024 * 1024

SH1, SC1, G1, SH2, SC2, G2 = (i * D_MODEL for i in range(6))
MOD_ROWS = 8

COARSE_LANE0 = N_EXPERTS

f32 = jnp.float32
bf16 = jnp.bfloat16


def _params(*semantics, vmem=VMEM_LIMIT_BYTES):
    return pltpu.CompilerParams(dimension_semantics=semantics, vmem_limit_bytes=vmem)


def _rms(x):
    return x * lax.rsqrt(jnp.mean(x * x, axis=-1, keepdims=True) + EPS)


def _dot(a, b):
    return jnp.dot(a, b, preferred_element_type=f32)


def _split_bf16(w):
    hi = w.astype(bf16)
    lo = (w - hi.astype(f32)).astype(bf16)
    return jnp.concatenate([hi, lo], axis=1)


def _dot_nt(a, b):
    return lax.dot_general(a, b, (((1,), (1,)), ((), ())), preferred_element_type=f32)


def _mod_kernel(cv_ref, w_ref, b_ref, o_ref):
    a = cv_ref[...]
    a = a * jax.nn.sigmoid(a)
    o_ref[...] = jnp.dot(a, w_ref[...], precision=lax.Precision.HIGHEST,
                         preferred_element_type=f32) + b_ref[...]


def _modulation(cvecs, w_mod, b_mod):
    tn = D_MODEL
    return pl.pallas_call(
        _mod_kernel,
        grid=(6 * D_MODEL // tn,),
        in_specs=[pl.BlockSpec((MOD_ROWS, D_MODEL), lambda j: (0, 0)),
                  pl.BlockSpec((D_MODEL, tn), lambda j: (0, j)),
                  pl.BlockSpec((1, tn), lambda j: (0, j))],
        out_specs=pl.BlockSpec((MOD_ROWS, tn), lambda j: (0, j)),
        out_shape=jax.ShapeDtypeStruct((MOD_ROWS, 6 * D_MODEL), f32),
        compiler_params=_params("arbitrary"),
        name="mod",
    )(cvecs, w_mod, b_mod.reshape(1, -1))


def _swap_halves(x):
    lane = lax.broadcasted_iota(jnp.int32, x.shape, 1)
    up = pltpu.roll(x, LANES - N_FREQ, axis=1)
    dn = pltpu.roll(x, N_FREQ, axis=1)
    return jnp.where((lane % (2 * N_FREQ)) < N_FREQ, up, dn)


def _rope(x, cos, sin):
    parts = []
    for c in range(x.shape[1] // LANES):
        xc = x[:, c * LANES:(c + 1) * LANES]
        parts.append(xc * cos + _swap_halves(xc) * sin)
    return parts[0] if len(parts) == 1 else jnp.concatenate(parts, axis=1)


def _inproj_kernel(*refs, rope, n_cast):
    n_in = 4 + 2 * rope + n_cast
    for src_ref, dst_ref in zip(refs[n_in - n_cast:n_in], refs[len(refs) - n_cast:]):
        dst_ref[...] = src_ref[...].astype(dst_ref.dtype)
    refs = refs[:n_in - n_cast] + refs[n_in:len(refs) - n_cast]
    if rope:
        x_ref, mod_ref, g_ref, w_ref, cos_ref, sin_ref, q_ref, k_ref, v_ref, u_ref, bg_ref = refs
    else:
        x_ref, mod_ref, g_ref, w_ref, q_ref, k_ref, v_ref, u_ref, bg_ref = refs
    h = _rms(x_ref[...]) * g_ref[...]
    h = h * (1.0 + mod_ref[:, SC1:SC1 + D_MODEL]) + mod_ref[:, SH1:SH1 + D_MODEL]
    z = _dot(h.astype(bf16), w_ref[...])
    o = 0
    q = z[:, o:o + ATTN_WIDTH]; o += ATTN_WIDTH
    k = z[:, o:o + KV_WIDTH]; o += KV_WIDTH
    v = z[:, o:o + KV_WIDTH]; o += KV_WIDTH
    bg = z[:, o:o + CONV_WIDTH]; o += CONV_WIDTH
    cg = z[:, o:o + CONV_WIDTH]; o += CONV_WIDTH
    xin = z[:, o:o + CONV_WIDTH]
    if rope:
        q = _rope(q, cos_ref[...], sin_ref[...])
        k = _rope(k, cos_ref[...], sin_ref[...])
    q_ref[...] = q.astype(q_ref.dtype)
    k_ref[...] = k.astype(k_ref.dtype)
    v_ref[...] = (v.T if rope else v).astype(v_ref.dtype)
    u_ref[...] = (cg * xin).astype(u_ref.dtype)
    bg_ref[...] = bg.astype(bg_ref.dtype)


def _inproj(x, mod3, norm_g, w_in, *, seq, mod_row0, kv_dtype, rope_tabs=None, cast=(),
            tm=INPROJ_ROWS):
    t = x.shape[0]
    tiles_per_seq = seq // tm
    row = lambda i: (mod_row0 + (i // tiles_per_seq if mod_row0 else 0), 0, 0)
    in_specs = [pl.BlockSpec((tm, D_MODEL), lambda i: (i, 0)),
                pl.BlockSpec((None, 1, 6 * D_MODEL), row),
                pl.BlockSpec((1, D_MODEL), lambda i: (0, 0)),
                pl.BlockSpec((D_MODEL, IN_WIDTH), lambda i: (0, 0))]
    args = [x, mod3, norm_g, w_in]
    if rope_tabs is not None:
        in_specs += [pl.BlockSpec((tm, LANES), lambda i: (i % tiles_per_seq, 0))] * 2
        args += list(rope_tabs)
    widths = (ATTN_WIDTH, KV_WIDTH, KV_WIDTH, CONV_WIDTH, CONV_WIDTH)
    dtypes = (bf16, kv_dtype, kv_dtype, bf16, bf16)
    out_specs = [pl.BlockSpec((tm, w), lambda i: (i, 0)) for w in widths]
    out_shape = [jax.ShapeDtypeStruct((t, w), d) for w, d in zip(widths, dtypes)]
    if rope_tabs is not None:
        out_specs[2] = pl.BlockSpec((None, KV_WIDTH, tm),
                                    lambda i: (i // tiles_per_seq, 0, i % tiles_per_seq))
        out_shape[2] = jax.ShapeDtypeStruct((t // seq, KV_WIDTH, seq), kv_dtype)
    for a in cast:
        assert a.shape[0] == t // tm
        spec = pl.BlockSpec((None,) + a.shape[1:], lambda i: (i, 0, 0))
        in_specs.append(spec)
        args.append(a)
        out_specs.append(spec)
        out_shape.append(jax.ShapeDtypeStruct(a.shape, bf16))
    return pl.pallas_call(
        functools.partial(_inproj_kernel, rope=rope_tabs is not None, n_cast=len(cast)),
        grid=(t // tm,),
        in_specs=in_specs,
        out_specs=out_specs,
        out_shape=out_shape,
        compiler_params=_params("arbitrary"),
        name="inproj_rope" if rope_tabs is not None else "inproj",
    )(*args)


def _rope_tables(n):
    t = np.arange(n)
    inv = ROPE_BASE ** (-np.arange(N_FREQ, dtype=np.float32) / N_FREQ)
    rows = (t // GRID_W).astype(np.float32)
    cols = (t % GRID_W).astype(np.float32)
    d = np.arange(LANES) % HEAD_DIM
    pos = np.where(d[None, :] < HEAD_DIM // 2, rows[:, None], cols[:, None])
    ang = jnp.asarray(pos.astype(np.float32) * inv[d % N_FREQ][None, :])
    sign = np.where((d % (2 * N_FREQ)) < N_FREQ, -1.0, 1.0).astype(np.float32)
    return jnp.cos(ang), jnp.sin(ang) * sign[None, :]


def _attend_group(q_rows, sink_col, keys, vals, masks):
    scores = []
    m = sink_col
    for kk, mk in zip(keys, masks):
        s = _dot_nt(q_rows, kk) * SCALE
        if mk is not None:
            s = jnp.where(mk, s, NEG)
        scores.append(s)
        m = jnp.maximum(m, jnp.max(s, axis=-1, keepdims=True))
    den = jnp.exp(sink_col - m)
    acc = None
    for s, vv in zip(scores, vals):
        p = jnp.exp(s - m)
        den = den + jnp.sum(p, axis=-1, keepdims=True)
        pv = _dot(p.astype(bf16), vv)
        acc = pv if acc is None else acc + pv
    return acc / den


def _heads_attention(q, sink_ref, key_sets, val_sets, masks):
    bq = q.shape[0]
    ridx = lax.broadcasted_iota(jnp.int32, (Q_PER_KV * bq, 1), 0)
    outs = []
    for g in range(N_KV_HEADS):
        heads = range(g * Q_PER_KV, (g + 1) * Q_PER_KV)
        q_rows = jnp.concatenate([q[:, h * HEAD_DIM:(h + 1) * HEAD_DIM] for h in heads], axis=0)
        sink_col = jnp.full((Q_PER_KV * bq, 1), sink_ref[g * Q_PER_KV], f32)
        for j in range(1, Q_PER_KV):
            sink_col = jnp.where(ridx >= j * bq, sink_ref[g * Q_PER_KV + j], sink_col)
        lo, hi = g * HEAD_DIM, (g + 1) * HEAD_DIM
        o = _attend_group(q_rows, sink_col, [kk[:, lo:hi] for kk in key_sets],
                          [vv[:, lo:hi] for vv in val_sets], masks)
        outs += [o[j * bq:(j + 1) * bq] for j in range(Q_PER_KV)]
    return jnp.concatenate(outs, axis=1)


def _ctx_attn_kernel(sink_ref, q_ref, k_ref, v_ref, o_ref):
    k = k_ref[...].astype(bf16)
    v = v_ref[...].astype(bf16)
    o_ref[...] = _heads_attention(q_ref[...], sink_ref, [k], [v], [None]).astype(o_ref.dtype)


def _context_attention(q, k, v, sink, *, seq):
    t = q.shape[0]
    return pl.pallas_call(
        _ctx_attn_kernel,
        grid=(t // seq,),
        in_specs=[pl.BlockSpec(memory_space=pltpu.SMEM),
                  pl.BlockSpec((seq, ATTN_WIDTH), lambda b: (b, 0)),
                  pl.BlockSpec((seq, KV_WIDTH), lambda b: (b, 0)),
                  pl.BlockSpec((seq, KV_WIDTH), lambda b: (b, 0))],
        out_specs=pl.BlockSpec((seq, ATTN_WIDTH), lambda b: (b, 0)),
        out_shape=jax.ShapeDtypeStruct((t, ATTN_WIDTH), bf16),
        compiler_params=_params("arbitrary"),
        name="ctx_attn",
    )(sink, q, k, v)


def _lat_attn_kernel(sink_ref, q_ref, k_ref, vt_ref, ck_ref, cv_ref, o_ref, *, bq, seq):
    band = bq + 2 * WINDOW
    n_col = Q_PER_KV * bq
    ck = ck_ref[...].astype(bf16)
    cvt = cv_ref[...].T.astype(bf16)
    delta = ((lax.broadcasted_iota(jnp.int32, (band, n_col), 1) & (bq - 1))
             - lax.broadcasted_iota(jnp.int32, (band, n_col), 0))
    col_head = lax.broadcasted_iota(jnp.int32, (1, n_col), 1) // bq
    sink_rows = []
    for g in range(N_KV_HEADS):
        sink_row = jnp.full((1, n_col), sink_ref[g * Q_PER_KV], f32)
        for j in range(1, Q_PER_KV):
            sink_row = jnp.where(col_head == j, sink_ref[g * Q_PER_KV + j], sink_row)
        sink_rows.append(sink_row * LOG2_E)
    ones_rows = jnp.ones((BF16_SUBLANES, 1), bf16)
    cvt_aug = [jnp.concatenate([cvt[g * HEAD_DIM:(g + 1) * HEAD_DIM, :],
                                jnp.broadcast_to(ones_rows, (BF16_SUBLANES, cvt.shape[1]))], axis=0)
               for g in range(N_KV_HEADS)]

    n_sub = q_ref.shape[0] // bq
    subs = []
    for sb in range(n_sub):
        i = pl.program_id(1) * n_sub + sb
        start = pl.multiple_of(jnp.clip(i * bq - WINDOW, 0, seq - band), LANES)
        shifted = delta + (i * bq - start + WINDOW)
        subs.append(dict(
            kb=k_ref[pl.ds(start, band), :],
            vbt=vt_ref[:, pl.ds(start, band)],
            qt=(q_ref[sb * bq:(sb + 1) * bq, :].astype(f32) * (SCALE * LOG2_E)).T.astype(bf16),
            mask=shifted.astype(jnp.uint32) <= 2 * WINDOW))

    def scores(sub, g):
        lo, hi = g * HEAD_DIM, (g + 1) * HEAD_DIM
        heads = range(g * Q_PER_KV, (g + 1) * Q_PER_KV)
        qt_g = jnp.concatenate([sub["qt"][h * HEAD_DIM:(h + 1) * HEAD_DIM, :] for h in heads],
                               axis=1)
        s_loc = jnp.where(sub["mask"], _dot(sub["kb"][:, lo:hi], qt_g), NEG)
        s_ctx = _dot(ck[:, lo:hi], qt_g)
        return s_loc, s_ctx

    def weighted_values(sub, g, s_loc, s_ctx):
        lo, hi = g * HEAD_DIM, (g + 1) * HEAD_DIM
        m = jnp.maximum(jnp.maximum(jnp.max(s_loc, axis=0, keepdims=True),
                                    jnp.max(s_ctx, axis=0, keepdims=True)), sink_rows[g])
        p_loc = jnp.exp2(s_loc - m).astype(bf16)
        p_ctx = jnp.exp2(s_ctx - m).astype(bf16)
        vbt_aug = jnp.concatenate(
            [sub["vbt"][lo:hi, :], jnp.broadcast_to(ones_rows, (BF16_SUBLANES, band))], axis=0)
        o_aug = _dot(vbt_aug, p_loc) + _dot(cvt_aug[g], p_ctx)
        den = o_aug[HEAD_DIM:HEAD_DIM + 1, :] + jnp.exp2(sink_rows[g] - m)
        return o_aug[:HEAD_DIM, :] / den

    passes = [(sb, g) for sb in range(n_sub) for g in range(N_KV_HEADS)]
    out_t = {}
    pending = scores(subs[0], 0)
    for n, (sb, g) in enumerate(passes):
        nxt = scores(subs[passes[n + 1][0]], passes[n + 1][1]) if n + 1 < len(passes) else None
        out_t[sb, g] = weighted_values(subs[sb], g, *pending)
        pending = nxt
    for sb in range(n_sub):
        blocks = [jnp.concatenate([out_t[sb, g][:, j * bq:(j + 1) * bq]
                                   for g in range(N_KV_HEADS)], axis=0).T
                  for j in range(Q_PER_KV)]
        o_ref[sb * bq:(sb + 1) * bq, :] = jnp.concatenate(blocks, axis=1).astype(o_ref.dtype)


LATENT_HEAD_PERM = np.array([(Q_PER_KV * g + j) * HEAD_DIM + d for j in range(Q_PER_KV)
                             for g in range(N_KV_HEADS) for d in range(HEAD_DIM)])


def _latent_attention(q, k, vt, ck, cv, sink, *, seq, bq=128, sub_blocks=4):
    nb, past = ck.shape[0], ck.shape[1]
    assert bq == LANES and N_KV_HEADS * HEAD_DIM == LANES
    q3, k3 = (a.reshape(nb, seq, a.shape[-1]) for a in (q, k))
    tq = bq * sub_blocks
    out = pl.pallas_call(
        functools.partial(_lat_attn_kernel, bq=bq, seq=seq),
        grid=(nb, seq // tq),
        in_specs=[pl.BlockSpec(memory_space=pltpu.SMEM),
                  pl.BlockSpec((None, tq, ATTN_WIDTH), lambda b, i: (b, i, 0)),
                  pl.BlockSpec((None, seq, KV_WIDTH), lambda b, i: (b, 0, 0)),
                  pl.BlockSpec((None, KV_WIDTH, seq), lambda b, i: (b, 0, 0)),
                  pl.BlockSpec((None, past, KV_WIDTH), lambda b, i: (b, 0, 0)),
                  pl.BlockSpec((None, past, KV_WIDTH), lambda b, i: (b, 0, 0))],
        out_specs=pl.BlockSpec((None, tq, ATTN_WIDTH), lambda b, i: (b, i, 0)),
        out_shape=jax.ShapeDtypeStruct((nb, seq, ATTN_WIDTH), bf16),
        compiler_params=_params("arbitrary", "arbitrary"),
        name="lat_attn",
    )(sink, q3, k3, vt, ck, cv)
    return out.reshape(nb * seq, ATTN_WIDTH)


def _route(logits):
    lane = lax.broadcasted_iota(jnp.int32, logits.shape, 1)
    lane_f = lane.astype(f32)
    big = jnp.float32(LANES)
    is_c = jnp.logical_and(lane >= COARSE_LANE0, lane < COARSE_LANE0 + N_GROUPS)
    lc = jnp.where(is_c, logits, -jnp.inf)
    mc = jnp.max(lc, axis=-1, keepdims=True)
    grp = jnp.min(jnp.where(lc == mc, lane_f, big), axis=-1, keepdims=True) - COARSE_LANE0
    pg = 1.0 / jnp.sum(jnp.exp(lc - mc), axis=-1, keepdims=True)
    in_g = jnp.floor(lane_f * (1.0 / EXPERTS_PER_GROUP)) == grp
    fl = jnp.where(in_g, logits, -jnp.inf)
    m1 = jnp.max(fl, axis=-1, keepdims=True)
    i1 = jnp.min(jnp.where(fl == m1, lane_f, big), axis=-1, keepdims=True)
    fl2 = jnp.where(lane_f == i1, -jnp.inf, fl)
    m2 = jnp.max(fl2, axis=-1, keepdims=True)
    i2 = jnp.min(jnp.where(fl2 == m2, lane_f, big), axis=-1, keepdims=True)
    e2 = jnp.exp(m2 - m1)
    p1 = pg / (1.0 + e2)
    p2 = pg * e2 / (1.0 + e2)
    packed = jnp.where(lane == 0, i1, jnp.where(lane == 1, i2, jnp.where(lane == 2, p1, p2)))
    return jnp.where(lane < 4, packed, 0.0)


def _post_kernel(x_ref, attn_ref, u_ref, up_ref, un_ref, bg_ref, mod_ref, ag_ref, cg_ref, cw_ref,
                 cb_ref, wo_ref, n2_ref, wr_ref, br_ref, x1_ref, h2_ref, route_ref, *, tm, seq):
    i = pl.program_id(0)
    u = u_ref[...].astype(f32)
    rows = lax.broadcasted_iota(jnp.int32, (tm, 1), 0)
    spos = (i * tm + rows) % seq
    u_dn = jnp.where(rows == 0, up_ref[...].astype(f32)[BF16_SUBLANES - 1:, :],
                     pltpu.roll(u, 1, axis=0))
    u_dn = jnp.where(spos == 0, 0.0, u_dn)
    u_up = jnp.where(rows == tm - 1, un_ref[...].astype(f32)[0:1, :], pltpu.roll(u, tm - 1, axis=0))
    u_up = jnp.where(spos == seq - 1, 0.0, u_up)
    y = u_dn * cw_ref[0:1, :] + u * cw_ref[1:2, :] + u_up * cw_ref[2:3, :] + cb_ref[...]
    conv = bg_ref[...].astype(f32) * y
    attn_n = _rms(attn_ref[...].astype(f32)) * ag_ref[...]
    conv_n = _rms(conv) * cg_ref[...]
    mixed = (_dot(attn_n.astype(bf16), wo_ref[0:ATTN_WIDTH, :])
             + _dot(conv_n.astype(bf16), wo_ref[ATTN_WIDTH:, :]))
    x1 = x_ref[...] + mod_ref[:, G1:G1 + D_MODEL] * mixed
    x1_ref[...] = x1
    h2 = _rms(x1) * n2_ref[...]
    h2 = h2 * (1.0 + mod_ref[:, SC2:SC2 + D_MODEL]) + mod_ref[:, SH2:SH2 + D_MODEL]
    for c in range(D_MODEL // LANES):
        h2_ref[pl.ds(c, tm, stride=F32_SUBLANES), :] = h2[:, c * LANES:(c + 1) * LANES]
    h2_hi = h2.astype(bf16)
    h2_lo = (h2 - h2_hi.astype(f32)).astype(bf16)
    both = _dot(h2_hi, wr_ref[...])
    logits = (both[:, :LANES] + both[:, LANES:] + _dot(h2_lo, wr_ref[:, :LANES])) + br_ref[...]
    route_ref[...] = _route(logits)


def _post(x, attn, u, bg, mod3, p, *, seq, mod_row0, tm=256):
    t = x.shape[0]
    tiles_per_seq = seq // tm
    halo = BF16_SUBLANES
    n_halo = t // halo
    row = lambda i: (mod_row0 + (i // tiles_per_seq if mod_row0 else 0), 0, 0)
    tile = lambda w: pl.BlockSpec((tm, w), lambda i: (i, 0))
    full = lambda a: pl.BlockSpec(a.shape, lambda i: (0,) * a.ndim)
    small = [p["attn_out_g"], p["conv_out_g"], p["conv_w"], p["conv_b"], p["w_o"], p["norm2_g"],
             p["w_router"], p["b_router"]]
    return pl.pallas_call(
        functools.partial(_post_kernel, tm=tm, seq=seq),
        grid=(t // tm,),
        in_specs=[tile(D_MODEL), tile(ATTN_WIDTH), tile(CONV_WIDTH),
                  pl.BlockSpec((halo, CONV_WIDTH),
                               lambda i: (jnp.maximum(i * (tm // halo) - 1, 0), 0)),
                  pl.BlockSpec((halo, CONV_WIDTH),
                               lambda i: (jnp.minimum((i + 1) * (tm // halo), n_halo - 1), 0)),
                  tile(CONV_WIDTH),
                  pl.BlockSpec((None, 1, 6 * D_MODEL), row)] + [full(a) for a in small],
        out_specs=[tile(D_MODEL), pl.BlockSpec((tm * F32_SUBLANES, LANES), lambda i: (i, 0)),
                   tile(LANES)],
        out_shape=[jax.ShapeDtypeStruct((t, D_MODEL), f32),
                   jax.ShapeDtypeStruct((t * F32_SUBLANES, LANES), f32),
                   jax.ShapeDtypeStruct((t, LANES), f32)],
        compiler_params=_params("arbitrary"),
        name="post",
    )(x, attn, u, u, u, bg, mod3, *small)


def _max_chunks(blk):
    n = -(-TOP_K * blk // MOE_CHUNK) + EXPERTS_PER_STEP
    return n + n % 2


def _dispatch_tables(route, blk):
    nblk = route.shape[0] // blk
    n_slots = TOP_K * blk
    flat = lambda a: a.reshape(nblk, blk, TOP_K).transpose(0, 2, 1).reshape(nblk, n_slots)
    experts = flat(route[:, 0:TOP_K].astype(jnp.int32))
    weights = flat(route[:, TOP_K:2 * TOP_K])
    slot_ids = lax.broadcasted_iota(jnp.int32, experts.shape, 1)
    _, order, w_sorted = lax.sort((experts, slot_ids, weights), dimension=1, num_keys=1)
    bounds = jnp.arange(N_EXPERTS + 1, dtype=jnp.int32)
    offs = jnp.sum(experts[:, :, None] < bounds[None, None, :], axis=1, dtype=jnp.int32)
    tail = ((0, 0), (0, MOE_CHUNK))
    src_rows = jnp.pad((order & (blk - 1)) * F32_SUBLANES, tail)
    dst_rows = jnp.pad(order * F32_SUBLANES, tail, constant_values=n_slots * F32_SUBLANES)
    w_sorted = jnp.pad(w_sorted, tail)

    eg, n_steps, max_c = EXPERTS_PER_STEP, N_EXPERTS // EXPERTS_PER_STEP, _max_chunks(blk)
    lo = offs[:, :N_EXPERTS].reshape(nblk, n_steps, eg)
    n_ch = (offs[:, 1:] - offs[:, :N_EXPERTS] + MOE_CHUNK - 1) // MOE_CHUNK
    n_ch = n_ch.reshape(nblk, n_steps, eg)
    cum = jnp.cumsum(n_ch, axis=2)
    total = cum[..., -1]
    pos = jnp.arange(max_c, dtype=jnp.int32)
    j = jnp.sum(cum[:, :, None, :] <= pos[None, None, :, None], axis=-1, dtype=jnp.int32)
    j = jnp.minimum(j, eg - 1)
    hit = j[..., None] == jnp.arange(eg, dtype=jnp.int32)
    pick = lambda a: jnp.sum(jnp.where(hit, a[:, :, None, :], 0), axis=-1)
    base = pick(lo) + (pos - pick(cum - n_ch)) * MOE_CHUNK
    valid = pos < total[..., None]
    base = jnp.where(valid, base, n_slots).reshape(nblk, 1, n_steps * max_c)
    j = jnp.where(valid, j, 0).reshape(nblk, 1, n_steps * max_c)
    n_pairs = ((total + 1) // 2)[:, None, :]
    return src_rows[:, None, :], dst_rows[:, None, :], w_sorted[:, None, :], j, base, n_pairs


def _moe_kernel(src_ref, dst_ref, wts_ref, cj_ref, cbase_g_ref, cbase_s_ref, npairs_ref,
                h_ref, wg_ref, wu_ref, wd_ref, mod_ref, fg_ref, x1_hbm, out_hbm,
                ytmp_ref, xg0_ref, og0_ref, xg1_ref, og1_ref, xbuf_ref, obuf_ref, sem_in, sem_out,
                *, blk):
    n_col = D_MODEL // LANES
    blk_id, step = pl.program_id(0), pl.program_id(1)
    last_step = step == pl.num_programs(1) - 1
    max_c = _max_chunks(blk)
    stage_refs = (xg0_ref, og0_ref, xg1_ref, og1_ref)
    n_sub = blk // FINAL_ROWS

    def x1_copy(s):
        rows = pl.ds(blk_id * blk + s * FINAL_ROWS, FINAL_ROWS)
        slot = s % FINAL_SLOTS
        return pltpu.make_async_copy(x1_hbm.at[rows, :], xbuf_ref.at[slot], sem_in.at[slot])

    def out_copy(s):
        rows = pl.ds(blk_id * blk + s * FINAL_ROWS, FINAL_ROWS)
        slot = s % FINAL_SLOTS
        return pltpu.make_async_copy(obuf_ref.at[slot], out_hbm.at[rows, :], sem_out.at[slot])

    @pl.when(last_step)
    def _():
        for s in range(min(FINAL_SLOTS, n_sub)):
            x1_copy(s).start()

    def gather(base, xg_ref):
        for r in range(MOE_CHUNK):
            src = pl.multiple_of(src_ref[0, base + r], F32_SUBLANES)
            xg_ref[pl.ds(r, F32_SUBLANES, stride=CHUNK_PITCH), :] = h_ref[pl.ds(src, F32_SUBLANES), :]

    def experts_mlp(j, xg_ref, og_ref):
        x = jnp.concatenate([xg_ref[c * CHUNK_PITCH:c * CHUNK_PITCH + MOE_CHUNK, :]
                             for c in range(n_col)], axis=1).astype(bf16)
        g = _dot(x, wg_ref[j])
        up = _dot(x, wu_ref[j])
        act = (g * jax.nn.sigmoid(g)) * up
        out = _dot(act.astype(bf16), wd_ref[j])
        for c in range(n_col):
            og_ref[c * CHUNK_PITCH:c * CHUNK_PITCH + MOE_CHUNK, :] = out[:, c * LANES:(c + 1) * LANES]

    def scatter(base, og_ref):
        for r in range(MOE_CHUNK):
            dst = pl.multiple_of(dst_ref[0, base + r], F32_SUBLANES)
            ytmp_ref[pl.ds(dst, F32_SUBLANES), :] = (
                og_ref[pl.ds(r, F32_SUBLANES, stride=CHUNK_PITCH), :] * wts_ref[0, base + r])

    def pair(pi, carry):
        chunks = [(step * max_c + 2 * pi + half, stage_refs[2 * half], stage_refs[2 * half + 1])
                  for half in range(2)]
        for p, xg_ref, _ in chunks:
            gather(cbase_g_ref[0, p], xg_ref)
        for p, xg_ref, og_ref in chunks:
            experts_mlp(cj_ref[0, p], xg_ref, og_ref)
        for p, _, og_ref in chunks:
            scatter(cbase_s_ref[0, p], og_ref)
        return carry

    lax.fori_loop(0, npairs_ref[0, step], pair, 0)

    @pl.when(last_step)
    def _():
        n = blk * F32_SUBLANES
        for k in range(1, TOP_K):
            ytmp_ref[0:n, :] = ytmp_ref[0:n, :] + ytmp_ref[k * n:(k + 1) * n, :]
        gate2 = mod_ref[:, G2:G2 + D_MODEL]
        for s in range(n_sub):
            slot = s % FINAL_SLOTS
            x1_copy(s).wait()
            if s >= FINAL_SLOTS:
                out_copy(s - FINAL_SLOTS).wait()
            first = s * FINAL_ROWS * F32_SUBLANES
            y = jnp.concatenate([ytmp_ref[pl.ds(first + c, FINAL_ROWS, stride=F32_SUBLANES), :]
                                 for c in range(n_col)], axis=1)
            x2 = xbuf_ref[slot] + gate2 * y
            obuf_ref[slot] = _rms(x2) * fg_ref[...]
            out_copy(s).start()
            if s + FINAL_SLOTS < n_sub:
                x1_copy(s + FINAL_SLOTS).start()
        for s in range(max(n_sub - FINAL_SLOTS, 0), n_sub):
            out_copy(s).wait()


def _moe(h2_tiles, tables, x1, mod3, final_g, w_gate, w_up, w_down, *, seq, mod_row0,
         blk=MOE_BLOCK):
    t = h2_tiles.shape[0] // F32_SUBLANES
    src_rows, dst_rows, w_sorted, cj, cbase, n_pairs = tables
    n_slots = TOP_K * blk
    eg, n_steps = EXPERTS_PER_STEP, N_EXPERTS // EXPERTS_PER_STEP
    blocks_per_seq = max(seq // blk, 1)
    row = lambda b, g: (mod_row0 + (b // blocks_per_seq if mod_row0 else 0), 0, 0)
    smem = lambda n: pl.BlockSpec((None, 1, n), lambda b, g: (b, 0, 0), memory_space=pltpu.SMEM)
    stage = pltpu.VMEM((D_MODEL // LANES * CHUNK_PITCH, LANES), f32)
    final_buf = pltpu.VMEM((FINAL_SLOTS, FINAL_ROWS, D_MODEL), f32)
    return pl.pallas_call(
        functools.partial(_moe_kernel, blk=blk),
        grid=(t // blk, n_steps),
        in_specs=[smem(n_slots + MOE_CHUNK)] * 3 + [smem(n_steps * _max_chunks(blk))] * 3 + [
                  smem(n_steps),
                  pl.BlockSpec((blk * F32_SUBLANES, LANES), lambda b, g: (b, 0)),
                  pl.BlockSpec((eg, D_MODEL, D_EXPERT), lambda b, g: (g, 0, 0)),
                  pl.BlockSpec((eg, D_MODEL, D_EXPERT), lambda b, g: (g, 0, 0)),
                  pl.BlockSpec((eg, D_EXPERT, D_MODEL), lambda b, g: (g, 0, 0)),
                  pl.BlockSpec((None, 1, 6 * D_MODEL), row),
                  pl.BlockSpec((1, D_MODEL), lambda b, g: (0, 0)),
                  pl.BlockSpec(memory_space=pl.ANY)],
        out_specs=pl.BlockSpec(memory_space=pl.ANY),
        out_shape=jax.ShapeDtypeStruct((t, D_MODEL), f32),
        scratch_shapes=[pltpu.VMEM(((n_slots + 1) * F32_SUBLANES, LANES), f32)] + [stage] * 4 + [
                        final_buf, final_buf,
                        pltpu.SemaphoreType.DMA((FINAL_SLOTS,)),
                        pltpu.SemaphoreType.DMA((FINAL_SLOTS,))],
        compiler_params=_params("arbitrary", "arbitrary", vmem=MOE_VMEM_LIMIT_BYTES),
        name="moe",
    )(src_rows, dst_rows, w_sorted, cj, cbase, cbase, n_pairs, h2_tiles, w_gate, w_up, w_down,
      mod3, final_g, x1)


def _mixers(x, mod3, p, *, seq, mod_row0, ctx_kv, cast=()):
    latent = ctx_kv is not None
    q, k, v, u, bg, *cast = _inproj(x, mod3, p["norm1_g"], p["w_in"], seq=seq, mod_row0=mod_row0,
                                    kv_dtype=bf16 if latent else f32,
                                    rope_tabs=_rope_tables(seq) if latent else None, cast=cast)
    if latent:
        attn = _latent_attention(q, k, v, ctx_kv[0], ctx_kv[1], p["sink"], seq=seq)
        rows = np.concatenate([LATENT_HEAD_PERM, np.arange(ATTN_WIDTH, ATTN_WIDTH + CONV_WIDTH)])
        p = dict(p, attn_out_g=p["attn_out_g"][:, LATENT_HEAD_PERM], w_o=p["w_o"][rows])
    else:
        attn = _context_attention(q, k, v, p["sink"], seq=seq)
    x1, h2_tiles, route = _post(x, attn, u, bg, mod3, p, seq=seq, mod_row0=mod_row0)
    return x1, h2_tiles, route[:, :2 * TOP_K], k, v, cast


def _experts(x1, h2_tiles, tables, mod3, p, *, seq, mod_row0):
    out = _moe(h2_tiles, tables, x1, mod3, p["final_g"], p["w_gate"], p["w_up"], p["w_down"],
               seq=seq, mod_row0=mod_row0)
    return out.reshape(-1, seq, D_MODEL)


def kernel(x_prompt, x_sample, cache_k, cache_v, c, c_ctx, w_mod, b_mod, norm1_g, w_in, conv_w,
           conv_b, sink, attn_out_g, conv_out_g, w_o, norm2_g, w_coarse, b_coarse, w_fine, b_fine,
           w_gate, w_up, w_down, final_g):
    batch, seq, _ = x_prompt.shape
    dec_batch, dec_seq, _ = x_sample.shape
    past = cache_k.shape[2]
    assert w_mod.shape[0] == 1 and 1 + dec_batch <= MOD_ROWS

    cvecs = jnp.concatenate([c_ctx[None], c, jnp.zeros((MOD_ROWS - 1 - dec_batch, D_MODEL), f32)])
    mod3 = _modulation(cvecs, w_mod[0], b_mod[0]).reshape(MOD_ROWS, 1, 6 * D_MODEL)

    pad = jnp.zeros((D_MODEL, LANES - N_EXPERTS - N_GROUPS), f32)
    p = {
        "norm1_g": norm1_g, "w_in": w_in[0].astype(bf16), "conv_w": conv_w[0], "conv_b": conv_b,
        "sink": sink[0], "attn_out_g": attn_out_g, "conv_out_g": conv_out_g,
        "w_o": w_o[0].astype(bf16), "norm2_g": norm2_g,
        "w_router": _split_bf16(jnp.concatenate([w_fine[0], w_coarse[0], pad], axis=1)),
        "b_router": jnp.concatenate([b_fine[0], b_coarse[0], pad[0]])[None],
        "final_g": final_g[None],
    }

    x1_p, h2_p, route_p, k_p, v_p, _ = _mixers(x_prompt.reshape(batch * seq, D_MODEL), mod3, p,
                                               seq=seq, mod_row0=0, ctx_kv=None)
    new_k = k_p.reshape(batch, 1, seq, N_KV_HEADS, HEAD_DIM)
    new_v = v_p.reshape(batch, 1, seq, N_KV_HEADS, HEAD_DIM)

    ctx_kv = (cache_k[:, 0].reshape(dec_batch, past, KV_WIDTH),
              cache_v[:, 0].reshape(dec_batch, past, KV_WIDTH))
    experts_f32 = (w_gate[0], w_up[0], w_down[0])
    ride = dec_batch * dec_seq // INPROJ_ROWS == N_EXPERTS
    x1_s, h2_s, route_s, _, _, experts_bf16 = _mixers(
        x_sample.reshape(dec_batch * dec_seq, D_MODEL), mod3, p, seq=dec_seq, mod_row0=1,
        ctx_kv=ctx_kv, cast=experts_f32 if ride else ())
    if not ride:
        experts_bf16 = [w.astype(bf16) for w in experts_f32]
    p = dict(p, w_gate=experts_bf16[0], w_up=experts_bf16[1], w_down=experts_bf16[2])

    n_blk_p = batch * seq // MOE_BLOCK
    tables = _dispatch_tables(jnp.concatenate([route_p, route_s]), MOE_BLOCK)
    y_prompt = _experts(x1_p, h2_p, [t[:n_blk_p] for t in tables], mod3, p, seq=seq, mod_row0=0)
    y_sample = _experts(x1_s, h2_s, [t[n_blk_p:] for t in tables], mod3, p, seq=dec_seq,
                        mod_row0=1)
    return y_prompt, y_sample, new_k, new_v
```

```python
import functools

import jax
import jax.numpy as jnp
import numpy as np
from jax import lax
from jax.experimental import pallas as pl
from jax.experimental.pallas import tpu as pltpu

D_MODEL = 1024
HEAD_DIM = 64
ATTN_WIDTH = 512
N_HEADS = 8
N_KV_HEADS = 2
Q_PER_KV = 4
KV_WIDTH = 128
CONV_WIDTH = 512
CONV_K = 3
WINDOW = 128
GRID_W = 64
ROPE_BASE = 10000.0
N_FREQ = 16
N_GROUPS = 4
EXPERTS_PER_GROUP = 8
N_EXPERTS = 32
TOP_K = 2
D_EXPERT = 256
IN_WIDTH = ATTN_WIDTH + 2 * KV_WIDTH + 3 * CONV_WIDTH
EPS = 1e-6
NEG = -1e30
SCALE = HEAD_DIM ** -0.5
LOG2_E = 1.4426950408889634

LANES = 128
F32_SUBLANES = 8
BF16_SUBLANES = 16
assert D_MODEL == F32_SUBLANES * LANES

INPROJ_ROWS = 512
POST_SUB = 256
MOE_BLOCK = 2048
MOE_CHUNK = 160
CHUNK_PITCH = MOE_CHUNK + F32_SUBLANES
EXPERTS_PER_STEP = 4
FINAL_ROWS = 256
FINAL_SLOTS = 4
MOE_VMEM_LIMIT_BYTES = 60 * 1024 * 1024
VMEM_LIMIT_BYTES = 48 * 1024 * 1024

SH1, SC1, G1, SH2, SC2, G2 = (i * D_MODEL for i in range(6))
MOD_ROWS = 8

COARSE_LANE0 = N_EXPERTS

f32 = jnp.float32
bf16 = jnp.bfloat16


def _params(*semantics, vmem=VMEM_LIMIT_BYTES):
    return pltpu.CompilerParams(dimension_semantics=semantics, vmem_limit_bytes=vmem)


def _rms(x):
    return x * lax.rsqrt(jnp.mean(x * x, axis=-1, keepdims=True) + EPS)


def _dot(a, b):
    return jnp.dot(a, b, preferred_element_type=f32)


def _split_bf16(w):
    hi = w.astype(bf16)
    lo = (w - hi.astype(f32)).astype(bf16)
    return jnp.concatenate([hi, lo], axis=1)


def _dot_nt(a, b):
    return lax.dot_general(a, b, (((1,), (1,)), ((), ())), preferred_element_type=f32)


def _mod_kernel(cv_ref, w_ref, b_ref, o_ref):
    a = cv_ref[...]
    a = a * jax.nn.sigmoid(a)
    o_ref[...] = jnp.dot(a, w_ref[...], precision=lax.Precision.HIGHEST,
                         preferred_element_type=f32) + b_ref[...]


def _modulation(cvecs, w_mod, b_mod):
    tn = D_MODEL
    return pl.pallas_call(
        _mod_kernel,
        grid=(6 * D_MODEL // tn,),
        in_specs=[pl.BlockSpec((MOD_ROWS, D_MODEL), lambda j: (0, 0)),
                  pl.BlockSpec((D_MODEL, tn), lambda j: (0, j)),
                  pl.BlockSpec((1, tn), lambda j: (0, j))],
        out_specs=pl.BlockSpec((MOD_ROWS, tn), lambda j: (0, j)),
        out_shape=jax.ShapeDtypeStruct((MOD_ROWS, 6 * D_MODEL), f32),
        compiler_params=_params("arbitrary"),
        name="mod",
    )(cvecs, w_mod, b_mod.reshape(1, -1))


def _swap_halves(x):
    lane = lax.broadcasted_iota(jnp.int32, x.shape, 1)
    up = pltpu.roll(x, LANES - N_FREQ, axis=1)
    dn = pltpu.roll(x, N_FREQ, axis=1)
    return jnp.where((lane % (2 * N_FREQ)) < N_FREQ, up, dn)


def _rope(x, cos, sin):
    parts = []
    for c in range(x.shape[1] // LANES):
        xc = x[:, c * LANES:(c + 1) * LANES]
        parts.append(xc * cos + _swap_halves(xc) * sin)
    return parts[0] if len(parts) == 1 else jnp.concatenate(parts, axis=1)


def _inproj_kernel(*refs, rope, n_cast):
    n_in = 4 + 2 * rope + n_cast
    for src_ref, dst_ref in zip(refs[n_in - n_cast:n_in], refs[len(refs) - n_cast:]):
        dst_ref[...] = src_ref[...].astype(dst_ref.dtype)
    refs = refs[:n_in - n_cast] + refs[n_in:len(refs) - n_cast]
    if rope:
        x_ref, mod_ref, g_ref, w_ref, cos_ref, sin_ref, q_ref, k_ref, v_ref, u_ref, bg_ref = refs
    else:
        x_ref, mod_ref, g_ref, w_ref, q_ref, k_ref, v_ref, u_ref, bg_ref = refs
    h = _rms(x_ref[...]) * g_ref[...]
    h = h * (1.0 + mod_ref[:, SC1:SC1 + D_MODEL]) + mod_ref[:, SH1:SH1 + D_MODEL]
    z = _dot(h.astype(bf16), w_ref[...])
    o = 0
    q = z[:, o:o + ATTN_WIDTH]; o += ATTN_WIDTH
    k = z[:, o:o + KV_WIDTH]; o += KV_WIDTH
    v = z[:, o:o + KV_WIDTH]; o += KV_WIDTH
    bg = z[:, o:o + CONV_WIDTH]; o += CONV_WIDTH
    cg = z[:, o:o + CONV_WIDTH]; o += CONV_WIDTH
    xin = z[:, o:o + CONV_WIDTH]
    if rope:
        q = _rope(q, cos_ref[...], sin_ref[...])
        k = _rope(k, cos_ref[...], sin_ref[...])
    q_ref[...] = q.astype(q_ref.dtype)
    k_ref[...] = k.astype(k_ref.dtype)
    v_ref[...] = (v.T if rope else v).astype(v_ref.dtype)
    u_ref[...] = (cg * xin).astype(u_ref.dtype)
    bg_ref[...] = bg.astype(bg_ref.dtype)


def _inproj(x, mod3, norm_g, w_in, *, seq, mod_row0, kv_dtype, rope_tabs=None, cast=(),
            tm=INPROJ_ROWS):
    t = x.shape[0]
    tiles_per_seq = seq // tm
    row = lambda i: (mod_row0 + (i // tiles_per_seq if mod_row0 else 0), 0, 0)
    in_specs = [pl.BlockSpec((tm, D_MODEL), lambda i: (i, 0)),
                pl.BlockSpec((None, 1, 6 * D_MODEL), row),
                pl.BlockSpec((1, D_MODEL), lambda i: (0, 0)),
                pl.BlockSpec((D_MODEL, IN_WIDTH), lambda i: (0, 0))]
    args = [x, mod3, norm_g, w_in]
    if rope_tabs is not None:
        in_specs += [pl.BlockSpec((tm, LANES), lambda i: (i % tiles_per_seq, 0))] * 2
        args += list(rope_tabs)
    widths = (ATTN_WIDTH, KV_WIDTH, KV_WIDTH, CONV_WIDTH, CONV_WIDTH)
    dtypes = (bf16, kv_dtype, kv_dtype, bf16, bf16)
    out_specs = [pl.BlockSpec((tm, w), lambda i: (i, 0)) for w in widths]
    out_shape = [jax.ShapeDtypeStruct((t, w), d) for w, d in zip(widths, dtypes)]
    if rope_tabs is not None:
        out_specs[2] = pl.BlockSpec((None, KV_WIDTH, tm),
                                    lambda i: (i // tiles_per_seq, 0, i % tiles_per_seq))
        out_shape[2] = jax.ShapeDtypeStruct((t // seq, KV_WIDTH, seq), kv_dtype)
    for a in cast:
        assert a.shape[0] == t // tm
        spec = pl.BlockSpec((None,) + a.shape[1:], lambda i: (i, 0, 0))
        in_specs.append(spec)
        args.append(a)
        out_specs.append(spec)
        out_shape.append(jax.ShapeDtypeStruct(a.shape, bf16))
    return pl.pallas_call(
        functools.partial(_inproj_kernel, rope=rope_tabs is not None, n_cast=len(cast)),
        grid=(t // tm,),
        in_specs=in_specs,
        out_specs=out_specs,
        out_shape=out_shape,
        compiler_params=_params("arbitrary"),
        name="inproj_rope" if rope_tabs is not None else "inproj",
    )(*args)


def _rope_tables(n):
    t = np.arange(n)
    inv = ROPE_BASE ** (-np.arange(N_FREQ, dtype=np.float32) / N_FREQ)
    rows = (t // GRID_W).astype(np.float32)
    cols = (t % GRID_W).astype(np.float32)
    d = np.arange(LANES) % HEAD_DIM
    pos = np.where(d[None, :] < HEAD_DIM // 2, rows[:, None], cols[:, None])
    ang = jnp.asarray(pos.astype(np.float32) * inv[d % N_FREQ][None, :])
    sign = np.where((d % (2 * N_FREQ)) < N_FREQ, -1.0, 1.0).astype(np.float32)
    return jnp.cos(ang), jnp.sin(ang) * sign[None, :]


def _attend_group(q_rows, sink_col, keys, vals, masks):
    scores = []
    m = sink_col
    for kk, mk in zip(keys, masks):
        s = _dot_nt(q_rows, kk) * SCALE
        if mk is not None:
            s = jnp.where(mk, s, NEG)
        scores.append(s)
        m = jnp.maximum(m, jnp.max(s, axis=-1, keepdims=True))
    den = jnp.exp(sink_col - m)
    acc = None
    for s, vv in zip(scores, vals):
        p = jnp.exp(s - m)
        den = den + jnp.sum(p, axis=-1, keepdims=True)
        pv = _dot(p.astype(bf16), vv)
        acc = pv if acc is None else acc + pv
    return acc / den


def _heads_attention(q, sink_ref, key_sets, val_sets, masks):
    bq = q.shape[0]
    ridx = lax.broadcasted_iota(jnp.int32, (Q_PER_KV * bq, 1), 0)
    outs = []
    for g in range(N_KV_HEADS):
        heads = range(g * Q_PER_KV, (g + 1) * Q_PER_KV)
        q_rows = jnp.concatenate([q[:, h * HEAD_DIM:(h + 1) * HEAD_DIM] for h in heads], axis=0)
        sink_col = jnp.full((Q_PER_KV * bq, 1), sink_ref[g * Q_PER_KV], f32)
        for j in range(1, Q_PER_KV):
            sink_col = jnp.where(ridx >= j * bq, sink_ref[g * Q_PER_KV + j], sink_col)
        lo, hi = g * HEAD_DIM, (g + 1) * HEAD_DIM
        o = _attend_group(q_rows, sink_col, [kk[:, lo:hi] for kk in key_sets],
                          [vv[:, lo:hi] for vv in val_sets], masks)
        outs += [o[j * bq:(j + 1) * bq] for j in range(Q_PER_KV)]
    return jnp.concatenate(outs, axis=1)


def _ctx_attn_kernel(sink_ref, q_ref, k_ref, v_ref, o_ref):
    k = k_ref[...].astype(bf16)
    v = v_ref[...].astype(bf16)
    o_ref[...] = _heads_attention(q_ref[...], sink_ref, [k], [v], [None]).astype(o_ref.dtype)


def _context_attention(q, k, v, sink, *, seq):
    t = q.shape[0]
    return pl.pallas_call(
        _ctx_attn_kernel,
        grid=(t // seq,),
        in_specs=[pl.BlockSpec(memory_space=pltpu.SMEM),
                  pl.BlockSpec((seq, ATTN_WIDTH), lambda b: (b, 0)),
                  pl.BlockSpec((seq, KV_WIDTH), lambda b: (b, 0)),
                  pl.BlockSpec((seq, KV_WIDTH), lambda b: (b, 0))],
        out_specs=pl.BlockSpec((seq, ATTN_WIDTH), lambda b: (b, 0)),
        out_shape=jax.ShapeDtypeStruct((t, ATTN_WIDTH), bf16),
        compiler_params=_params("arbitrary"),
        name="ctx_attn",
    )(sink, q, k, v)


def _lat_attn_kernel(sink_ref, q_ref, k_ref, vt_ref, ck_ref, cv_ref, o_ref, *, bq, seq):
    band = bq + 2 * WINDOW
    n_col = Q_PER_KV * bq
    ck = ck_ref[...].astype(bf16)
    cvt = cv_ref[...].T.astype(bf16)
    delta = ((lax.broadcasted_iota(jnp.int32, (band, n_col), 1) & (bq - 1))
             - lax.broadcasted_iota(jnp.int32, (band, n_col), 0))
    col_head = lax.broadcasted_iota(jnp.int32, (1, n_col), 1) // bq
    sink_rows = []
    for g in range(N_KV_HEADS):
        sink_row = jnp.full((1, n_col), sink_ref[g * Q_PER_KV], f32)
        for j in range(1, Q_PER_KV):
            sink_row = jnp.where(col_head == j, sink_ref[g * Q_PER_KV + j], sink_row)
        sink_rows.append(sink_row * LOG2_E)
    ones_rows = jnp.ones((BF16_SUBLANES, 1), bf16)
    cvt_aug = [jnp.concatenate([cvt[g * HEAD_DIM:(g + 1) * HEAD_DIM, :],
                                jnp.broadcast_to(ones_rows, (BF16_SUBLANES, cvt.shape[1]))], axis=0)
               for g in range(N_KV_HEADS)]

    n_sub = q_ref.shape[0] // bq
    subs = []
    for sb in range(n_sub):
        i = pl.program_id(1) * n_sub + sb
        start = pl.multiple_of(jnp.clip(i * bq - WINDOW, 0, seq - band), LANES)
        shifted = delta + (i * bq - start + WINDOW)
        subs.append(dict(
            kb=k_ref[pl.ds(start, band), :],
            vbt=vt_ref[:, pl.ds(start, band)],
            qt=(q_ref[sb * bq:(sb + 1) * bq, :].astype(f32) * (SCALE * LOG2_E)).T.astype(bf16),
            mask=shifted.astype(jnp.uint32) <= 2 * WINDOW))

    def scores(sub, g):
        lo, hi = g * HEAD_DIM, (g + 1) * HEAD_DIM
        heads = range(g * Q_PER_KV, (g + 1) * Q_PER_KV)
        qt_g = jnp.concatenate([sub["qt"][h * HEAD_DIM:(h + 1) * HEAD_DIM, :] for h in heads],
                               axis=1)
        s_loc = jnp.where(sub["mask"], _dot(sub["kb"][:, lo:hi], qt_g), NEG)
        s_ctx = _dot(ck[:, lo:hi], qt_g)
        return s_loc, s_ctx

    def weighted_values(sub, g, s_loc, s_ctx):
        lo, hi = g * HEAD_DIM, (g + 1) * HEAD_DIM
        m = jnp.maximum(jnp.maximum(jnp.max(s_loc, axis=0, keepdims=True),
                                    jnp.max(s_ctx, axis=0, keepdims=True)), sink_rows[g])
        p_loc = jnp.exp2(s_loc - m).astype(bf16)
        p_ctx = jnp.exp2(s_ctx - m).astype(bf16)
        vbt_aug = jnp.concatenate(
            [sub["vbt"][lo:hi, :], jnp.broadcast_to(ones_rows, (BF16_SUBLANES, band))], axis=0)
        o_aug = _dot(vbt_aug, p_loc) + _dot(cvt_aug[g], p_ctx)
        den = o_aug[HEAD_DIM:HEAD_DIM + 1, :] + jnp.exp2(sink_rows[g] - m)
        return o_aug[:HEAD_DIM, :] / den

    passes = [(sb, g) for sb in range(n_sub) for g in range(N_KV_HEADS)]
    out_t = {}
    pending = scores(subs[0], 0)
    for n, (sb, g) in enumerate(passes):
        nxt = scores(subs[passes[n + 1][0]], passes[n + 1][1]) if n + 1 < len(passes) else None
        out_t[sb, g] = weighted_values(subs[sb], g, *pending)
        pending = nxt
    for sb in range(n_sub):
        blocks = [jnp.concatenate([out_t[sb, g][:, j * bq:(j + 1) * bq]
                                   for g in range(N_KV_HEADS)], axis=0).T
                  for j in range(Q_PER_KV)]
        o_ref[sb * bq:(sb + 1) * bq, :] = jnp.concatenate(blocks, axis=1).astype(o_ref.dtype)


LATENT_HEAD_PERM = np.array([(Q_PER_KV * g + j) * HEAD_DIM + d for j in range(Q_PER_KV)
                             for g in range(N_KV_HEADS) for d in range(HEAD_DIM)])


def _latent_attention(q, k, vt, ck, cv, sink, *, seq, bq=128, sub_blocks=8):
    nb, past = ck.shape[0], ck.shape[1]
    assert bq == LANES and N_KV_HEADS * HEAD_DIM == LANES
    q3, k3 = (a.reshape(nb, seq, a.shape[-1]) for a in (q, k))
    tq = bq * sub_blocks
    out = pl.pallas_call(
        functools.partial(_lat_attn_kernel, bq=bq, seq=seq),
        grid=(nb, seq // tq),
        in_specs=[pl.BlockSpec(memory_space=pltpu.SMEM),
                  pl.BlockSpec((None, tq, ATTN_WIDTH), lambda b, i: (b, i, 0)),
                  pl.BlockSpec((None, seq, KV_WIDTH), lambda b, i: (b, 0, 0)),
                  pl.BlockSpec((None, KV_WIDTH, seq), lambda b, i: (b, 0, 0)),
                  pl.BlockSpec((None, past, KV_WIDTH), lambda b, i: (b, 0, 0)),
                  pl.BlockSpec((None, past, KV_WIDTH), lambda b, i: (b, 0, 0))],
        out_specs=pl.BlockSpec((None, tq, ATTN_WIDTH), lambda b, i: (b, i, 0)),
        out_shape=jax.ShapeDtypeStruct((nb, seq, ATTN_WIDTH), bf16),
        compiler_params=_params("arbitrary", "arbitrary"),
        name="lat_attn",
    )(sink, q3, k3, vt, ck, cv)
    return out.reshape(nb * seq, ATTN_WIDTH)


def _route(logits):
    lane = lax.broadcasted_iota(jnp.int32, logits.shape, 1)
    lane_f = lane.astype(f32)
    big = jnp.float32(LANES)
    is_c = jnp.logical_and(lane >= COARSE_LANE0, lane < COARSE_LANE0 + N_GROUPS)
    lc = jnp.where(is_c, logits, -jnp.inf)
    mc = jnp.max(lc, axis=-1, keepdims=True)
    grp = jnp.min(jnp.where(lc == mc, lane_f, big), axis=-1, keepdims=True) - COARSE_LANE0
    pg = 1.0 / jnp.sum(jnp.exp(lc - mc), axis=-1, keepdims=True)
    in_g = jnp.floor(lane_f * (1.0 / EXPERTS_PER_GROUP)) == grp
    fl = jnp.where(in_g, logits, -jnp.inf)
    m1 = jnp.max(fl, axis=-1, keepdims=True)
    i1 = jnp.min(jnp.where(fl == m1, lane_f, big), axis=-1, keepdims=True)
    fl2 = jnp.where(lane_f == i1, -jnp.inf, fl)
    m2 = jnp.max(fl2, axis=-1, keepdims=True)
    i2 = jnp.min(jnp.where(fl2 == m2, lane_f, big), axis=-1, keepdims=True)
    e2 = jnp.exp(m2 - m1)
    p1 = pg / (1.0 + e2)
    p2 = pg * e2 / (1.0 + e2)
    packed = jnp.where(lane == 0, i1, jnp.where(lane == 1, i2, jnp.where(lane == 2, p1, p2)))
    return jnp.where(lane < 4, packed, 0.0)


def _post_kernel(x_ref, attn_ref, u_ref, up_ref, un_ref, bg_ref, mod_ref, ag_ref, cg_ref, cw_ref,
                 cb_ref, wo_ref, n2_ref, wr_ref, br_ref, x1_ref, h2_ref, route_ref, *, tm, seq):
    i = pl.program_id(0)
    u = u_ref[...].astype(f32)
    rows = lax.broadcasted_iota(jnp.int32, (tm, 1), 0)
    spos = (i * tm + rows) % seq
    u_dn = jnp.where(rows == 0, up_ref[...].astype(f32)[BF16_SUBLANES - 1:, :],
                     pltpu.roll(u, 1, axis=0))
    u_dn = jnp.where(spos == 0, 0.0, u_dn)
    u_up = jnp.where(rows == tm - 1, un_ref[...].astype(f32)[0:1, :], pltpu.roll(u, tm - 1, axis=0))
    u_up = jnp.where(spos == seq - 1, 0.0, u_up)
    y = u_dn * cw_ref[0:1, :] + u * cw_ref[1:2, :] + u_up * cw_ref[2:3, :] + cb_ref[...]
    conv = bg_ref[...].astype(f32) * y
    attn_n = (_rms(attn_ref[...].astype(f32)) * ag_ref[...]).astype(bf16)
    conv_n = (_rms(conv) * cg_ref[...]).astype(bf16)
    subs = [slice(s * POST_SUB, (s + 1) * POST_SUB) for s in range(tm // POST_SUB)]
    mixed = [_dot(attn_n[rs], wo_ref[0:ATTN_WIDTH, :]) + _dot(conv_n[rs], wo_ref[ATTN_WIDTH:, :])
             for rs in subs]
    logits = []
    for s, rs in enumerate(subs):
        x1 = x_ref[rs, :] + mod_ref[:, G1:G1 + D_MODEL] * mixed[s]
        x1_ref[rs, :] = x1
        h2 = _rms(x1) * n2_ref[...]
        h2 = h2 * (1.0 + mod_ref[:, SC2:SC2 + D_MODEL]) + mod_ref[:, SH2:SH2 + D_MODEL]
        for c in range(D_MODEL // LANES):
            h2_ref[pl.ds(rs.start * F32_SUBLANES + c, POST_SUB, stride=F32_SUBLANES), :] = (
                h2[:, c * LANES:(c + 1) * LANES])
        h2_hi = h2.astype(bf16)
        h2_lo = (h2 - h2_hi.astype(f32)).astype(bf16)
        both = _dot(h2_hi, wr_ref[...])
        logits.append(both[:, :LANES] + both[:, LANES:] + _dot(h2_lo, wr_ref[:, :LANES]))
    for s, rs in enumerate(subs):
        route_ref[rs, :] = _route(logits[s] + br_ref[...])


def _post(x, attn, u, bg, mod3, p, *, seq, mod_row0, tm=4 * POST_SUB):
    t = x.shape[0]
    tiles_per_seq = max(seq // tm, 1)
    halo = BF16_SUBLANES
    n_halo = t // halo
    row = lambda i: (mod_row0 + (i // tiles_per_seq if mod_row0 else 0), 0, 0)
    tile = lambda w: pl.BlockSpec((tm, w), lambda i: (i, 0))
    full = lambda a: pl.BlockSpec(a.shape, lambda i: (0,) * a.ndim)
    small = [p["attn_out_g"], p["conv_out_g"], p["conv_w"], p["conv_b"], p["w_o"], p["norm2_g"],
             p["w_router"], p["b_router"]]
    return pl.pallas_call(
        functools.partial(_post_kernel, tm=tm, seq=seq),
        grid=(t // tm,),
        in_specs=[tile(D_MODEL), tile(ATTN_WIDTH), tile(CONV_WIDTH),
                  pl.BlockSpec((halo, CONV_WIDTH),
                               lambda i: (jnp.maximum(i * (tm // halo) - 1, 0), 0)),
                  pl.BlockSpec((halo, CONV_WIDTH),
                               lambda i: (jnp.minimum((i + 1) * (tm // halo), n_halo - 1), 0)),
                  tile(CONV_WIDTH),
                  pl.BlockSpec((None, 1, 6 * D_MODEL), row)] + [full(a) for a in small],
        out_specs=[tile(D_MODEL), pl.BlockSpec((tm * F32_SUBLANES, LANES), lambda i: (i, 0)),
                   tile(LANES)],
        out_shape=[jax.ShapeDtypeStruct((t, D_MODEL), f32),
                   jax.ShapeDtypeStruct((t * F32_SUBLANES, LANES), f32),
                   jax.ShapeDtypeStruct((t, LANES), f32)],
        compiler_params=_params("arbitrary"),
        name="post",
    )(x, attn, u, u, u, bg, mod3, *small)


def _max_chunks(blk):
    n = -(-TOP_K * blk // MOE_CHUNK) + EXPERTS_PER_STEP
    return n + n % 2


def _dispatch_tables(route, blk):
    nblk = route.shape[0] // blk
    n_slots = TOP_K * blk
    flat = lambda a: a.reshape(nblk, blk, TOP_K).transpose(0, 2, 1).reshape(nblk, n_slots)
    experts = flat(route[:, 0:TOP_K].astype(jnp.int32))
    weights = flat(route[:, TOP_K:2 * TOP_K])
    slot_ids = lax.broadcasted_iota(jnp.int32, experts.shape, 1)
    _, order, w_sorted = lax.sort((experts, slot_ids, weights), dimension=1, num_keys=1)
    bounds = jnp.arange(N_EXPERTS + 1, dtype=jnp.int32)
    offs = jnp.sum(experts[:, :, None] < bounds[None, None, :], axis=1, dtype=jnp.int32)
    tail = ((0, 0), (0, MOE_CHUNK))
    src_rows = jnp.pad((order & (blk - 1)) * F32_SUBLANES, tail)
    dst_rows = jnp.pad(order * F32_SUBLANES, tail, constant_values=n_slots * F32_SUBLANES)
    w_sorted = jnp.pad(w_sorted, tail)

    eg, n_steps, max_c = EXPERTS_PER_STEP, N_EXPERTS // EXPERTS_PER_STEP, _max_chunks(blk)
    lo = offs[:, :N_EXPERTS].reshape(nblk, n_steps, eg)
    n_ch = (offs[:, 1:] - offs[:, :N_EXPERTS] + MOE_CHUNK - 1) // MOE_CHUNK
    n_ch = n_ch.reshape(nblk, n_steps, eg)
    cum = jnp.cumsum(n_ch, axis=2)
    total = cum[..., -1]
    pos = jnp.arange(max_c, dtype=jnp.int32)
    j = jnp.sum(cum[:, :, None, :] <= pos[None, None, :, None], axis=-1, dtype=jnp.int32)
    j = jnp.minimum(j, eg - 1)
    hit = j[..., None] == jnp.arange(eg, dtype=jnp.int32)
    pick = lambda a: jnp.sum(jnp.where(hit, a[:, :, None, :], 0), axis=-1)
    base = pick(lo) + (pos - pick(cum - n_ch)) * MOE_CHUNK
    valid = pos < total[..., None]
    base = jnp.where(valid, base, n_slots).reshape(nblk, 1, n_steps * max_c)
    j = jnp.where(valid, j, 0).reshape(nblk, 1, n_steps * max_c)
    n_pairs = ((total + 1) // 2)[:, None, :]
    return src_rows[:, None, :], dst_rows[:, None, :], w_sorted[:, None, :], j, base, n_pairs


def _moe_kernel(src_ref, dst_ref, wts_ref, cj_ref, cbase_g_ref, cbase_s_ref, npairs_ref,
                h_ref, wg_ref, wu_ref, wd_ref, mod_ref, fg_ref, x1_hbm, out_hbm,
                ytmp_ref, xg0_ref, og0_ref, xg1_ref, og1_ref, xbuf_ref, obuf_ref, sem_in, sem_out,
                *, blk):
    n_col = D_MODEL // LANES
    blk_id, step = pl.program_id(0), pl.program_id(1)
    last_step = step == pl.num_programs(1) - 1
    max_c = _max_chunks(blk)
    stage_refs = (xg0_ref, og0_ref, xg1_ref, og1_ref)
    n_sub = blk // FINAL_ROWS

    def x1_copy(s):
        rows = pl.ds(blk_id * blk + s * FINAL_ROWS, FINAL_ROWS)
        slot = s % FINAL_SLOTS
        return pltpu.make_async_copy(x1_hbm.at[rows, :], xbuf_ref.at[slot], sem_in.at[slot])

    def out_copy(s):
        rows = pl.ds(blk_id * blk + s * FINAL_ROWS, FINAL_ROWS)
        slot = s % FINAL_SLOTS
        return pltpu.make_async_copy(obuf_ref.at[slot], out_hbm.at[rows, :], sem_out.at[slot])

    @pl.when(last_step)
    def _():
        for s in range(min(FINAL_SLOTS, n_sub)):
            x1_copy(s).start()

    def gather(base, xg_ref):
        for r in range(MOE_CHUNK):
            src = pl.multiple_of(src_ref[0, base + r], F32_SUBLANES)
            xg_ref[pl.ds(r, F32_SUBLANES, stride=CHUNK_PITCH), :] = h_ref[pl.ds(src, F32_SUBLANES), :]

    def experts_mlp(j, xg_ref, og_ref):
        x = jnp.concatenate([xg_ref[c * CHUNK_PITCH:c * CHUNK_PITCH + MOE_CHUNK, :]
                             for c in range(n_col)], axis=1).astype(bf16)
        g = _dot(x, wg_ref[j])
        up = _dot(x, wu_ref[j])
        act = (g * jax.nn.sigmoid(g)) * up
        out = _dot(act.astype(bf16), wd_ref[j])
        for c in range(n_col):
            og_ref[c * CHUNK_PITCH:c * CHUNK_PITCH + MOE_CHUNK, :] = out[:, c * LANES:(c + 1) * LANES]

    def scatter(base, og_ref):
        for r in range(MOE_CHUNK):
            dst = pl.multiple_of(dst_ref[0, base + r], F32_SUBLANES)
            ytmp_ref[pl.ds(dst, F32_SUBLANES), :] = (
                og_ref[pl.ds(r, F32_SUBLANES, stride=CHUNK_PITCH), :] * wts_ref[0, base + r])

    def pair(pi, carry):
        chunks = [(step * max_c + 2 * pi + half, stage_refs[2 * half], stage_refs[2 * half + 1])
                  for half in range(2)]
        for p, xg_ref, _ in chunks:
            gather(cbase_g_ref[0, p], xg_ref)
        for p, xg_ref, og_ref in chunks:
            experts_mlp(cj_ref[0, p], xg_ref, og_ref)
        for p, _, og_ref in chunks:
            scatter(cbase_s_ref[0, p], og_ref)
        return carry

    lax.fori_loop(0, npairs_ref[0, step], pair, 0)

    @pl.when(last_step)
    def _():
        n = blk * F32_SUBLANES
        for k in range(1, TOP_K):
            ytmp_ref[0:n, :] = ytmp_ref[0:n, :] + ytmp_ref[k * n:(k + 1) * n, :]
        gate2 = mod_ref[:, G2:G2 + D_MODEL]
        for s in range(n_sub):
            slot = s % FINAL_SLOTS
            x1_copy(s).wait()
            if s >= FINAL_SLOTS:
                out_copy(s - FINAL_SLOTS).wait()
            first = s * FINAL_ROWS * F32_SUBLANES
            y = jnp.concatenate([ytmp_ref[pl.ds(first + c, FINAL_ROWS, stride=F32_SUBLANES), :]
                                 for c in range(n_col)], axis=1)
            x2 = xbuf_ref[slot] + gate2 * y
            obuf_ref[slot] = _rms(x2) * fg_ref[...]
            out_copy(s).start()
            if s + FINAL_SLOTS < n_sub:
                x1_copy(s + FINAL_SLOTS).start()
        for s in range(max(n_sub - FINAL_SLOTS, 0), n_sub):
            out_copy(s).wait()


def _moe(h2_tiles, tables, x1, mod3, final_g, w_gate, w_up, w_down, *, seq, mod_row0,
         blk=MOE_BLOCK):
    t = h2_tiles.shape[0] // F32_SUBLANES
    src_rows, dst_rows, w_sorted, cj, cbase, n_pairs = tables
    n_slots = TOP_K * blk
    eg, n_steps = EXPERTS_PER_STEP, N_EXPERTS // EXPERTS_PER_STEP
    blocks_per_seq = max(seq // blk, 1)
    row = lambda b, g: (mod_row0 + (b // blocks_per_seq if mod_row0 else 0), 0, 0)
    smem = lambda n: pl.BlockSpec((None, 1, n), lambda b, g: (b, 0, 0), memory_space=pltpu.SMEM)
    stage = pltpu.VMEM((D_MODEL // LANES * CHUNK_PITCH, LANES), f32)
    final_buf = pltpu.VMEM((FINAL_SLOTS, FINAL_ROWS, D_MODEL), f32)
    return pl.pallas_call(
        functools.partial(_moe_kernel, blk=blk),
        grid=(t // blk, n_steps),
        in_specs=[smem(n_slots + MOE_CHUNK)] * 3 + [smem(n_steps * _max_chunks(blk))] * 3 + [
                  smem(n_steps),
                  pl.BlockSpec((blk * F32_SUBLANES, LANES), lambda b, g: (b, 0)),
                  pl.BlockSpec((eg, D_MODEL, D_EXPERT), lambda b, g: (g, 0, 0)),
                  pl.BlockSpec((eg, D_MODEL, D_EXPERT), lambda b, g: (g, 0, 0)),
                  pl.BlockSpec((eg, D_EXPERT, D_MODEL), lambda b, g: (g, 0, 0)),
                  pl.BlockSpec((None, 1, 6 * D_MODEL), row),
                  pl.BlockSpec((1, D_MODEL), lambda b, g: (0, 0)),
                  pl.BlockSpec(memory_space=pl.ANY)],
        out_specs=pl.BlockSpec(memory_space=pl.ANY),
        out_shape=jax.ShapeDtypeStruct((t, D_MODEL), f32),
        scratch_shapes=[pltpu.VMEM(((n_slots + 1) * F32_SUBLANES, LANES), f32)] + [stage] * 4 + [
                        final_buf, final_buf,
                        pltpu.SemaphoreType.DMA((FINAL_SLOTS,)),
                        pltpu.SemaphoreType.DMA((FINAL_SLOTS,))],
        compiler_params=_params("arbitrary", "arbitrary", vmem=MOE_VMEM_LIMIT_BYTES),
        name="moe",
    )(src_rows, dst_rows, w_sorted, cj, cbase, cbase, n_pairs, h2_tiles, w_gate, w_up, w_down,
      mod3, final_g, x1)


def _mixers(x, mod3, p, *, seq, mod_row0, ctx_kv, cast=()):
    latent = ctx_kv is not None
    q, k, v, u, bg, *cast = _inproj(x, mod3, p["norm1_g"], p["w_in"], seq=seq, mod_row0=mod_row0,
                                    kv_dtype=bf16 if latent else f32,
                                    rope_tabs=_rope_tables(seq) if latent else None, cast=cast)
    if latent:
        attn = _latent_attention(q, k, v, ctx_kv[0], ctx_kv[1], p["sink"], seq=seq)
        rows = np.concatenate([LATENT_HEAD_PERM, np.arange(ATTN_WIDTH, ATTN_WIDTH + CONV_WIDTH)])
        p = dict(p, attn_out_g=p["attn_out_g"][:, LATENT_HEAD_PERM], w_o=p["w_o"][rows])
    else:
        attn = _context_attention(q, k, v, p["sink"], seq=seq)
    x1, h2_tiles, route = _post(x, attn, u, bg, mod3, p, seq=seq, mod_row0=mod_row0)
    return x1, h2_tiles, route[:, :2 * TOP_K], k, v, cast


def _experts(x1, h2_tiles, tables, mod3, p, *, seq, mod_row0):
    out = _moe(h2_tiles, tables, x1, mod3, p["final_g"], p["w_gate"], p["w_up"], p["w_down"],
               seq=seq, mod_row0=mod_row0)
    return out.reshape(-1, seq, D_MODEL)


def kernel(x_prompt, x_sample, cache_k, cache_v, c, c_ctx, w_mod, b_mod, norm1_g, w_in, conv_w,
           conv_b, sink, attn_out_g, conv_out_g, w_o, norm2_g, w_coarse, b_coarse, w_fine, b_fine,
           w_gate, w_up, w_down, final_g):
    batch, seq, _ = x_prompt.shape
    dec_batch, dec_seq, _ = x_sample.shape
    past = cache_k.shape[2]
    assert w_mod.shape[0] == 1 and 1 + dec_batch <= MOD_ROWS

    cvecs = jnp.concatenate([c_ctx[None], c, jnp.zeros((MOD_ROWS - 1 - dec_batch, D_MODEL), f32)])
    mod3 = _modulation(cvecs, w_mod[0], b_mod[0]).reshape(MOD_ROWS, 1, 6 * D_MODEL)

    pad = jnp.zeros((D_MODEL, LANES - N_EXPERTS - N_GROUPS), f32)
    p = {
        "norm1_g": norm1_g, "w_in": w_in[0].astype(bf16), "conv_w": conv_w[0], "conv_b": conv_b,
        "sink": sink[0], "attn_out_g": attn_out_g, "conv_out_g": conv_out_g,
        "w_o": w_o[0].astype(bf16), "norm2_g": norm2_g,
        "w_router": _split_bf16(jnp.concatenate([w_fine[0], w_coarse[0], pad], axis=1)),
        "b_router": jnp.concatenate([b_fine[0], b_coarse[0], pad[0]])[None],
        "final_g": final_g[None],
    }

    x1_p, h2_p, route_p, k_p, v_p, _ = _mixers(x_prompt.reshape(batch * seq, D_MODEL), mod3, p,
                                               seq=seq, mod_row0=0, ctx_kv=None)
    new_k = k_p.reshape(batch, 1, seq, N_KV_HEADS, HEAD_DIM)
    new_v = v_p.reshape(batch, 1, seq, N_KV_HEADS, HEAD_DIM)

    ctx_kv = (cache_k[:, 0].reshape(dec_batch, past, KV_WIDTH),
              cache_v[:, 0].reshape(dec_batch, past, KV_WIDTH))
    experts_f32 = (w_gate[0], w_up[0], w_down[0])
    ride = dec_batch * dec_seq // INPROJ_ROWS == N_EXPERTS
    x1_s, h2_s, route_s, _, _, experts_bf16 = _mixers(
        x_sample.reshape(dec_batch * dec_seq, D_MODEL), mod3, p, seq=dec_seq, mod_row0=1,
        ctx_kv=ctx_kv, cast=experts_f32 if ride else ())
    if not ride:
        experts_bf16 = [w.astype(bf16) for w in experts_f32]
    p = dict(p, w_gate=experts_bf16[0], w_up=experts_bf16[1], w_down=experts_bf16[2])

    n_blk_p = batch * seq // MOE_BLOCK
    tables = _dispatch_tables(jnp.concatenate([route_p, route_s]), MOE_BLOCK)
    y_prompt = _experts(x1_p, h2_p, [t[:n_blk_p] for t in tables], mod3, p, seq=seq, mod_row0=0)
    y_sample = _experts(x1_s, h2_s, [t[n_blk_p:] for t in tables], mod3, p, seq=dec_seq,
                        mod_row0=1)
    return y_prompt, y_sample, new_k, new_v
```

```python
import functools

import jax
import jax.numpy as jnp
import numpy as np
from jax import lax
from jax.experimental import pallas as pl
from jax.experimental.pallas import tpu as pltpu

D_MODEL = 1024
HEAD_DIM = 64
ATTN_WIDTH = 512
N_HEADS = 8
N_KV_HEADS = 2
Q_PER_KV = 4
KV_WIDTH = 128
CONV_WIDTH = 512
CONV_K = 3
WINDOW = 128
GRID_W = 64
ROPE_BASE = 10000.0
N_FREQ = 16
N_GROUPS = 4
EXPERTS_PER_GROUP = 8
N_EXPERTS = 32
TOP_K = 2
D_EXPERT = 256
IN_WIDTH = ATTN_WIDTH + 2 * KV_WIDTH + 3 * CONV_WIDTH
EPS = 1e-6
NEG = -1e30
SCALE = HEAD_DIM ** -0.5
LOG2_E = 1.4426950408889634

LANES = 128
F32_SUBLANES = 8
BF16_SUBLANES = 16
assert D_MODEL == F32_SUBLANES * LANES

INPROJ_ROWS = 512
INPROJ_SUB = 256
POST_SUB = 256
MOE_BLOCK = 2048
MOE_CHUNK = 160
CHUNK_PITCH = MOE_CHUNK + F32_SUBLANES
EXPERTS_PER_STEP = 4
FINAL_ROWS = 256
FINAL_SLOTS = 4
MOE_VMEM_LIMIT_BYTES = 60 * 1024 * 1024
VMEM_LIMIT_BYTES = 48 * 1024 * 1024

SH1, SC1, G1, SH2, SC2, G2 = (i * D_MODEL for i in range(6))
MOD_ROWS = 8

COARSE_LANE0 = N_EXPERTS

f32 = jnp.float32
bf16 = jnp.bfloat16


def _params(*semantics, vmem=VMEM_LIMIT_BYTES):
    return pltpu.CompilerParams(dimension_semantics=semantics, vmem_limit_bytes=vmem)


def _rms(x):
    return x * lax.rsqrt(jnp.mean(x * x, axis=-1, keepdims=True) + EPS)


def _dot(a, b):
    return jnp.dot(a, b, preferred_element_type=f32)


def _split_bf16(w):
    hi = w.astype(bf16)
    lo = (w - hi.astype(f32)).astype(bf16)
    return jnp.concatenate([hi, lo], axis=1)


def _dot_nt(a, b):
    return lax.dot_general(a, b, (((1,), (1,)), ((), ())), preferred_element_type=f32)


def _mod_kernel(cv_ref, w_ref, b_ref, o_ref):
    a = cv_ref[...]
    a = a * jax.nn.sigmoid(a)
    o_ref[...] = jnp.dot(a, w_ref[...], precision=lax.Precision.HIGHEST,
                         preferred_element_type=f32) + b_ref[...]


def _modulation(cvecs, w_mod, b_mod):
    tn = 2 * D_MODEL
    return pl.pallas_call(
        _mod_kernel,
        grid=(6 * D_MODEL // tn,),
        in_specs=[pl.BlockSpec((MOD_ROWS, D_MODEL), lambda j: (0, 0)),
                  pl.BlockSpec((D_MODEL, tn), lambda j: (0, j)),
                  pl.BlockSpec((1, tn), lambda j: (0, j))],
        out_specs=pl.BlockSpec((MOD_ROWS, tn), lambda j: (0, j)),
        out_shape=jax.ShapeDtypeStruct((MOD_ROWS, 6 * D_MODEL), f32),
        compiler_params=_params("arbitrary"),
        name="mod",
    )(cvecs, w_mod, b_mod.reshape(1, -1))


def _swap_halves(x):
    lane = lax.broadcasted_iota(jnp.int32, x.shape, 1)
    up = pltpu.roll(x, LANES - N_FREQ, axis=1)
    dn = pltpu.roll(x, N_FREQ, axis=1)
    return jnp.where((lane % (2 * N_FREQ)) < N_FREQ, up, dn)


def _rope(x, cos, sin):
    parts = []
    for c in range(x.shape[1] // LANES):
        xc = x[:, c * LANES:(c + 1) * LANES]
        parts.append(xc * cos + _swap_halves(xc) * sin)
    return parts[0] if len(parts) == 1 else jnp.concatenate(parts, axis=1)


def _inproj_kernel(*refs, rope, n_cast):
    n_in = 4 + 2 * rope + n_cast
    for src_ref, dst_ref in zip(refs[n_in - n_cast:n_in], refs[len(refs) - n_cast:]):
        dst_ref[...] = src_ref[...].astype(dst_ref.dtype)
    refs = refs[:n_in - n_cast] + refs[n_in:len(refs) - n_cast]
    if rope:
        x_ref, mod_ref, g_ref, w_ref, cos_ref, sin_ref, q_ref, k_ref, v_ref, u_ref, bg_ref = refs
    else:
        x_ref, mod_ref, g_ref, w_ref, q_ref, k_ref, v_ref, u_ref, bg_ref = refs
    tm = x_ref.shape[0]
    subs = [slice(s * INPROJ_SUB, (s + 1) * INPROJ_SUB) for s in range(tm // INPROJ_SUB)]
    zs = []
    for rs in subs:
        h = _rms(x_ref[rs, :]) * g_ref[...]
        h = h * (1.0 + mod_ref[:, SC1:SC1 + D_MODEL]) + mod_ref[:, SH1:SH1 + D_MODEL]
        zs.append(_dot(h.astype(bf16), w_ref[...]))
    for rs, z in zip(subs, zs):
        o = 0
        q = z[:, o:o + ATTN_WIDTH]; o += ATTN_WIDTH
        k = z[:, o:o + KV_WIDTH]; o += KV_WIDTH
        v = z[:, o:o + KV_WIDTH]; o += KV_WIDTH
        bg = z[:, o:o + CONV_WIDTH]; o += CONV_WIDTH
        cg = z[:, o:o + CONV_WIDTH]; o += CONV_WIDTH
        xin = z[:, o:o + CONV_WIDTH]
        if rope:
            q = _rope(q, cos_ref[rs, :], sin_ref[rs, :])
            k = _rope(k, cos_ref[rs, :], sin_ref[rs, :])
        q_ref[rs, :] = q.astype(q_ref.dtype)
        k_ref[rs, :] = k.astype(k_ref.dtype)
        if rope:
            v_ref[:, rs] = v.T.astype(v_ref.dtype)
        else:
            v_ref[rs, :] = v.astype(v_ref.dtype)
        u_ref[rs, :] = (cg * xin).astype(u_ref.dtype)
        bg_ref[rs, :] = bg.astype(bg_ref.dtype)


def _inproj(x, mod3, norm_g, w_in, *, seq, mod_row0, kv_dtype, rope_tabs=None, cast=(),
            tm=INPROJ_ROWS):
    t = x.shape[0]
    tiles_per_seq = seq // tm
    row = lambda i: (mod_row0 + (i // tiles_per_seq if mod_row0 else 0), 0, 0)
    in_specs = [pl.BlockSpec((tm, D_MODEL), lambda i: (i, 0)),
                pl.BlockSpec((None, 1, 6 * D_MODEL), row),
                pl.BlockSpec((1, D_MODEL), lambda i: (0, 0)),
                pl.BlockSpec((D_MODEL, IN_WIDTH), lambda i: (0, 0))]
    args = [x, mod3, norm_g, w_in]
    if rope_tabs is not None:
        in_specs += [pl.BlockSpec((tm, LANES), lambda i: (i % tiles_per_seq, 0))] * 2
        args += list(rope_tabs)
    widths = (ATTN_WIDTH, KV_WIDTH, KV_WIDTH, CONV_WIDTH, CONV_WIDTH)
    dtypes = (bf16, kv_dtype, kv_dtype, bf16, bf16)
    out_specs = [pl.BlockSpec((tm, w), lambda i: (i, 0)) for w in widths]
    out_shape = [jax.ShapeDtypeStruct((t, w), d) for w, d in zip(widths, dtypes)]
    if rope_tabs is not None:
        out_specs[2] = pl.BlockSpec((None, KV_WIDTH, tm),
                                    lambda i: (i // tiles_per_seq, 0, i % tiles_per_seq))
        out_shape[2] = jax.ShapeDtypeStruct((t // seq, KV_WIDTH, seq), kv_dtype)
    for a in cast:
        assert a.shape[0] == t // tm
        spec = pl.BlockSpec((None,) + a.shape[1:], lambda i: (i, 0, 0))
        in_specs.append(spec)
        args.append(a)
        out_specs.append(spec)
        out_shape.append(jax.ShapeDtypeStruct(a.shape, bf16))
    return pl.pallas_call(
        functools.partial(_inproj_kernel, rope=rope_tabs is not None, n_cast=len(cast)),
        grid=(t // tm,),
        in_specs=in_specs,
        out_specs=out_specs,
        out_shape=out_shape,
        compiler_params=_params("arbitrary"),
        name="inproj_rope" if rope_tabs is not None else "inproj",
    )(*args)


def _rope_tables(n):
    t = np.arange(n)
    inv = ROPE_BASE ** (-np.arange(N_FREQ, dtype=np.float32) / N_FREQ)
    rows = (t // GRID_W).astype(np.float32)
    cols = (t % GRID_W).astype(np.float32)
    d = np.arange(LANES) % HEAD_DIM
    pos = np.where(d[None, :] < HEAD_DIM // 2, rows[:, None], cols[:, None])
    ang = jnp.asarray(pos.astype(np.float32) * inv[d % N_FREQ][None, :])
    sign = np.where((d % (2 * N_FREQ)) < N_FREQ, -1.0, 1.0).astype(np.float32)
    return jnp.cos(ang), jnp.sin(ang) * sign[None, :]


def _attend_group(q_rows, sink_col, keys, vals, masks):
    scores = []
    m = sink_col
    for kk, mk in zip(keys, masks):
        s = _dot_nt(q_rows, kk) * SCALE
        if mk is not None:
            s = jnp.where(mk, s, NEG)
        scores.append(s)
        m = jnp.maximum(m, jnp.max(s, axis=-1, keepdims=True))
    den = jnp.exp(sink_col - m)
    acc = None
    for s, vv in zip(scores, vals):
        p = jnp.exp(s - m)
        den = den + jnp.sum(p, axis=-1, keepdims=True)
        pv = _dot(p.astype(bf16), vv)
        acc = pv if acc is None else acc + pv
    return acc / den


def _heads_attention(q, sink_ref, key_sets, val_sets, masks):
    bq = q.shape[0]
    ridx = lax.broadcasted_iota(jnp.int32, (Q_PER_KV * bq, 1), 0)
    outs = []
    for g in range(N_KV_HEADS):
        heads = range(g * Q_PER_KV, (g + 1) * Q_PER_KV)
        q_rows = jnp.concatenate([q[:, h * HEAD_DIM:(h + 1) * HEAD_DIM] for h in heads], axis=0)
        sink_col = jnp.full((Q_PER_KV * bq, 1), sink_ref[g * Q_PER_KV], f32)
        for j in range(1, Q_PER_KV):
            sink_col = jnp.where(ridx >= j * bq, sink_ref[g * Q_PER_KV + j], sink_col)
        lo, hi = g * HEAD_DIM, (g + 1) * HEAD_DIM
        o = _attend_group(q_rows, sink_col, [kk[:, lo:hi] for kk in key_sets],
                          [vv[:, lo:hi] for vv in val_sets], masks)
        outs += [o[j * bq:(j + 1) * bq] for j in range(Q_PER_KV)]
    return jnp.concatenate(outs, axis=1)


def _ctx_attn_kernel(sink_ref, q_ref, k_ref, v_ref, o_ref):
    k = k_ref[...].astype(bf16)
    v = v_ref[...].astype(bf16)
    o_ref[...] = _heads_attention(q_ref[...], sink_ref, [k], [v], [None]).astype(o_ref.dtype)


def _context_attention(q, k, v, sink, *, seq):
    t = q.shape[0]
    return pl.pallas_call(
        _ctx_attn_kernel,
        grid=(t // seq,),
        in_specs=[pl.BlockSpec(memory_space=pltpu.SMEM),
                  pl.BlockSpec((seq, ATTN_WIDTH), lambda b: (b, 0)),
                  pl.BlockSpec((seq, KV_WIDTH), lambda b: (b, 0)),
                  pl.BlockSpec((seq, KV_WIDTH), lambda b: (b, 0))],
        out_specs=pl.BlockSpec((seq, ATTN_WIDTH), lambda b: (b, 0)),
        out_shape=jax.ShapeDtypeStruct((t, ATTN_WIDTH), bf16),
        compiler_params=_params("arbitrary"),
        name="ctx_attn",
    )(sink, q, k, v)


def _lat_attn_kernel(sink_ref, q_ref, k_ref, vt_ref, ck_ref, cv_ref, o_ref, *, bq, seq):
    band = bq + 2 * WINDOW
    n_col = Q_PER_KV * bq
    ck = ck_ref[...].astype(bf16)
    cvt = cv_ref[...].T.astype(bf16)
    delta = ((lax.broadcasted_iota(jnp.int32, (band, n_col), 1) & (bq - 1))
             - lax.broadcasted_iota(jnp.int32, (band, n_col), 0))
    col_head = lax.broadcasted_iota(jnp.int32, (1, n_col), 1) // bq
    sink_rows = []
    for g in range(N_KV_HEADS):
        sink_row = jnp.full((1, n_col), sink_ref[g * Q_PER_KV], f32)
        for j in range(1, Q_PER_KV):
            sink_row = jnp.where(col_head == j, sink_ref[g * Q_PER_KV + j], sink_row)
        sink_rows.append(sink_row * LOG2_E)
    ones_rows = jnp.ones((BF16_SUBLANES, 1), bf16)
    cvt_aug = [jnp.concatenate([cvt[g * HEAD_DIM:(g + 1) * HEAD_DIM, :],
                                jnp.broadcast_to(ones_rows, (BF16_SUBLANES, cvt.shape[1]))], axis=0)
               for g in range(N_KV_HEADS)]

    n_sub = q_ref.shape[0] // bq
    subs = []
    for sb in range(n_sub):
        i = pl.program_id(1) * n_sub + sb
        start = pl.multiple_of(jnp.clip(i * bq - WINDOW, 0, seq - band), LANES)
        shifted = delta + (i * bq - start + WINDOW)
        subs.append(dict(
            kb=k_ref[pl.ds(start, band), :],
            vbt=vt_ref[:, pl.ds(start, band)],
            qt=(q_ref[sb * bq:(sb + 1) * bq, :].astype(f32) * (SCALE * LOG2_E)).T.astype(bf16),
            mask=shifted.astype(jnp.uint32) <= 2 * WINDOW))

    def scores(sub, g):
        lo, hi = g * HEAD_DIM, (g + 1) * HEAD_DIM
        heads = range(g * Q_PER_KV, (g + 1) * Q_PER_KV)
        qt_g = jnp.concatenate([sub["qt"][h * HEAD_DIM:(h + 1) * HEAD_DIM, :] for h in heads],
                               axis=1)
        s_loc = jnp.where(sub["mask"], _dot(sub["kb"][:, lo:hi], qt_g), NEG)
        s_ctx = _dot(ck[:, lo:hi], qt_g)
        return s_loc, s_ctx

    def weighted_values(sub, g, s_loc, s_ctx):
        lo, hi = g * HEAD_DIM, (g + 1) * HEAD_DIM
        m = jnp.maximum(jnp.maximum(jnp.max(s_loc, axis=0, keepdims=True),
                                    jnp.max(s_ctx, axis=0, keepdims=True)), sink_rows[g])
        p_loc = jnp.exp2(s_loc - m).astype(bf16)
        p_ctx = jnp.exp2(s_ctx - m).astype(bf16)
        vbt_aug = jnp.concatenate(
            [sub["vbt"][lo:hi, :], jnp.broadcast_to(ones_rows, (BF16_SUBLANES, band))], axis=0)
        o_aug = _dot(vbt_aug, p_loc) + _dot(cvt_aug[g], p_ctx)
        den = o_aug[HEAD_DIM:HEAD_DIM + 1, :] + jnp.exp2(sink_rows[g] - m)
        return o_aug[:HEAD_DIM, :] / den

    passes = [(sb, g) for sb in range(n_sub) for g in range(N_KV_HEADS)]
    out_t = {}
    pending = scores(subs[0], 0)
    for n, (sb, g) in enumerate(passes):
        nxt = scores(subs[passes[n + 1][0]], passes[n + 1][1]) if n + 1 < len(passes) else None
        out_t[sb, g] = weighted_values(subs[sb], g, *pending)
        pending = nxt
    for sb in range(n_sub):
        blocks = [jnp.concatenate([out_t[sb, g][:, j * bq:(j + 1) * bq]
                                   for g in range(N_KV_HEADS)], axis=0).T
                  for j in range(Q_PER_KV)]
        o_ref[sb * bq:(sb + 1) * bq, :] = jnp.concatenate(blocks, axis=1).astype(o_ref.dtype)


LATENT_HEAD_PERM = np.array([(Q_PER_KV * g + j) * HEAD_DIM + d for j in range(Q_PER_KV)
                             for g in range(N_KV_HEADS) for d in range(HEAD_DIM)])


def _latent_attention(q, k, vt, ck, cv, sink, *, seq, bq=128, sub_blocks=8):
    nb, past = ck.shape[0], ck.shape[1]
    assert bq == LANES and N_KV_HEADS * HEAD_DIM == LANES
    q3, k3 = (a.reshape(nb, seq, a.shape[-1]) for a in (q, k))
    tq = bq * sub_blocks
    out = pl.pallas_call(
        functools.partial(_lat_attn_kernel, bq=bq, seq=seq),
        grid=(nb, seq // tq),
        in_specs=[pl.BlockSpec(memory_space=pltpu.SMEM),
                  pl.BlockSpec((None, tq, ATTN_WIDTH), lambda b, i: (b, i, 0)),
                  pl.BlockSpec((None, seq, KV_WIDTH), lambda b, i: (b, 0, 0)),
                  pl.BlockSpec((None, KV_WIDTH, seq), lambda b, i: (b, 0, 0)),
                  pl.BlockSpec((None, past, KV_WIDTH), lambda b, i: (b, 0, 0)),
                  pl.BlockSpec((None, past, KV_WIDTH), lambda b, i: (b, 0, 0))],
        out_specs=pl.BlockSpec((None, tq, ATTN_WIDTH), lambda b, i: (b, i, 0)),
        out_shape=jax.ShapeDtypeStruct((nb, seq, ATTN_WIDTH), bf16),
        compiler_params=_params("arbitrary", "arbitrary"),
        name="lat_attn",
    )(sink, q3, k3, vt, ck, cv)
    return out.reshape(nb * seq, ATTN_WIDTH)


def _route(logits):
    lane = lax.broadcasted_iota(jnp.int32, logits.shape, 1)
    lane_f = lane.astype(f32)
    big = jnp.float32(LANES)
    is_c = jnp.logical_and(lane >= COARSE_LANE0, lane < COARSE_LANE0 + N_GROUPS)
    lc = jnp.where(is_c, logits, -jnp.inf)
    mc = jnp.max(lc, axis=-1, keepdims=True)
    grp = jnp.min(jnp.where(lc == mc, lane_f, big), axis=-1, keepdims=True) - COARSE_LANE0
    pg = 1.0 / jnp.sum(jnp.exp(lc - mc), axis=-1, keepdims=True)
    in_g = jnp.floor(lane_f * (1.0 / EXPERTS_PER_GROUP)) == grp
    fl = jnp.where(in_g, logits, -jnp.inf)
    m1 = jnp.max(fl, axis=-1, keepdims=True)
    i1 = jnp.min(jnp.where(fl == m1, lane_f, big), axis=-1, keepdims=True)
    fl2 = jnp.where(lane_f == i1, -jnp.inf, fl)
    m2 = jnp.max(fl2, axis=-1, keepdims=True)
    i2 = jnp.min(jnp.where(fl2 == m2, lane_f, big), axis=-1, keepdims=True)
    e2 = jnp.exp(m2 - m1)
    p1 = pg / (1.0 + e2)
    p2 = pg * e2 / (1.0 + e2)
    packed = jnp.where(lane == 0, i1, jnp.where(lane == 1, i2, jnp.where(lane == 2, p1, p2)))
    return jnp.where(lane < 4, packed, 0.0)


def _post_kernel(x_ref, attn_ref, u_ref, up_ref, un_ref, bg_ref, mod_ref, ag_ref, cg_ref, cw_ref,
                 cb_ref, wo_ref, n2_ref, wr_ref, br_ref, x1_ref, h2_ref, route_ref, *, tm, seq):
    i = pl.program_id(0)
    u = u_ref[...].astype(f32)
    rows = lax.broadcasted_iota(jnp.int32, (tm, 1), 0)
    spos = (i * tm + rows) % seq
    u_dn = jnp.where(rows == 0, up_ref[...].astype(f32)[BF16_SUBLANES - 1:, :],
                     pltpu.roll(u, 1, axis=0))
    u_dn = jnp.where(spos == 0, 0.0, u_dn)
    u_up = jnp.where(rows == tm - 1, un_ref[...].astype(f32)[0:1, :], pltpu.roll(u, tm - 1, axis=0))
    u_up = jnp.where(spos == seq - 1, 0.0, u_up)
    y = u_dn * cw_ref[0:1, :] + u * cw_ref[1:2, :] + u_up * cw_ref[2:3, :] + cb_ref[...]
    conv = bg_ref[...].astype(f32) * y
    attn_n = (_rms(attn_ref[...].astype(f32)) * ag_ref[...]).astype(bf16)
    conv_n = (_rms(conv) * cg_ref[...]).astype(bf16)
    subs = [slice(s * POST_SUB, (s + 1) * POST_SUB) for s in range(tm // POST_SUB)]
    mixed = [_dot(attn_n[rs], wo_ref[0:ATTN_WIDTH, :]) + _dot(conv_n[rs], wo_ref[ATTN_WIDTH:, :])
             for rs in subs]
    logits = []
    for s, rs in enumerate(subs):
        x1 = x_ref[rs, :] + mod_ref[:, G1:G1 + D_MODEL] * mixed[s]
        x1_ref[rs, :] = x1
        h2 = _rms(x1) * n2_ref[...]
        h2 = h2 * (1.0 + mod_ref[:, SC2:SC2 + D_MODEL]) + mod_ref[:, SH2:SH2 + D_MODEL]
        for c in range(D_MODEL // LANES):
            h2_ref[pl.ds(rs.start * F32_SUBLANES + c, POST_SUB, stride=F32_SUBLANES), :] = (
                h2[:, c * LANES:(c + 1) * LANES])
        h2_hi = h2.astype(bf16)
        h2_lo = (h2 - h2_hi.astype(f32)).astype(bf16)
        both = _dot(h2_hi, wr_ref[...])
        logits.append(both[:, :LANES] + both[:, LANES:] + _dot(h2_lo, wr_ref[:, :LANES]))
    for s, rs in enumerate(subs):
        route_ref[rs, :] = _route(logits[s] + br_ref[...])


def _post(x, attn, u, bg, mod3, p, *, seq, mod_row0, tm=4 * POST_SUB):
    t = x.shape[0]
    tiles_per_seq = max(seq // tm, 1)
    halo = BF16_SUBLANES
    n_halo = t // halo
    row = lambda i: (mod_row0 + (i // tiles_per_seq if mod_row0 else 0), 0, 0)
    tile = lambda w: pl.BlockSpec((tm, w), lambda i: (i, 0))
    full = lambda a: pl.BlockSpec(a.shape, lambda i: (0,) * a.ndim)
    small = [p["attn_out_g"], p["conv_out_g"], p["conv_w"], p["conv_b"], p["w_o"], p["norm2_g"],
             p["w_router"], p["b_router"]]
    return pl.pallas_call(
        functools.partial(_post_kernel, tm=tm, seq=seq),
        grid=(t // tm,),
        in_specs=[tile(D_MODEL), tile(ATTN_WIDTH), tile(CONV_WIDTH),
                  pl.BlockSpec((halo, CONV_WIDTH),
                               lambda i: (jnp.maximum(i * (tm // halo) - 1, 0), 0)),
                  pl.BlockSpec((halo, CONV_WIDTH),
                               lambda i: (jnp.minimum((i + 1) * (tm // halo), n_halo - 1), 0)),
                  tile(CONV_WIDTH),
                  pl.BlockSpec((None, 1, 6 * D_MODEL), row)] + [full(a) for a in small],
        out_specs=[tile(D_MODEL), pl.BlockSpec((tm * F32_SUBLANES, LANES), lambda i: (i, 0)),
                   tile(LANES)],
        out_shape=[jax.ShapeDtypeStruct((t, D_MODEL), f32),
                   jax.ShapeDtypeStruct((t * F32_SUBLANES, LANES), f32),
                   jax.ShapeDtypeStruct((t, LANES), f32)],
        compiler_params=_params("arbitrary"),
        name="post",
    )(x, attn, u, u, u, bg, mod3, *small)


def _max_chunks(blk):
    n = -(-TOP_K * blk // MOE_CHUNK) + EXPERTS_PER_STEP
    return n + n % 2


def _dispatch_tables(route, blk):
    nblk = route.shape[0] // blk
    n_slots = TOP_K * blk
    flat = lambda a: a.reshape(nblk, blk, TOP_K).transpose(0, 2, 1).reshape(nblk, n_slots)
    experts = flat(route[:, 0:TOP_K].astype(jnp.int32))
    weights = flat(route[:, TOP_K:2 * TOP_K])
    slot_ids = lax.broadcasted_iota(jnp.int32, experts.shape, 1)
    _, order, w_sorted = lax.sort((experts, slot_ids, weights), dimension=1, num_keys=1)
    bounds = jnp.arange(N_EXPERTS + 1, dtype=jnp.int32)
    offs = jnp.sum(experts[:, :, None] < bounds[None, None, :], axis=1, dtype=jnp.int32)
    tail = ((0, 0), (0, MOE_CHUNK))
    src_rows = jnp.pad((order & (blk - 1)) * F32_SUBLANES, tail)
    dst_rows = jnp.pad(order * F32_SUBLANES, tail, constant_values=n_slots * F32_SUBLANES)
    w_sorted = jnp.pad(w_sorted, tail)

    eg, n_steps, max_c = EXPERTS_PER_STEP, N_EXPERTS // EXPERTS_PER_STEP, _max_chunks(blk)
    lo = offs[:, :N_EXPERTS].reshape(nblk, n_steps, eg)
    n_ch = (offs[:, 1:] - offs[:, :N_EXPERTS] + MOE_CHUNK - 1) // MOE_CHUNK
    n_ch = n_ch.reshape(nblk, n_steps, eg)
    cum = jnp.cumsum(n_ch, axis=2)
    total = cum[..., -1]
    pos = jnp.arange(max_c, dtype=jnp.int32)
    j = jnp.sum(cum[:, :, None, :] <= pos[None, None, :, None], axis=-1, dtype=jnp.int32)
    j = jnp.minimum(j, eg - 1)
    hit = j[..., None] == jnp.arange(eg, dtype=jnp.int32)
    pick = lambda a: jnp.sum(jnp.where(hit, a[:, :, None, :], 0), axis=-1)
    base = pick(lo) + (pos - pick(cum - n_ch)) * MOE_CHUNK
    valid = pos < total[..., None]
    base = jnp.where(valid, base, n_slots).reshape(nblk, 1, n_steps * max_c)
    j = jnp.where(valid, j, 0).reshape(nblk, 1, n_steps * max_c)
    return (src_rows[:, None, :], dst_rows[:, None, :], w_sorted[:, None, :], j, base,
            total[:, None, :])


def _moe_kernel(src_ref, dst_ref, wts_ref, cj_ref, cbase_g_ref, cbase_s_ref, nchunks_ref,
                h_ref, wg_ref, wu_ref, wd_ref, mod_ref, fg_ref, x1_hbm, out_hbm,
                ytmp_ref, xg0_ref, og0_ref, xg1_ref, og1_ref, xbuf_ref, obuf_ref, sem_in, sem_out,
                *, blk):
    n_col = D_MODEL // LANES
    blk_id, step = pl.program_id(0), pl.program_id(1)
    last_step = step == pl.num_programs(1) - 1
    max_c = _max_chunks(blk)
    stage_refs = (xg0_ref, og0_ref, xg1_ref, og1_ref)
    n_sub = blk // FINAL_ROWS

    def x1_copy(s):
        rows = pl.ds(blk_id * blk + s * FINAL_ROWS, FINAL_ROWS)
        slot = s % FINAL_SLOTS
        return pltpu.make_async_copy(x1_hbm.at[rows, :], xbuf_ref.at[slot], sem_in.at[slot])

    def out_copy(s):
        rows = pl.ds(blk_id * blk + s * FINAL_ROWS, FINAL_ROWS)
        slot = s % FINAL_SLOTS
        return pltpu.make_async_copy(obuf_ref.at[slot], out_hbm.at[rows, :], sem_out.at[slot])

    @pl.when(last_step)
    def _():
        for s in range(min(FINAL_SLOTS, n_sub)):
            x1_copy(s).start()

    def gather(base, xg_ref):
        for r in range(MOE_CHUNK):
            src = pl.multiple_of(src_ref[0, base + r], F32_SUBLANES)
            xg_ref[pl.ds(r, F32_SUBLANES, stride=CHUNK_PITCH), :] = h_ref[pl.ds(src, F32_SUBLANES), :]

    def experts_mlp(j, xg_ref, og_ref):
        x = jnp.concatenate([xg_ref[c * CHUNK_PITCH:c * CHUNK_PITCH + MOE_CHUNK, :]
                             for c in range(n_col)], axis=1).astype(bf16)
        g = _dot(x, wg_ref[j])
        up = _dot(x, wu_ref[j])
        act = (g * jax.nn.sigmoid(g)) * up
        out = _dot(act.astype(bf16), wd_ref[j])
        for c in range(n_col):
            og_ref[c * CHUNK_PITCH:c * CHUNK_PITCH + MOE_CHUNK, :] = out[:, c * LANES:(c + 1) * LANES]

    def scatter(base, og_ref):
        for r in range(MOE_CHUNK):
            dst = pl.multiple_of(dst_ref[0, base + r], F32_SUBLANES)
            ytmp_ref[pl.ds(dst, F32_SUBLANES), :] = (
                og_ref[pl.ds(r, F32_SUBLANES, stride=CHUNK_PITCH), :] * wts_ref[0, base + r])

    def pair(pi, carry):
        chunks = [(step * max_c + 2 * pi + half, stage_refs[2 * half], stage_refs[2 * half + 1])
                  for half in range(2)]
        for p, xg_ref, _ in chunks:
            gather(cbase_g_ref[0, p], xg_ref)
        for p, xg_ref, og_ref in chunks:
            experts_mlp(cj_ref[0, p], xg_ref, og_ref)
        for p, _, og_ref in chunks:
            scatter(cbase_s_ref[0, p], og_ref)
        return carry

    n_chunks = nchunks_ref[0, step]
    lax.fori_loop(0, n_chunks // 2, pair, 0)

    @pl.when(n_chunks % 2 == 1)
    def _():
        p = step * max_c + n_chunks - 1
        gather(cbase_g_ref[0, p], xg0_ref)
        experts_mlp(cj_ref[0, p], xg0_ref, og0_ref)
        scatter(cbase_s_ref[0, p], og0_ref)

    @pl.when(last_step)
    def _():
        n = blk * F32_SUBLANES
        for k in range(1, TOP_K):
            ytmp_ref[0:n, :] = ytmp_ref[0:n, :] + ytmp_ref[k * n:(k + 1) * n, :]
        gate2 = mod_ref[:, G2:G2 + D_MODEL]
        for s in range(n_sub):
            slot = s % FINAL_SLOTS
            x1_copy(s).wait()
            if s >= FINAL_SLOTS:
                out_copy(s - FINAL_SLOTS).wait()
            first = s * FINAL_ROWS * F32_SUBLANES
            y = jnp.concatenate([ytmp_ref[pl.ds(first + c, FINAL_ROWS, stride=F32_SUBLANES), :]
                                 for c in range(n_col)], axis=1)
            x2 = xbuf_ref[slot] + gate2 * y
            obuf_ref[slot] = _rms(x2) * fg_ref[...]
            out_copy(s).start()
            if s + FINAL_SLOTS < n_sub:
                x1_copy(s + FINAL_SLOTS).start()
        for s in range(max(n_sub - FINAL_SLOTS, 0), n_sub):
            out_copy(s).wait()


def _moe(h2_tiles, tables, x1, mod3, final_g, w_gate, w_up, w_down, *, seq, mod_row0,
         blk=MOE_BLOCK):
    t = h2_tiles.shape[0] // F32_SUBLANES
    src_rows, dst_rows, w_sorted, cj, cbase, n_pairs = tables
    n_slots = TOP_K * blk
    eg, n_steps = EXPERTS_PER_STEP, N_EXPERTS // EXPERTS_PER_STEP
    blocks_per_seq = max(seq // blk, 1)
    row = lambda b, g: (mod_row0 + (b // blocks_per_seq if mod_row0 else 0), 0, 0)
    smem = lambda n: pl.BlockSpec((None, 1, n), lambda b, g: (b, 0, 0), memory_space=pltpu.SMEM)
    stage = pltpu.VMEM((D_MODEL // LANES * CHUNK_PITCH, LANES), f32)
    final_buf = pltpu.VMEM((FINAL_SLOTS, FINAL_ROWS, D_MODEL), f32)
    return pl.pallas_call(
        functools.partial(_moe_kernel, blk=blk),
        grid=(t // blk, n_steps),
        in_specs=[smem(n_slots + MOE_CHUNK)] * 3 + [smem(n_steps * _max_chunks(blk))] * 3 + [
                  smem(n_steps),
                  pl.BlockSpec((blk * F32_SUBLANES, LANES), lambda b, g: (b, 0)),
                  pl.BlockSpec((eg, D_MODEL, D_EXPERT), lambda b, g: (g, 0, 0)),
                  pl.BlockSpec((eg, D_MODEL, D_EXPERT), lambda b, g: (g, 0, 0)),
                  pl.BlockSpec((eg, D_EXPERT, D_MODEL), lambda b, g: (g, 0, 0)),
                  pl.BlockSpec((None, 1, 6 * D_MODEL), row),
                  pl.BlockSpec((1, D_MODEL), lambda b, g: (0, 0)),
                  pl.BlockSpec(memory_space=pl.ANY)],
        out_specs=pl.BlockSpec(memory_space=pl.ANY),
        out_shape=jax.ShapeDtypeStruct((t, D_MODEL), f32),
        scratch_shapes=[pltpu.VMEM(((n_slots + 1) * F32_SUBLANES, LANES), f32)] + [stage] * 4 + [
                        final_buf, final_buf,
                        pltpu.SemaphoreType.DMA((FINAL_SLOTS,)),
                        pltpu.SemaphoreType.DMA((FINAL_SLOTS,))],
        compiler_params=_params("arbitrary", "arbitrary", vmem=MOE_VMEM_LIMIT_BYTES),
        name="moe",
    )(src_rows, dst_rows, w_sorted, cj, cbase, cbase, n_pairs, h2_tiles, w_gate, w_up, w_down,
      mod3, final_g, x1)


def _mixers(x, mod3, p, *, seq, mod_row0, ctx_kv, cast=()):
    latent = ctx_kv is not None
    q, k, v, u, bg, *cast = _inproj(x, mod3, p["norm1_g"], p["w_in"], seq=seq, mod_row0=mod_row0,
                                    kv_dtype=bf16 if latent else f32,
                                    rope_tabs=_rope_tables(seq) if latent else None, cast=cast)
    if latent:
        attn = _latent_attention(q, k, v, ctx_kv[0], ctx_kv[1], p["sink"], seq=seq)
        rows = np.concatenate([LATENT_HEAD_PERM, np.arange(ATTN_WIDTH, ATTN_WIDTH + CONV_WIDTH)])
        p = dict(p, attn_out_g=p["attn_out_g"][:, LATENT_HEAD_PERM], w_o=p["w_o"][rows])
    else:
        attn = _context_attention(q, k, v, p["sink"], seq=seq)
    x1, h2_tiles, route = _post(x, attn, u, bg, mod3, p, seq=seq, mod_row0=mod_row0)
    return x1, h2_tiles, route[:, :2 * TOP_K], k, v, cast


def _experts(x1, h2_tiles, tables, mod3, p, *, seq, mod_row0):
    out = _moe(h2_tiles, tables, x1, mod3, p["final_g"], p["w_gate"], p["w_up"], p["w_down"],
               seq=seq, mod_row0=mod_row0)
    return out.reshape(-1, seq, D_MODEL)


def kernel(x_prompt, x_sample, cache_k, cache_v, c, c_ctx, w_mod, b_mod, norm1_g, w_in, conv_w,
           conv_b, sink, attn_out_g, conv_out_g, w_o, norm2_g, w_coarse, b_coarse, w_fine, b_fine,
           w_gate, w_up, w_down, final_g):
    batch, seq, _ = x_prompt.shape
    dec_batch, dec_seq, _ = x_sample.shape
    past = cache_k.shape[2]
    assert w_mod.shape[0] == 1 and 1 + dec_batch <= MOD_ROWS

    cvecs = jnp.concatenate([c_ctx[None], c, jnp.zeros((MOD_ROWS - 1 - dec_batch, D_MODEL), f32)])
    mod3 = _modulation(cvecs, w_mod[0], b_mod[0]).reshape(MOD_ROWS, 1, 6 * D_MODEL)

    pad = jnp.zeros((D_MODEL, LANES - N_EXPERTS - N_GROUPS), f32)
    p = {
        "norm1_g": norm1_g, "w_in": w_in[0].astype(bf16), "conv_w": conv_w[0], "conv_b": conv_b,
        "sink": sink[0], "attn_out_g": attn_out_g, "conv_out_g": conv_out_g,
        "w_o": w_o[0].astype(bf16), "norm2_g": norm2_g,
        "w_router": _split_bf16(jnp.concatenate([w_fine[0], w_coarse[0], pad], axis=1)),
        "b_router": jnp.concatenate([b_fine[0], b_coarse[0], pad[0]])[None],
        "final_g": final_g[None],
    }

    x1_p, h2_p, route_p, k_p, v_p, _ = _mixers(x_prompt.reshape(batch * seq, D_MODEL), mod3, p,
                                               seq=seq, mod_row0=0, ctx_kv=None)
    new_k = k_p.reshape(batch, 1, seq, N_KV_HEADS, HEAD_DIM)
    new_v = v_p.reshape(batch, 1, seq, N_KV_HEADS, HEAD_DIM)

    ctx_kv = (cache_k[:, 0].reshape(dec_batch, past, KV_WIDTH),
              cache_v[:, 0].reshape(dec_batch, past, KV_WIDTH))
    experts_f32 = (w_gate[0], w_up[0], w_down[0])
    ride = dec_batch * dec_seq // INPROJ_ROWS == N_EXPERTS
    x1_s, h2_s, route_s, _, _, experts_bf16 = _mixers(
        x_sample.reshape(dec_batch * dec_seq, D_MODEL), mod3, p, seq=dec_seq, mod_row0=1,
        ctx_kv=ctx_kv, cast=experts_f32 if ride else ())
    if not ride:
        experts_bf16 = [w.astype(bf16) for w in experts_f32]
    p = dict(p, w_gate=experts_bf16[0], w_up=experts_bf16[1], w_down=experts_bf16[2])

    n_blk_p = batch * seq // MOE_BLOCK
    tables = _dispatch_tables(jnp.concatenate([route_p, route_s]), MOE_BLOCK)
    y_prompt = _experts(x1_p, h2_p, [t[:n_blk_p] for t in tables], mod3, p, seq=seq, mod_row0=0)
    y_sample = _experts(x1_s, h2_s, [t[n_blk_p:] for t in tables], mod3, p, seq=dec_seq,
                        mod_row0=1)
    return y_prompt, y_sample, new_k, new_v
```

```python
import functools

import jax
import jax.numpy as jnp
import numpy as np
from jax import lax
from jax.experimental import pallas as pl
from jax.experimental.pallas import tpu as pltpu

D_MODEL = 1024
HEAD_DIM = 64
ATTN_WIDTH = 512
N_HEADS = 8
N_KV_HEADS = 2
Q_PER_KV = 4
KV_WIDTH = 128
CONV_WIDTH = 512
CONV_K = 3
WINDOW = 128
GRID_W = 64
ROPE_BASE = 10000.0
N_FREQ = 16
N_GROUPS = 4
EXPERTS_PER_GROUP = 8
N_EXPERTS = 32
TOP_K = 2
D_EXPERT = 256
IN_WIDTH = ATTN_WIDTH + 2 * KV_WIDTH + 3 * CONV_WIDTH
EPS = 1e-6
NEG = -1e30
SCALE = HEAD_DIM ** -0.5
LOG2_E = 1.4426950408889634

LANES = 128
F32_SUBLANES = 8
BF16_SUBLANES = 16
assert D_MODEL == F32_SUBLANES * LANES

INPROJ_ROWS = 512
INPROJ_SUB = 256
POST_SUB = 256
MOE_BLOCK = 2048
MOE_CHUNK = 160
CHUNK_PITCH = MOE_CHUNK + F32_SUBLANES
EXPERTS_PER_STEP = 4
FINAL_ROWS = 256
FINAL_SLOTS = 4
MOE_VMEM_LIMIT_BYTES = 60 * 1024 * 1024
VMEM_LIMIT_BYTES = 48 * 1024 * 1024

SH1, SC1, G1, SH2, SC2, G2 = (i * D_MODEL for i in range(6))
MOD_ROWS = 8

COARSE_LANE0 = N_EXPERTS

f32 = jnp.float32
bf16 = jnp.bfloat16


def _params(*semantics, vmem=VMEM_LIMIT_BYTES):
    return pltpu.CompilerParams(dimension_semantics=semantics, vmem_limit_bytes=vmem)


def _rms(x):
    return x * lax.rsqrt(jnp.mean(x * x, axis=-1, keepdims=True) + EPS)


def _dot(a, b):
    return jnp.dot(a, b, preferred_element_type=f32)


def _split_bf16(w):
    hi = w.astype(bf16)
    lo = (w - hi.astype(f32)).astype(bf16)
    return jnp.concatenate([hi, lo], axis=1)


def _dot_nt(a, b):
    return lax.dot_general(a, b, (((1,), (1,)), ((), ())), preferred_element_type=f32)


def _mod_kernel(cv_ref, w_ref, b_ref, o_ref):
    a = cv_ref[...]
    a = a * jax.nn.sigmoid(a)
    o_ref[...] = jnp.dot(a, w_ref[...], precision=lax.Precision.HIGHEST,
                         preferred_element_type=f32) + b_ref[...]


def _modulation(cvecs, w_mod, b_mod):
    tn = 2 * D_MODEL
    return pl.pallas_call(
        _mod_kernel,
        grid=(6 * D_MODEL // tn,),
        in_specs=[pl.BlockSpec((MOD_ROWS, D_MODEL), lambda j: (0, 0)),
                  pl.BlockSpec((D_MODEL, tn), lambda j: (0, j)),
                  pl.BlockSpec((1, tn), lambda j: (0, j))],
        out_specs=pl.BlockSpec((MOD_ROWS, tn), lambda j: (0, j)),
        out_shape=jax.ShapeDtypeStruct((MOD_ROWS, 6 * D_MODEL), f32),
        compiler_params=_params("arbitrary"),
        name="mod",
    )(cvecs, w_mod, b_mod.reshape(1, -1))


def _swap_halves(x):
    lane = lax.broadcasted_iota(jnp.int32, x.shape, 1)
    up = pltpu.roll(x, LANES - N_FREQ, axis=1)
    dn = pltpu.roll(x, N_FREQ, axis=1)
    return jnp.where((lane % (2 * N_FREQ)) < N_FREQ, up, dn)


def _rope(x, cos, sin):
    parts = []
    for c in range(x.shape[1] // LANES):
        xc = x[:, c * LANES:(c + 1) * LANES]
        parts.append(xc * cos + _swap_halves(xc) * sin)
    return parts[0] if len(parts) == 1 else jnp.concatenate(parts, axis=1)


def _inproj_kernel(*refs, rope):
    if rope:
        x_ref, mod_ref, g_ref, w_ref, cos_ref, sin_ref, q_ref, k_ref, v_ref, u_ref, bg_ref = refs
    else:
        x_ref, mod_ref, g_ref, w_ref, q_ref, k_ref, v_ref, u_ref, bg_ref = refs
    tm = x_ref.shape[0]
    subs = [slice(s * INPROJ_SUB, (s + 1) * INPROJ_SUB) for s in range(tm // INPROJ_SUB)]
    zs = []
    for rs in subs:
        h = _rms(x_ref[rs, :]) * g_ref[...]
        h = h * (1.0 + mod_ref[:, SC1:SC1 + D_MODEL]) + mod_ref[:, SH1:SH1 + D_MODEL]
        zs.append(_dot(h.astype(bf16), w_ref[...]))
    for rs, z in zip(subs, zs):
        o = 0
        q = z[:, o:o + ATTN_WIDTH]; o += ATTN_WIDTH
        k = z[:, o:o + KV_WIDTH]; o += KV_WIDTH
        v = z[:, o:o + KV_WIDTH]; o += KV_WIDTH
        bg = z[:, o:o + CONV_WIDTH]; o += CONV_WIDTH
        cg = z[:, o:o + CONV_WIDTH]; o += CONV_WIDTH
        xin = z[:, o:o + CONV_WIDTH]
        if rope:
            q = _rope(q, cos_ref[rs, :], sin_ref[rs, :])
            k = _rope(k, cos_ref[rs, :], sin_ref[rs, :])
        q_ref[rs, :] = q.astype(q_ref.dtype)
        k_ref[rs, :] = k.astype(k_ref.dtype)
        if rope:
            v_ref[:, rs] = v.T.astype(v_ref.dtype)
        else:
            v_ref[rs, :] = v.astype(v_ref.dtype)
        u_ref[rs, :] = (cg * xin).astype(u_ref.dtype)
        bg_ref[rs, :] = bg.astype(bg_ref.dtype)


def _inproj(x, mod3, norm_g, w_in, *, seq, mod_row0, kv_dtype, rope_tabs=None, tm=INPROJ_ROWS):
    t = x.shape[0]
    tiles_per_seq = seq // tm
    row = lambda i: (mod_row0 + (i // tiles_per_seq if mod_row0 else 0), 0, 0)
    in_specs = [pl.BlockSpec((tm, D_MODEL), lambda i: (i, 0)),
                pl.BlockSpec((None, 1, 6 * D_MODEL), row),
                pl.BlockSpec((1, D_MODEL), lambda i: (0, 0)),
                pl.BlockSpec((D_MODEL, IN_WIDTH), lambda i: (0, 0))]
    args = [x, mod3, norm_g, w_in]
    if rope_tabs is not None:
        in_specs += [pl.BlockSpec((tm, LANES), lambda i: (i % tiles_per_seq, 0))] * 2
        args += list(rope_tabs)
    widths = (ATTN_WIDTH, KV_WIDTH, KV_WIDTH, CONV_WIDTH, CONV_WIDTH)
    dtypes = (bf16, kv_dtype, kv_dtype, bf16, bf16)
    out_specs = [pl.BlockSpec((tm, w), lambda i: (i, 0)) for w in widths]
    out_shape = [jax.ShapeDtypeStruct((t, w), d) for w, d in zip(widths, dtypes)]
    if rope_tabs is not None:
        out_specs[2] = pl.BlockSpec((None, KV_WIDTH, tm),
                                    lambda i: (i // tiles_per_seq, 0, i % tiles_per_seq))
        out_shape[2] = jax.ShapeDtypeStruct((t // seq, KV_WIDTH, seq), kv_dtype)
    return pl.pallas_call(
        functools.partial(_inproj_kernel, rope=rope_tabs is not None),
        grid=(t // tm,),
        in_specs=in_specs,
        out_specs=out_specs,
        out_shape=out_shape,
        compiler_params=_params("arbitrary"),
        name="inproj_rope" if rope_tabs is not None else "inproj",
    )(*args)


def _rope_tables(n):
    t = np.arange(n)
    inv = ROPE_BASE ** (-np.arange(N_FREQ, dtype=np.float32) / N_FREQ)
    rows = (t // GRID_W).astype(np.float32)
    cols = (t % GRID_W).astype(np.float32)
    d = np.arange(LANES) % HEAD_DIM
    pos = np.where(d[None, :] < HEAD_DIM // 2, rows[:, None], cols[:, None])
    ang = jnp.asarray(pos.astype(np.float32) * inv[d % N_FREQ][None, :])
    sign = np.where((d % (2 * N_FREQ)) < N_FREQ, -1.0, 1.0).astype(np.float32)
    return jnp.cos(ang), jnp.sin(ang) * sign[None, :]


def _attend_group(q_rows, sink_col, keys, vals, masks):
    scores = []
    m = sink_col
    for kk, mk in zip(keys, masks):
        s = _dot_nt(q_rows, kk) * SCALE
        if mk is not None:
            s = jnp.where(mk, s, NEG)
        scores.append(s)
        m = jnp.maximum(m, jnp.max(s, axis=-1, keepdims=True))
    den = jnp.exp(sink_col - m)
    acc = None
    for s, vv in zip(scores, vals):
        p = jnp.exp(s - m)
        den = den + jnp.sum(p, axis=-1, keepdims=True)
        pv = _dot(p.astype(bf16), vv)
        acc = pv if acc is None else acc + pv
    return acc / den


def _heads_attention(q, sink_ref, key_sets, val_sets, masks):
    bq = q.shape[0]
    ridx = lax.broadcasted_iota(jnp.int32, (Q_PER_KV * bq, 1), 0)
    outs = []
    for g in range(N_KV_HEADS):
        heads = range(g * Q_PER_KV, (g + 1) * Q_PER_KV)
        q_rows = jnp.concatenate([q[:, h * HEAD_DIM:(h + 1) * HEAD_DIM] for h in heads], axis=0)
        sink_col = jnp.full((Q_PER_KV * bq, 1), sink_ref[g * Q_PER_KV], f32)
        for j in range(1, Q_PER_KV):
            sink_col = jnp.where(ridx >= j * bq, sink_ref[g * Q_PER_KV + j], sink_col)
        lo, hi = g * HEAD_DIM, (g + 1) * HEAD_DIM
        o = _attend_group(q_rows, sink_col, [kk[:, lo:hi] for kk in key_sets],
                          [vv[:, lo:hi] for vv in val_sets], masks)
        outs += [o[j * bq:(j + 1) * bq] for j in range(Q_PER_KV)]
    return jnp.concatenate(outs, axis=1)


def _ctx_attn_kernel(sink_ref, q_ref, k_ref, v_ref, o_ref):
    k = k_ref[...].astype(bf16)
    v = v_ref[...].astype(bf16)
    o_ref[...] = _heads_attention(q_ref[...], sink_ref, [k], [v], [None]).astype(o_ref.dtype)


def _context_attention(q, k, v, sink, *, seq):
    t = q.shape[0]
    return pl.pallas_call(
        _ctx_attn_kernel,
        grid=(t // seq,),
        in_specs=[pl.BlockSpec(memory_space=pltpu.SMEM),
                  pl.BlockSpec((seq, ATTN_WIDTH), lambda b: (b, 0)),
                  pl.BlockSpec((seq, KV_WIDTH), lambda b: (b, 0)),
                  pl.BlockSpec((seq, KV_WIDTH), lambda b: (b, 0))],
        out_specs=pl.BlockSpec((seq, ATTN_WIDTH), lambda b: (b, 0)),
        out_shape=jax.ShapeDtypeStruct((t, ATTN_WIDTH), bf16),
        compiler_params=_params("arbitrary"),
        name="ctx_attn",
    )(sink, q, k, v)


def _lat_attn_kernel(sink_ref, q_ref, k_ref, vt_ref, ck_ref, cv_ref, *rest, bq, seq, n_cast):
    o_ref = rest[n_cast]
    for src_ref, dst_ref in zip(rest[:n_cast], rest[n_cast + 1:]):
        dst_ref[...] = src_ref[...].astype(dst_ref.dtype)
    band = bq + 2 * WINDOW
    n_col = Q_PER_KV * bq
    ck = ck_ref[...].astype(bf16)
    cvt = cv_ref[...].T.astype(bf16)
    delta = ((lax.broadcasted_iota(jnp.int32, (band, n_col), 1) & (bq - 1))
             - lax.broadcasted_iota(jnp.int32, (band, n_col), 0))
    col_head = lax.broadcasted_iota(jnp.int32, (1, n_col), 1) // bq
    sink_rows = []
    for g in range(N_KV_HEADS):
        sink_row = jnp.full((1, n_col), sink_ref[g * Q_PER_KV], f32)
        for j in range(1, Q_PER_KV):
            sink_row = jnp.where(col_head == j, sink_ref[g * Q_PER_KV + j], sink_row)
        sink_rows.append(sink_row * LOG2_E)
    ones_rows = jnp.ones((BF16_SUBLANES, 1), bf16)
    cvt_aug = [jnp.concatenate([cvt[g * HEAD_DIM:(g + 1) * HEAD_DIM, :],
                                jnp.broadcast_to(ones_rows, (BF16_SUBLANES, cvt.shape[1]))], axis=0)
               for g in range(N_KV_HEADS)]

    n_sub = q_ref.shape[0] // bq
    subs = []
    for sb in range(n_sub):
        i = pl.program_id(1) * n_sub + sb
        start = pl.multiple_of(jnp.clip(i * bq - WINDOW, 0, seq - band), LANES)
        shifted = delta + (i * bq - start + WINDOW)
        subs.append(dict(
            kb=k_ref[pl.ds(start, band), :],
            vbt=vt_ref[:, pl.ds(start, band)],
            qt=(q_ref[sb * bq:(sb + 1) * bq, :].astype(f32) * (SCALE * LOG2_E)).T.astype(bf16),
            mask=shifted.astype(jnp.uint32) <= 2 * WINDOW))

    def scores(sub, g):
        lo, hi = g * HEAD_DIM, (g + 1) * HEAD_DIM
        heads = range(g * Q_PER_KV, (g + 1) * Q_PER_KV)
        qt_g = jnp.concatenate([sub["qt"][h * HEAD_DIM:(h + 1) * HEAD_DIM, :] for h in heads],
                               axis=1)
        s_loc = jnp.where(sub["mask"], _dot(sub["kb"][:, lo:hi], qt_g), NEG)
        s_ctx = _dot(ck[:, lo:hi], qt_g)
        return s_loc, s_ctx

    def weighted_values(sub, g, s_loc, s_ctx):
        lo, hi = g * HEAD_DIM, (g + 1) * HEAD_DIM
        m = jnp.maximum(jnp.maximum(jnp.max(s_loc, axis=0, keepdims=True),
                                    jnp.max(s_ctx, axis=0, keepdims=True)), sink_rows[g])
        p_loc = jnp.exp2(s_loc - m).astype(bf16)
        p_ctx = jnp.exp2(s_ctx - m).astype(bf16)
        vbt_aug = jnp.concatenate(
            [sub["vbt"][lo:hi, :], jnp.broadcast_to(ones_rows, (BF16_SUBLANES, band))], axis=0)
        o_aug = _dot(vbt_aug, p_loc) + _dot(cvt_aug[g], p_ctx)
        den = o_aug[HEAD_DIM:HEAD_DIM + 1, :] + jnp.exp2(sink_rows[g] - m)
        return o_aug[:HEAD_DIM, :] / den

    passes = [(sb, g) for sb in range(n_sub) for g in range(N_KV_HEADS)]
    out_t = {}
    pending = scores(subs[0], 0)
    for n, (sb, g) in enumerate(passes):
        nxt = scores(subs[passes[n + 1][0]], passes[n + 1][1]) if n + 1 < len(passes) else None
        out_t[sb, g] = weighted_values(subs[sb], g, *pending)
        pending = nxt
    for sb in range(n_sub):
        blocks = [jnp.concatenate([out_t[sb, g][:, j * bq:(j + 1) * bq]
                                   for g in range(N_KV_HEADS)], axis=0).T
                  for j in range(Q_PER_KV)]
        o_ref[sb * bq:(sb + 1) * bq, :] = jnp.concatenate(blocks, axis=1).astype(o_ref.dtype)


LATENT_HEAD_PERM = np.array([(Q_PER_KV * g + j) * HEAD_DIM + d for j in range(Q_PER_KV)
                             for g in range(N_KV_HEADS) for d in range(HEAD_DIM)])


LAT_ATTN_SUB_BLOCKS = 8


def _latent_attention(q, k, vt, ck, cv, sink, *, seq, cast=(), bq=128,
                      sub_blocks=LAT_ATTN_SUB_BLOCKS):
    nb, past = ck.shape[0], ck.shape[1]
    assert bq == LANES and N_KV_HEADS * HEAD_DIM == LANES
    q3, k3 = (a.reshape(nb, seq, a.shape[-1]) for a in (q, k))
    tq = bq * sub_blocks
    steps_per_seq = seq // tq
    n_steps = nb * steps_per_seq
    cast3 = [a.reshape((n_steps, -1) + a.shape[2:]) for a in cast]
    cast_specs = [pl.BlockSpec((None,) + a.shape[1:], lambda b, i: (b * steps_per_seq + i, 0, 0))
                  for a in cast3]
    out, *cast_out = pl.pallas_call(
        functools.partial(_lat_attn_kernel, bq=bq, seq=seq, n_cast=len(cast)),
        grid=(nb, steps_per_seq),
        in_specs=[pl.BlockSpec(memory_space=pltpu.SMEM),
                  pl.BlockSpec((None, tq, ATTN_WIDTH), lambda b, i: (b, i, 0)),
                  pl.BlockSpec((None, seq, KV_WIDTH), lambda b, i: (b, 0, 0)),
                  pl.BlockSpec((None, KV_WIDTH, seq), lambda b, i: (b, 0, 0)),
                  pl.BlockSpec((None, past, KV_WIDTH), lambda b, i: (b, 0, 0)),
                  pl.BlockSpec((None, past, KV_WIDTH), lambda b, i: (b, 0, 0))] + cast_specs,
        out_specs=[pl.BlockSpec((None, tq, ATTN_WIDTH), lambda b, i: (b, i, 0))] + cast_specs,
        out_shape=[jax.ShapeDtypeStruct((nb, seq, ATTN_WIDTH), bf16)]
        + [jax.ShapeDtypeStruct(a.shape, bf16) for a in cast3],
        compiler_params=_params("arbitrary", "arbitrary"),
        name="lat_attn",
    )(sink, q3, k3, vt, ck, cv, *cast3)
    return [out.reshape(nb * seq, ATTN_WIDTH)] + [o.reshape(a.shape) for o, a in zip(cast_out, cast)]


def _route(logits):
    lane = lax.broadcasted_iota(jnp.int32, logits.shape, 1)
    lane_f = lane.astype(f32)
    big = jnp.float32(LANES)
    is_c = jnp.logical_and(lane >= COARSE_LANE0, lane < COARSE_LANE0 + N_GROUPS)
    lc = jnp.where(is_c, logits, -jnp.inf)
    mc = jnp.max(lc, axis=-1, keepdims=True)
    grp = jnp.min(jnp.where(lc == mc, lane_f, big), axis=-1, keepdims=True) - COARSE_LANE0
    pg = 1.0 / jnp.sum(jnp.exp(lc - mc), axis=-1, keepdims=True)
    in_g = jnp.floor(lane_f * (1.0 / EXPERTS_PER_GROUP)) == grp
    fl = jnp.where(in_g, logits, -jnp.inf)
    m1 = jnp.max(fl, axis=-1, keepdims=True)
    i1 = jnp.min(jnp.where(fl == m1, lane_f, big), axis=-1, keepdims=True)
    fl2 = jnp.where(lane_f == i1, -jnp.inf, fl)
    m2 = jnp.max(fl2, axis=-1, keepdims=True)
    i2 = jnp.min(jnp.where(fl2 == m2, lane_f, big), axis=-1, keepdims=True)
    e2 = jnp.exp(m2 - m1)
    p1 = pg / (1.0 + e2)
    p2 = pg * e2 / (1.0 + e2)
    packed = jnp.where(lane == 0, i1, jnp.where(lane == 1, i2, jnp.where(lane == 2, p1, p2)))
    return jnp.where(lane < 4, packed, 0.0)


def _post_kernel(x_ref, attn_ref, u_ref, up_ref, un_ref, bg_ref, mod_ref, ag_ref, cg_ref, cw_ref,
                 cb_ref, wo_ref, n2_ref, wr_ref, br_ref, x1_ref, h2_ref, route_ref, *, tm, seq):
    i = pl.program_id(0)
    u = u_ref[...].astype(f32)
    rows = lax.broadcasted_iota(jnp.int32, (tm, 1), 0)
    spos = (i * tm + rows) % seq
    u_dn = jnp.where(rows == 0, up_ref[...].astype(f32)[BF16_SUBLANES - 1:, :],
                     pltpu.roll(u, 1, axis=0))
    u_dn = jnp.where(spos == 0, 0.0, u_dn)
    u_up = jnp.where(rows == tm - 1, un_ref[...].astype(f32)[0:1, :], pltpu.roll(u, tm - 1, axis=0))
    u_up = jnp.where(spos == seq - 1, 0.0, u_up)
    y = u_dn * cw_ref[0:1, :] + u * cw_ref[1:2, :] + u_up * cw_ref[2:3, :] + cb_ref[...]
    conv = bg_ref[...].astype(f32) * y
    attn_n = (_rms(attn_ref[...].astype(f32)) * ag_ref[...]).astype(bf16)
    conv_n = (_rms(conv) * cg_ref[...]).astype(bf16)
    subs = [slice(s * POST_SUB, (s + 1) * POST_SUB) for s in range(tm // POST_SUB)]
    mixed = [_dot(attn_n[rs], wo_ref[0:ATTN_WIDTH, :]) + _dot(conv_n[rs], wo_ref[ATTN_WIDTH:, :])
             for rs in subs]
    logits = []
    for s, rs in enumerate(subs):
        x1 = x_ref[rs, :] + mod_ref[:, G1:G1 + D_MODEL] * mixed[s]
        x1_ref[rs, :] = x1
        h2 = _rms(x1) * n2_ref[...]
        h2 = h2 * (1.0 + mod_ref[:, SC2:SC2 + D_MODEL]) + mod_ref[:, SH2:SH2 + D_MODEL]
        for c in range(D_MODEL // LANES):
            h2_ref[pl.ds(rs.start * F32_SUBLANES + c, POST_SUB, stride=F32_SUBLANES), :] = (
                h2[:, c * LANES:(c + 1) * LANES])
        h2_hi = h2.astype(bf16)
        h2_lo = (h2 - h2_hi.astype(f32)).astype(bf16)
        both = _dot(h2_hi, wr_ref[...])
        logits.append(both[:, :LANES] + both[:, LANES:] + _dot(h2_lo, wr_ref[:, :LANES]))
    for s, rs in enumerate(subs):
        route_ref[rs, :] = _route(logits[s] + br_ref[...])


def _post(x, attn, u, bg, mod3, p, *, seq, mod_row0, tm=4 * POST_SUB):
    t = x.shape[0]
    tiles_per_seq = max(seq // tm, 1)
    halo = BF16_SUBLANES
    n_halo = t // halo
    row = lambda i: (mod_row0 + (i // tiles_per_seq if mod_row0 else 0), 0, 0)
    tile = lambda w: pl.BlockSpec((tm, w), lambda i: (i, 0))
    full = lambda a: pl.BlockSpec(a.shape, lambda i: (0,) * a.ndim)
    small = [p["attn_out_g"], p["conv_out_g"], p["conv_w"], p["conv_b"], p["w_o"], p["norm2_g"],
             p["w_router"], p["b_router"]]
    return pl.pallas_call(
        functools.partial(_post_kernel, tm=tm, seq=seq),
        grid=(t // tm,),
        in_specs=[tile(D_MODEL), tile(ATTN_WIDTH), tile(CONV_WIDTH),
                  pl.BlockSpec((halo, CONV_WIDTH),
                               lambda i: (jnp.maximum(i * (tm // halo) - 1, 0), 0)),
                  pl.BlockSpec((halo, CONV_WIDTH),
                               lambda i: (jnp.minimum((i + 1) * (tm // halo), n_halo - 1), 0)),
                  tile(CONV_WIDTH),
                  pl.BlockSpec((None, 1, 6 * D_MODEL), row)] + [full(a) for a in small],
        out_specs=[tile(D_MODEL), pl.BlockSpec((tm * F32_SUBLANES, LANES), lambda i: (i, 0)),
                   tile(LANES)],
        out_shape=[jax.ShapeDtypeStruct((t, D_MODEL), f32),
                   jax.ShapeDtypeStruct((t * F32_SUBLANES, LANES), f32),
                   jax.ShapeDtypeStruct((t, LANES), f32)],
        compiler_params=_params("arbitrary"),
        name="post",
    )(x, attn, u, u, u, bg, mod3, *small)


def _max_chunks(blk):
    n = -(-TOP_K * blk // MOE_CHUNK) + EXPERTS_PER_STEP
    return n + n % 2


def _dispatch_tables(route, blk):
    nblk = route.shape[0] // blk
    n_slots = TOP_K * blk
    flat = lambda a: a.reshape(nblk, blk, TOP_K).transpose(0, 2, 1).reshape(nblk, n_slots)
    experts = flat(route[:, 0:TOP_K].astype(jnp.int32))
    weights = flat(route[:, TOP_K:2 * TOP_K])
    slot_ids = lax.broadcasted_iota(jnp.int32, experts.shape, 1)
    _, order, w_sorted = lax.sort((experts, slot_ids, weights), dimension=1, num_keys=1)
    bounds = jnp.arange(N_EXPERTS + 1, dtype=jnp.int32)
    offs = jnp.sum(experts[:, :, None] < bounds[None, None, :], axis=1, dtype=jnp.int32)
    tail = ((0, 0), (0, MOE_CHUNK))
    src_rows = jnp.pad((order & (blk - 1)) * F32_SUBLANES, tail)
    dst_rows = jnp.pad(order * F32_SUBLANES, tail, constant_values=n_slots * F32_SUBLANES)
    w_sorted = jnp.pad(w_sorted, tail)

    eg, n_steps, max_c = EXPERTS_PER_STEP, N_EXPERTS // EXPERTS_PER_STEP, _max_chunks(blk)
    lo = offs[:, :N_EXPERTS].reshape(nblk, n_steps, eg)
    n_ch = (offs[:, 1:] - offs[:, :N_EXPERTS] + MOE_CHUNK - 1) // MOE_CHUNK
    n_ch = n_ch.reshape(nblk, n_steps, eg)
    cum = jnp.cumsum(n_ch, axis=2)
    total = cum[..., -1]
    pos = jnp.arange(max_c, dtype=jnp.int32)
    j = jnp.sum(cum[:, :, None, :] <= pos[None, None, :, None], axis=-1, dtype=jnp.int32)
    j = jnp.minimum(j, eg - 1)
    hit = j[..., None] == jnp.arange(eg, dtype=jnp.int32)
    pick = lambda a: jnp.sum(jnp.where(hit, a[:, :, None, :], 0), axis=-1)
    base = pick(lo) + (pos - pick(cum - n_ch)) * MOE_CHUNK
    valid = pos < total[..., None]
    base = jnp.where(valid, base, n_slots).reshape(nblk, 1, n_steps * max_c)
    j = jnp.where(valid, j, 0).reshape(nblk, 1, n_steps * max_c)
    return (src_rows[:, None, :], dst_rows[:, None, :], w_sorted[:, None, :], j, base,
            total[:, None, :])


def _moe_kernel(src_ref, dst_ref, wts_ref, cj_ref, cbase_g_ref, cbase_s_ref, nchunks_ref,
                h_ref, wg_ref, wu_ref, wd_ref, mod_ref, fg_ref, x1_hbm, out_hbm,
                ytmp_ref, xg0_ref, og0_ref, xg1_ref, og1_ref, xbuf_ref, obuf_ref, sem_in, sem_out,
                *, blk):
    n_col = D_MODEL // LANES
    blk_id, step = pl.program_id(0), pl.program_id(1)
    last_step = step == pl.num_programs(1) - 1
    max_c = _max_chunks(blk)
    stage_refs = (xg0_ref, og0_ref, xg1_ref, og1_ref)
    n_sub = blk // FINAL_ROWS

    def x1_copy(s):
        rows = pl.ds(blk_id * blk + s * FINAL_ROWS, FINAL_ROWS)
        slot = s % FINAL_SLOTS
        return pltpu.make_async_copy(x1_hbm.at[rows, :], xbuf_ref.at[slot], sem_in.at[slot])

    def out_copy(s):
        rows = pl.ds(blk_id * blk + s * FINAL_ROWS, FINAL_ROWS)
        slot = s % FINAL_SLOTS
        return pltpu.make_async_copy(obuf_ref.at[slot], out_hbm.at[rows, :], sem_out.at[slot])

    @pl.when(last_step)
    def _():
        for s in range(min(FINAL_SLOTS, n_sub)):
            x1_copy(s).start()

    def gather(base, xg_ref):
        for r in range(MOE_CHUNK):
            src = pl.multiple_of(src_ref[0, base + r], F32_SUBLANES)
            xg_ref[pl.ds(r, F32_SUBLANES, stride=CHUNK_PITCH), :] = h_ref[pl.ds(src, F32_SUBLANES), :]

    def experts_mlp(j, xg_ref, og_ref):
        x = jnp.concatenate([xg_ref[c * CHUNK_PITCH:c * CHUNK_PITCH + MOE_CHUNK, :]
                             for c in range(n_col)], axis=1).astype(bf16)
        g = _dot(x, wg_ref[j])
        up = _dot(x, wu_ref[j])
        act = (g * jax.nn.sigmoid(g)) * up
        out = _dot(act.astype(bf16), wd_ref[j])
        for c in range(n_col):
            og_ref[c * CHUNK_PITCH:c * CHUNK_PITCH + MOE_CHUNK, :] = out[:, c * LANES:(c + 1) * LANES]

    def scatter(base, og_ref):
        for r in range(MOE_CHUNK):
            dst = pl.multiple_of(dst_ref[0, base + r], F32_SUBLANES)
            ytmp_ref[pl.ds(dst, F32_SUBLANES), :] = (
                og_ref[pl.ds(r, F32_SUBLANES, stride=CHUNK_PITCH), :] * wts_ref[0, base + r])

    def pair(pi, carry):
        chunks = [(step * max_c + 2 * pi + half, stage_refs[2 * half], stage_refs[2 * half + 1])
                  for half in range(2)]
        for p, xg_ref, _ in chunks:
            gather(cbase_g_ref[0, p], xg_ref)
        for p, xg_ref, og_ref in chunks:
            experts_mlp(cj_ref[0, p], xg_ref, og_ref)
        for p, _, og_ref in chunks:
            scatter(cbase_s_ref[0, p], og_ref)
        return carry

    n_chunks = nchunks_ref[0, step]
    lax.fori_loop(0, n_chunks // 2, pair, 0)

    @pl.when(n_chunks % 2 == 1)
    def _():
        p = step * max_c + n_chunks - 1
        gather(cbase_g_ref[0, p], xg0_ref)
        experts_mlp(cj_ref[0, p], xg0_ref, og0_ref)
        scatter(cbase_s_ref[0, p], og0_ref)

    @pl.when(last_step)
    def _():
        n = blk * F32_SUBLANES
        for k in range(1, TOP_K):
            ytmp_ref[0:n, :] = ytmp_ref[0:n, :] + ytmp_ref[k * n:(k + 1) * n, :]
        gate2 = mod_ref[:, G2:G2 + D_MODEL]
        for s in range(n_sub):
            slot = s % FINAL_SLOTS
            x1_copy(s).wait()
            if s >= FINAL_SLOTS:
                out_copy(s - FINAL_SLOTS).wait()
            first = s * FINAL_ROWS * F32_SUBLANES
            y = jnp.concatenate([ytmp_ref[pl.ds(first + c, FINAL_ROWS, stride=F32_SUBLANES), :]
                                 for c in range(n_col)], axis=1)
            x2 = xbuf_ref[slot] + gate2 * y
            obuf_ref[slot] = _rms(x2) * fg_ref[...]
            out_copy(s).start()
            if s + FINAL_SLOTS < n_sub:
                x1_copy(s + FINAL_SLOTS).start()
        for s in range(max(n_sub - FINAL_SLOTS, 0), n_sub):
            out_copy(s).wait()


def _moe(h2_tiles, tables, x1, mod3, final_g, w_gate, w_up, w_down, *, seq, mod_row0,
         blk=MOE_BLOCK):
    t = h2_tiles.shape[0] // F32_SUBLANES
    src_rows, dst_rows, w_sorted, cj, cbase, n_pairs = tables
    n_slots = TOP_K * blk
    eg, n_steps = EXPERTS_PER_STEP, N_EXPERTS // EXPERTS_PER_STEP
    blocks_per_seq = max(seq // blk, 1)
    row = lambda b, g: (mod_row0 + (b // blocks_per_seq if mod_row0 else 0), 0, 0)
    smem = lambda n: pl.BlockSpec((None, 1, n), lambda b, g: (b, 0, 0), memory_space=pltpu.SMEM)
    stage = pltpu.VMEM((D_MODEL // LANES * CHUNK_PITCH, LANES), f32)
    final_buf = pltpu.VMEM((FINAL_SLOTS, FINAL_ROWS, D_MODEL), f32)
    return pl.pallas_call(
        functools.partial(_moe_kernel, blk=blk),
        grid=(t // blk, n_steps),
        in_specs=[smem(n_slots + MOE_CHUNK)] * 3 + [smem(n_steps * _max_chunks(blk))] * 3 + [
                  smem(n_steps),
                  pl.BlockSpec((blk * F32_SUBLANES, LANES), lambda b, g: (b, 0)),
                  pl.BlockSpec((eg, D_MODEL, D_EXPERT), lambda b, g: (g, 0, 0)),
                  pl.BlockSpec((eg, D_MODEL, D_EXPERT), lambda b, g: (g, 0, 0)),
                  pl.BlockSpec((eg, D_EXPERT, D_MODEL), lambda b, g: (g, 0, 0)),
                  pl.BlockSpec((None, 1, 6 * D_MODEL), row),
                  pl.BlockSpec((1, D_MODEL), lambda b, g: (0, 0)),
                  pl.BlockSpec(memory_space=pl.ANY)],
        out_specs=pl.BlockSpec(memory_space=pl.ANY),
        out_shape=jax.ShapeDtypeStruct((t, D_MODEL), f32),
        scratch_shapes=[pltpu.VMEM(((n_slots + 1) * F32_SUBLANES, LANES), f32)] + [stage] * 4 + [
                        final_buf, final_buf,
                        pltpu.SemaphoreType.DMA((FINAL_SLOTS,)),
                        pltpu.SemaphoreType.DMA((FINAL_SLOTS,))],
        compiler_params=_params("arbitrary", "arbitrary", vmem=MOE_VMEM_LIMIT_BYTES),
        name="moe",
    )(src_rows, dst_rows, w_sorted, cj, cbase, cbase, n_pairs, h2_tiles, w_gate, w_up, w_down,
      mod3, final_g, x1)


def _mixers(x, mod3, p, *, seq, mod_row0, ctx_kv, cast=()):
    latent = ctx_kv is not None
    q, k, v, u, bg = _inproj(x, mod3, p["norm1_g"], p["w_in"], seq=seq, mod_row0=mod_row0,
                             kv_dtype=bf16 if latent else f32,
                             rope_tabs=_rope_tables(seq) if latent else None)
    if latent:
        attn, *cast = _latent_attention(q, k, v, ctx_kv[0], ctx_kv[1], p["sink"], seq=seq,
                                        cast=cast)
        rows = np.concatenate([LATENT_HEAD_PERM, np.arange(ATTN_WIDTH, ATTN_WIDTH + CONV_WIDTH)])
        p = dict(p, attn_out_g=p["attn_out_g"][:, LATENT_HEAD_PERM], w_o=p["w_o"][rows])
    else:
        attn = _context_attention(q, k, v, p["sink"], seq=seq)
    x1, h2_tiles, route = _post(x, attn, u, bg, mod3, p, seq=seq, mod_row0=mod_row0)
    return x1, h2_tiles, route[:, :2 * TOP_K], k, v, cast


def _experts(x1, h2_tiles, tables, mod3, p, *, seq, mod_row0):
    out = _moe(h2_tiles, tables, x1, mod3, p["final_g"], p["w_gate"], p["w_up"], p["w_down"],
               seq=seq, mod_row0=mod_row0)
    return out.reshape(-1, seq, D_MODEL)


def kernel(x_prompt, x_sample, cache_k, cache_v, c, c_ctx, w_mod, b_mod, norm1_g, w_in, conv_w,
           conv_b, sink, attn_out_g, conv_out_g, w_o, norm2_g, w_coarse, b_coarse, w_fine, b_fine,
           w_gate, w_up, w_down, final_g):
    batch, seq, _ = x_prompt.shape
    dec_batch, dec_seq, _ = x_sample.shape
    past = cache_k.shape[2]
    assert w_mod.shape[0] == 1 and 1 + dec_batch <= MOD_ROWS

    cvecs = jnp.concatenate([c_ctx[None], c, jnp.zeros((MOD_ROWS - 1 - dec_batch, D_MODEL), f32)])
    mod3 = _modulation(cvecs, w_mod[0], b_mod[0]).reshape(MOD_ROWS, 1, 6 * D_MODEL)

    pad = jnp.zeros((D_MODEL, LANES - N_EXPERTS - N_GROUPS), f32)
    p = {
        "norm1_g": norm1_g, "w_in": w_in[0].astype(bf16), "conv_w": conv_w[0], "conv_b": conv_b,
        "sink": sink[0], "attn_out_g": attn_out_g, "conv_out_g": conv_out_g,
        "w_o": w_o[0].astype(bf16), "norm2_g": norm2_g,
        "w_router": _split_bf16(jnp.concatenate([w_fine[0], w_coarse[0], pad], axis=1)),
        "b_router": jnp.concatenate([b_fine[0], b_coarse[0], pad[0]])[None],
        "final_g": final_g[None],
    }

    x1_p, h2_p, route_p, k_p, v_p, _ = _mixers(x_prompt.reshape(batch * seq, D_MODEL), mod3, p,
                                               seq=seq, mod_row0=0, ctx_kv=None)
    new_k = k_p.reshape(batch, 1, seq, N_KV_HEADS, HEAD_DIM)
    new_v = v_p.reshape(batch, 1, seq, N_KV_HEADS, HEAD_DIM)

    ctx_kv = (cache_k[:, 0].reshape(dec_batch, past, KV_WIDTH),
              cache_v[:, 0].reshape(dec_batch, past, KV_WIDTH))
    experts_f32 = (w_gate[0], w_up[0], w_down[0])
    ride = N_EXPERTS % (dec_batch * dec_seq // (LANES * LAT_ATTN_SUB_BLOCKS)) == 0
    x1_s, h2_s, route_s, _, _, experts_bf16 = _mixers(
        x_sample.reshape(dec_batch * dec_seq, D_MODEL), mod3, p, seq=dec_seq, mod_row0=1,
        ctx_kv=ctx_kv, cast=experts_f32 if ride else ())
    if not ride:
        experts_bf16 = [w.astype(bf16) for w in experts_f32]
    p = dict(p, w_gate=experts_bf16[0], w_up=experts_bf16[1], w_down=experts_bf16[2])

    n_blk_p = batch * seq // MOE_BLOCK
    tables = _dispatch_tables(jnp.concatenate([route_p, route_s]), MOE_BLOCK)
    y_prompt = _experts(x1_p, h2_p, [t[:n_blk_p] for t in tables], mod3, p, seq=seq, mod_row0=0)
    y_sample = _experts(x1_s, h2_s, [t[n_blk_p:] for t in tables], mod3, p, seq=dec_seq,
                        mod_row0=1)
    return y_prompt, y_sample, new_k, new_v
```

```python
import functools

import jax
import jax.numpy as jnp
import numpy as np
from jax import lax
from jax.experimental import pallas as pl
from jax.experimental.pallas import tpu as pltpu

D_MODEL = 1024
HEAD_DIM = 64
ATTN_WIDTH = 512
N_HEADS = 8
N_KV_HEADS = 2
Q_PER_KV = 4
KV_WIDTH = 128
CONV_WIDTH = 512
CONV_K = 3
WINDOW = 128
GRID_W = 64
ROPE_BASE = 10000.0
N_FREQ = 16
N_GROUPS = 4
EXPERTS_PER_GROUP = 8
N_EXPERTS = 32
TOP_K = 2
D_EXPERT = 256
IN_WIDTH = ATTN_WIDTH + 2 * KV_WIDTH + 3 * CONV_WIDTH
EPS = 1e-6
NEG = -1e30
SCALE = HEAD_DIM ** -0.5
LOG2_E = 1.4426950408889634

LANES = 128
F32_SUBLANES = 8
BF16_SUBLANES = 16
assert D_MODEL == F32_SUBLANES * LANES

INPROJ_ROWS = 512
INPROJ_SUB = 256
POST_SUB = 256
MOE_BLOCK = 2048
MOE_CHUNK = 160
CHUNK_PITCH = MOE_CHUNK + F32_SUBLANES
EXPERTS_PER_STEP = 4
FINAL_ROWS = 256
FINAL_SLOTS = 4
MOE_VMEM_LIMIT_BYTES = 60 * 1024 * 1024
VMEM_LIMIT_BYTES = 48 * 1024 * 1024

SH1, SC1, G1, SH2, SC2, G2 = (i * D_MODEL for i in range(6))
MOD_ROWS = 8

COARSE_LANE0 = N_EXPERTS

f32 = jnp.float32
bf16 = jnp.bfloat16


def _params(*semantics, vmem=VMEM_LIMIT_BYTES):
    return pltpu.CompilerParams(dimension_semantics=semantics, vmem_limit_bytes=vmem)


def _rms(x):
    return x * lax.rsqrt(jnp.mean(x * x, axis=-1, keepdims=True) + EPS)


def _dot(a, b):
    return jnp.dot(a, b, preferred_element_type=f32)


def _split_bf16(w):
    hi = w.astype(bf16)
    lo = (w - hi.astype(f32)).astype(bf16)
    return jnp.concatenate([hi, lo], axis=1)


def _mod_kernel(cv_ref, w_ref, b_ref, o_ref):
    a = cv_ref[...]
    a = a * jax.nn.sigmoid(a)
    o_ref[...] = jnp.dot(a, w_ref[...], precision=lax.Precision.HIGHEST,
                         preferred_element_type=f32) + b_ref[...]


def _modulation(cvecs, w_mod, b_mod):
    tn = 2 * D_MODEL
    return pl.pallas_call(
        _mod_kernel,
        grid=(6 * D_MODEL // tn,),
        in_specs=[pl.BlockSpec((MOD_ROWS, D_MODEL), lambda j: (0, 0)),
                  pl.BlockSpec((D_MODEL, tn), lambda j: (0, j)),
                  pl.BlockSpec((1, tn), lambda j: (0, j))],
        out_specs=pl.BlockSpec((MOD_ROWS, tn), lambda j: (0, j)),
        out_shape=jax.ShapeDtypeStruct((MOD_ROWS, 6 * D_MODEL), f32),
        compiler_params=_params("arbitrary"),
        name="mod",
    )(cvecs, w_mod, b_mod.reshape(1, -1))


def _swap_halves(x):
    lane = lax.broadcasted_iota(jnp.int32, x.shape, 1)
    up = pltpu.roll(x, LANES - N_FREQ, axis=1)
    dn = pltpu.roll(x, N_FREQ, axis=1)
    return jnp.where((lane % (2 * N_FREQ)) < N_FREQ, up, dn)


def _rope(x, cos, sin):
    parts = []
    for c in range(x.shape[1] // LANES):
        xc = x[:, c * LANES:(c + 1) * LANES]
        parts.append(xc * cos + _swap_halves(xc) * sin)
    return parts[0] if len(parts) == 1 else jnp.concatenate(parts, axis=1)


def _inproj_kernel(*refs, rope):
    if rope:
        x_ref, mod_ref, g_ref, w_ref, cos_ref, sin_ref, q_ref, k_ref, v_ref, u_ref, bg_ref = refs
    else:
        x_ref, mod_ref, g_ref, w_ref, q_ref, k_ref, v_ref, u_ref, bg_ref = refs
    tm = x_ref.shape[0]
    subs = [slice(s * INPROJ_SUB, (s + 1) * INPROJ_SUB) for s in range(tm // INPROJ_SUB)]
    zs = []
    for rs in subs:
        h = _rms(x_ref[rs, :]) * g_ref[...]
        h = h * (1.0 + mod_ref[:, SC1:SC1 + D_MODEL]) + mod_ref[:, SH1:SH1 + D_MODEL]
        zs.append(_dot(h.astype(bf16), w_ref[...]))
    for rs, z in zip(subs, zs):
        o = 0
        q = z[:, o:o + ATTN_WIDTH]; o += ATTN_WIDTH
        k = z[:, o:o + KV_WIDTH]; o += KV_WIDTH
        v = z[:, o:o + KV_WIDTH]; o += KV_WIDTH
        bg = z[:, o:o + CONV_WIDTH]; o += CONV_WIDTH
        cg = z[:, o:o + CONV_WIDTH]; o += CONV_WIDTH
        xin = z[:, o:o + CONV_WIDTH]
        if rope:
            q = _rope(q, cos_ref[rs, :], sin_ref[rs, :])
            k = _rope(k, cos_ref[rs, :], sin_ref[rs, :])
        q_ref[rs, :] = q.astype(q_ref.dtype)
        k_ref[rs, :] = k.astype(k_ref.dtype)
        if rope:
            v_ref[:, rs] = v.T.astype(v_ref.dtype)
        else:
            v_ref[rs, :] = v.astype(v_ref.dtype)
        u_ref[rs, :] = (cg * xin).astype(u_ref.dtype)
        bg_ref[rs, :] = bg.astype(bg_ref.dtype)


def _inproj(x, mod3, norm_g, w_in, *, seq, mod_row0, kv_dtype, rope_tabs=None, tm=INPROJ_ROWS):
    t = x.shape[0]
    tiles_per_seq = seq // tm
    row = lambda i: (mod_row0 + (i // tiles_per_seq if mod_row0 else 0), 0, 0)
    in_specs = [pl.BlockSpec((tm, D_MODEL), lambda i: (i, 0)),
                pl.BlockSpec((None, 1, 6 * D_MODEL), row),
                pl.BlockSpec((1, D_MODEL), lambda i: (0, 0)),
                pl.BlockSpec((D_MODEL, IN_WIDTH), lambda i: (0, 0))]
    args = [x, mod3, norm_g, w_in]
    if rope_tabs is not None:
        in_specs += [pl.BlockSpec((tm, LANES), lambda i: (i % tiles_per_seq, 0))] * 2
        args += list(rope_tabs)
    widths = (ATTN_WIDTH, KV_WIDTH, KV_WIDTH, CONV_WIDTH, CONV_WIDTH)
    dtypes = (bf16, kv_dtype, kv_dtype, bf16, bf16)
    out_specs = [pl.BlockSpec((tm, w), lambda i: (i, 0)) for w in widths]
    out_shape = [jax.ShapeDtypeStruct((t, w), d) for w, d in zip(widths, dtypes)]
    if rope_tabs is not None:
        out_specs[2] = pl.BlockSpec((None, KV_WIDTH, tm),
                                    lambda i: (i // tiles_per_seq, 0, i % tiles_per_seq))
        out_shape[2] = jax.ShapeDtypeStruct((t // seq, KV_WIDTH, seq), kv_dtype)
    return pl.pallas_call(
        functools.partial(_inproj_kernel, rope=rope_tabs is not None),
        grid=(t // tm,),
        in_specs=in_specs,
        out_specs=out_specs,
        out_shape=out_shape,
        compiler_params=_params("arbitrary"),
        name="inproj_rope" if rope_tabs is not None else "inproj",
    )(*args)


def _rope_tables(n):
    t = np.arange(n)
    inv = ROPE_BASE ** (-np.arange(N_FREQ, dtype=np.float32) / N_FREQ)
    rows = (t // GRID_W).astype(np.float32)
    cols = (t % GRID_W).astype(np.float32)
    d = np.arange(LANES) % HEAD_DIM
    pos = np.where(d[None, :] < HEAD_DIM // 2, rows[:, None], cols[:, None])
    ang = jnp.asarray(pos.astype(np.float32) * inv[d % N_FREQ][None, :])
    sign = np.where((d % (2 * N_FREQ)) < N_FREQ, -1.0, 1.0).astype(np.float32)
    return jnp.cos(ang), jnp.sin(ang) * sign[None, :]


def _attention_core(sink_ref, q_ref, o_ref, subs, shared, bq):
    n_col = Q_PER_KV * bq
    col_head = lax.broadcasted_iota(jnp.int32, (1, n_col), 1) // bq
    sink_rows = []
    for g in range(N_KV_HEADS):
        sink_row = jnp.full((1, n_col), sink_ref[g * Q_PER_KV], f32)
        for j in range(1, Q_PER_KV):
            sink_row = jnp.where(col_head == j, sink_ref[g * Q_PER_KV + j], sink_row)
        sink_rows.append(sink_row * LOG2_E)
    ones_rows = jnp.ones((BF16_SUBLANES, 1), bf16)

    def with_ones(vals_t, g):
        return jnp.concatenate(
            [vals_t[g * HEAD_DIM:(g + 1) * HEAD_DIM, :],
             jnp.broadcast_to(ones_rows, (BF16_SUBLANES, vals_t.shape[1]))], axis=0)

    shared_vals = None if shared is None else [with_ones(shared[1], g) for g in range(N_KV_HEADS)]
    qts = [(q_ref[i * bq:(i + 1) * bq, :].astype(f32) * (SCALE * LOG2_E)).T.astype(bf16)
           for i in range(len(subs))]

    def scores(i, g):
        lo, hi = g * HEAD_DIM, (g + 1) * HEAD_DIM
        heads = range(g * Q_PER_KV, (g + 1) * Q_PER_KV)
        qt_g = jnp.concatenate([qts[i][h * HEAD_DIM:(h + 1) * HEAD_DIM, :] for h in heads], axis=1)
        s_own = _dot(subs[i]["keys"][:, lo:hi], qt_g)
        if subs[i]["mask"] is not None:
            s_own = jnp.where(subs[i]["mask"], s_own, NEG)
        s_shared = None if shared is None else _dot(shared[0][:, lo:hi], qt_g)
        return s_own, s_shared

    def weighted_values(i, g, s_own, s_shared):
        m = jnp.maximum(jnp.max(s_own, axis=0, keepdims=True), sink_rows[g])
        if s_shared is not None:
            m = jnp.maximum(m, jnp.max(s_shared, axis=0, keepdims=True))
        o_aug = _dot(with_ones(subs[i]["vals_t"], g), jnp.exp2(s_own - m).astype(bf16))
        if s_shared is not None:
            o_aug = o_aug + _dot(shared_vals[g], jnp.exp2(s_shared - m).astype(bf16))
        den = o_aug[HEAD_DIM:HEAD_DIM + 1, :] + jnp.exp2(sink_rows[g] - m)
        return o_aug[:HEAD_DIM, :] / den

    passes = [(i, g) for i in range(len(subs)) for g in range(N_KV_HEADS)]
    out_t = {}
    pending = scores(*passes[0])
    for n, (i, g) in enumerate(passes):
        nxt = scores(*passes[n + 1]) if n + 1 < len(passes) else None
        out_t[i, g] = weighted_values(i, g, *pending)
        pending = nxt
    for i in range(len(subs)):
        blocks = [jnp.concatenate([out_t[i, g][:, j * bq:(j + 1) * bq]
                                   for g in range(N_KV_HEADS)], axis=0).T
                  for j in range(Q_PER_KV)]
        o_ref[i * bq:(i + 1) * bq, :] = jnp.concatenate(blocks, axis=1).astype(o_ref.dtype)


ATTN_HEAD_PERM = np.array([(Q_PER_KV * g + j) * HEAD_DIM + d for j in range(Q_PER_KV)
                           for g in range(N_KV_HEADS) for d in range(HEAD_DIM)])


def _ctx_attn_kernel(sink_ref, q_ref, k_ref, v_ref, o_ref, *, bq, seq):
    subs = []
    for s in range(q_ref.shape[0] // seq):
        keys = k_ref[s * seq:(s + 1) * seq, :].astype(bf16)
        vals_t = v_ref[s * seq:(s + 1) * seq, :].T.astype(bf16)
        subs += [dict(keys=keys, vals_t=vals_t, mask=None)] * (seq // bq)
    _attention_core(sink_ref, q_ref, o_ref, subs, None, bq)


def _context_attention(q, k, v, sink, *, seq, bq=128, seqs_per_step=4):
    t = q.shape[0]
    rows = seq * seqs_per_step
    return pl.pallas_call(
        functools.partial(_ctx_attn_kernel, bq=bq, seq=seq),
        grid=(t // rows,),
        in_specs=[pl.BlockSpec(memory_space=pltpu.SMEM),
                  pl.BlockSpec((rows, ATTN_WIDTH), lambda b: (b, 0)),
                  pl.BlockSpec((rows, KV_WIDTH), lambda b: (b, 0)),
                  pl.BlockSpec((rows, KV_WIDTH), lambda b: (b, 0))],
        out_specs=pl.BlockSpec((rows, ATTN_WIDTH), lambda b: (b, 0)),
        out_shape=jax.ShapeDtypeStruct((t, ATTN_WIDTH), bf16),
        compiler_params=_params("arbitrary"),
        name="ctx_attn",
    )(sink, q, k, v)


def _lat_attn_kernel(sink_ref, q_ref, k_ref, vt_ref, ck_ref, cv_ref, *rest, bq, seq, n_cast):
    o_ref = rest[n_cast]
    for src_ref, dst_ref in zip(rest[:n_cast], rest[n_cast + 1:]):
        dst_ref[...] = src_ref[...].astype(dst_ref.dtype)
    band = bq + 2 * WINDOW
    n_col = Q_PER_KV * bq
    delta = ((lax.broadcasted_iota(jnp.int32, (band, n_col), 1) & (bq - 1))
             - lax.broadcasted_iota(jnp.int32, (band, n_col), 0))
    n_sub = q_ref.shape[0] // bq
    subs = []
    for sb in range(n_sub):
        i = pl.program_id(1) * n_sub + sb
        start = pl.multiple_of(jnp.clip(i * bq - WINDOW, 0, seq - band), LANES)
        shifted = delta + (i * bq - start + WINDOW)
        subs.append(dict(keys=k_ref[pl.ds(start, band), :],
                         vals_t=vt_ref[:, pl.ds(start, band)],
                         mask=shifted.astype(jnp.uint32) <= 2 * WINDOW))
    shared = (ck_ref[...].astype(bf16), cv_ref[...].T.astype(bf16))
    _attention_core(sink_ref, q_ref, o_ref, subs, shared, bq)


LAT_ATTN_SUB_BLOCKS = 8


def _latent_attention(q, k, vt, ck, cv, sink, *, seq, cast=(), bq=128,
                      sub_blocks=LAT_ATTN_SUB_BLOCKS):
    nb, past = ck.shape[0], ck.shape[1]
    assert bq == LANES and N_KV_HEADS * HEAD_DIM == LANES
    q3, k3 = (a.reshape(nb, seq, a.shape[-1]) for a in (q, k))
    tq = bq * sub_blocks
    steps_per_seq = seq // tq
    n_steps = nb * steps_per_seq
    cast3 = [a.reshape((n_steps, -1) + a.shape[2:]) for a in cast]
    cast_specs = [pl.BlockSpec((None,) + a.shape[1:], lambda b, i: (b * steps_per_seq + i, 0, 0))
                  for a in cast3]
    out, *cast_out = pl.pallas_call(
        functools.partial(_lat_attn_kernel, bq=bq, seq=seq, n_cast=len(cast)),
        grid=(nb, steps_per_seq),
        in_specs=[pl.BlockSpec(memory_space=pltpu.SMEM),
                  pl.BlockSpec((None, tq, ATTN_WIDTH), lambda b, i: (b, i, 0)),
                  pl.BlockSpec((None, seq, KV_WIDTH), lambda b, i: (b, 0, 0)),
                  pl.BlockSpec((None, KV_WIDTH, seq), lambda b, i: (b, 0, 0)),
                  pl.BlockSpec((None, past, KV_WIDTH), lambda b, i: (b, 0, 0)),
                  pl.BlockSpec((None, past, KV_WIDTH), lambda b, i: (b, 0, 0))] + cast_specs,
        out_specs=[pl.BlockSpec((None, tq, ATTN_WIDTH), lambda b, i: (b, i, 0))] + cast_specs,
        out_shape=[jax.ShapeDtypeStruct((nb, seq, ATTN_WIDTH), bf16)]
        + [jax.ShapeDtypeStruct(a.shape, bf16) for a in cast3],
        compiler_params=_params("arbitrary", "arbitrary"),
        name="lat_attn",
    )(sink, q3, k3, vt, ck, cv, *cast3)
    return [out.reshape(nb * seq, ATTN_WIDTH)] + [o.reshape(a.shape) for o, a in zip(cast_out, cast)]


def _route(logits):
    lane = lax.broadcasted_iota(jnp.int32, logits.shape, 1)
    lane_f = lane.astype(f32)
    big = jnp.float32(LANES)
    is_c = jnp.logical_and(lane >= COARSE_LANE0, lane < COARSE_LANE0 + N_GROUPS)
    lc = jnp.where(is_c, logits, -jnp.inf)
    mc = jnp.max(lc, axis=-1, keepdims=True)
    grp = jnp.min(jnp.where(lc == mc, lane_f, big), axis=-1, keepdims=True) - COARSE_LANE0
    pg = 1.0 / jnp.sum(jnp.exp(lc - mc), axis=-1, keepdims=True)
    in_g = jnp.floor(lane_f * (1.0 / EXPERTS_PER_GROUP)) == grp
    fl = jnp.where(in_g, logits, -jnp.inf)
    m1 = jnp.max(fl, axis=-1, keepdims=True)
    i1 = jnp.min(jnp.where(fl == m1, lane_f, big), axis=-1, keepdims=True)
    fl2 = jnp.where(lane_f == i1, -jnp.inf, fl)
    m2 = jnp.max(fl2, axis=-1, keepdims=True)
    i2 = jnp.min(jnp.where(fl2 == m2, lane_f, big), axis=-1, keepdims=True)
    e2 = jnp.exp(m2 - m1)
    p1 = pg / (1.0 + e2)
    p2 = pg * e2 / (1.0 + e2)
    packed = jnp.where(lane == 0, i1, jnp.where(lane == 1, i2, jnp.where(lane == 2, p1, p2)))
    return jnp.where(lane < 4, packed, 0.0)


def _post_kernel(x_ref, attn_ref, u_ref, up_ref, un_ref, bg_ref, mod_ref, ag_ref, cg_ref, cw_ref,
                 cb_ref, wo_ref, n2_ref, wr_ref, br_ref, x1_ref, h2_ref, route_ref, *, tm, seq):
    i = pl.program_id(0)
    u = u_ref[...].astype(f32)
    rows = lax.broadcasted_iota(jnp.int32, (tm, 1), 0)
    spos = (i * tm + rows) % seq
    u_dn = jnp.where(rows == 0, up_ref[...].astype(f32)[BF16_SUBLANES - 1:, :],
                     pltpu.roll(u, 1, axis=0))
    u_dn = jnp.where(spos == 0, 0.0, u_dn)
    u_up = jnp.where(rows == tm - 1, un_ref[...].astype(f32)[0:1, :], pltpu.roll(u, tm - 1, axis=0))
    u_up = jnp.where(spos == seq - 1, 0.0, u_up)
    y = u_dn * cw_ref[0:1, :] + u * cw_ref[1:2, :] + u_up * cw_ref[2:3, :] + cb_ref[...]
    conv = bg_ref[...].astype(f32) * y
    attn_n = (_rms(attn_ref[...].astype(f32)) * ag_ref[...]).astype(bf16)
    conv_n = (_rms(conv) * cg_ref[...]).astype(bf16)
    subs = [slice(s * POST_SUB, (s + 1) * POST_SUB) for s in range(tm // POST_SUB)]
    mixed = [_dot(attn_n[rs], wo_ref[0:ATTN_WIDTH, :]) + _dot(conv_n[rs], wo_ref[ATTN_WIDTH:, :])
             for rs in subs]
    logits = []
    for s, rs in enumerate(subs):
        x1 = x_ref[rs, :] + mod_ref[:, G1:G1 + D_MODEL] * mixed[s]
        x1_ref[rs, :] = x1
        h2 = _rms(x1) * n2_ref[...]
        h2 = h2 * (1.0 + mod_ref[:, SC2:SC2 + D_MODEL]) + mod_ref[:, SH2:SH2 + D_MODEL]
        for c in range(D_MODEL // LANES):
            h2_ref[pl.ds(rs.start * F32_SUBLANES + c, POST_SUB, stride=F32_SUBLANES), :] = (
                h2[:, c * LANES:(c + 1) * LANES])
        h2_hi = h2.astype(bf16)
        h2_lo = (h2 - h2_hi.astype(f32)).astype(bf16)
        both = _dot(h2_hi, wr_ref[...])
        logits.append(both[:, :LANES] + both[:, LANES:] + _dot(h2_lo, wr_ref[:, :LANES]))
    for s, rs in enumerate(subs):
        route_ref[rs, :] = _route(logits[s] + br_ref[...])


def _post(x, attn, u, bg, mod3, p, *, seq, mod_row0, tm=4 * POST_SUB):
    t = x.shape[0]
    tiles_per_seq = max(seq // tm, 1)
    halo = BF16_SUBLANES
    n_halo = t // halo
    row = lambda i: (mod_row0 + (i // tiles_per_seq if mod_row0 else 0), 0, 0)
    tile = lambda w: pl.BlockSpec((tm, w), lambda i: (i, 0))
    full = lambda a: pl.BlockSpec(a.shape, lambda i: (0,) * a.ndim)
    small = [p["attn_out_g"], p["conv_out_g"], p["conv_w"], p["conv_b"], p["w_o"], p["norm2_g"],
             p["w_router"], p["b_router"]]
    return pl.pallas_call(
        functools.partial(_post_kernel, tm=tm, seq=seq),
        grid=(t // tm,),
        in_specs=[tile(D_MODEL), tile(ATTN_WIDTH), tile(CONV_WIDTH),
                  pl.BlockSpec((halo, CONV_WIDTH),
                               lambda i: (jnp.maximum(i * (tm // halo) - 1, 0), 0)),
                  pl.BlockSpec((halo, CONV_WIDTH),
                               lambda i: (jnp.minimum((i + 1) * (tm // halo), n_halo - 1), 0)),
                  tile(CONV_WIDTH),
                  pl.BlockSpec((None, 1, 6 * D_MODEL), row)] + [full(a) for a in small],
        out_specs=[tile(D_MODEL), pl.BlockSpec((tm * F32_SUBLANES, LANES), lambda i: (i, 0)),
                   tile(LANES)],
        out_shape=[jax.ShapeDtypeStruct((t, D_MODEL), f32),
                   jax.ShapeDtypeStruct((t * F32_SUBLANES, LANES), f32),
                   jax.ShapeDtypeStruct((t, LANES), f32)],
        compiler_params=_params("arbitrary"),
        name="post",
    )(x, attn, u, u, u, bg, mod3, *small)


def _max_chunks(blk):
    n = -(-TOP_K * blk // MOE_CHUNK) + EXPERTS_PER_STEP
    return n + n % 2


def _dispatch_tables(route, blk):
    nblk = route.shape[0] // blk
    n_slots = TOP_K * blk
    flat = lambda a: a.reshape(nblk, blk, TOP_K).transpose(0, 2, 1).reshape(nblk, n_slots)
    experts = flat(route[:, 0:TOP_K].astype(jnp.int32))
    weights = flat(route[:, TOP_K:2 * TOP_K])
    slot_ids = lax.broadcasted_iota(jnp.int32, experts.shape, 1)
    _, order, w_sorted = lax.sort((experts, slot_ids, weights), dimension=1, num_keys=1)
    bounds = jnp.arange(N_EXPERTS + 1, dtype=jnp.int32)
    offs = jnp.sum(experts[:, :, None] < bounds[None, None, :], axis=1, dtype=jnp.int32)
    tail = ((0, 0), (0, MOE_CHUNK))
    src_rows = jnp.pad((order & (blk - 1)) * F32_SUBLANES, tail)
    dst_rows = jnp.pad(order * F32_SUBLANES, tail, constant_values=n_slots * F32_SUBLANES)
    w_sorted = jnp.pad(w_sorted, tail)

    eg, n_steps, max_c = EXPERTS_PER_STEP, N_EXPERTS // EXPERTS_PER_STEP, _max_chunks(blk)
    lo = offs[:, :N_EXPERTS].reshape(nblk, n_steps, eg)
    n_ch = (offs[:, 1:] - offs[:, :N_EXPERTS] + MOE_CHUNK - 1) // MOE_CHUNK
    n_ch = n_ch.reshape(nblk, n_steps, eg)
    cum = jnp.cumsum(n_ch, axis=2)
    total = cum[..., -1]
    pos = jnp.arange(max_c, dtype=jnp.int32)
    j = jnp.sum(cum[:, :, None, :] <= pos[None, None, :, None], axis=-1, dtype=jnp.int32)
    j = jnp.minimum(j, eg - 1)
    hit = j[..., None] == jnp.arange(eg, dtype=jnp.int32)
    pick = lambda a: jnp.sum(jnp.where(hit, a[:, :, None, :], 0), axis=-1)
    base = pick(lo) + (pos - pick(cum - n_ch)) * MOE_CHUNK
    valid = pos < total[..., None]
    base = jnp.where(valid, base, n_slots).reshape(nblk, 1, n_steps * max_c)
    j = jnp.where(valid, j, 0).reshape(nblk, 1, n_steps * max_c)
    return (src_rows[:, None, :], dst_rows[:, None, :], w_sorted[:, None, :], j, base,
            total[:, None, :])


def _moe_kernel(src_ref, dst_ref, wts_ref, cj_ref, cbase_g_ref, cbase_s_ref, nchunks_ref,
                h_ref, wg_ref, wu_ref, wd_ref, mod_ref, fg_ref, x1_hbm, out_hbm,
                ytmp_ref, xg0_ref, og0_ref, xg1_ref, og1_ref, xbuf_ref, obuf_ref, sem_in, sem_out,
                *, blk):
    n_col = D_MODEL // LANES
    blk_id, step = pl.program_id(0), pl.program_id(1)
    last_step = step == pl.num_programs(1) - 1
    max_c = _max_chunks(blk)
    stage_refs = (xg0_ref, og0_ref, xg1_ref, og1_ref)
    n_sub = blk // FINAL_ROWS

    def x1_copy(s):
        rows = pl.ds(blk_id * blk + s * FINAL_ROWS, FINAL_ROWS)
        slot = s % FINAL_SLOTS
        return pltpu.make_async_copy(x1_hbm.at[rows, :], xbuf_ref.at[slot], sem_in.at[slot])

    def out_copy(s):
        rows = pl.ds(blk_id * blk + s * FINAL_ROWS, FINAL_ROWS)
        slot = s % FINAL_SLOTS
        return pltpu.make_async_copy(obuf_ref.at[slot], out_hbm.at[rows, :], sem_out.at[slot])

    @pl.when(last_step)
    def _():
        for s in range(min(FINAL_SLOTS, n_sub)):
            x1_copy(s).start()

    def gather(base, xg_ref):
        for r in range(MOE_CHUNK):
            src = pl.multiple_of(src_ref[0, base + r], F32_SUBLANES)
            xg_ref[pl.ds(r, F32_SUBLANES, stride=CHUNK_PITCH), :] = h_ref[pl.ds(src, F32_SUBLANES), :]

    def experts_mlp(j, xg_ref, og_ref):
        x = jnp.concatenate([xg_ref[c * CHUNK_PITCH:c * CHUNK_PITCH + MOE_CHUNK, :]
                             for c in range(n_col)], axis=1).astype(bf16)
        g = _dot(x, wg_ref[j])
        up = _dot(x, wu_ref[j])
        act = (g * jax.nn.sigmoid(g)) * up
        out = _dot(act.astype(bf16), wd_ref[j])
        for c in range(n_col):
            og_ref[c * CHUNK_PITCH:c * CHUNK_PITCH + MOE_CHUNK, :] = out[:, c * LANES:(c + 1) * LANES]

    def scatter(base, og_ref):
        for r in range(MOE_CHUNK):
            dst = pl.multiple_of(dst_ref[0, base + r], F32_SUBLANES)
            ytmp_ref[pl.ds(dst, F32_SUBLANES), :] = (
                og_ref[pl.ds(r, F32_SUBLANES, stride=CHUNK_PITCH), :] * wts_ref[0, base + r])

    def pair(pi, carry):
        chunks = [(step * max_c + 2 * pi + half, stage_refs[2 * half], stage_refs[2 * half + 1])
                  for half in range(2)]
        for p, xg_ref, _ in chunks:
            gather(cbase_g_ref[0, p], xg_ref)
        for p, xg_ref, og_ref in chunks:
            experts_mlp(cj_ref[0, p], xg_ref, og_ref)
        for p, _, og_ref in chunks:
            scatter(cbase_s_ref[0, p], og_ref)
        return carry

    n_chunks = nchunks_ref[0, step]
    lax.fori_loop(0, n_chunks // 2, pair, 0)

    @pl.when(n_chunks % 2 == 1)
    def _():
        p = step * max_c + n_chunks - 1
        gather(cbase_g_ref[0, p], xg0_ref)
        experts_mlp(cj_ref[0, p], xg0_ref, og0_ref)
        scatter(cbase_s_ref[0, p], og0_ref)

    @pl.when(last_step)
    def _():
        n = blk * F32_SUBLANES
        for k in range(1, TOP_K):
            ytmp_ref[0:n, :] = ytmp_ref[0:n, :] + ytmp_ref[k * n:(k + 1) * n, :]
        gate2 = mod_ref[:, G2:G2 + D_MODEL]
        for s in range(n_sub):
            slot = s % FINAL_SLOTS
            x1_copy(s).wait()
            if s >= FINAL_SLOTS:
                out_copy(s - FINAL_SLOTS).wait()
            first = s * FINAL_ROWS * F32_SUBLANES
            y = jnp.concatenate([ytmp_ref[pl.ds(first + c, FINAL_ROWS, stride=F32_SUBLANES), :]
                                 for c in range(n_col)], axis=1)
            x2 = xbuf_ref[slot] + gate2 * y
            obuf_ref[slot] = _rms(x2) * fg_ref[...]
            out_copy(s).start()
            if s + FINAL_SLOTS < n_sub:
                x1_copy(s + FINAL_SLOTS).start()
        for s in range(max(n_sub - FINAL_SLOTS, 0), n_sub):
            out_copy(s).wait()


def _moe(h2_tiles, tables, x1, mod3, final_g, w_gate, w_up, w_down, *, seq, mod_row0,
         blk=MOE_BLOCK):
    t = h2_tiles.shape[0] // F32_SUBLANES
    src_rows, dst_rows, w_sorted, cj, cbase, n_pairs = tables
    n_slots = TOP_K * blk
    eg, n_steps = EXPERTS_PER_STEP, N_EXPERTS // EXPERTS_PER_STEP
    blocks_per_seq = max(seq // blk, 1)
    row = lambda b, g: (mod_row0 + (b // blocks_per_seq if mod_row0 else 0), 0, 0)
    smem = lambda n: pl.BlockSpec((None, 1, n), lambda b, g: (b, 0, 0), memory_space=pltpu.SMEM)
    stage = pltpu.VMEM((D_MODEL // LANES * CHUNK_PITCH, LANES), f32)
    final_buf = pltpu.VMEM((FINAL_SLOTS, FINAL_ROWS, D_MODEL), f32)
    return pl.pallas_call(
        functools.partial(_moe_kernel, blk=blk),
        grid=(t // blk, n_steps),
        in_specs=[smem(n_slots + MOE_CHUNK)] * 3 + [smem(n_steps * _max_chunks(blk))] * 3 + [
                  smem(n_steps),
                  pl.BlockSpec((blk * F32_SUBLANES, LANES), lambda b, g: (b, 0)),
                  pl.BlockSpec((eg, D_MODEL, D_EXPERT), lambda b, g: (g, 0, 0)),
                  pl.BlockSpec((eg, D_MODEL, D_EXPERT), lambda b, g: (g, 0, 0)),
                  pl.BlockSpec((eg, D_EXPERT, D_MODEL), lambda b, g: (g, 0, 0)),
                  pl.BlockSpec((None, 1, 6 * D_MODEL), row),
                  pl.BlockSpec((1, D_MODEL), lambda b, g: (0, 0)),
                  pl.BlockSpec(memory_space=pl.ANY)],
        out_specs=pl.BlockSpec(memory_space=pl.ANY),
        out_shape=jax.ShapeDtypeStruct((t, D_MODEL), f32),
        scratch_shapes=[pltpu.VMEM(((n_slots + 1) * F32_SUBLANES, LANES), f32)] + [stage] * 4 + [
                        final_buf, final_buf,
                        pltpu.SemaphoreType.DMA((FINAL_SLOTS,)),
                        pltpu.SemaphoreType.DMA((FINAL_SLOTS,))],
        compiler_params=_params("arbitrary", "arbitrary", vmem=MOE_VMEM_LIMIT_BYTES),
        name="moe",
    )(src_rows, dst_rows, w_sorted, cj, cbase, cbase, n_pairs, h2_tiles, w_gate, w_up, w_down,
      mod3, final_g, x1)


def _mixers(x, mod3, p, *, seq, mod_row0, ctx_kv, cast=()):
    latent = ctx_kv is not None
    q, k, v, u, bg = _inproj(x, mod3, p["norm1_g"], p["w_in"], seq=seq, mod_row0=mod_row0,
                             kv_dtype=bf16 if latent else f32,
                             rope_tabs=_rope_tables(seq) if latent else None)
    if latent:
        attn, *cast = _latent_attention(q, k, v, ctx_kv[0], ctx_kv[1], p["sink"], seq=seq,
                                        cast=cast)
    else:
        attn = _context_attention(q, k, v, p["sink"], seq=seq)
    x1, h2_tiles, route = _post(x, attn, u, bg, mod3, p, seq=seq, mod_row0=mod_row0)
    return x1, h2_tiles, route[:, :2 * TOP_K], k, v, cast


def _experts(x1, h2_tiles, tables, mod3, p, *, seq, mod_row0):
    out = _moe(h2_tiles, tables, x1, mod3, p["final_g"], p["w_gate"], p["w_up"], p["w_down"],
               seq=seq, mod_row0=mod_row0)
    return out.reshape(-1, seq, D_MODEL)


def kernel(x_prompt, x_sample, cache_k, cache_v, c, c_ctx, w_mod, b_mod, norm1_g, w_in, conv_w,
           conv_b, sink, attn_out_g, conv_out_g, w_o, norm2_g, w_coarse, b_coarse, w_fine, b_fine,
           w_gate, w_up, w_down, final_g):
    batch, seq, _ = x_prompt.shape
    dec_batch, dec_seq, _ = x_sample.shape
    past = cache_k.shape[2]
    assert w_mod.shape[0] == 1 and 1 + dec_batch <= MOD_ROWS

    cvecs = jnp.concatenate([c_ctx[None], c, jnp.zeros((MOD_ROWS - 1 - dec_batch, D_MODEL), f32)])
    mod3 = _modulation(cvecs, w_mod[0], b_mod[0]).reshape(MOD_ROWS, 1, 6 * D_MODEL)

    pad = jnp.zeros((D_MODEL, LANES - N_EXPERTS - N_GROUPS), f32)
    mix_rows = np.concatenate([ATTN_HEAD_PERM, np.arange(ATTN_WIDTH, ATTN_WIDTH + CONV_WIDTH)])
    p = {
        "norm1_g": norm1_g, "w_in": w_in[0].astype(bf16), "conv_w": conv_w[0], "conv_b": conv_b,
        "sink": sink[0], "attn_out_g": attn_out_g[:, ATTN_HEAD_PERM], "conv_out_g": conv_out_g,
        "w_o": w_o[0][mix_rows].astype(bf16), "norm2_g": norm2_g,
        "w_router": _split_bf16(jnp.concatenate([w_fine[0], w_coarse[0], pad], axis=1)),
        "b_router": jnp.concatenate([b_fine[0], b_coarse[0], pad[0]])[None],
        "final_g": final_g[None],
    }

    x1_p, h2_p, route_p, k_p, v_p, _ = _mixers(x_prompt.reshape(batch * seq, D_MODEL), mod3, p,
                                               seq=seq, mod_row0=0, ctx_kv=None)
    new_k = k_p.reshape(batch, 1, seq, N_KV_HEADS, HEAD_DIM)
    new_v = v_p.reshape(batch, 1, seq, N_KV_HEADS, HEAD_DIM)

    ctx_kv = (cache_k[:, 0].reshape(dec_batch, past, KV_WIDTH),
              cache_v[:, 0].reshape(dec_batch, past, KV_WIDTH))
    experts_f32 = (w_gate[0], w_up[0], w_down[0])
    ride = N_EXPERTS % (dec_batch * dec_seq // (LANES * LAT_ATTN_SUB_BLOCKS)) == 0
    x1_s, h2_s, route_s, _, _, experts_bf16 = _mixers(
        x_sample.reshape(dec_batch * dec_seq, D_MODEL), mod3, p, seq=dec_seq, mod_row0=1,
        ctx_kv=ctx_kv, cast=experts_f32 if ride else ())
    if not ride:
        experts_bf16 = [w.astype(bf16) for w in experts_f32]
    p = dict(p, w_gate=experts_bf16[0], w_up=experts_bf16[1], w_down=experts_bf16[2])

    n_blk_p = batch * seq // MOE_BLOCK
    tables = _dispatch_tables(jnp.concatenate([route_p, route_s]), MOE_BLOCK)
    y_prompt = _experts(x1_p, h2_p, [t[:n_blk_p] for t in tables], mod3, p, seq=seq, mod_row0=0)
    y_sample = _experts(x1_s, h2_s, [t[n_blk_p:] for t in tables], mod3, p, seq=dec_seq,
                        mod_row0=1)
    return y_prompt, y_sample, new_k, new_v
```

```python
import functools

import jax
import jax.numpy as jnp
import numpy as np
from jax import lax
from jax.experimental import pallas as pl
from jax.experimental.pallas import tpu as pltpu

D_MODEL = 1024
HEAD_DIM = 64
ATTN_WIDTH = 512
N_HEADS = 8
N_KV_HEADS = 2
Q_PER_KV = 4
KV_WIDTH = 128
CONV_WIDTH = 512
CONV_K = 3
WINDOW = 128
GRID_W = 64
ROPE_BASE = 10000.0
N_FREQ = 16
N_GROUPS = 4
EXPERTS_PER_GROUP = 8
N_EXPERTS = 32
TOP_K = 2
D_EXPERT = 256
IN_WIDTH = ATTN_WIDTH + 2 * KV_WIDTH + 3 * CONV_WIDTH
EPS = 1e-6
NEG = -1e30
SCALE = HEAD_DIM ** -0.5
LOG2_E = 1.4426950408889634

LANES = 128
F32_SUBLANES = 8
BF16_SUBLANES = 16
assert D_MODEL == F32_SUBLANES * LANES

INPROJ_ROWS = 512
INPROJ_SUB = 256
POST_SUB = 256
MOE_BLOCK = 2048
MOE_CHUNK = 160
CHUNK_PITCH = MOE_CHUNK + F32_SUBLANES
EXPERTS_PER_STEP = 4
FINAL_ROWS = 256
FINAL_SLOTS = 4
MOE_VMEM_LIMIT_BYTES = 60 * 1024 * 1024
VMEM_LIMIT_BYTES = 48 * 1024 * 1024

SH1, SC1, G1, SH2, SC2, G2 = (i * D_MODEL for i in range(6))
MOD_ROWS = 8

COARSE_LANE0 = N_EXPERTS

f32 = jnp.float32
bf16 = jnp.bfloat16


def _params(*semantics, vmem=VMEM_LIMIT_BYTES):
    return pltpu.CompilerParams(dimension_semantics=semantics, vmem_limit_bytes=vmem)


def _rms(x):
    return x * lax.rsqrt(jnp.mean(x * x, axis=-1, keepdims=True) + EPS)


def _dot(a, b):
    return jnp.dot(a, b, preferred_element_type=f32)


def _split_bf16(w):
    hi = w.astype(bf16)
    lo = (w - hi.astype(f32)).astype(bf16)
    return jnp.concatenate([hi, lo], axis=1)


def _mod_kernel(cv_ref, w_ref, b_ref, o_ref):
    a = cv_ref[...]
    a = a * jax.nn.sigmoid(a)
    o_ref[...] = jnp.dot(a, w_ref[...], precision=lax.Precision.HIGHEST,
                         preferred_element_type=f32) + b_ref[...]


def _modulation(cvecs, w_mod, b_mod):
    tn = 2 * D_MODEL
    return pl.pallas_call(
        _mod_kernel,
        grid=(6 * D_MODEL // tn,),
        in_specs=[pl.BlockSpec((MOD_ROWS, D_MODEL), lambda j: (0, 0)),
                  pl.BlockSpec((D_MODEL, tn), lambda j: (0, j)),
                  pl.BlockSpec((1, tn), lambda j: (0, j))],
        out_specs=pl.BlockSpec((MOD_ROWS, tn), lambda j: (0, j)),
        out_shape=jax.ShapeDtypeStruct((MOD_ROWS, 6 * D_MODEL), f32),
        compiler_params=_params("arbitrary"),
        name="mod",
    )(cvecs, w_mod, b_mod.reshape(1, -1))


def _swap_halves(x):
    lane = lax.broadcasted_iota(jnp.int32, x.shape, 1)
    up = pltpu.roll(x, LANES - N_FREQ, axis=1)
    dn = pltpu.roll(x, N_FREQ, axis=1)
    return jnp.where((lane % (2 * N_FREQ)) < N_FREQ, up, dn)


def _rope(x, cos, sin):
    parts = []
    for c in range(x.shape[1] // LANES):
        xc = x[:, c * LANES:(c + 1) * LANES]
        parts.append(xc * cos + _swap_halves(xc) * sin)
    return parts[0] if len(parts) == 1 else jnp.concatenate(parts, axis=1)


def _inproj_kernel(*refs, rope):
    if rope:
        x_ref, mod_ref, g_ref, w_ref, cos_ref, sin_ref, q_ref, k_ref, v_ref, u_ref, bg_ref = refs
    else:
        x_ref, mod_ref, g_ref, w_ref, q_ref, k_ref, v_ref, u_ref, bg_ref = refs
    tm = x_ref.shape[0]
    subs = [slice(s * INPROJ_SUB, (s + 1) * INPROJ_SUB) for s in range(tm // INPROJ_SUB)]
    zs = []
    for rs in subs:
        h = _rms(x_ref[rs, :]) * g_ref[...]
        h = h * (1.0 + mod_ref[:, SC1:SC1 + D_MODEL]) + mod_ref[:, SH1:SH1 + D_MODEL]
        zs.append(_dot(h.astype(bf16), w_ref[...]))
    for rs, z in zip(subs, zs):
        o = 0
        q = z[:, o:o + ATTN_WIDTH]; o += ATTN_WIDTH
        k = z[:, o:o + KV_WIDTH]; o += KV_WIDTH
        v = z[:, o:o + KV_WIDTH]; o += KV_WIDTH
        bg = z[:, o:o + CONV_WIDTH]; o += CONV_WIDTH
        cg = z[:, o:o + CONV_WIDTH]; o += CONV_WIDTH
        xin = z[:, o:o + CONV_WIDTH]
        if rope:
            q = _rope(q, cos_ref[rs, :], sin_ref[rs, :])
            k = _rope(k, cos_ref[rs, :], sin_ref[rs, :])
        q_ref[rs, :] = q.astype(q_ref.dtype)
        k_ref[rs, :] = k.astype(k_ref.dtype)
        if rope:
            v_ref[:, rs] = v.T.astype(v_ref.dtype)
        else:
            v_ref[rs, :] = v.astype(v_ref.dtype)
        u_ref[rs, :] = (cg * xin).astype(u_ref.dtype)
        bg_ref[rs, :] = bg.astype(bg_ref.dtype)


def _inproj(x, mod3, norm_g, w_in, *, seq, mod_row0, kv_dtype, rope_tabs=None, tm=INPROJ_ROWS):
    t = x.shape[0]
    tiles_per_seq = seq // tm
    row = lambda i: (mod_row0 + (i // tiles_per_seq if mod_row0 else 0), 0, 0)
    in_specs = [pl.BlockSpec((tm, D_MODEL), lambda i: (i, 0)),
                pl.BlockSpec((None, 1, 6 * D_MODEL), row),
                pl.BlockSpec((1, D_MODEL), lambda i: (0, 0)),
                pl.BlockSpec((D_MODEL, IN_WIDTH), lambda i: (0, 0))]
    args = [x, mod3, norm_g, w_in]
    if rope_tabs is not None:
        in_specs += [pl.BlockSpec((tm, LANES), lambda i: (i % tiles_per_seq, 0))] * 2
        args += list(rope_tabs)
    widths = (ATTN_WIDTH, KV_WIDTH, KV_WIDTH, CONV_WIDTH, CONV_WIDTH)
    dtypes = (bf16, kv_dtype, kv_dtype, bf16, bf16)
    out_specs = [pl.BlockSpec((tm, w), lambda i: (i, 0)) for w in widths]
    out_shape = [jax.ShapeDtypeStruct((t, w), d) for w, d in zip(widths, dtypes)]
    if rope_tabs is not None:
        out_specs[2] = pl.BlockSpec((None, KV_WIDTH, tm),
                                    lambda i: (i // tiles_per_seq, 0, i % tiles_per_seq))
        out_shape[2] = jax.ShapeDtypeStruct((t // seq, KV_WIDTH, seq), kv_dtype)
    return pl.pallas_call(
        functools.partial(_inproj_kernel, rope=rope_tabs is not None),
        grid=(t // tm,),
        in_specs=in_specs,
        out_specs=out_specs,
        out_shape=out_shape,
        compiler_params=_params("arbitrary"),
        name="inproj_rope" if rope_tabs is not None else "inproj",
    )(*args)


def _rope_tables(n):
    t = np.arange(n)
    inv = ROPE_BASE ** (-np.arange(N_FREQ, dtype=np.float32) / N_FREQ)
    rows = (t // GRID_W).astype(np.float32)
    cols = (t % GRID_W).astype(np.float32)
    d = np.arange(LANES) % HEAD_DIM
    pos = np.where(d[None, :] < HEAD_DIM // 2, rows[:, None], cols[:, None])
    ang = jnp.asarray(pos.astype(np.float32) * inv[d % N_FREQ][None, :])
    sign = np.where((d % (2 * N_FREQ)) < N_FREQ, -1.0, 1.0).astype(np.float32)
    return jnp.cos(ang), jnp.sin(ang) * sign[None, :]


def _attention_core(sink_ref, q_ref, o_ref, subs, shared, bq):
    n_col = Q_PER_KV * bq
    col_head = lax.broadcasted_iota(jnp.int32, (1, n_col), 1) // bq
    sink_rows = []
    for g in range(N_KV_HEADS):
        sink_row = jnp.full((1, n_col), sink_ref[g * Q_PER_KV], f32)
        for j in range(1, Q_PER_KV):
            sink_row = jnp.where(col_head == j, sink_ref[g * Q_PER_KV + j], sink_row)
        sink_rows.append(sink_row * LOG2_E)
    ones_rows = jnp.ones((BF16_SUBLANES, 1), bf16)

    def with_ones(vals_t, g):
        return jnp.concatenate(
            [vals_t[g * HEAD_DIM:(g + 1) * HEAD_DIM, :],
             jnp.broadcast_to(ones_rows, (BF16_SUBLANES, vals_t.shape[1]))], axis=0)

    shared_vals = None if shared is None else [with_ones(shared[1], g) for g in range(N_KV_HEADS)]
    qts = [(q_ref[i * bq:(i + 1) * bq, :].astype(f32) * (SCALE * LOG2_E)).T.astype(bf16)
           for i in range(len(subs))]

    def scores(i, g):
        lo, hi = g * HEAD_DIM, (g + 1) * HEAD_DIM
        heads = range(g * Q_PER_KV, (g + 1) * Q_PER_KV)
        qt_g = jnp.concatenate([qts[i][h * HEAD_DIM:(h + 1) * HEAD_DIM, :] for h in heads], axis=1)
        s_own = _dot(subs[i]["keys"][:, lo:hi], qt_g)
        if subs[i]["mask"] is not None:
            s_own = jnp.where(subs[i]["mask"], s_own, NEG)
        s_shared = None if shared is None else _dot(shared[0][:, lo:hi], qt_g)
        return s_own, s_shared

    def weighted_values(i, g, s_own, s_shared):
        m = jnp.maximum(jnp.max(s_own, axis=0, keepdims=True), sink_rows[g])
        if s_shared is not None:
            m = jnp.maximum(m, jnp.max(s_shared, axis=0, keepdims=True))
        o_aug = _dot(with_ones(subs[i]["vals_t"], g), jnp.exp2(s_own - m).astype(bf16))
        if s_shared is not None:
            o_aug = o_aug + _dot(shared_vals[g], jnp.exp2(s_shared - m).astype(bf16))
        den = o_aug[HEAD_DIM:HEAD_DIM + 1, :] + jnp.exp2(sink_rows[g] - m)
        return o_aug[:HEAD_DIM, :] / den

    passes = [(i, g) for i in range(len(subs)) for g in range(N_KV_HEADS)]
    out_t = {}
    pending = scores(*passes[0])
    for n, (i, g) in enumerate(passes):
        nxt = scores(*passes[n + 1]) if n + 1 < len(passes) else None
        out_t[i, g] = weighted_values(i, g, *pending)
        pending = nxt
    for i in range(len(subs)):
        blocks = [jnp.concatenate([out_t[i, g][:, j * bq:(j + 1) * bq]
                                   for g in range(N_KV_HEADS)], axis=0).T
                  for j in range(Q_PER_KV)]
        o_ref[i * bq:(i + 1) * bq, :] = jnp.concatenate(blocks, axis=1).astype(o_ref.dtype)


ATTN_HEAD_PERM = np.array([(Q_PER_KV * g + j) * HEAD_DIM + d for j in range(Q_PER_KV)
                           for g in range(N_KV_HEADS) for d in range(HEAD_DIM)])


def _ctx_attn_kernel(sink_ref, q_ref, k_ref, v_ref, o_ref, *, bq, seq):
    subs = []
    for s in range(q_ref.shape[0] // seq):
        keys = k_ref[s * seq:(s + 1) * seq, :].astype(bf16)
        vals_t = v_ref[s * seq:(s + 1) * seq, :].T.astype(bf16)
        subs += [dict(keys=keys, vals_t=vals_t, mask=None)] * (seq // bq)
    _attention_core(sink_ref, q_ref, o_ref, subs, None, bq)


def _context_attention(q, k, v, sink, *, seq, bq=128, seqs_per_step=4):
    t = q.shape[0]
    rows = seq * seqs_per_step
    return pl.pallas_call(
        functools.partial(_ctx_attn_kernel, bq=bq, seq=seq),
        grid=(t // rows,),
        in_specs=[pl.BlockSpec(memory_space=pltpu.SMEM),
                  pl.BlockSpec((rows, ATTN_WIDTH), lambda b: (b, 0)),
                  pl.BlockSpec((rows, KV_WIDTH), lambda b: (b, 0)),
                  pl.BlockSpec((rows, KV_WIDTH), lambda b: (b, 0))],
        out_specs=pl.BlockSpec((rows, ATTN_WIDTH), lambda b: (b, 0)),
        out_shape=jax.ShapeDtypeStruct((t, ATTN_WIDTH), bf16),
        compiler_params=_params("arbitrary"),
        name="ctx_attn",
    )(sink, q, k, v)


def _lat_attn_kernel(sink_ref, q_ref, k_ref, vt_ref, ck_ref, cv_ref, *rest, bq, seq, n_cast):
    o_ref = rest[n_cast]
    for src_ref, dst_ref in zip(rest[:n_cast], rest[n_cast + 1:]):
        dst_ref[...] = src_ref[...].astype(dst_ref.dtype)
    band = bq + 2 * WINDOW
    n_col = Q_PER_KV * bq
    delta = ((lax.broadcasted_iota(jnp.int32, (band, n_col), 1) & (bq - 1))
             - lax.broadcasted_iota(jnp.int32, (band, n_col), 0))
    n_sub = q_ref.shape[0] // bq
    subs = []
    for sb in range(n_sub):
        i = pl.program_id(1) * n_sub + sb
        start = pl.multiple_of(jnp.clip(i * bq - WINDOW, 0, seq - band), LANES)
        shifted = delta + (i * bq - start + WINDOW)
        subs.append(dict(keys=k_ref[pl.ds(start, band), :],
                         vals_t=vt_ref[:, pl.ds(start, band)],
                         mask=shifted.astype(jnp.uint32) <= 2 * WINDOW))
    shared = (ck_ref[...].astype(bf16), cv_ref[...].T.astype(bf16))
    _attention_core(sink_ref, q_ref, o_ref, subs, shared, bq)


LAT_ATTN_SUB_BLOCKS = 8


def _latent_attention(q, k, vt, ck, cv, sink, *, seq, cast=(), bq=128,
                      sub_blocks=LAT_ATTN_SUB_BLOCKS):
    nb, past = ck.shape[0], ck.shape[1]
    assert bq == LANES and N_KV_HEADS * HEAD_DIM == LANES
    q3, k3 = (a.reshape(nb, seq, a.shape[-1]) for a in (q, k))
    tq = bq * sub_blocks
    steps_per_seq = seq // tq
    n_steps = nb * steps_per_seq
    cast3 = [a.reshape((n_steps, -1) + a.shape[2:]) for a in cast]
    cast_specs = [pl.BlockSpec((None,) + a.shape[1:], lambda b, i: (b * steps_per_seq + i, 0, 0))
                  for a in cast3]
    out, *cast_out = pl.pallas_call(
        functools.partial(_lat_attn_kernel, bq=bq, seq=seq, n_cast=len(cast)),
        grid=(nb, steps_per_seq),
        in_specs=[pl.BlockSpec(memory_space=pltpu.SMEM),
                  pl.BlockSpec((None, tq, ATTN_WIDTH), lambda b, i: (b, i, 0)),
                  pl.BlockSpec((None, seq, KV_WIDTH), lambda b, i: (b, 0, 0)),
                  pl.BlockSpec((None, KV_WIDTH, seq), lambda b, i: (b, 0, 0)),
                  pl.BlockSpec((None, past, KV_WIDTH), lambda b, i: (b, 0, 0)),
                  pl.BlockSpec((None, past, KV_WIDTH), lambda b, i: (b, 0, 0))] + cast_specs,
        out_specs=[pl.BlockSpec((None, tq, ATTN_WIDTH), lambda b, i: (b, i, 0))] + cast_specs,
        out_shape=[jax.ShapeDtypeStruct((nb, seq, ATTN_WIDTH), bf16)]
        + [jax.ShapeDtypeStruct(a.shape, bf16) for a in cast3],
        compiler_params=_params("arbitrary", "arbitrary"),
        name="lat_attn",
    )(sink, q3, k3, vt, ck, cv, *cast3)
    return [out.reshape(nb * seq, ATTN_WIDTH)] + [o.reshape(a.shape) for o, a in zip(cast_out, cast)]


def _route(logits_t):
    r = logits_t.shape[1]
    big = jnp.float32(LANES)
    crow = lax.broadcasted_iota(jnp.int32, (F32_SUBLANES, r), 0)
    crow_f = crow.astype(f32)
    lc = jnp.where(crow < N_GROUPS, logits_t[COARSE_LANE0:COARSE_LANE0 + F32_SUBLANES, :], -jnp.inf)
    mc = jnp.max(lc, axis=0, keepdims=True)
    grp = jnp.min(jnp.where(lc == mc, crow_f, big), axis=0, keepdims=True)
    pg = 1.0 / jnp.sum(jnp.exp(lc - mc), axis=0, keepdims=True)
    erow = lax.broadcasted_iota(jnp.int32, (N_EXPERTS, r), 0).astype(f32)
    in_g = jnp.floor(erow * (1.0 / EXPERTS_PER_GROUP)) == grp
    fl = jnp.where(in_g, logits_t[0:N_EXPERTS, :], -jnp.inf)
    m1 = jnp.max(fl, axis=0, keepdims=True)
    i1 = jnp.min(jnp.where(fl == m1, erow, big), axis=0, keepdims=True)
    fl2 = jnp.where(erow == i1, -jnp.inf, fl)
    m2 = jnp.max(fl2, axis=0, keepdims=True)
    i2 = jnp.min(jnp.where(fl2 == m2, erow, big), axis=0, keepdims=True)
    e2 = jnp.exp(m2 - m1)
    p1 = pg / (1.0 + e2)
    p2 = pg * e2 / (1.0 + e2)
    packed = jnp.where(crow == 0, i1, jnp.where(crow == 1, i2, jnp.where(crow == 2, p1, p2)))
    return jnp.where(crow < 4, packed, 0.0)


def _post_kernel(x_ref, attn_ref, u_ref, up_ref, un_ref, bg_ref, mod_ref, ag_ref, cg_ref, cw_ref,
                 cb_ref, wo_ref, n2_ref, wr_ref, br_ref, x1_ref, h2_ref, route_ref, *, tm, seq):
    i = pl.program_id(0)
    u = u_ref[...].astype(f32)
    rows = lax.broadcasted_iota(jnp.int32, (tm, 1), 0)
    spos = (i * tm + rows) % seq
    u_dn = jnp.where(rows == 0, up_ref[...].astype(f32)[BF16_SUBLANES - 1:, :],
                     pltpu.roll(u, 1, axis=0))
    u_dn = jnp.where(spos == 0, 0.0, u_dn)
    u_up = jnp.where(rows == tm - 1, un_ref[...].astype(f32)[0:1, :], pltpu.roll(u, tm - 1, axis=0))
    u_up = jnp.where(spos == seq - 1, 0.0, u_up)
    y = u_dn * cw_ref[0:1, :] + u * cw_ref[1:2, :] + u_up * cw_ref[2:3, :] + cb_ref[...]
    conv = bg_ref[...].astype(f32) * y
    attn_n = (_rms(attn_ref[...].astype(f32)) * ag_ref[...]).astype(bf16)
    conv_n = (_rms(conv) * cg_ref[...]).astype(bf16)
    subs = [slice(s * POST_SUB, (s + 1) * POST_SUB) for s in range(tm // POST_SUB)]
    mixed = [_dot(attn_n[rs], wo_ref[0:ATTN_WIDTH, :]) + _dot(conv_n[rs], wo_ref[ATTN_WIDTH:, :])
             for rs in subs]
    logits = []
    for s, rs in enumerate(subs):
        x1 = x_ref[rs, :] + mod_ref[:, G1:G1 + D_MODEL] * mixed[s]
        x1_ref[rs, :] = x1
        h2 = _rms(x1) * n2_ref[...]
        h2 = h2 * (1.0 + mod_ref[:, SC2:SC2 + D_MODEL]) + mod_ref[:, SH2:SH2 + D_MODEL]
        for c in range(D_MODEL // LANES):
            h2_ref[pl.ds(rs.start * F32_SUBLANES + c, POST_SUB, stride=F32_SUBLANES), :] = (
                h2[:, c * LANES:(c + 1) * LANES])
        h2_hi = h2.astype(bf16)
        h2_lo = (h2 - h2_hi.astype(f32)).astype(bf16)
        both = _dot(h2_hi, wr_ref[...])
        logits.append(both[:, :LANES] + both[:, LANES:] + _dot(h2_lo, wr_ref[:, :LANES]))
    for s, rs in enumerate(subs):
        route_ref[:, rs] = _route((logits[s] + br_ref[...]).T)


def _post(x, attn, u, bg, mod3, p, *, seq, mod_row0, tm=4 * POST_SUB):
    t = x.shape[0]
    tiles_per_seq = max(seq // tm, 1)
    halo = BF16_SUBLANES
    n_halo = t // halo
    row = lambda i: (mod_row0 + (i // tiles_per_seq if mod_row0 else 0), 0, 0)
    tile = lambda w: pl.BlockSpec((tm, w), lambda i: (i, 0))
    full = lambda a: pl.BlockSpec(a.shape, lambda i: (0,) * a.ndim)
    small = [p["attn_out_g"], p["conv_out_g"], p["conv_w"], p["conv_b"], p["w_o"], p["norm2_g"],
             p["w_router"], p["b_router"]]
    return pl.pallas_call(
        functools.partial(_post_kernel, tm=tm, seq=seq),
        grid=(t // tm,),
        in_specs=[tile(D_MODEL), tile(ATTN_WIDTH), tile(CONV_WIDTH),
                  pl.BlockSpec((halo, CONV_WIDTH),
                               lambda i: (jnp.maximum(i * (tm // halo) - 1, 0), 0)),
                  pl.BlockSpec((halo, CONV_WIDTH),
                               lambda i: (jnp.minimum((i + 1) * (tm // halo), n_halo - 1), 0)),
                  tile(CONV_WIDTH),
                  pl.BlockSpec((None, 1, 6 * D_MODEL), row)] + [full(a) for a in small],
        out_specs=[tile(D_MODEL), pl.BlockSpec((tm * F32_SUBLANES, LANES), lambda i: (i, 0)),
                   pl.BlockSpec((F32_SUBLANES, tm), lambda i: (0, i))],
        out_shape=[jax.ShapeDtypeStruct((t, D_MODEL), f32),
                   jax.ShapeDtypeStruct((t * F32_SUBLANES, LANES), f32),
                   jax.ShapeDtypeStruct((F32_SUBLANES, t), f32)],
        compiler_params=_params("arbitrary"),
        name="post",
    )(x, attn, u, u, u, bg, mod3, *small)


def _max_chunks(blk):
    n = -(-TOP_K * blk // MOE_CHUNK) + EXPERTS_PER_STEP
    return n + n % 2


def _dispatch_tables(route, blk):
    nblk = route.shape[1] // blk
    n_slots = TOP_K * blk
    flat = lambda a: a.reshape(TOP_K, nblk, blk).transpose(1, 0, 2).reshape(nblk, n_slots)
    experts = flat(route[0:TOP_K].astype(jnp.int32))
    weights = flat(route[TOP_K:2 * TOP_K])
    slot_ids = lax.broadcasted_iota(jnp.int32, experts.shape, 1)
    _, order, w_sorted = lax.sort((experts, slot_ids, weights), dimension=1, num_keys=1)
    bounds = jnp.arange(N_EXPERTS + 1, dtype=jnp.int32)
    offs = jnp.sum(experts[:, :, None] < bounds[None, None, :], axis=1, dtype=jnp.int32)
    tail = ((0, 0), (0, MOE_CHUNK))
    src_rows = jnp.pad((order & (blk - 1)) * F32_SUBLANES, tail)
    dst_rows = jnp.pad(order * F32_SUBLANES, tail, constant_values=n_slots * F32_SUBLANES)
    w_sorted = jnp.pad(w_sorted, tail)

    eg, n_steps, max_c = EXPERTS_PER_STEP, N_EXPERTS // EXPERTS_PER_STEP, _max_chunks(blk)
    lo = offs[:, :N_EXPERTS].reshape(nblk, n_steps, eg)
    n_ch = (offs[:, 1:] - offs[:, :N_EXPERTS] + MOE_CHUNK - 1) // MOE_CHUNK
    n_ch = n_ch.reshape(nblk, n_steps, eg)
    cum = jnp.cumsum(n_ch, axis=2)
    total = cum[..., -1]
    pos = jnp.arange(max_c, dtype=jnp.int32)
    j = jnp.sum(cum[:, :, None, :] <= pos[None, None, :, None], axis=-1, dtype=jnp.int32)
    j = jnp.minimum(j, eg - 1)
    hit = j[..., None] == jnp.arange(eg, dtype=jnp.int32)
    pick = lambda a: jnp.sum(jnp.where(hit, a[:, :, None, :], 0), axis=-1)
    base = pick(lo) + (pos - pick(cum - n_ch)) * MOE_CHUNK
    valid = pos < total[..., None]
    base = jnp.where(valid, base, n_slots).reshape(nblk, 1, n_steps * max_c)
    j = jnp.where(valid, j, 0).reshape(nblk, 1, n_steps * max_c)
    return (src_rows[:, None, :], dst_rows[:, None, :], w_sorted[:, None, :], j, base,
            total[:, None, :])


def _moe_kernel(src_ref, dst_ref, wts_ref, cj_ref, cbase_g_ref, cbase_s_ref, nchunks_ref,
                h_ref, wg_ref, wu_ref, wd_ref, mod_ref, fg_ref, x1_hbm, out_hbm,
                ytmp_ref, xg0_ref, og0_ref, xg1_ref, og1_ref, xbuf_ref, obuf_ref, sem_in, sem_out,
                *, blk):
    n_col = D_MODEL // LANES
    blk_id, step = pl.program_id(0), pl.program_id(1)
    last_step = step == pl.num_programs(1) - 1
    max_c = _max_chunks(blk)
    stage_refs = (xg0_ref, og0_ref, xg1_ref, og1_ref)
    n_sub = blk // FINAL_ROWS

    def x1_copy(s):
        rows = pl.ds(blk_id * blk + s * FINAL_ROWS, FINAL_ROWS)
        slot = s % FINAL_SLOTS
        return pltpu.make_async_copy(x1_hbm.at[rows, :], xbuf_ref.at[slot], sem_in.at[slot])

    def out_copy(s):
        rows = pl.ds(blk_id * blk + s * FINAL_ROWS, FINAL_ROWS)
        slot = s % FINAL_SLOTS
        return pltpu.make_async_copy(obuf_ref.at[slot], out_hbm.at[rows, :], sem_out.at[slot])

    @pl.when(last_step)
    def _():
        for s in range(min(FINAL_SLOTS, n_sub)):
            x1_copy(s).start()

    def gather(base, xg_ref):
        for r in range(MOE_CHUNK):
            src = pl.multiple_of(src_ref[0, base + r], F32_SUBLANES)
            xg_ref[pl.ds(r, F32_SUBLANES, stride=CHUNK_PITCH), :] = h_ref[pl.ds(src, F32_SUBLANES), :]

    def experts_mlp(j, xg_ref, og_ref):
        x = jnp.concatenate([xg_ref[c * CHUNK_PITCH:c * CHUNK_PITCH + MOE_CHUNK, :]
                             for c in range(n_col)], axis=1).astype(bf16)
        g = _dot(x, wg_ref[j])
        up = _dot(x, wu_ref[j])
        act = (g * jax.nn.sigmoid(g)) * up
        out = _dot(act.astype(bf16), wd_ref[j])
        for c in range(n_col):
            og_ref[c * CHUNK_PITCH:c * CHUNK_PITCH + MOE_CHUNK, :] = out[:, c * LANES:(c + 1) * LANES]

    def scatter(base, og_ref):
        for r in range(MOE_CHUNK):
            dst = pl.multiple_of(dst_ref[0, base + r], F32_SUBLANES)
            ytmp_ref[pl.ds(dst, F32_SUBLANES), :] = (
                og_ref[pl.ds(r, F32_SUBLANES, stride=CHUNK_PITCH), :] * wts_ref[0, base + r])

    def pair(pi, carry):
        chunks = [(step * max_c + 2 * pi + half, stage_refs[2 * half], stage_refs[2 * half + 1])
                  for half in range(2)]
        for p, xg_ref, _ in chunks:
            gather(cbase_g_ref[0, p], xg_ref)
        for p, xg_ref, og_ref in chunks:
            experts_mlp(cj_ref[0, p], xg_ref, og_ref)
        for p, _, og_ref in chunks:
            scatter(cbase_s_ref[0, p], og_ref)
        return carry

    n_chunks = nchunks_ref[0, step]
    lax.fori_loop(0, n_chunks // 2, pair, 0)

    @pl.when(n_chunks % 2 == 1)
    def _():
        p = step * max_c + n_chunks - 1
        gather(cbase_g_ref[0, p], xg0_ref)
        experts_mlp(cj_ref[0, p], xg0_ref, og0_ref)
        scatter(cbase_s_ref[0, p], og0_ref)

    @pl.when(last_step)
    def _():
        n = blk * F32_SUBLANES
        for k in range(1, TOP_K):
            ytmp_ref[0:n, :] = ytmp_ref[0:n, :] + ytmp_ref[k * n:(k + 1) * n, :]
        gate2 = mod_ref[:, G2:G2 + D_MODEL]
        for s in range(n_sub):
            slot = s % FINAL_SLOTS
            x1_copy(s).wait()
            if s >= FINAL_SLOTS:
                out_copy(s - FINAL_SLOTS).wait()
            first = s * FINAL_ROWS * F32_SUBLANES
            y = jnp.concatenate([ytmp_ref[pl.ds(first + c, FINAL_ROWS, stride=F32_SUBLANES), :]
                                 for c in range(n_col)], axis=1)
            x2 = xbuf_ref[slot] + gate2 * y
            obuf_ref[slot] = _rms(x2) * fg_ref[...]
            out_copy(s).start()
            if s + FINAL_SLOTS < n_sub:
                x1_copy(s + FINAL_SLOTS).start()
        for s in range(max(n_sub - FINAL_SLOTS, 0), n_sub):
            out_copy(s).wait()


def _moe(h2_tiles, tables, x1, mod3, final_g, w_gate, w_up, w_down, *, seq, mod_row0,
         blk=MOE_BLOCK):
    t = h2_tiles.shape[0] // F32_SUBLANES
    src_rows, dst_rows, w_sorted, cj, cbase, n_pairs = tables
    n_slots = TOP_K * blk
    eg, n_steps = EXPERTS_PER_STEP, N_EXPERTS // EXPERTS_PER_STEP
    blocks_per_seq = max(seq // blk, 1)
    row = lambda b, g: (mod_row0 + (b // blocks_per_seq if mod_row0 else 0), 0, 0)
    smem = lambda n: pl.BlockSpec((None, 1, n), lambda b, g: (b, 0, 0), memory_space=pltpu.SMEM)
    stage = pltpu.VMEM((D_MODEL // LANES * CHUNK_PITCH, LANES), f32)
    final_buf = pltpu.VMEM((FINAL_SLOTS, FINAL_ROWS, D_MODEL), f32)
    return pl.pallas_call(
        functools.partial(_moe_kernel, blk=blk),
        grid=(t // blk, n_steps),
        in_specs=[smem(n_slots + MOE_CHUNK)] * 3 + [smem(n_steps * _max_chunks(blk))] * 3 + [
                  smem(n_steps),
                  pl.BlockSpec((blk * F32_SUBLANES, LANES), lambda b, g: (b, 0)),
                  pl.BlockSpec((eg, D_MODEL, D_EXPERT), lambda b, g: (g, 0, 0)),
                  pl.BlockSpec((eg, D_MODEL, D_EXPERT), lambda b, g: (g, 0, 0)),
                  pl.BlockSpec((eg, D_EXPERT, D_MODEL), lambda b, g: (g, 0, 0)),
                  pl.BlockSpec((None, 1, 6 * D_MODEL), row),
                  pl.BlockSpec((1, D_MODEL), lambda b, g: (0, 0)),
                  pl.BlockSpec(memory_space=pl.ANY)],
        out_specs=pl.BlockSpec(memory_space=pl.ANY),
        out_shape=jax.ShapeDtypeStruct((t, D_MODEL), f32),
        scratch_shapes=[pltpu.VMEM(((n_slots + 1) * F32_SUBLANES, LANES), f32)] + [stage] * 4 + [
                        final_buf, final_buf,
                        pltpu.SemaphoreType.DMA((FINAL_SLOTS,)),
                        pltpu.SemaphoreType.DMA((FINAL_SLOTS,))],
        compiler_params=_params("arbitrary", "arbitrary", vmem=MOE_VMEM_LIMIT_BYTES),
        name="moe",
    )(src_rows, dst_rows, w_sorted, cj, cbase, cbase, n_pairs, h2_tiles, w_gate, w_up, w_down,
      mod3, final_g, x1)


def _mixers(x, mod3, p, *, seq, mod_row0, ctx_kv, cast=()):
    latent = ctx_kv is not None
    q, k, v, u, bg = _inproj(x, mod3, p["norm1_g"], p["w_in"], seq=seq, mod_row0=mod_row0,
                             kv_dtype=bf16 if latent else f32,
                             rope_tabs=_rope_tables(seq) if latent else None)
    if latent:
        attn, *cast = _latent_attention(q, k, v, ctx_kv[0], ctx_kv[1], p["sink"], seq=seq,
                                        cast=cast)
    else:
        attn = _context_attention(q, k, v, p["sink"], seq=seq)
    x1, h2_tiles, route = _post(x, attn, u, bg, mod3, p, seq=seq, mod_row0=mod_row0)
    return x1, h2_tiles, route, k, v, cast


def _experts(x1, h2_tiles, tables, mod3, p, *, seq, mod_row0):
    out = _moe(h2_tiles, tables, x1, mod3, p["final_g"], p["w_gate"], p["w_up"], p["w_down"],
               seq=seq, mod_row0=mod_row0)
    return out.reshape(-1, seq, D_MODEL)


def kernel(x_prompt, x_sample, cache_k, cache_v, c, c_ctx, w_mod, b_mod, norm1_g, w_in, conv_w,
           conv_b, sink, attn_out_g, conv_out_g, w_o, norm2_g, w_coarse, b_coarse, w_fine, b_fine,
           w_gate, w_up, w_down, final_g):
    batch, seq, _ = x_prompt.shape
    dec_batch, dec_seq, _ = x_sample.shape
    past = cache_k.shape[2]
    assert w_mod.shape[0] == 1 and 1 + dec_batch <= MOD_ROWS

    cvecs = jnp.concatenate([c_ctx[None], c, jnp.zeros((MOD_ROWS - 1 - dec_batch, D_MODEL), f32)])
    mod3 = _modulation(cvecs, w_mod[0], b_mod[0]).reshape(MOD_ROWS, 1, 6 * D_MODEL)

    pad = jnp.zeros((D_MODEL, LANES - N_EXPERTS - N_GROUPS), f32)
    mix_rows = np.concatenate([ATTN_HEAD_PERM, np.arange(ATTN_WIDTH, ATTN_WIDTH + CONV_WIDTH)])
    p = {
        "norm1_g": norm1_g, "w_in": w_in[0].astype(bf16), "conv_w": conv_w[0], "conv_b": conv_b,
        "sink": sink[0], "attn_out_g": attn_out_g[:, ATTN_HEAD_PERM], "conv_out_g": conv_out_g,
        "w_o": w_o[0][mix_rows].astype(bf16), "norm2_g": norm2_g,
        "w_router": _split_bf16(jnp.concatenate([w_fine[0], w_coarse[0], pad], axis=1)),
        "b_router": jnp.concatenate([b_fine[0], b_coarse[0], pad[0]])[None],
        "final_g": final_g[None],
    }

    x1_p, h2_p, route_p, k_p, v_p, _ = _mixers(x_prompt.reshape(batch * seq, D_MODEL), mod3, p,
                                               seq=seq, mod_row0=0, ctx_kv=None)
    new_k = k_p.reshape(batch, 1, seq, N_KV_HEADS, HEAD_DIM)
    new_v = v_p.reshape(batch, 1, seq, N_KV_HEADS, HEAD_DIM)

    ctx_kv = (cache_k[:, 0].reshape(dec_batch, past, KV_WIDTH),
              cache_v[:, 0].reshape(dec_batch, past, KV_WIDTH))
    experts_f32 = (w_gate[0], w_up[0], w_down[0])
    ride = N_EXPERTS % (dec_batch * dec_seq // (LANES * LAT_ATTN_SUB_BLOCKS)) == 0
    x1_s, h2_s, route_s, _, _, experts_bf16 = _mixers(
        x_sample.reshape(dec_batch * dec_seq, D_MODEL), mod3, p, seq=dec_seq, mod_row0=1,
        ctx_kv=ctx_kv, cast=experts_f32 if ride else ())
    if not ride:
        experts_bf16 = [w.astype(bf16) for w in experts_f32]
    p = dict(p, w_gate=experts_bf16[0], w_up=experts_bf16[1], w_down=experts_bf16[2])

    n_blk_p = batch * seq // MOE_BLOCK
    tables = _dispatch_tables(jnp.concatenate([route_p, route_s], axis=1), MOE_BLOCK)
    y_prompt = _experts(x1_p, h2_p, [t[:n_blk_p] for t in tables], mod3, p, seq=seq, mod_row0=0)
    y_sample = _experts(x1_s, h2_s, [t[n_blk_p:] for t in tables], mod3, p, seq=dec_seq,
                        mod_row0=1)
    return y_prompt, y_sample, new_k, new_v
```

```python
import functools

import jax
import jax.numpy as jnp
import numpy as np
from jax import lax
from jax.experimental import pallas as pl
from jax.experimental.pallas import tpu as pltpu

D_MODEL = 1024
HEAD_DIM = 64
ATTN_WIDTH = 512
N_HEADS = 8
N_KV_HEADS = 2
Q_PER_KV = 4
KV_WIDTH = 128
CONV_WIDTH = 512
CONV_K = 3
WINDOW = 128
GRID_W = 64
ROPE_BASE = 10000.0
N_FREQ = 16
N_GROUPS = 4
EXPERTS_PER_GROUP = 8
N_EXPERTS = 32
TOP_K = 2
D_EXPERT = 256
IN_WIDTH = ATTN_WIDTH + 2 * KV_WIDTH + 3 * CONV_WIDTH
EPS = 1e-6
NEG = -1e30
SCALE = HEAD_DIM ** -0.5
LOG2_E = 1.4426950408889634

LANES = 128
F32_SUBLANES = 8
BF16_SUBLANES = 16
assert D_MODEL == F32_SUBLANES * LANES

INPROJ_ROWS = 512
INPROJ_SUB = 256
POST_SUB = 256
MOE_BLOCK = 2048
MOE_CHUNK = 160
CHUNK_PITCH = MOE_CHUNK + F32_SUBLANES
EXPERTS_PER_STEP = 4
FINAL_ROWS = 256
FINAL_SLOTS = 4
MOE_VMEM_LIMIT_BYTES = 60 * 1024 * 1024
VMEM_LIMIT_BYTES = 48 * 1024 * 1024

SH1, SC1, G1, SH2, SC2, G2 = (i * D_MODEL for i in range(6))
MOD_ROWS = 8

COARSE_LANE0 = N_EXPERTS

f32 = jnp.float32
bf16 = jnp.bfloat16


def _params(*semantics, vmem=VMEM_LIMIT_BYTES):
    return pltpu.CompilerParams(dimension_semantics=semantics, vmem_limit_bytes=vmem)


def _rms(x):
    return x * lax.rsqrt(jnp.mean(x * x, axis=-1, keepdims=True) + EPS)


def _dot(a, b):
    return jnp.dot(a, b, preferred_element_type=f32)


def _split_bf16(w):
    hi = w.astype(bf16)
    lo = (w - hi.astype(f32)).astype(bf16)
    return jnp.concatenate([hi, lo], axis=1)


def _mod_kernel(cv_ref, w_ref, b_ref, o_ref):
    a = cv_ref[...]
    a = a * jax.nn.sigmoid(a)
    o_ref[...] = jnp.dot(a, w_ref[...], precision=lax.Precision.HIGHEST,
                         preferred_element_type=f32) + b_ref[...]


def _modulation(cvecs, w_mod, b_mod):
    tn = 2 * D_MODEL
    return pl.pallas_call(
        _mod_kernel,
        grid=(6 * D_MODEL // tn,),
        in_specs=[pl.BlockSpec((MOD_ROWS, D_MODEL), lambda j: (0, 0)),
                  pl.BlockSpec((D_MODEL, tn), lambda j: (0, j)),
                  pl.BlockSpec((1, tn), lambda j: (0, j))],
        out_specs=pl.BlockSpec((MOD_ROWS, tn), lambda j: (0, j)),
        out_shape=jax.ShapeDtypeStruct((MOD_ROWS, 6 * D_MODEL), f32),
        compiler_params=_params("arbitrary"),
        name="mod",
    )(cvecs, w_mod, b_mod.reshape(1, -1))


def _swap_halves(x):
    lane = lax.broadcasted_iota(jnp.int32, x.shape, 1)
    up = pltpu.roll(x, LANES - N_FREQ, axis=1)
    dn = pltpu.roll(x, N_FREQ, axis=1)
    return jnp.where((lane % (2 * N_FREQ)) < N_FREQ, up, dn)


def _rope(x, cos, sin):
    parts = []
    for c in range(x.shape[1] // LANES):
        xc = x[:, c * LANES:(c + 1) * LANES]
        parts.append(xc * cos + _swap_halves(xc) * sin)
    return parts[0] if len(parts) == 1 else jnp.concatenate(parts, axis=1)


def _inproj_kernel(*refs, rope):
    if rope:
        x_ref, mod_ref, g_ref, w_ref, cos_ref, sin_ref, q_ref, k_ref, v_ref, u_ref, bg_ref = refs
    else:
        x_ref, mod_ref, g_ref, w_ref, q_ref, k_ref, v_ref, u_ref, bg_ref = refs
    tm = x_ref.shape[0]
    subs = [slice(s * INPROJ_SUB, (s + 1) * INPROJ_SUB) for s in range(tm // INPROJ_SUB)]
    zs = []
    for rs in subs:
        h = _rms(x_ref[rs, :]) * g_ref[...]
        h = h * (1.0 + mod_ref[:, SC1:SC1 + D_MODEL]) + mod_ref[:, SH1:SH1 + D_MODEL]
        zs.append(_dot(h.astype(bf16), w_ref[...]))
    for rs, z in zip(subs, zs):
        o = 0
        q = z[:, o:o + ATTN_WIDTH]; o += ATTN_WIDTH
        k = z[:, o:o + KV_WIDTH]; o += KV_WIDTH
        v = z[:, o:o + KV_WIDTH]; o += KV_WIDTH
        bg = z[:, o:o + CONV_WIDTH]; o += CONV_WIDTH
        cg = z[:, o:o + CONV_WIDTH]; o += CONV_WIDTH
        xin = z[:, o:o + CONV_WIDTH]
        if rope:
            q = _rope(q, cos_ref[rs, :], sin_ref[rs, :])
            k = _rope(k, cos_ref[rs, :], sin_ref[rs, :])
        q_ref[rs, :] = q.astype(q_ref.dtype)
        k_ref[rs, :] = k.astype(k_ref.dtype)
        if rope:
            v_ref[:, rs] = v.T.astype(v_ref.dtype)
        else:
            v_ref[rs, :] = v.astype(v_ref.dtype)
        u_ref[rs, :] = (cg * xin).astype(u_ref.dtype)
        bg_ref[rs, :] = bg.astype(bg_ref.dtype)


def _inproj(x, mod3, norm_g, w_in, *, seq, mod_row0, kv_dtype, rope_tabs=None, tm=INPROJ_ROWS):
    t = x.shape[0]
    tiles_per_seq = seq // tm
    row = lambda i: (mod_row0 + (i // tiles_per_seq if mod_row0 else 0), 0, 0)
    in_specs = [pl.BlockSpec((tm, D_MODEL), lambda i: (i, 0)),
                pl.BlockSpec((None, 1, 6 * D_MODEL), row),
                pl.BlockSpec((1, D_MODEL), lambda i: (0, 0)),
                pl.BlockSpec((D_MODEL, IN_WIDTH), lambda i: (0, 0))]
    args = [x, mod3, norm_g, w_in]
    if rope_tabs is not None:
        in_specs += [pl.BlockSpec((tm, LANES), lambda i: (i % tiles_per_seq, 0))] * 2
        args += list(rope_tabs)
    widths = (ATTN_WIDTH, KV_WIDTH, KV_WIDTH, CONV_WIDTH, CONV_WIDTH)
    dtypes = (bf16, kv_dtype, kv_dtype, bf16, bf16)
    out_specs = [pl.BlockSpec((tm, w), lambda i: (i, 0)) for w in widths]
    out_shape = [jax.ShapeDtypeStruct((t, w), d) for w, d in zip(widths, dtypes)]
    if rope_tabs is not None:
        out_specs[2] = pl.BlockSpec((None, KV_WIDTH, tm),
                                    lambda i: (i // tiles_per_seq, 0, i % tiles_per_seq))
        out_shape[2] = jax.ShapeDtypeStruct((t // seq, KV_WIDTH, seq), kv_dtype)
    return pl.pallas_call(
        functools.partial(_inproj_kernel, rope=rope_tabs is not None),
        grid=(t // tm,),
        in_specs=in_specs,
        out_specs=out_specs,
        out_shape=out_shape,
        compiler_params=_params("arbitrary"),
        name="inproj_rope" if rope_tabs is not None else "inproj",
    )(*args)


def _rope_tables(n):
    t = np.arange(n)
    inv = ROPE_BASE ** (-np.arange(N_FREQ, dtype=np.float32) / N_FREQ)
    rows = (t // GRID_W).astype(np.float32)
    cols = (t % GRID_W).astype(np.float32)
    d = np.arange(LANES) % HEAD_DIM
    pos = np.where(d[None, :] < HEAD_DIM // 2, rows[:, None], cols[:, None])
    ang = jnp.asarray(pos.astype(np.float32) * inv[d % N_FREQ][None, :])
    sign = np.where((d % (2 * N_FREQ)) < N_FREQ, -1.0, 1.0).astype(np.float32)
    return jnp.cos(ang), jnp.sin(ang) * sign[None, :]


def _attention_core(sink_ref, q_ref, o_ref, subs, shared, bq):
    n_col = Q_PER_KV * bq
    col_head = lax.broadcasted_iota(jnp.int32, (1, n_col), 1) // bq
    sink_rows = []
    for g in range(N_KV_HEADS):
        sink_row = jnp.full((1, n_col), sink_ref[g * Q_PER_KV], f32)
        for j in range(1, Q_PER_KV):
            sink_row = jnp.where(col_head == j, sink_ref[g * Q_PER_KV + j], sink_row)
        sink_rows.append(sink_row * LOG2_E)
    ones_rows = jnp.ones((BF16_SUBLANES, 1), bf16)

    def with_ones(vals_t):
        return jnp.concatenate(
            [vals_t, jnp.broadcast_to(ones_rows, (BF16_SUBLANES, vals_t.shape[1]))], axis=0)

    shared_vals = None if shared is None else [with_ones(vals_t) for _, vals_t in shared]
    qts = [(q_ref[i * bq:(i + 1) * bq, :].astype(f32) * (SCALE * LOG2_E)).T.astype(bf16)
           for i in range(len(subs))]

    def scores(i, g):
        lo, hi = g * HEAD_DIM, (g + 1) * HEAD_DIM
        heads = range(g * Q_PER_KV, (g + 1) * Q_PER_KV)
        qt_g = jnp.concatenate([qts[i][h * HEAD_DIM:(h + 1) * HEAD_DIM, :] for h in heads], axis=1)
        s_own = _dot(subs[i]["keys"][:, lo:hi], qt_g)
        if subs[i]["mask"] is not None:
            s_own = jnp.where(subs[i]["mask"], s_own, NEG)
        s_shared = None if shared is None else _dot(shared[g][0], qt_g)
        return s_own, s_shared

    def weighted_values(i, g, s_own, s_shared):
        m = jnp.maximum(jnp.max(s_own, axis=0, keepdims=True), sink_rows[g])
        if s_shared is not None:
            m = jnp.maximum(m, jnp.max(s_shared, axis=0, keepdims=True))
        own_vals = with_ones(subs[i]["vals_t"][g * HEAD_DIM:(g + 1) * HEAD_DIM, :])
        o_aug = _dot(own_vals, jnp.exp2(s_own - m).astype(bf16))
        if s_shared is not None:
            o_aug = o_aug + _dot(shared_vals[g], jnp.exp2(s_shared - m).astype(bf16))
        den = o_aug[HEAD_DIM:HEAD_DIM + 1, :] + jnp.exp2(sink_rows[g] - m)
        return o_aug[:HEAD_DIM, :] / den

    passes = [(i, g) for i in range(len(subs)) for g in range(N_KV_HEADS)]
    out_t = {}
    pending = scores(*passes[0])
    for n, (i, g) in enumerate(passes):
        nxt = scores(*passes[n + 1]) if n + 1 < len(passes) else None
        out_t[i, g] = weighted_values(i, g, *pending)
        pending = nxt
    for i in range(len(subs)):
        blocks = [jnp.concatenate([out_t[i, g][:, j * bq:(j + 1) * bq]
                                   for g in range(N_KV_HEADS)], axis=0).T
                  for j in range(Q_PER_KV)]
        o_ref[i * bq:(i + 1) * bq, :] = jnp.concatenate(blocks, axis=1).astype(o_ref.dtype)


ATTN_HEAD_PERM = np.array([(Q_PER_KV * g + j) * HEAD_DIM + d for j in range(Q_PER_KV)
                           for g in range(N_KV_HEADS) for d in range(HEAD_DIM)])


def _ctx_attn_kernel(sink_ref, q_ref, k_ref, v_ref, o_ref, *, bq, seq):
    subs = []
    for s in range(q_ref.shape[0] // seq):
        keys = k_ref[s * seq:(s + 1) * seq, :].astype(bf16)
        vals_t = v_ref[s * seq:(s + 1) * seq, :].T.astype(bf16)
        subs += [dict(keys=keys, vals_t=vals_t, mask=None)] * (seq // bq)
    _attention_core(sink_ref, q_ref, o_ref, subs, None, bq)


def _context_attention(q, k, v, sink, *, seq, bq=128, seqs_per_step=4):
    t = q.shape[0]
    rows = seq * seqs_per_step
    return pl.pallas_call(
        functools.partial(_ctx_attn_kernel, bq=bq, seq=seq),
        grid=(t // rows,),
        in_specs=[pl.BlockSpec(memory_space=pltpu.SMEM),
                  pl.BlockSpec((rows, ATTN_WIDTH), lambda b: (b, 0)),
                  pl.BlockSpec((rows, KV_WIDTH), lambda b: (b, 0)),
                  pl.BlockSpec((rows, KV_WIDTH), lambda b: (b, 0))],
        out_specs=pl.BlockSpec((rows, ATTN_WIDTH), lambda b: (b, 0)),
        out_shape=jax.ShapeDtypeStruct((t, ATTN_WIDTH), bf16),
        compiler_params=_params("arbitrary"),
        name="ctx_attn",
    )(sink, q, k, v)


def _lat_attn_kernel(sink_ref, q_ref, k_ref, vt_ref, ck_ref, cv_ref, *rest, bq, seq, n_cast):
    o_ref = rest[n_cast]
    for src_ref, dst_ref in zip(rest[:n_cast], rest[n_cast + 1:]):
        dst_ref[...] = src_ref[...].astype(dst_ref.dtype)
    band = bq + 2 * WINDOW
    n_col = Q_PER_KV * bq
    delta = ((lax.broadcasted_iota(jnp.int32, (band, n_col), 1) & (bq - 1))
             - lax.broadcasted_iota(jnp.int32, (band, n_col), 0))
    n_sub = q_ref.shape[0] // bq
    subs = []
    for sb in range(n_sub):
        i = pl.program_id(1) * n_sub + sb
        start = pl.multiple_of(jnp.clip(i * bq - WINDOW, 0, seq - band), LANES)
        shifted = delta + (i * bq - start + WINDOW)
        subs.append(dict(keys=k_ref[pl.ds(start, band), :],
                         vals_t=vt_ref[:, pl.ds(start, band)],
                         mask=shifted.astype(jnp.uint32) <= 2 * WINDOW))
    shared = [(ck_ref[:, g, :].astype(bf16), cv_ref[:, g, :].T.astype(bf16))
              for g in range(N_KV_HEADS)]
    _attention_core(sink_ref, q_ref, o_ref, subs, shared, bq)


LAT_ATTN_SUB_BLOCKS = 8


def _latent_attention(q, k, vt, ck, cv, sink, *, seq, cast=(), bq=128,
                      sub_blocks=LAT_ATTN_SUB_BLOCKS):
    nb = ck.shape[0]
    cache_spec = pl.BlockSpec((None,) + ck.shape[1:], lambda b, i: (b, 0, 0, 0))
    assert bq == LANES and N_KV_HEADS * HEAD_DIM == LANES
    q3, k3 = (a.reshape(nb, seq, a.shape[-1]) for a in (q, k))
    tq = bq * sub_blocks
    steps_per_seq = seq // tq
    n_steps = nb * steps_per_seq
    cast3 = [a.reshape((n_steps, -1) + a.shape[2:]) for a in cast]
    cast_specs = [pl.BlockSpec((None,) + a.shape[1:], lambda b, i: (b * steps_per_seq + i, 0, 0))
                  for a in cast3]
    out, *cast_out = pl.pallas_call(
        functools.partial(_lat_attn_kernel, bq=bq, seq=seq, n_cast=len(cast)),
        grid=(nb, steps_per_seq),
        in_specs=[pl.BlockSpec(memory_space=pltpu.SMEM),
                  pl.BlockSpec((None, tq, ATTN_WIDTH), lambda b, i: (b, i, 0)),
                  pl.BlockSpec((None, seq, KV_WIDTH), lambda b, i: (b, 0, 0)),
                  pl.BlockSpec((None, KV_WIDTH, seq), lambda b, i: (b, 0, 0)),
                  cache_spec, cache_spec] + cast_specs,
        out_specs=[pl.BlockSpec((None, tq, ATTN_WIDTH), lambda b, i: (b, i, 0))] + cast_specs,
        out_shape=[jax.ShapeDtypeStruct((nb, seq, ATTN_WIDTH), bf16)]
        + [jax.ShapeDtypeStruct(a.shape, bf16) for a in cast3],
        compiler_params=_params("arbitrary", "arbitrary"),
        name="lat_attn",
    )(sink, q3, k3, vt, ck, cv, *cast3)
    return [out.reshape(nb * seq, ATTN_WIDTH)] + [o.reshape(a.shape) for o, a in zip(cast_out, cast)]


def _route(logits_t):
    r = logits_t.shape[1]
    big = jnp.float32(LANES)
    crow = lax.broadcasted_iota(jnp.int32, (F32_SUBLANES, r), 0)
    crow_f = crow.astype(f32)
    lc = jnp.where(crow < N_GROUPS, logits_t[COARSE_LANE0:COARSE_LANE0 + F32_SUBLANES, :], -jnp.inf)
    mc = jnp.max(lc, axis=0, keepdims=True)
    grp = jnp.min(jnp.where(lc == mc, crow_f, big), axis=0, keepdims=True)
    pg = 1.0 / jnp.sum(jnp.exp(lc - mc), axis=0, keepdims=True)
    erow = lax.broadcasted_iota(jnp.int32, (N_EXPERTS, r), 0).astype(f32)
    in_g = jnp.floor(erow * (1.0 / EXPERTS_PER_GROUP)) == grp
    fl = jnp.where(in_g, logits_t[0:N_EXPERTS, :], -jnp.inf)
    m1 = jnp.max(fl, axis=0, keepdims=True)
    i1 = jnp.min(jnp.where(fl == m1, erow, big), axis=0, keepdims=True)
    fl2 = jnp.where(erow == i1, -jnp.inf, fl)
    m2 = jnp.max(fl2, axis=0, keepdims=True)
    i2 = jnp.min(jnp.where(fl2 == m2, erow, big), axis=0, keepdims=True)
    e2 = jnp.exp(m2 - m1)
    p1 = pg / (1.0 + e2)
    p2 = pg * e2 / (1.0 + e2)
    packed = jnp.where(crow == 0, i1, jnp.where(crow == 1, i2, jnp.where(crow == 2, p1, p2)))
    return jnp.where(crow < 4, packed, 0.0)


def _post_kernel(x_ref, attn_ref, u_ref, up_ref, un_ref, bg_ref, mod_ref, ag_ref, cg_ref, cw_ref,
                 cb_ref, wo_ref, n2_ref, wr_ref, br_ref, x1_ref, h2_ref, route_ref, *, tm, seq):
    i = pl.program_id(0)
    u = u_ref[...].astype(f32)
    rows = lax.broadcasted_iota(jnp.int32, (tm, 1), 0)
    spos = (i * tm + rows) % seq
    u_dn = jnp.where(rows == 0, up_ref[...].astype(f32)[BF16_SUBLANES - 1:, :],
                     pltpu.roll(u, 1, axis=0))
    u_dn = jnp.where(spos == 0, 0.0, u_dn)
    u_up = jnp.where(rows == tm - 1, un_ref[...].astype(f32)[0:1, :], pltpu.roll(u, tm - 1, axis=0))
    u_up = jnp.where(spos == seq - 1, 0.0, u_up)
    y = u_dn * cw_ref[0:1, :] + u * cw_ref[1:2, :] + u_up * cw_ref[2:3, :] + cb_ref[...]
    conv = bg_ref[...].astype(f32) * y
    attn_n = (_rms(attn_ref[...].astype(f32)) * ag_ref[...]).astype(bf16)
    conv_n = (_rms(conv) * cg_ref[...]).astype(bf16)
    subs = [slice(s * POST_SUB, (s + 1) * POST_SUB) for s in range(tm // POST_SUB)]
    mixed = [_dot(attn_n[rs], wo_ref[0:ATTN_WIDTH, :]) + _dot(conv_n[rs], wo_ref[ATTN_WIDTH:, :])
             for rs in subs]
    logits = []
    for s, rs in enumerate(subs):
        x1 = x_ref[rs, :] + mod_ref[:, G1:G1 + D_MODEL] * mixed[s]
        x1_ref[rs, :] = x1
        h2 = _rms(x1) * n2_ref[...]
        h2 = h2 * (1.0 + mod_ref[:, SC2:SC2 + D_MODEL]) + mod_ref[:, SH2:SH2 + D_MODEL]
        for c in range(D_MODEL // LANES):
            h2_ref[pl.ds(rs.start * F32_SUBLANES + c, POST_SUB, stride=F32_SUBLANES), :] = (
                h2[:, c * LANES:(c + 1) * LANES])
        h2_hi = h2.astype(bf16)
        h2_lo = (h2 - h2_hi.astype(f32)).astype(bf16)
        both = _dot(h2_hi, wr_ref[...])
        logits.append(both[:, :LANES] + both[:, LANES:] + _dot(h2_lo, wr_ref[:, :LANES]))
    for s, rs in enumerate(subs):
        route_ref[:, rs] = _route((logits[s] + br_ref[...]).T)


def _post(x, attn, u, bg, mod3, p, *, seq, mod_row0, tm=4 * POST_SUB):
    t = x.shape[0]
    tiles_per_seq = max(seq // tm, 1)
    halo = BF16_SUBLANES
    n_halo = t // halo
    row = lambda i: (mod_row0 + (i // tiles_per_seq if mod_row0 else 0), 0, 0)
    tile = lambda w: pl.BlockSpec((tm, w), lambda i: (i, 0))
    full = lambda a: pl.BlockSpec(a.shape, lambda i: (0,) * a.ndim)
    small = [p["attn_out_g"], p["conv_out_g"], p["conv_w"], p["conv_b"], p["w_o"], p["norm2_g"],
             p["w_router"], p["b_router"]]
    return pl.pallas_call(
        functools.partial(_post_kernel, tm=tm, seq=seq),
        grid=(t // tm,),
        in_specs=[tile(D_MODEL), tile(ATTN_WIDTH), tile(CONV_WIDTH),
                  pl.BlockSpec((halo, CONV_WIDTH),
                               lambda i: (jnp.maximum(i * (tm // halo) - 1, 0), 0)),
                  pl.BlockSpec((halo, CONV_WIDTH),
                               lambda i: (jnp.minimum((i + 1) * (tm // halo), n_halo - 1), 0)),
                  tile(CONV_WIDTH),
                  pl.BlockSpec((None, 1, 6 * D_MODEL), row)] + [full(a) for a in small],
        out_specs=[tile(D_MODEL), pl.BlockSpec((tm * F32_SUBLANES, LANES), lambda i: (i, 0)),
                   pl.BlockSpec((F32_SUBLANES, tm), lambda i: (0, i))],
        out_shape=[jax.ShapeDtypeStruct((t, D_MODEL), f32),
                   jax.ShapeDtypeStruct((t * F32_SUBLANES, LANES), f32),
                   jax.ShapeDtypeStruct((F32_SUBLANES, t), f32)],
        compiler_params=_params("arbitrary"),
        name="post",
    )(x, attn, u, u, u, bg, mod3, *small)


def _max_chunks(blk):
    n = -(-TOP_K * blk // MOE_CHUNK) + EXPERTS_PER_STEP
    return n + n % 2


def _dispatch_tables(route, blk):
    nblk = route.shape[1] // blk
    n_slots = TOP_K * blk
    flat = lambda a: a.reshape(TOP_K, nblk, blk).transpose(1, 0, 2).reshape(nblk, n_slots)
    experts = flat(route[0:TOP_K].astype(jnp.int32))
    weights = flat(route[TOP_K:2 * TOP_K])
    slot_ids = lax.broadcasted_iota(jnp.int32, experts.shape, 1)
    _, order, w_sorted = lax.sort((experts, slot_ids, weights), dimension=1, num_keys=1)
    bounds = jnp.arange(N_EXPERTS + 1, dtype=jnp.int32)
    offs = jnp.sum(experts[:, :, None] < bounds[None, None, :], axis=1, dtype=jnp.int32)
    tail = ((0, 0), (0, MOE_CHUNK))
    src_rows = jnp.pad((order & (blk - 1)) * F32_SUBLANES, tail)
    dst_rows = jnp.pad(order * F32_SUBLANES, tail, constant_values=n_slots * F32_SUBLANES)
    w_sorted = jnp.pad(w_sorted, tail)

    eg, n_steps, max_c = EXPERTS_PER_STEP, N_EXPERTS // EXPERTS_PER_STEP, _max_chunks(blk)
    lo = offs[:, :N_EXPERTS].reshape(nblk, n_steps, eg)
    n_ch = (offs[:, 1:] - offs[:, :N_EXPERTS] + MOE_CHUNK - 1) // MOE_CHUNK
    n_ch = n_ch.reshape(nblk, n_steps, eg)
    cum = jnp.cumsum(n_ch, axis=2)
    total = cum[..., -1]
    pos = jnp.arange(max_c, dtype=jnp.int32)
    j = jnp.sum(cum[:, :, None, :] <= pos[None, None, :, None], axis=-1, dtype=jnp.int32)
    j = jnp.minimum(j, eg - 1)
    hit = j[..., None] == jnp.arange(eg, dtype=jnp.int32)
    pick = lambda a: jnp.sum(jnp.where(hit, a[:, :, None, :], 0), axis=-1)
    base = pick(lo) + (pos - pick(cum - n_ch)) * MOE_CHUNK
    valid = pos < total[..., None]
    base = jnp.where(valid, base, n_slots).reshape(nblk, 1, n_steps * max_c)
    j = jnp.where(valid, j, 0).reshape(nblk, 1, n_steps * max_c)
    return (src_rows[:, None, :], dst_rows[:, None, :], w_sorted[:, None, :], j, base,
            total[:, None, :])


def _moe_kernel(src_ref, dst_ref, wts_ref, cj_ref, cbase_g_ref, cbase_s_ref, nchunks_ref,
                h_ref, wg_ref, wu_ref, wd_ref, mod_ref, fg_ref, x1_hbm, out_hbm,
                ytmp_ref, xg0_ref, og0_ref, xg1_ref, og1_ref, xbuf_ref, obuf_ref, sem_in, sem_out,
                *, blk):
    n_col = D_MODEL // LANES
    blk_id, step = pl.program_id(0), pl.program_id(1)
    last_step = step == pl.num_programs(1) - 1
    max_c = _max_chunks(blk)
    stage_refs = (xg0_ref, og0_ref, xg1_ref, og1_ref)
    n_sub = blk // FINAL_ROWS

    def x1_copy(s):
        rows = pl.ds(blk_id * blk + s * FINAL_ROWS, FINAL_ROWS)
        slot = s % FINAL_SLOTS
        return pltpu.make_async_copy(x1_hbm.at[rows, :], xbuf_ref.at[slot], sem_in.at[slot])

    def out_copy(s):
        rows = pl.ds(blk_id * blk + s * FINAL_ROWS, FINAL_ROWS)
        slot = s % FINAL_SLOTS
        return pltpu.make_async_copy(obuf_ref.at[slot], out_hbm.at[rows, :], sem_out.at[slot])

    @pl.when(last_step)
    def _():
        for s in range(min(FINAL_SLOTS, n_sub)):
            x1_copy(s).start()

    def gather(base, xg_ref):
        for r in range(MOE_CHUNK):
            src = pl.multiple_of(src_ref[0, base + r], F32_SUBLANES)
            xg_ref[pl.ds(r, F32_SUBLANES, stride=CHUNK_PITCH), :] = h_ref[pl.ds(src, F32_SUBLANES), :]

    def experts_mlp(j, xg_ref, og_ref):
        x = jnp.concatenate([xg_ref[c * CHUNK_PITCH:c * CHUNK_PITCH + MOE_CHUNK, :]
                             for c in range(n_col)], axis=1).astype(bf16)
        g = _dot(x, wg_ref[j])
        up = _dot(x, wu_ref[j])
        act = (g * jax.nn.sigmoid(g)) * up
        out = _dot(act.astype(bf16), wd_ref[j])
        for c in range(n_col):
            og_ref[c * CHUNK_PITCH:c * CHUNK_PITCH + MOE_CHUNK, :] = out[:, c * LANES:(c + 1) * LANES]

    def scatter(base, og_ref):
        for r in range(MOE_CHUNK):
            dst = pl.multiple_of(dst_ref[0, base + r], F32_SUBLANES)
            ytmp_ref[pl.ds(dst, F32_SUBLANES), :] = (
                og_ref[pl.ds(r, F32_SUBLANES, stride=CHUNK_PITCH), :] * wts_ref[0, base + r])

    def pair(pi, carry):
        chunks = [(step * max_c + 2 * pi + half, stage_refs[2 * half], stage_refs[2 * half + 1])
                  for half in range(2)]
        for p, xg_ref, _ in chunks:
            gather(cbase_g_ref[0, p], xg_ref)
        for p, xg_ref, og_ref in chunks:
            experts_mlp(cj_ref[0, p], xg_ref, og_ref)
        for p, _, og_ref in chunks:
            scatter(cbase_s_ref[0, p], og_ref)
        return carry

    n_chunks = nchunks_ref[0, step]
    lax.fori_loop(0, n_chunks // 2, pair, 0)

    @pl.when(n_chunks % 2 == 1)
    def _():
        p = step * max_c + n_chunks - 1
        gather(cbase_g_ref[0, p], xg0_ref)
        experts_mlp(cj_ref[0, p], xg0_ref, og0_ref)
        scatter(cbase_s_ref[0, p], og0_ref)

    @pl.when(last_step)
    def _():
        n = blk * F32_SUBLANES
        for k in range(1, TOP_K):
            ytmp_ref[0:n, :] = ytmp_ref[0:n, :] + ytmp_ref[k * n:(k + 1) * n, :]
        gate2 = mod_ref[:, G2:G2 + D_MODEL]
        for s in range(n_sub):
            slot = s % FINAL_SLOTS
            x1_copy(s).wait()
            if s >= FINAL_SLOTS:
                out_copy(s - FINAL_SLOTS).wait()
            first = s * FINAL_ROWS * F32_SUBLANES
            y = jnp.concatenate([ytmp_ref[pl.ds(first + c, FINAL_ROWS, stride=F32_SUBLANES), :]
                                 for c in range(n_col)], axis=1)
            x2 = xbuf_ref[slot] + gate2 * y
            obuf_ref[slot] = _rms(x2) * fg_ref[...]
            out_copy(s).start()
            if s + FINAL_SLOTS < n_sub:
                x1_copy(s + FINAL_SLOTS).start()
        for s in range(max(n_sub - FINAL_SLOTS, 0), n_sub):
            out_copy(s).wait()


def _moe(h2_tiles, tables, x1, mod3, final_g, w_gate, w_up, w_down, *, seq, mod_row0,
         blk=MOE_BLOCK):
    t = h2_tiles.shape[0] // F32_SUBLANES
    src_rows, dst_rows, w_sorted, cj, cbase, n_pairs = tables
    n_slots = TOP_K * blk
    eg, n_steps = EXPERTS_PER_STEP, N_EXPERTS // EXPERTS_PER_STEP
    blocks_per_seq = max(seq // blk, 1)
    row = lambda b, g: (mod_row0 + (b // blocks_per_seq if mod_row0 else 0), 0, 0)
    smem = lambda n: pl.BlockSpec((None, 1, n), lambda b, g: (b, 0, 0), memory_space=pltpu.SMEM)
    stage = pltpu.VMEM((D_MODEL // LANES * CHUNK_PITCH, LANES), f32)
    final_buf = pltpu.VMEM((FINAL_SLOTS, FINAL_ROWS, D_MODEL), f32)
    return pl.pallas_call(
        functools.partial(_moe_kernel, blk=blk),
        grid=(t // blk, n_steps),
        in_specs=[smem(n_slots + MOE_CHUNK)] * 3 + [smem(n_steps * _max_chunks(blk))] * 3 + [
                  smem(n_steps),
                  pl.BlockSpec((blk * F32_SUBLANES, LANES), lambda b, g: (b, 0)),
                  pl.BlockSpec((eg, D_MODEL, D_EXPERT), lambda b, g: (g, 0, 0)),
                  pl.BlockSpec((eg, D_MODEL, D_EXPERT), lambda b, g: (g, 0, 0)),
                  pl.BlockSpec((eg, D_EXPERT, D_MODEL), lambda b, g: (g, 0, 0)),
                  pl.BlockSpec((None, 1, 6 * D_MODEL), row),
                  pl.BlockSpec((1, D_MODEL), lambda b, g: (0, 0)),
                  pl.BlockSpec(memory_space=pl.ANY)],
        out_specs=pl.BlockSpec(memory_space=pl.ANY),
        out_shape=jax.ShapeDtypeStruct((t, D_MODEL), f32),
        scratch_shapes=[pltpu.VMEM(((n_slots + 1) * F32_SUBLANES, LANES), f32)] + [stage] * 4 + [
                        final_buf, final_buf,
                        pltpu.SemaphoreType.DMA((FINAL_SLOTS,)),
                        pltpu.SemaphoreType.DMA((FINAL_SLOTS,))],
        compiler_params=_params("arbitrary", "arbitrary", vmem=MOE_VMEM_LIMIT_BYTES),
        name="moe",
    )(src_rows, dst_rows, w_sorted, cj, cbase, cbase, n_pairs, h2_tiles, w_gate, w_up, w_down,
      mod3, final_g, x1)


def _mixers(x, mod3, p, *, seq, mod_row0, ctx_kv, cast=()):
    latent = ctx_kv is not None
    q, k, v, u, bg = _inproj(x, mod3, p["norm1_g"], p["w_in"], seq=seq, mod_row0=mod_row0,
                             kv_dtype=bf16 if latent else f32,
                             rope_tabs=_rope_tables(seq) if latent else None)
    if latent:
        attn, *cast = _latent_attention(q, k, v, ctx_kv[0], ctx_kv[1], p["sink"], seq=seq,
                                        cast=cast)
    else:
        attn = _context_attention(q, k, v, p["sink"], seq=seq)
    x1, h2_tiles, route = _post(x, attn, u, bg, mod3, p, seq=seq, mod_row0=mod_row0)
    return x1, h2_tiles, route, k, v, cast


def _experts(x1, h2_tiles, tables, mod3, p, *, seq, mod_row0):
    out = _moe(h2_tiles, tables, x1, mod3, p["final_g"], p["w_gate"], p["w_up"], p["w_down"],
               seq=seq, mod_row0=mod_row0)
    return out.reshape(-1, seq, D_MODEL)


def kernel(x_prompt, x_sample, cache_k, cache_v, c, c_ctx, w_mod, b_mod, norm1_g, w_in, conv_w,
           conv_b, sink, attn_out_g, conv_out_g, w_o, norm2_g, w_coarse, b_coarse, w_fine, b_fine,
           w_gate, w_up, w_down, final_g):
    batch, seq, _ = x_prompt.shape
    dec_batch, dec_seq, _ = x_sample.shape
    past = cache_k.shape[2]
    assert w_mod.shape[0] == 1 and 1 + dec_batch <= MOD_ROWS

    cvecs = jnp.concatenate([c_ctx[None], c, jnp.zeros((MOD_ROWS - 1 - dec_batch, D_MODEL), f32)])
    mod3 = _modulation(cvecs, w_mod[0], b_mod[0]).reshape(MOD_ROWS, 1, 6 * D_MODEL)

    pad = jnp.zeros((D_MODEL, LANES - N_EXPERTS - N_GROUPS), f32)
    mix_rows = np.concatenate([ATTN_HEAD_PERM, np.arange(ATTN_WIDTH, ATTN_WIDTH + CONV_WIDTH)])
    p = {
        "norm1_g": norm1_g, "w_in": w_in[0].astype(bf16), "conv_w": conv_w[0], "conv_b": conv_b,
        "sink": sink[0], "attn_out_g": attn_out_g[:, ATTN_HEAD_PERM], "conv_out_g": conv_out_g,
        "w_o": w_o[0][mix_rows].astype(bf16), "norm2_g": norm2_g,
        "w_router": _split_bf16(jnp.concatenate([w_fine[0], w_coarse[0], pad], axis=1)),
        "b_router": jnp.concatenate([b_fine[0], b_coarse[0], pad[0]])[None],
        "final_g": final_g[None],
    }

    x1_p, h2_p, route_p, k_p, v_p, _ = _mixers(x_prompt.reshape(batch * seq, D_MODEL), mod3, p,
                                               seq=seq, mod_row0=0, ctx_kv=None)
    new_k = k_p.reshape(batch, 1, seq, N_KV_HEADS, HEAD_DIM)
    new_v = v_p.reshape(batch, 1, seq, N_KV_HEADS, HEAD_DIM)

    ctx_kv = (cache_k[:, 0], cache_v[:, 0])
    experts_f32 = (w_gate[0], w_up[0], w_down[0])
    ride = N_EXPERTS % (dec_batch * dec_seq // (LANES * LAT_ATTN_SUB_BLOCKS)) == 0
    x1_s, h2_s, route_s, _, _, experts_bf16 = _mixers(
        x_sample.reshape(dec_batch * dec_seq, D_MODEL), mod3, p, seq=dec_seq, mod_row0=1,
        ctx_kv=ctx_kv, cast=experts_f32 if ride else ())
    if not ride:
        experts_bf16 = [w.astype(bf16) for w in experts_f32]
    p = dict(p, w_gate=experts_bf16[0], w_up=experts_bf16[1], w_down=experts_bf16[2])

    n_blk_p = batch * seq // MOE_BLOCK
    tables = _dispatch_tables(jnp.concatenate([route_p, route_s], axis=1), MOE_BLOCK)
    y_prompt = _experts(x1_p, h2_p, [t[:n_blk_p] for t in tables], mod3, p, seq=seq, mod_row0=0)
    y_sample = _experts(x1_s, h2_s, [t[n_blk_p:] for t in tables], mod3, p, seq=dec_seq,
                        mod_row0=1)
    return y_prompt, y_sample, new_k, new_v
```

```python
import functools

import jax
import jax.numpy as jnp
import numpy as np
from jax import lax
from jax.experimental import pallas as pl
from jax.experimental.pallas import tpu as pltpu

D_MODEL = 1024
HEAD_DIM = 64
ATTN_WIDTH = 512
N_HEADS = 8
N_KV_HEADS = 2
Q_PER_KV = 4
KV_WIDTH = 128
CONV_WIDTH = 512
CONV_K = 3
WINDOW = 128
GRID_W = 64
ROPE_BASE = 10000.0
N_FREQ = 16
N_GROUPS = 4
EXPERTS_PER_GROUP = 8
N_EXPERTS = 32
TOP_K = 2
D_EXPERT = 256
IN_WIDTH = ATTN_WIDTH + 2 * KV_WIDTH + 3 * CONV_WIDTH
EPS = 1e-6
NEG = -1e30
SCALE = HEAD_DIM ** -0.5
LOG2_E = 1.4426950408889634

LANES = 128
F32_SUBLANES = 8
BF16_SUBLANES = 16
assert D_MODEL == F32_SUBLANES * LANES

INPROJ_ROWS = 512
INPROJ_SUB = 256
POST_SUB = 256
MOE_BLOCK = 2048
MOE_CHUNK = 160
CHUNK_PITCH = MOE_CHUNK + F32_SUBLANES
EXPERTS_PER_STEP = 4
FINAL_ROWS = 256
FINAL_SLOTS = 4
MOE_VMEM_LIMIT_BYTES = 60 * 1024 * 1024
VMEM_LIMIT_BYTES = 48 * 1024 * 1024

SH1, SC1, G1, SH2, SC2, G2 = (i * D_MODEL for i in range(6))
MOD_ROWS = 8

COARSE_LANE0 = N_EXPERTS

f32 = jnp.float32
bf16 = jnp.bfloat16


def _params(*semantics, vmem=VMEM_LIMIT_BYTES):
    return pltpu.CompilerParams(dimension_semantics=semantics, vmem_limit_bytes=vmem)


def _rms(x):
    return x * lax.rsqrt(jnp.mean(x * x, axis=-1, keepdims=True) + EPS)


def _dot(a, b):
    return jnp.dot(a, b, preferred_element_type=f32)


def _split_bf16(w):
    hi = w.astype(bf16)
    lo = (w - hi.astype(f32)).astype(bf16)
    return jnp.concatenate([hi, lo], axis=1)


def _mod_kernel(cv_ref, w_ref, b_ref, o_ref):
    a = cv_ref[...]
    a = a * jax.nn.sigmoid(a)
    o_ref[...] = jnp.dot(a, w_ref[...], precision=lax.Precision.HIGHEST,
                         preferred_element_type=f32) + b_ref[...]


def _modulation(cvecs, w_mod, b_mod):
    tn = 2 * D_MODEL
    return pl.pallas_call(
        _mod_kernel,
        grid=(6 * D_MODEL // tn,),
        in_specs=[pl.BlockSpec((MOD_ROWS, D_MODEL), lambda j: (0, 0)),
                  pl.BlockSpec((D_MODEL, tn), lambda j: (0, j)),
                  pl.BlockSpec((1, tn), lambda j: (0, j))],
        out_specs=pl.BlockSpec((MOD_ROWS, tn), lambda j: (0, j)),
        out_shape=jax.ShapeDtypeStruct((MOD_ROWS, 6 * D_MODEL), f32),
        compiler_params=_params("arbitrary"),
        name="mod",
    )(cvecs, w_mod, b_mod.reshape(1, -1))


def _swap_halves(x):
    lane = lax.broadcasted_iota(jnp.int32, x.shape, 1)
    up = pltpu.roll(x, LANES - N_FREQ, axis=1)
    dn = pltpu.roll(x, N_FREQ, axis=1)
    return jnp.where((lane % (2 * N_FREQ)) < N_FREQ, up, dn)


def _rope(x, cos, sin):
    parts = []
    for c in range(x.shape[1] // LANES):
        xc = x[:, c * LANES:(c + 1) * LANES]
        parts.append(xc * cos + _swap_halves(xc) * sin)
    return parts[0] if len(parts) == 1 else jnp.concatenate(parts, axis=1)


def _inproj_kernel(*refs, rope, seq):
    if rope:
        x_ref, mod_ref, g_ref, w_ref, cos_ref, sin_ref, q_ref, k_ref, vt_ref, u_ref, bg_ref = refs
    else:
        x_ref, mod_ref, g_ref, w_ref, q_ref, k_ref, kt_ref, vt_ref, u_ref, bg_ref = refs
    tm = x_ref.shape[0]
    subs = [slice(s * INPROJ_SUB, (s + 1) * INPROJ_SUB) for s in range(tm // INPROJ_SUB)]
    zs = []
    for rs in subs:
        h = _rms(x_ref[rs, :]) * g_ref[...]
        h = h * (1.0 + mod_ref[:, SC1:SC1 + D_MODEL]) + mod_ref[:, SH1:SH1 + D_MODEL]
        zs.append(_dot(h.astype(bf16), w_ref[...]))
    for rs, z in zip(subs, zs):
        o = 0
        q = z[:, o:o + ATTN_WIDTH]; o += ATTN_WIDTH
        k = z[:, o:o + KV_WIDTH]; o += KV_WIDTH
        v = z[:, o:o + KV_WIDTH]; o += KV_WIDTH
        bg = z[:, o:o + CONV_WIDTH]; o += CONV_WIDTH
        cg = z[:, o:o + CONV_WIDTH]; o += CONV_WIDTH
        xin = z[:, o:o + CONV_WIDTH]
        if rope:
            q = _rope(q, cos_ref[rs, :], sin_ref[rs, :])
            k = _rope(k, cos_ref[rs, :], sin_ref[rs, :])
        q_ref[rs, :] = q.astype(q_ref.dtype)
        k_ref[rs, :] = k.astype(k_ref.dtype)
        if rope:
            vt_ref[:, rs] = v.T.astype(vt_ref.dtype)
        else:
            for s in range(rs.start // seq, rs.stop // seq):
                rows = slice(s * seq - rs.start, (s + 1) * seq - rs.start)
                kt_ref[s] = k[rows].T
                vt_ref[s] = v[rows].T
        u_ref[rs, :] = (cg * xin).astype(u_ref.dtype)
        bg_ref[rs, :] = bg.astype(bg_ref.dtype)


def _inproj(x, mod3, norm_g, w_in, *, seq, mod_row0, rope_tabs=None, tm=INPROJ_ROWS):
    t = x.shape[0]
    tiles_per_seq = seq // tm
    assert rope_tabs is not None or INPROJ_SUB % seq == 0
    row = lambda i: (mod_row0 + (i // tiles_per_seq if mod_row0 else 0), 0, 0)
    in_specs = [pl.BlockSpec((tm, D_MODEL), lambda i: (i, 0)),
                pl.BlockSpec((None, 1, 6 * D_MODEL), row),
                pl.BlockSpec((1, D_MODEL), lambda i: (0, 0)),
                pl.BlockSpec((D_MODEL, IN_WIDTH), lambda i: (0, 0))]
    args = [x, mod3, norm_g, w_in]
    if rope_tabs is not None:
        in_specs += [pl.BlockSpec((tm, LANES), lambda i: (i % tiles_per_seq, 0))] * 2
        args += list(rope_tabs)
    rows = lambda w: (pl.BlockSpec((tm, w), lambda i: (i, 0)), jax.ShapeDtypeStruct((t, w), bf16))
    if rope_tabs is not None:
        transposed = [(pl.BlockSpec((None, KV_WIDTH, tm),
                                    lambda i: (i // tiles_per_seq, 0, i % tiles_per_seq)),
                       jax.ShapeDtypeStruct((t // seq, KV_WIDTH, seq), bf16))]
    else:
        transposed = [(pl.BlockSpec((tm // seq, KV_WIDTH, seq), lambda i: (i, 0, 0)),
                       jax.ShapeDtypeStruct((t // seq, KV_WIDTH, seq), f32))] * 2
    outs = [rows(ATTN_WIDTH), rows(KV_WIDTH)] + transposed + [rows(CONV_WIDTH), rows(CONV_WIDTH)]
    out_specs, out_shape = (list(z) for z in zip(*outs))
    return pl.pallas_call(
        functools.partial(_inproj_kernel, rope=rope_tabs is not None, seq=seq),
        grid=(t // tm,),
        in_specs=in_specs,
        out_specs=out_specs,
        out_shape=out_shape,
        compiler_params=_params("arbitrary"),
        name="inproj_rope" if rope_tabs is not None else "inproj",
    )(*args)


def _rope_tables(n):
    t = np.arange(n)
    inv = ROPE_BASE ** (-np.arange(N_FREQ, dtype=np.float32) / N_FREQ)
    rows = (t // GRID_W).astype(np.float32)
    cols = (t % GRID_W).astype(np.float32)
    d = np.arange(LANES) % HEAD_DIM
    pos = np.where(d[None, :] < HEAD_DIM // 2, rows[:, None], cols[:, None])
    ang = jnp.asarray(pos.astype(np.float32) * inv[d % N_FREQ][None, :])
    sign = np.where((d % (2 * N_FREQ)) < N_FREQ, -1.0, 1.0).astype(np.float32)
    return jnp.cos(ang), jnp.sin(ang) * sign[None, :]


def _attention_core(sink_ref, q_ref, o_ref, subs, shared, bq):
    n_col = Q_PER_KV * bq
    col_head = lax.broadcasted_iota(jnp.int32, (1, n_col), 1) // bq
    sink_rows = []
    for g in range(N_KV_HEADS):
        sink_row = jnp.full((1, n_col), sink_ref[g * Q_PER_KV], f32)
        for j in range(1, Q_PER_KV):
            sink_row = jnp.where(col_head == j, sink_ref[g * Q_PER_KV + j], sink_row)
        sink_rows.append(sink_row * LOG2_E)
    ones_rows = jnp.ones((BF16_SUBLANES, 1), bf16)

    def with_ones(vals_t, g):
        return jnp.concatenate(
            [vals_t[g * HEAD_DIM:(g + 1) * HEAD_DIM, :],
             jnp.broadcast_to(ones_rows, (BF16_SUBLANES, vals_t.shape[1]))], axis=0)

    shared_vals = None if shared is None else [with_ones(shared[1], g) for g in range(N_KV_HEADS)]
    qts = [(q_ref[i * bq:(i + 1) * bq, :].astype(f32) * (SCALE * LOG2_E)).T.astype(bf16)
           for i in range(len(subs))]

    def scores(i, g):
        lo, hi = g * HEAD_DIM, (g + 1) * HEAD_DIM
        heads = range(g * Q_PER_KV, (g + 1) * Q_PER_KV)
        qt_g = jnp.concatenate([qts[i][h * HEAD_DIM:(h + 1) * HEAD_DIM, :] for h in heads], axis=1)
        s_own = _dot(subs[i]["keys"][:, lo:hi], qt_g)
        if subs[i]["mask"] is not None:
            s_own = jnp.where(subs[i]["mask"], s_own, NEG)
        s_shared = None if shared is None else _dot(shared[0][:, lo:hi], qt_g)
        return s_own, s_shared

    def weighted_values(i, g, s_own, s_shared):
        m = jnp.maximum(jnp.max(s_own, axis=0, keepdims=True), sink_rows[g])
        if s_shared is not None:
            m = jnp.maximum(m, jnp.max(s_shared, axis=0, keepdims=True))
        o_aug = _dot(with_ones(subs[i]["vals_t"], g), jnp.exp2(s_own - m).astype(bf16))
        if s_shared is not None:
            o_aug = o_aug + _dot(shared_vals[g], jnp.exp2(s_shared - m).astype(bf16))
        den = o_aug[HEAD_DIM:HEAD_DIM + 1, :] + jnp.exp2(sink_rows[g] - m)
        return o_aug[:HEAD_DIM, :] / den

    passes = [(i, g) for i in range(len(subs)) for g in range(N_KV_HEADS)]
    out_t = {}
    pending = scores(*passes[0])
    for n, (i, g) in enumerate(passes):
        nxt = scores(*passes[n + 1]) if n + 1 < len(passes) else None
        out_t[i, g] = weighted_values(i, g, *pending)
        pending = nxt
    for i in range(len(subs)):
        blocks = [jnp.concatenate([out_t[i, g][:, j * bq:(j + 1) * bq]
                                   for g in range(N_KV_HEADS)], axis=0).T
                  for j in range(Q_PER_KV)]
        o_ref[i * bq:(i + 1) * bq, :] = jnp.concatenate(blocks, axis=1).astype(o_ref.dtype)


ATTN_HEAD_PERM = np.array([(Q_PER_KV * g + j) * HEAD_DIM + d for j in range(Q_PER_KV)
                           for g in range(N_KV_HEADS) for d in range(HEAD_DIM)])


def _ctx_attn_kernel(sink_ref, q_ref, k_ref, vt_ref, o_ref, *, bq, seq):
    subs = []
    for s in range(q_ref.shape[0] // seq):
        keys = k_ref[s * seq:(s + 1) * seq, :]
        subs += [dict(keys=keys, vals_t=vt_ref[s].astype(bf16), mask=None)] * (seq // bq)
    _attention_core(sink_ref, q_ref, o_ref, subs, None, bq)


def _context_attention(q, k, vt, sink, *, seq, bq=128, seqs_per_step=4):
    t = q.shape[0]
    rows = seq * seqs_per_step
    return pl.pallas_call(
        functools.partial(_ctx_attn_kernel, bq=bq, seq=seq),
        grid=(t // rows,),
        in_specs=[pl.BlockSpec(memory_space=pltpu.SMEM),
                  pl.BlockSpec((rows, ATTN_WIDTH), lambda b: (b, 0)),
                  pl.BlockSpec((rows, KV_WIDTH), lambda b: (b, 0)),
                  pl.BlockSpec((seqs_per_step, KV_WIDTH, seq), lambda b: (b, 0, 0))],
        out_specs=pl.BlockSpec((rows, ATTN_WIDTH), lambda b: (b, 0)),
        out_shape=jax.ShapeDtypeStruct((t, ATTN_WIDTH), bf16),
        compiler_params=_params("arbitrary"),
        name="ctx_attn",
    )(sink, q, k, vt)


def _lat_attn_kernel(sink_ref, q_ref, k_ref, vt_ref, ck_ref, cv_ref, *rest, bq, seq, n_cast):
    o_ref = rest[n_cast]
    for src_ref, dst_ref in zip(rest[:n_cast], rest[n_cast + 1:]):
        dst_ref[...] = src_ref[...].astype(dst_ref.dtype)
    band = bq + 2 * WINDOW
    n_col = Q_PER_KV * bq
    delta = ((lax.broadcasted_iota(jnp.int32, (band, n_col), 1) & (bq - 1))
             - lax.broadcasted_iota(jnp.int32, (band, n_col), 0))
    n_sub = q_ref.shape[0] // bq
    subs = []
    for sb in range(n_sub):
        i = pl.program_id(1) * n_sub + sb
        start = pl.multiple_of(jnp.clip(i * bq - WINDOW, 0, seq - band), LANES)
        shifted = delta + (i * bq - start + WINDOW)
        subs.append(dict(keys=k_ref[pl.ds(start, band), :],
                         vals_t=vt_ref[:, pl.ds(start, band)],
                         mask=shifted.astype(jnp.uint32) <= 2 * WINDOW))
    shared = (ck_ref[...].astype(bf16), cv_ref[...].T.astype(bf16))
    _attention_core(sink_ref, q_ref, o_ref, subs, shared, bq)


LAT_ATTN_SUB_BLOCKS = 8


def _latent_attention(q, k, vt, ck, cv, sink, *, seq, cast=(), bq=128,
                      sub_blocks=LAT_ATTN_SUB_BLOCKS):
    nb, past = ck.shape[0], ck.shape[1]
    assert bq == LANES and N_KV_HEADS * HEAD_DIM == LANES
    q3, k3 = (a.reshape(nb, seq, a.shape[-1]) for a in (q, k))
    tq = bq * sub_blocks
    steps_per_seq = seq // tq
    n_steps = nb * steps_per_seq
    cast3 = [a.reshape((n_steps, -1) + a.shape[2:]) for a in cast]
    cast_specs = [pl.BlockSpec((None,) + a.shape[1:], lambda b, i: (b * steps_per_seq + i, 0, 0))
                  for a in cast3]
    out, *cast_out = pl.pallas_call(
        functools.partial(_lat_attn_kernel, bq=bq, seq=seq, n_cast=len(cast)),
        grid=(nb, steps_per_seq),
        in_specs=[pl.BlockSpec(memory_space=pltpu.SMEM),
                  pl.BlockSpec((None, tq, ATTN_WIDTH), lambda b, i: (b, i, 0)),
                  pl.BlockSpec((None, seq, KV_WIDTH), lambda b, i: (b, 0, 0)),
                  pl.BlockSpec((None, KV_WIDTH, seq), lambda b, i: (b, 0, 0)),
                  pl.BlockSpec((None, past, KV_WIDTH), lambda b, i: (b, 0, 0)),
                  pl.BlockSpec((None, past, KV_WIDTH), lambda b, i: (b, 0, 0))] + cast_specs,
        out_specs=[pl.BlockSpec((None, tq, ATTN_WIDTH), lambda b, i: (b, i, 0))] + cast_specs,
        out_shape=[jax.ShapeDtypeStruct((nb, seq, ATTN_WIDTH), bf16)]
        + [jax.ShapeDtypeStruct(a.shape, bf16) for a in cast3],
        compiler_params=_params("arbitrary", "arbitrary"),
        name="lat_attn",
    )(sink, q3, k3, vt, ck, cv, *cast3)
    return [out.reshape(nb * seq, ATTN_WIDTH)] + [o.reshape(a.shape) for o, a in zip(cast_out, cast)]


def _route(logits_t):
    r = logits_t.shape[1]
    big = jnp.float32(LANES)
    crow = lax.broadcasted_iota(jnp.int32, (F32_SUBLANES, r), 0)
    crow_f = crow.astype(f32)
    lc = jnp.where(crow < N_GROUPS, logits_t[COARSE_LANE0:COARSE_LANE0 + F32_SUBLANES, :], -jnp.inf)
    mc = jnp.max(lc, axis=0, keepdims=True)
    grp = jnp.min(jnp.where(lc == mc, crow_f, big), axis=0, keepdims=True)
    pg = 1.0 / jnp.sum(jnp.exp(lc - mc), axis=0, keepdims=True)
    erow = lax.broadcasted_iota(jnp.int32, (N_EXPERTS, r), 0).astype(f32)
    in_g = jnp.floor(erow * (1.0 / EXPERTS_PER_GROUP)) == grp
    fl = jnp.where(in_g, logits_t[0:N_EXPERTS, :], -jnp.inf)
    m1 = jnp.max(fl, axis=0, keepdims=True)
    i1 = jnp.min(jnp.where(fl == m1, erow, big), axis=0, keepdims=True)
    fl2 = jnp.where(erow == i1, -jnp.inf, fl)
    m2 = jnp.max(fl2, axis=0, keepdims=True)
    i2 = jnp.min(jnp.where(fl2 == m2, erow, big), axis=0, keepdims=True)
    e2 = jnp.exp(m2 - m1)
    p1 = pg / (1.0 + e2)
    p2 = pg * e2 / (1.0 + e2)
    packed = jnp.where(crow == 0, i1, jnp.where(crow == 1, i2, jnp.where(crow == 2, p1, p2)))
    return jnp.where(crow < 4, packed, 0.0)


def _post_kernel(x_ref, attn_ref, u_ref, up_ref, un_ref, bg_ref, mod_ref, ag_ref, cg_ref, cw_ref,
                 cb_ref, wo_ref, n2_ref, wr_ref, br_ref, x1_ref, h2_ref, route_ref, *, tm, seq):
    i = pl.program_id(0)
    u = u_ref[...].astype(f32)
    rows = lax.broadcasted_iota(jnp.int32, (tm, 1), 0)
    spos = (i * tm + rows) % seq
    u_dn = jnp.where(rows == 0, up_ref[...].astype(f32)[BF16_SUBLANES - 1:, :],
                     pltpu.roll(u, 1, axis=0))
    u_dn = jnp.where(spos == 0, 0.0, u_dn)
    u_up = jnp.where(rows == tm - 1, un_ref[...].astype(f32)[0:1, :], pltpu.roll(u, tm - 1, axis=0))
    u_up = jnp.where(spos == seq - 1, 0.0, u_up)
    y = u_dn * cw_ref[0:1, :] + u * cw_ref[1:2, :] + u_up * cw_ref[2:3, :] + cb_ref[...]
    conv = bg_ref[...].astype(f32) * y
    attn_n = (_rms(attn_ref[...].astype(f32)) * ag_ref[...]).astype(bf16)
    conv_n = (_rms(conv) * cg_ref[...]).astype(bf16)
    subs = [slice(s * POST_SUB, (s + 1) * POST_SUB) for s in range(tm // POST_SUB)]
    mixed = [_dot(attn_n[rs], wo_ref[0:ATTN_WIDTH, :]) + _dot(conv_n[rs], wo_ref[ATTN_WIDTH:, :])
             for rs in subs]
    logits = []
    for s, rs in enumerate(subs):
        x1 = x_ref[rs, :] + mod_ref[:, G1:G1 + D_MODEL] * mixed[s]
        x1_ref[rs, :] = x1
        h2 = _rms(x1) * n2_ref[...]
        h2 = h2 * (1.0 + mod_ref[:, SC2:SC2 + D_MODEL]) + mod_ref[:, SH2:SH2 + D_MODEL]
        for c in range(D_MODEL // LANES):
            h2_ref[pl.ds(rs.start * F32_SUBLANES + c, POST_SUB, stride=F32_SUBLANES), :] = (
                h2[:, c * LANES:(c + 1) * LANES])
        h2_hi = h2.astype(bf16)
        h2_lo = (h2 - h2_hi.astype(f32)).astype(bf16)
        both = _dot(h2_hi, wr_ref[...])
        logits.append(both[:, :LANES] + both[:, LANES:] + _dot(h2_lo, wr_ref[:, :LANES]))
    for s, rs in enumerate(subs):
        route_ref[:, rs] = _route((logits[s] + br_ref[...]).T)


def _post(x, attn, u, bg, mod3, p, *, seq, mod_row0, tm=4 * POST_SUB):
    t = x.shape[0]
    tiles_per_seq = max(seq // tm, 1)
    halo = BF16_SUBLANES
    n_halo = t // halo
    row = lambda i: (mod_row0 + (i // tiles_per_seq if mod_row0 else 0), 0, 0)
    tile = lambda w: pl.BlockSpec((tm, w), lambda i: (i, 0))
    full = lambda a: pl.BlockSpec(a.shape, lambda i: (0,) * a.ndim)
    small = [p["attn_out_g"], p["conv_out_g"], p["conv_w"], p["conv_b"], p["w_o"], p["norm2_g"],
             p["w_router"], p["b_router"]]
    return pl.pallas_call(
        functools.partial(_post_kernel, tm=tm, seq=seq),
        grid=(t // tm,),
        in_specs=[tile(D_MODEL), tile(ATTN_WIDTH), tile(CONV_WIDTH),
                  pl.BlockSpec((halo, CONV_WIDTH),
                               lambda i: (jnp.maximum(i * (tm // halo) - 1, 0), 0)),
                  pl.BlockSpec((halo, CONV_WIDTH),
                               lambda i: (jnp.minimum((i + 1) * (tm // halo), n_halo - 1), 0)),
                  tile(CONV_WIDTH),
                  pl.BlockSpec((None, 1, 6 * D_MODEL), row)] + [full(a) for a in small],
        out_specs=[tile(D_MODEL), pl.BlockSpec((tm * F32_SUBLANES, LANES), lambda i: (i, 0)),
                   pl.BlockSpec((F32_SUBLANES, tm), lambda i: (0, i))],
        out_shape=[jax.ShapeDtypeStruct((t, D_MODEL), f32),
                   jax.ShapeDtypeStruct((t * F32_SUBLANES, LANES), f32),
                   jax.ShapeDtypeStruct((F32_SUBLANES, t), f32)],
        compiler_params=_params("arbitrary"),
        name="post",
    )(x, attn, u, u, u, bg, mod3, *small)


def _max_chunks(blk):
    n = -(-TOP_K * blk // MOE_CHUNK) + EXPERTS_PER_STEP
    return n + n % 2


def _dispatch_tables(route, blk):
    nblk = route.shape[1] // blk
    n_slots = TOP_K * blk
    flat = lambda a: a.reshape(TOP_K, nblk, blk).transpose(1, 0, 2).reshape(nblk, n_slots)
    experts = flat(route[0:TOP_K].astype(jnp.int32))
    weights = flat(route[TOP_K:2 * TOP_K])
    slot_ids = lax.broadcasted_iota(jnp.int32, experts.shape, 1)
    _, order, w_sorted = lax.sort((experts, slot_ids, weights), dimension=1, num_keys=1)
    bounds = jnp.arange(N_EXPERTS + 1, dtype=jnp.int32)
    offs = jnp.sum(experts[:, :, None] < bounds[None, None, :], axis=1, dtype=jnp.int32)
    tail = ((0, 0), (0, MOE_CHUNK))
    src_rows = jnp.pad((order & (blk - 1)) * F32_SUBLANES, tail)
    dst_rows = jnp.pad(order * F32_SUBLANES, tail, constant_values=n_slots * F32_SUBLANES)
    w_sorted = jnp.pad(w_sorted, tail)

    eg, n_steps, max_c = EXPERTS_PER_STEP, N_EXPERTS // EXPERTS_PER_STEP, _max_chunks(blk)
    lo = offs[:, :N_EXPERTS].reshape(nblk, n_steps, eg)
    n_ch = (offs[:, 1:] - offs[:, :N_EXPERTS] + MOE_CHUNK - 1) // MOE_CHUNK
    n_ch = n_ch.reshape(nblk, n_steps, eg)
    cum = jnp.cumsum(n_ch, axis=2)
    total = cum[..., -1]
    pos = jnp.arange(max_c, dtype=jnp.int32)
    j = jnp.sum(cum[:, :, None, :] <= pos[None, None, :, None], axis=-1, dtype=jnp.int32)
    j = jnp.minimum(j, eg - 1)
    hit = j[..., None] == jnp.arange(eg, dtype=jnp.int32)
    pick = lambda a: jnp.sum(jnp.where(hit, a[:, :, None, :], 0), axis=-1)
    base = pick(lo) + (pos - pick(cum - n_ch)) * MOE_CHUNK
    valid = pos < total[..., None]
    base = jnp.where(valid, base, n_slots).reshape(nblk, 1, n_steps * max_c)
    j = jnp.where(valid, j, 0).reshape(nblk, 1, n_steps * max_c)
    return (src_rows[:, None, :], dst_rows[:, None, :], w_sorted[:, None, :], j, base,
            total[:, None, :])


def _moe_kernel(src_ref, dst_ref, wts_ref, cj_ref, cbase_g_ref, cbase_s_ref, nchunks_ref,
                h_ref, wg_ref, wu_ref, wd_ref, mod_ref, fg_ref, x1_hbm, out_hbm,
                ytmp_ref, xg0_ref, og0_ref, xg1_ref, og1_ref, xbuf_ref, obuf_ref, sem_in, sem_out,
                *, blk):
    n_col = D_MODEL // LANES
    blk_id, step = pl.program_id(0), pl.program_id(1)
    last_step = step == pl.num_programs(1) - 1
    max_c = _max_chunks(blk)
    stage_refs = (xg0_ref, og0_ref, xg1_ref, og1_ref)
    n_sub = blk // FINAL_ROWS

    def x1_copy(s):
        rows = pl.ds(blk_id * blk + s * FINAL_ROWS, FINAL_ROWS)
        slot = s % FINAL_SLOTS
        return pltpu.make_async_copy(x1_hbm.at[rows, :], xbuf_ref.at[slot], sem_in.at[slot])

    def out_copy(s):
        rows = pl.ds(blk_id * blk + s * FINAL_ROWS, FINAL_ROWS)
        slot = s % FINAL_SLOTS
        return pltpu.make_async_copy(obuf_ref.at[slot], out_hbm.at[rows, :], sem_out.at[slot])

    @pl.when(last_step)
    def _():
        for s in range(min(FINAL_SLOTS, n_sub)):
            x1_copy(s).start()

    def gather(base, xg_ref):
        for r in range(MOE_CHUNK):
            src = pl.multiple_of(src_ref[0, base + r], F32_SUBLANES)
            xg_ref[pl.ds(r, F32_SUBLANES, stride=CHUNK_PITCH), :] = h_ref[pl.ds(src, F32_SUBLANES), :]

    def experts_mlp(j, xg_ref, og_ref):
        x = jnp.concatenate([xg_ref[c * CHUNK_PITCH:c * CHUNK_PITCH + MOE_CHUNK, :]
                             for c in range(n_col)], axis=1).astype(bf16)
        g = _dot(x, wg_ref[j])
        up = _dot(x, wu_ref[j])
        act = (g * jax.nn.sigmoid(g)) * up
        out = _dot(act.astype(bf16), wd_ref[j])
        for c in range(n_col):
            og_ref[c * CHUNK_PITCH:c * CHUNK_PITCH + MOE_CHUNK, :] = out[:, c * LANES:(c + 1) * LANES]

    def scatter(base, og_ref):
        for r in range(MOE_CHUNK):
            dst = pl.multiple_of(dst_ref[0, base + r], F32_SUBLANES)
            ytmp_ref[pl.ds(dst, F32_SUBLANES), :] = (
                og_ref[pl.ds(r, F32_SUBLANES, stride=CHUNK_PITCH), :] * wts_ref[0, base + r])

    def pair(pi, carry):
        chunks = [(step * max_c + 2 * pi + half, stage_refs[2 * half], stage_refs[2 * half + 1])
                  for half in range(2)]
        for p, xg_ref, _ in chunks:
            gather(cbase_g_ref[0, p], xg_ref)
        for p, xg_ref, og_ref in chunks:
            experts_mlp(cj_ref[0, p], xg_ref, og_ref)
        for p, _, og_ref in chunks:
            scatter(cbase_s_ref[0, p], og_ref)
        return carry

    n_chunks = nchunks_ref[0, step]
    lax.fori_loop(0, n_chunks // 2, pair, 0)

    @pl.when(n_chunks % 2 == 1)
    def _():
        p = step * max_c + n_chunks - 1
        gather(cbase_g_ref[0, p], xg0_ref)
        experts_mlp(cj_ref[0, p], xg0_ref, og0_ref)
        scatter(cbase_s_ref[0, p], og0_ref)

    @pl.when(last_step)
    def _():
        n = blk * F32_SUBLANES
        for k in range(1, TOP_K):
            ytmp_ref[0:n, :] = ytmp_ref[0:n, :] + ytmp_ref[k * n:(k + 1) * n, :]
        gate2 = mod_ref[:, G2:G2 + D_MODEL]
        for s in range(n_sub):
            slot = s % FINAL_SLOTS
            x1_copy(s).wait()
            if s >= FINAL_SLOTS:
                out_copy(s - FINAL_SLOTS).wait()
            first = s * FINAL_ROWS * F32_SUBLANES
            y = jnp.concatenate([ytmp_ref[pl.ds(first + c, FINAL_ROWS, stride=F32_SUBLANES), :]
                                 for c in range(n_col)], axis=1)
            x2 = xbuf_ref[slot] + gate2 * y
            obuf_ref[slot] = _rms(x2) * fg_ref[...]
            out_copy(s).start()
            if s + FINAL_SLOTS < n_sub:
                x1_copy(s + FINAL_SLOTS).start()
        for s in range(max(n_sub - FINAL_SLOTS, 0), n_sub):
            out_copy(s).wait()


def _moe(h2_tiles, tables, x1, mod3, final_g, w_gate, w_up, w_down, *, seq, mod_row0,
         blk=MOE_BLOCK):
    t = h2_tiles.shape[0] // F32_SUBLANES
    src_rows, dst_rows, w_sorted, cj, cbase, n_pairs = tables
    n_slots = TOP_K * blk
    eg, n_steps = EXPERTS_PER_STEP, N_EXPERTS // EXPERTS_PER_STEP
    blocks_per_seq = max(seq // blk, 1)
    row = lambda b, g: (mod_row0 + (b // blocks_per_seq if mod_row0 else 0), 0, 0)
    smem = lambda n: pl.BlockSpec((None, 1, n), lambda b, g: (b, 0, 0), memory_space=pltpu.SMEM)
    stage = pltpu.VMEM((D_MODEL // LANES * CHUNK_PITCH, LANES), f32)
    final_buf = pltpu.VMEM((FINAL_SLOTS, FINAL_ROWS, D_MODEL), f32)
    return pl.pallas_call(
        functools.partial(_moe_kernel, blk=blk),
        grid=(t // blk, n_steps),
        in_specs=[smem(n_slots + MOE_CHUNK)] * 3 + [smem(n_steps * _max_chunks(blk))] * 3 + [
                  smem(n_steps),
                  pl.BlockSpec((blk * F32_SUBLANES, LANES), lambda b, g: (b, 0)),
                  pl.BlockSpec((eg, D_MODEL, D_EXPERT), lambda b, g: (g, 0, 0)),
                  pl.BlockSpec((eg, D_MODEL, D_EXPERT), lambda b, g: (g, 0, 0)),
                  pl.BlockSpec((eg, D_EXPERT, D_MODEL), lambda b, g: (g, 0, 0)),
                  pl.BlockSpec((None, 1, 6 * D_MODEL), row),
                  pl.BlockSpec((1, D_MODEL), lambda b, g: (0, 0)),
                  pl.BlockSpec(memory_space=pl.ANY)],
        out_specs=pl.BlockSpec(memory_space=pl.ANY),
        out_shape=jax.ShapeDtypeStruct((t, D_MODEL), f32),
        scratch_shapes=[pltpu.VMEM(((n_slots + 1) * F32_SUBLANES, LANES), f32)] + [stage] * 4 + [
                        final_buf, final_buf,
                        pltpu.SemaphoreType.DMA((FINAL_SLOTS,)),
                        pltpu.SemaphoreType.DMA((FINAL_SLOTS,))],
        compiler_params=_params("arbitrary", "arbitrary", vmem=MOE_VMEM_LIMIT_BYTES),
        name="moe",
    )(src_rows, dst_rows, w_sorted, cj, cbase, cbase, n_pairs, h2_tiles, w_gate, w_up, w_down,
      mod3, final_g, x1)


def _mixers(x, mod3, p, *, seq, mod_row0, ctx_kv, cast=()):
    if ctx_kv is not None:
        q, k, vt, u, bg = _inproj(x, mod3, p["norm1_g"], p["w_in"], seq=seq, mod_row0=mod_row0,
                                  rope_tabs=_rope_tables(seq))
        attn, *cast = _latent_attention(q, k, vt, ctx_kv[0], ctx_kv[1], p["sink"], seq=seq,
                                        cast=cast)
        kv_t = None
    else:
        q, k, kt, vt, u, bg = _inproj(x, mod3, p["norm1_g"], p["w_in"], seq=seq,
                                      mod_row0=mod_row0)
        attn = _context_attention(q, k, vt, p["sink"], seq=seq)
        kv_t = (kt, vt)
    x1, h2_tiles, route = _post(x, attn, u, bg, mod3, p, seq=seq, mod_row0=mod_row0)
    return x1, h2_tiles, route, kv_t, cast


def _experts(x1, h2_tiles, tables, mod3, p, *, seq, mod_row0):
    out = _moe(h2_tiles, tables, x1, mod3, p["final_g"], p["w_gate"], p["w_up"], p["w_down"],
               seq=seq, mod_row0=mod_row0)
    return out.reshape(-1, seq, D_MODEL)


def kernel(x_prompt, x_sample, cache_k, cache_v, c, c_ctx, w_mod, b_mod, norm1_g, w_in, conv_w,
           conv_b, sink, attn_out_g, conv_out_g, w_o, norm2_g, w_coarse, b_coarse, w_fine, b_fine,
           w_gate, w_up, w_down, final_g):
    batch, seq, _ = x_prompt.shape
    dec_batch, dec_seq, _ = x_sample.shape
    past = cache_k.shape[2]
    assert w_mod.shape[0] == 1 and 1 + dec_batch <= MOD_ROWS

    cvecs = jnp.concatenate([c_ctx[None], c, jnp.zeros((MOD_ROWS - 1 - dec_batch, D_MODEL), f32)])
    mod3 = _modulation(cvecs, w_mod[0], b_mod[0]).reshape(MOD_ROWS, 1, 6 * D_MODEL)

    pad = jnp.zeros((D_MODEL, LANES - N_EXPERTS - N_GROUPS), f32)
    mix_rows = np.concatenate([ATTN_HEAD_PERM, np.arange(ATTN_WIDTH, ATTN_WIDTH + CONV_WIDTH)])
    p = {
        "norm1_g": norm1_g, "w_in": w_in[0].astype(bf16), "conv_w": conv_w[0], "conv_b": conv_b,
        "sink": sink[0], "attn_out_g": attn_out_g[:, ATTN_HEAD_PERM], "conv_out_g": conv_out_g,
        "w_o": w_o[0][mix_rows].astype(bf16), "norm2_g": norm2_g,
        "w_router": _split_bf16(jnp.concatenate([w_fine[0], w_coarse[0], pad], axis=1)),
        "b_router": jnp.concatenate([b_fine[0], b_coarse[0], pad[0]])[None],
        "final_g": final_g[None],
    }

    x1_p, h2_p, route_p, kv_t, _ = _mixers(x_prompt.reshape(batch * seq, D_MODEL), mod3, p,
                                           seq=seq, mod_row0=0, ctx_kv=None)
    new_k, new_v = (a.reshape(batch, 1, N_KV_HEADS, HEAD_DIM, seq).transpose(0, 1, 4, 2, 3)
                    for a in kv_t)

    ctx_kv = (cache_k[:, 0].reshape(dec_batch, past, KV_WIDTH),
              cache_v[:, 0].reshape(dec_batch, past, KV_WIDTH))
    experts_f32 = (w_gate[0], w_up[0], w_down[0])
    ride = N_EXPERTS % (dec_batch * dec_seq // (LANES * LAT_ATTN_SUB_BLOCKS)) == 0
    x1_s, h2_s, route_s, _, experts_bf16 = _mixers(
        x_sample.reshape(dec_batch * dec_seq, D_MODEL), mod3, p, seq=dec_seq, mod_row0=1,
        ctx_kv=ctx_kv, cast=experts_f32 if ride else ())
    if not ride:
        experts_bf16 = [w.astype(bf16) for w in experts_f32]
    p = dict(p, w_gate=experts_bf16[0], w_up=experts_bf16[1], w_down=experts_bf16[2])

    n_blk_p = batch * seq // MOE_BLOCK
    tables = _dispatch_tables(jnp.concatenate([route_p, route_s], axis=1), MOE_BLOCK)
    y_prompt = _experts(x1_p, h2_p, [t[:n_blk_p] for t in tables], mod3, p, seq=seq, mod_row0=0)
    y_sample = _experts(x1_s, h2_s, [t[n_blk_p:] for t in tables], mod3, p, seq=dec_seq,
                        mod_row0=1)
    return y_prompt, y_sample, new_k, new_v
```

```python
import functools

import jax
import jax.numpy as jnp
import numpy as np
from jax import lax
from jax.experimental import pallas as pl
from jax.experimental.pallas import tpu as pltpu

D_MODEL = 1024
HEAD_DIM = 64
ATTN_WIDTH = 512
N_HEADS = 8
N_KV_HEADS = 2
Q_PER_KV = 4
KV_WIDTH = 128
CONV_WIDTH = 512
CONV_K = 3
WINDOW = 128
GRID_W = 64
ROPE_BASE = 10000.0
N_FREQ = 16
N_GROUPS = 4
EXPERTS_PER_GROUP = 8
N_EXPERTS = 32
TOP_K = 2
D_EXPERT = 256
IN_WIDTH = ATTN_WIDTH + 2 * KV_WIDTH + 3 * CONV_WIDTH
EPS = 1e-6
NEG = -1e30
SCALE = HEAD_DIM ** -0.5
LOG2_E = 1.4426950408889634

LANES = 128
F32_SUBLANES = 8
BF16_SUBLANES = 16
assert D_MODEL == F32_SUBLANES * LANES

INPROJ_ROWS = 512
INPROJ_SUB = 256
POST_SUB = 256
MOE_BLOCK = 2048
MOE_CHUNK = 160
DISPATCH_ROWS = 128
EXPERTS_PER_STEP = 4
FINAL_ROWS = 256
FINAL_SLOTS = 4
MOE_VMEM_LIMIT_BYTES = 60 * 1024 * 1024
VMEM_LIMIT_BYTES = 48 * 1024 * 1024

SH1, SC1, G1, SH2, SC2, G2 = (i * D_MODEL for i in range(6))
MOD_ROWS = 8

COARSE_LANE0 = N_EXPERTS

f32 = jnp.float32
bf16 = jnp.bfloat16


def _params(*semantics, vmem=VMEM_LIMIT_BYTES):
    return pltpu.CompilerParams(dimension_semantics=semantics, vmem_limit_bytes=vmem)


def _rms(x):
    return x * lax.rsqrt(jnp.mean(x * x, axis=-1, keepdims=True) + EPS)


def _dot(a, b):
    return jnp.dot(a, b, preferred_element_type=f32)


def _split_bf16(w):
    hi = w.astype(bf16)
    lo = (w - hi.astype(f32)).astype(bf16)
    return jnp.concatenate([hi, lo], axis=1)


def _mod_kernel(cv_ref, w_ref, b_ref, o_ref):
    a = cv_ref[...]
    a = a * jax.nn.sigmoid(a)
    o_ref[...] = jnp.dot(a, w_ref[...], precision=lax.Precision.HIGHEST,
                         preferred_element_type=f32) + b_ref[...]


def _modulation(cvecs, w_mod, b_mod):
    tn = 2 * D_MODEL
    return pl.pallas_call(
        _mod_kernel,
        grid=(6 * D_MODEL // tn,),
        in_specs=[pl.BlockSpec((MOD_ROWS, D_MODEL), lambda j: (0, 0)),
                  pl.BlockSpec((D_MODEL, tn), lambda j: (0, j)),
                  pl.BlockSpec((1, tn), lambda j: (0, j))],
        out_specs=pl.BlockSpec((MOD_ROWS, tn), lambda j: (0, j)),
        out_shape=jax.ShapeDtypeStruct((MOD_ROWS, 6 * D_MODEL), f32),
        compiler_params=_params("arbitrary"),
        name="mod",
    )(cvecs, w_mod, b_mod.reshape(1, -1))


def _swap_halves(x):
    lane = lax.broadcasted_iota(jnp.int32, x.shape, 1)
    up = pltpu.roll(x, LANES - N_FREQ, axis=1)
    dn = pltpu.roll(x, N_FREQ, axis=1)
    return jnp.where((lane % (2 * N_FREQ)) < N_FREQ, up, dn)


def _rope(x, cos, sin):
    parts = []
    for c in range(x.shape[1] // LANES):
        xc = x[:, c * LANES:(c + 1) * LANES]
        parts.append(xc * cos + _swap_halves(xc) * sin)
    return parts[0] if len(parts) == 1 else jnp.concatenate(parts, axis=1)


def _inproj_kernel(*refs, rope, seq):
    if rope:
        x_ref, mod_ref, g_ref, w_ref, cos_ref, sin_ref, q_ref, k_ref, vt_ref, u_ref, bg_ref = refs
    else:
        x_ref, mod_ref, g_ref, w_ref, q_ref, k_ref, kt_ref, vt_ref, u_ref, bg_ref = refs
    tm = x_ref.shape[0]
    subs = [slice(s * INPROJ_SUB, (s + 1) * INPROJ_SUB) for s in range(tm // INPROJ_SUB)]
    zs = []
    for rs in subs:
        h = _rms(x_ref[rs, :]) * g_ref[...]
        h = h * (1.0 + mod_ref[:, SC1:SC1 + D_MODEL]) + mod_ref[:, SH1:SH1 + D_MODEL]
        zs.append(_dot(h.astype(bf16), w_ref[...]))
    for rs, z in zip(subs, zs):
        o = 0
        q = z[:, o:o + ATTN_WIDTH]; o += ATTN_WIDTH
        k = z[:, o:o + KV_WIDTH]; o += KV_WIDTH
        v = z[:, o:o + KV_WIDTH]; o += KV_WIDTH
        bg = z[:, o:o + CONV_WIDTH]; o += CONV_WIDTH
        cg = z[:, o:o + CONV_WIDTH]; o += CONV_WIDTH
        xin = z[:, o:o + CONV_WIDTH]
        if rope:
            q = _rope(q, cos_ref[rs, :], sin_ref[rs, :])
            k = _rope(k, cos_ref[rs, :], sin_ref[rs, :])
        q_ref[rs, :] = q.astype(q_ref.dtype)
        k_ref[rs, :] = k.astype(k_ref.dtype)
        if rope:
            vt_ref[:, rs] = v.T.astype(vt_ref.dtype)
        else:
            for s in range(rs.start // seq, rs.stop // seq):
                rows = slice(s * seq - rs.start, (s + 1) * seq - rs.start)
                kt_ref[s] = k[rows].T
                vt_ref[s] = v[rows].T
        u_ref[rs, :] = (cg * xin).astype(u_ref.dtype)
        bg_ref[rs, :] = bg.astype(bg_ref.dtype)


def _inproj(x, mod3, norm_g, w_in, *, seq, mod_row0, rope_tabs=None, tm=INPROJ_ROWS):
    t = x.shape[0]
    tiles_per_seq = seq // tm
    assert rope_tabs is not None or INPROJ_SUB % seq == 0
    row = lambda i: (mod_row0 + (i // tiles_per_seq if mod_row0 else 0), 0, 0)
    in_specs = [pl.BlockSpec((tm, D_MODEL), lambda i: (i, 0)),
                pl.BlockSpec((None, 1, 6 * D_MODEL), row),
                pl.BlockSpec((1, D_MODEL), lambda i: (0, 0)),
                pl.BlockSpec((D_MODEL, IN_WIDTH), lambda i: (0, 0))]
    args = [x, mod3, norm_g, w_in]
    if rope_tabs is not None:
        in_specs += [pl.BlockSpec((tm, LANES), lambda i: (i % tiles_per_seq, 0))] * 2
        args += list(rope_tabs)
    rows = lambda w: (pl.BlockSpec((tm, w), lambda i: (i, 0)), jax.ShapeDtypeStruct((t, w), bf16))
    if rope_tabs is not None:
        transposed = [(pl.BlockSpec((None, KV_WIDTH, tm),
                                    lambda i: (i // tiles_per_seq, 0, i % tiles_per_seq)),
                       jax.ShapeDtypeStruct((t // seq, KV_WIDTH, seq), bf16))]
    else:
        transposed = [(pl.BlockSpec((tm // seq, KV_WIDTH, seq), lambda i: (i, 0, 0)),
                       jax.ShapeDtypeStruct((t // seq, KV_WIDTH, seq), f32))] * 2
    outs = [rows(ATTN_WIDTH), rows(KV_WIDTH)] + transposed + [rows(CONV_WIDTH), rows(CONV_WIDTH)]
    out_specs, out_shape = (list(z) for z in zip(*outs))
    return pl.pallas_call(
        functools.partial(_inproj_kernel, rope=rope_tabs is not None, seq=seq),
        grid=(t // tm,),
        in_specs=in_specs,
        out_specs=out_specs,
        out_shape=out_shape,
        compiler_params=_params("arbitrary"),
        name="inproj_rope" if rope_tabs is not None else "inproj",
    )(*args)


def _rope_tables(n):
    t = np.arange(n)
    inv = ROPE_BASE ** (-np.arange(N_FREQ, dtype=np.float32) / N_FREQ)
    rows = (t // GRID_W).astype(np.float32)
    cols = (t % GRID_W).astype(np.float32)
    d = np.arange(LANES) % HEAD_DIM
    pos = np.where(d[None, :] < HEAD_DIM // 2, rows[:, None], cols[:, None])
    ang = jnp.asarray(pos.astype(np.float32) * inv[d % N_FREQ][None, :])
    sign = np.where((d % (2 * N_FREQ)) < N_FREQ, -1.0, 1.0).astype(np.float32)
    return jnp.cos(ang), jnp.sin(ang) * sign[None, :]


def _attention_core(sink_ref, q_ref, o_ref, subs, shared, bq):
    n_col = Q_PER_KV * bq
    col_head = lax.broadcasted_iota(jnp.int32, (1, n_col), 1) // bq
    sink_rows = []
    for g in range(N_KV_HEADS):
        sink_row = jnp.full((1, n_col), sink_ref[g * Q_PER_KV], f32)
        for j in range(1, Q_PER_KV):
            sink_row = jnp.where(col_head == j, sink_ref[g * Q_PER_KV + j], sink_row)
        sink_rows.append(sink_row * LOG2_E)
    ones_rows = jnp.ones((BF16_SUBLANES, 1), bf16)

    def with_ones(vals_t, g):
        return jnp.concatenate(
            [vals_t[g * HEAD_DIM:(g + 1) * HEAD_DIM, :],
             jnp.broadcast_to(ones_rows, (BF16_SUBLANES, vals_t.shape[1]))], axis=0)

    shared_vals = None if shared is None else [with_ones(shared[1], g) for g in range(N_KV_HEADS)]
    qts = [(q_ref[i * bq:(i + 1) * bq, :].astype(f32) * (SCALE * LOG2_E)).T.astype(bf16)
           for i in range(len(subs))]

    def scores(i, g):
        lo, hi = g * HEAD_DIM, (g + 1) * HEAD_DIM
        heads = range(g * Q_PER_KV, (g + 1) * Q_PER_KV)
        qt_g = jnp.concatenate([qts[i][h * HEAD_DIM:(h + 1) * HEAD_DIM, :] for h in heads], axis=1)
        s_own = _dot(subs[i]["keys"][:, lo:hi], qt_g)
        if subs[i]["mask"] is not None:
            s_own = jnp.where(subs[i]["mask"], s_own, NEG)
        s_shared = None if shared is None else _dot(shared[0][:, lo:hi], qt_g)
        return s_own, s_shared

    def weighted_values(i, g, s_own, s_shared):
        m = jnp.maximum(jnp.max(s_own, axis=0, keepdims=True), sink_rows[g])
        if s_shared is not None:
            m = jnp.maximum(m, jnp.max(s_shared, axis=0, keepdims=True))
        o_aug = _dot(with_ones(subs[i]["vals_t"], g), jnp.exp2(s_own - m).astype(bf16))
        if s_shared is not None:
            o_aug = o_aug + _dot(shared_vals[g], jnp.exp2(s_shared - m).astype(bf16))
        den = o_aug[HEAD_DIM:HEAD_DIM + 1, :] + jnp.exp2(sink_rows[g] - m)
        return o_aug[:HEAD_DIM, :] / den

    passes = [(i, g) for i in range(len(subs)) for g in range(N_KV_HEADS)]
    out_t = {}
    pending = scores(*passes[0])
    for n, (i, g) in enumerate(passes):
        nxt = scores(*passes[n + 1]) if n + 1 < len(passes) else None
        out_t[i, g] = weighted_values(i, g, *pending)
        pending = nxt
    for i in range(len(subs)):
        blocks = [jnp.concatenate([out_t[i, g][:, j * bq:(j + 1) * bq]
                                   for g in range(N_KV_HEADS)], axis=0).T
                  for j in range(Q_PER_KV)]
        o_ref[i * bq:(i + 1) * bq, :] = jnp.concatenate(blocks, axis=1).astype(o_ref.dtype)


ATTN_HEAD_PERM = np.array([(Q_PER_KV * g + j) * HEAD_DIM + d for j in range(Q_PER_KV)
                           for g in range(N_KV_HEADS) for d in range(HEAD_DIM)])


def _ctx_attn_kernel(sink_ref, q_ref, k_ref, vt_ref, o_ref, *, bq, seq):
    subs = []
    for s in range(q_ref.shape[0] // seq):
        keys = k_ref[s * seq:(s + 1) * seq, :]
        subs += [dict(keys=keys, vals_t=vt_ref[s].astype(bf16), mask=None)] * (seq // bq)
    _attention_core(sink_ref, q_ref, o_ref, subs, None, bq)


def _context_attention(q, k, vt, sink, *, seq, bq=128, seqs_per_step=4):
    t = q.shape[0]
    rows = seq * seqs_per_step
    return pl.pallas_call(
        functools.partial(_ctx_attn_kernel, bq=bq, seq=seq),
        grid=(t // rows,),
        in_specs=[pl.BlockSpec(memory_space=pltpu.SMEM),
                  pl.BlockSpec((rows, ATTN_WIDTH), lambda b: (b, 0)),
                  pl.BlockSpec((rows, KV_WIDTH), lambda b: (b, 0)),
                  pl.BlockSpec((seqs_per_step, KV_WIDTH, seq), lambda b: (b, 0, 0))],
        out_specs=pl.BlockSpec((rows, ATTN_WIDTH), lambda b: (b, 0)),
        out_shape=jax.ShapeDtypeStruct((t, ATTN_WIDTH), bf16),
        compiler_params=_params("arbitrary"),
        name="ctx_attn",
    )(sink, q, k, vt)


def _lat_attn_kernel(sink_ref, q_ref, k_ref, vt_ref, ck_ref, cv_ref, *rest, bq, seq, n_cast):
    o_ref = rest[n_cast]
    for src_ref, dst_ref in zip(rest[:n_cast], rest[n_cast + 1:]):
        dst_ref[...] = src_ref[...].astype(dst_ref.dtype)
    band = bq + 2 * WINDOW
    n_col = Q_PER_KV * bq
    delta = ((lax.broadcasted_iota(jnp.int32, (band, n_col), 1) & (bq - 1))
             - lax.broadcasted_iota(jnp.int32, (band, n_col), 0))
    n_sub = q_ref.shape[0] // bq
    subs = []
    for sb in range(n_sub):
        i = pl.program_id(1) * n_sub + sb
        start = pl.multiple_of(jnp.clip(i * bq - WINDOW, 0, seq - band), LANES)
        shifted = delta + (i * bq - start + WINDOW)
        subs.append(dict(keys=k_ref[pl.ds(start, band), :],
                         vals_t=vt_ref[:, pl.ds(start, band)],
                         mask=shifted.astype(jnp.uint32) <= 2 * WINDOW))
    shared = (ck_ref[...].astype(bf16), cv_ref[...].T.astype(bf16))
    _attention_core(sink_ref, q_ref, o_ref, subs, shared, bq)


LAT_ATTN_SUB_BLOCKS = 8


def _latent_attention(q, k, vt, ck, cv, sink, *, seq, cast=(), bq=128,
                      sub_blocks=LAT_ATTN_SUB_BLOCKS):
    nb, past = ck.shape[0], ck.shape[1]
    assert bq == LANES and N_KV_HEADS * HEAD_DIM == LANES
    q3, k3 = (a.reshape(nb, seq, a.shape[-1]) for a in (q, k))
    tq = bq * sub_blocks
    steps_per_seq = seq // tq
    n_steps = nb * steps_per_seq
    cast3 = [a.reshape((n_steps, -1) + a.shape[2:]) for a in cast]
    cast_specs = [pl.BlockSpec((None,) + a.shape[1:], lambda b, i: (b * steps_per_seq + i, 0, 0))
                  for a in cast3]
    out, *cast_out = pl.pallas_call(
        functools.partial(_lat_attn_kernel, bq=bq, seq=seq, n_cast=len(cast)),
        grid=(nb, steps_per_seq),
        in_specs=[pl.BlockSpec(memory_space=pltpu.SMEM),
                  pl.BlockSpec((None, tq, ATTN_WIDTH), lambda b, i: (b, i, 0)),
                  pl.BlockSpec((None, seq, KV_WIDTH), lambda b, i: (b, 0, 0)),
                  pl.BlockSpec((None, KV_WIDTH, seq), lambda b, i: (b, 0, 0)),
                  pl.BlockSpec((None, past, KV_WIDTH), lambda b, i: (b, 0, 0)),
                  pl.BlockSpec((None, past, KV_WIDTH), lambda b, i: (b, 0, 0))] + cast_specs,
        out_specs=[pl.BlockSpec((None, tq, ATTN_WIDTH), lambda b, i: (b, i, 0))] + cast_specs,
        out_shape=[jax.ShapeDtypeStruct((nb, seq, ATTN_WIDTH), bf16)]
        + [jax.ShapeDtypeStruct(a.shape, bf16) for a in cast3],
        compiler_params=_params("arbitrary", "arbitrary"),
        name="lat_attn",
    )(sink, q3, k3, vt, ck, cv, *cast3)
    return [out.reshape(nb * seq, ATTN_WIDTH)] + [o.reshape(a.shape) for o, a in zip(cast_out, cast)]


def _route(logits_t):
    r = logits_t.shape[1]
    big = jnp.float32(LANES)
    crow = lax.broadcasted_iota(jnp.int32, (F32_SUBLANES, r), 0)
    crow_f = crow.astype(f32)
    lc = jnp.where(crow < N_GROUPS, logits_t[COARSE_LANE0:COARSE_LANE0 + F32_SUBLANES, :], -jnp.inf)
    mc = jnp.max(lc, axis=0, keepdims=True)
    grp = jnp.min(jnp.where(lc == mc, crow_f, big), axis=0, keepdims=True)
    pg = 1.0 / jnp.sum(jnp.exp(lc - mc), axis=0, keepdims=True)
    erow = lax.broadcasted_iota(jnp.int32, (N_EXPERTS, r), 0).astype(f32)
    in_g = jnp.floor(erow * (1.0 / EXPERTS_PER_GROUP)) == grp
    fl = jnp.where(in_g, logits_t[0:N_EXPERTS, :], -jnp.inf)
    m1 = jnp.max(fl, axis=0, keepdims=True)
    i1 = jnp.min(jnp.where(fl == m1, erow, big), axis=0, keepdims=True)
    fl2 = jnp.where(erow == i1, -jnp.inf, fl)
    m2 = jnp.max(fl2, axis=0, keepdims=True)
    i2 = jnp.min(jnp.where(fl2 == m2, erow, big), axis=0, keepdims=True)
    e2 = jnp.exp(m2 - m1)
    p1 = pg / (1.0 + e2)
    p2 = pg * e2 / (1.0 + e2)
    packed = jnp.where(crow == 0, i1, jnp.where(crow == 1, i2, jnp.where(crow == 2, p1, p2)))
    return jnp.where(crow < 4, packed, 0.0)


def _post_kernel(x_ref, attn_ref, u_ref, up_ref, un_ref, bg_ref, mod_ref, ag_ref, cg_ref, cw_ref,
                 cb_ref, wo_ref, n2_ref, wr_ref, br_ref, x1_ref, h2_ref, route_ref, *, tm, seq):
    i = pl.program_id(0)
    u = u_ref[...].astype(f32)
    rows = lax.broadcasted_iota(jnp.int32, (tm, 1), 0)
    spos = (i * tm + rows) % seq
    u_dn = jnp.where(rows == 0, up_ref[...].astype(f32)[BF16_SUBLANES - 1:, :],
                     pltpu.roll(u, 1, axis=0))
    u_dn = jnp.where(spos == 0, 0.0, u_dn)
    u_up = jnp.where(rows == tm - 1, un_ref[...].astype(f32)[0:1, :], pltpu.roll(u, tm - 1, axis=0))
    u_up = jnp.where(spos == seq - 1, 0.0, u_up)
    y = u_dn * cw_ref[0:1, :] + u * cw_ref[1:2, :] + u_up * cw_ref[2:3, :] + cb_ref[...]
    conv = bg_ref[...].astype(f32) * y
    attn_n = (_rms(attn_ref[...].astype(f32)) * ag_ref[...]).astype(bf16)
    conv_n = (_rms(conv) * cg_ref[...]).astype(bf16)
    subs = [slice(s * POST_SUB, (s + 1) * POST_SUB) for s in range(tm // POST_SUB)]
    mixed = [_dot(attn_n[rs], wo_ref[0:ATTN_WIDTH, :]) + _dot(conv_n[rs], wo_ref[ATTN_WIDTH:, :])
             for rs in subs]
    logits = []
    for s, rs in enumerate(subs):
        x1 = x_ref[rs, :] + mod_ref[:, G1:G1 + D_MODEL] * mixed[s]
        x1_ref[rs, :] = x1
        h2 = _rms(x1) * n2_ref[...]
        h2 = h2 * (1.0 + mod_ref[:, SC2:SC2 + D_MODEL]) + mod_ref[:, SH2:SH2 + D_MODEL]
        for c in range(D_MODEL // LANES):
            h2_ref[pl.ds(rs.start * F32_SUBLANES + c, POST_SUB, stride=F32_SUBLANES), :] = (
                h2[:, c * LANES:(c + 1) * LANES])
        h2_hi = h2.astype(bf16)
        h2_lo = (h2 - h2_hi.astype(f32)).astype(bf16)
        both = _dot(h2_hi, wr_ref[...])
        logits.append(both[:, :LANES] + both[:, LANES:] + _dot(h2_lo, wr_ref[:, :LANES]))
    for s, rs in enumerate(subs):
        route_ref[:, rs] = _route((logits[s] + br_ref[...]).T)


def _post(x, attn, u, bg, mod3, p, *, seq, mod_row0, tm=4 * POST_SUB):
    t = x.shape[0]
    tiles_per_seq = max(seq // tm, 1)
    halo = BF16_SUBLANES
    n_halo = t // halo
    row = lambda i: (mod_row0 + (i // tiles_per_seq if mod_row0 else 0), 0, 0)
    tile = lambda w: pl.BlockSpec((tm, w), lambda i: (i, 0))
    full = lambda a: pl.BlockSpec(a.shape, lambda i: (0,) * a.ndim)
    small = [p["attn_out_g"], p["conv_out_g"], p["conv_w"], p["conv_b"], p["w_o"], p["norm2_g"],
             p["w_router"], p["b_router"]]
    return pl.pallas_call(
        functools.partial(_post_kernel, tm=tm, seq=seq),
        grid=(t // tm,),
        in_specs=[tile(D_MODEL), tile(ATTN_WIDTH), tile(CONV_WIDTH),
                  pl.BlockSpec((halo, CONV_WIDTH),
                               lambda i: (jnp.maximum(i * (tm // halo) - 1, 0), 0)),
                  pl.BlockSpec((halo, CONV_WIDTH),
                               lambda i: (jnp.minimum((i + 1) * (tm // halo), n_halo - 1), 0)),
                  tile(CONV_WIDTH),
                  pl.BlockSpec((None, 1, 6 * D_MODEL), row)] + [full(a) for a in small],
        out_specs=[tile(D_MODEL), pl.BlockSpec((tm * F32_SUBLANES, LANES), lambda i: (i, 0)),
                   pl.BlockSpec((F32_SUBLANES, tm), lambda i: (0, i))],
        out_shape=[jax.ShapeDtypeStruct((t, D_MODEL), f32),
                   jax.ShapeDtypeStruct((t * F32_SUBLANES, LANES), f32),
                   jax.ShapeDtypeStruct((F32_SUBLANES, t), f32)],
        compiler_params=_params("arbitrary"),
        name="post",
    )(x, attn, u, u, u, bg, mod3, *small)


def _max_chunks(blk):
    n = -(-TOP_K * blk // MOE_CHUNK) + EXPERTS_PER_STEP
    return n + n % 2


def _sorted_rows(blk):
    rows = TOP_K * blk + N_EXPERTS * (F32_SUBLANES - 1) + MOE_CHUNK
    assert rows % F32_SUBLANES == 0
    return rows + (F32_SUBLANES if rows // F32_SUBLANES % 2 == 0 else 0)


def _dispatch_tables(route, blk):
    nblk = route.shape[1] // blk
    n_slots = TOP_K * blk
    by_block = lambda a: a.reshape(TOP_K, nblk, blk).transpose(1, 0, 2)
    experts = by_block(route[0:TOP_K].astype(jnp.int32))
    weights = by_block(route[TOP_K:2 * TOP_K])
    ids = jnp.arange(N_EXPERTS, dtype=jnp.int32)
    onehot = experts.reshape(nblk, n_slots // LANES, LANES, 1) == ids
    earlier = jnp.tril(jnp.ones((LANES, LANES), bf16), -1)
    within = jnp.einsum("ij,bgje->bgie", earlier, onehot.astype(bf16),
                        preferred_element_type=f32)
    per_group = jnp.sum(onehot, axis=2, dtype=jnp.int32)
    before_group = jnp.cumsum(per_group, axis=1) - per_group
    counts = jnp.sum(per_group, axis=1)
    padded = (counts + F32_SUBLANES - 1) // F32_SUBLANES * F32_SUBLANES
    lo = jnp.cumsum(padded, axis=1) - padded
    row = within.astype(jnp.int32) + before_group[:, :, None, :] + lo[:, None, None, :]
    row = jnp.sum(jnp.where(onehot, row, 0), axis=-1).reshape(nblk, TOP_K, blk)

    eg, n_steps, max_c = EXPERTS_PER_STEP, N_EXPERTS // EXPERTS_PER_STEP, _max_chunks(blk)
    lo = lo.reshape(nblk, n_steps, eg)
    counts = counts.reshape(nblk, n_steps, eg)
    n_ch = (counts + MOE_CHUNK - 1) // MOE_CHUNK
    cum = jnp.cumsum(n_ch, axis=2)
    total = cum[..., -1]
    pos = jnp.arange(max_c, dtype=jnp.int32)
    j = jnp.sum(cum[:, :, None, :] <= pos[None, None, :, None], axis=-1, dtype=jnp.int32)
    j = jnp.minimum(j, eg - 1)
    hit = j[..., None] == jnp.arange(eg, dtype=jnp.int32)
    pick = lambda a: jnp.sum(jnp.where(hit, a[:, :, None, :], 0), axis=-1)
    done = (pos - pick(cum - n_ch)) * MOE_CHUNK
    live = pos < total[..., None]
    flat = lambda a: jnp.where(live, a, 0).reshape(nblk, 1, n_steps * max_c)
    return (row[:, 0:1], row[:, 1:2], weights[:, 0:1], weights[:, 1:2], flat(j),
            flat(pick(lo) + done), flat(jnp.clip(pick(counts) - done, 0, MOE_CHUNK)),
            total[:, None, :])


def _moe_kernel(row0_ref, row1_ref, w0_ref, w1_ref, cj_ref, cfirst_ref, cvalid_ref, nchunks_ref,
                h_ref, wg_ref, wu_ref, wd_ref, mod_ref, fg_ref, x1_hbm, out_hbm,
                sorted_ref, yt_ref, xbuf_ref, obuf_ref, sem_in, sem_out, *, blk):
    n_col = D_MODEL // LANES
    pitch = _sorted_rows(blk)
    blk_id, step = pl.program_id(0), pl.program_id(1)
    last_step = step == pl.num_programs(1) - 1
    max_c = _max_chunks(blk)
    n_sub = blk // FINAL_ROWS
    assignment_rows = (row0_ref, w0_ref), (row1_ref, w1_ref)

    def column_tile(row):
        return pl.ds(row, F32_SUBLANES, stride=pitch)

    def x1_copy(s):
        rows = pl.ds(blk_id * blk + s * FINAL_ROWS, FINAL_ROWS)
        slot = s % FINAL_SLOTS
        return pltpu.make_async_copy(x1_hbm.at[rows, :], xbuf_ref.at[slot], sem_in.at[slot])

    def out_copy(s):
        rows = pl.ds(blk_id * blk + s * FINAL_ROWS, FINAL_ROWS)
        slot = s % FINAL_SLOTS
        return pltpu.make_async_copy(obuf_ref.at[slot], out_hbm.at[rows, :], sem_out.at[slot])

    @pl.when(last_step)
    def _():
        for s in range(min(FINAL_SLOTS, n_sub)):
            x1_copy(s).start()

    @pl.when((blk_id == 0) & (step == 0))
    def _():
        zero_rows = F32_SUBLANES * LANES
        zeros = jnp.zeros((zero_rows, LANES), f32)

        def clear(i, carry):
            sorted_ref[pl.ds(pl.multiple_of(i * zero_rows, zero_rows), zero_rows), :] = zeros
            return carry

        lax.fori_loop(0, n_col * pitch // zero_rows, clear, 0)
        sorted_ref[n_col * pitch - zero_rows:, :] = zeros

    @pl.when(step == 0)
    def _():
        def dispatch(i, carry):
            first = pl.multiple_of(i * DISPATCH_ROWS * F32_SUBLANES, DISPATCH_ROWS * F32_SUBLANES)
            for r in range(DISPATCH_ROWS):
                tile = h_ref[pl.ds(first + r * F32_SUBLANES, F32_SUBLANES), :]
                for rows_ref, _ in assignment_rows:
                    sorted_ref[column_tile(rows_ref[0, i * DISPATCH_ROWS + r]), :] = tile
            return carry

        lax.fori_loop(0, blk // DISPATCH_ROWS, dispatch, 0)

    def chunk_rows(p, c):
        first = pl.multiple_of(cfirst_ref[0, p], F32_SUBLANES)
        return pl.ds(pl.multiple_of(c * pitch + first, F32_SUBLANES), MOE_CHUNK)

    def hidden(p):
        x = jnp.concatenate([sorted_ref[chunk_rows(p, c), :] for c in range(n_col)],
                            axis=1).astype(bf16)
        j = cj_ref[0, p]
        g = _dot(x, wg_ref[j])
        up = _dot(x, wu_ref[j])
        return ((g * jax.nn.sigmoid(g)) * up).astype(bf16)

    def project(p, act):
        out = _dot(act, wd_ref[cj_ref[0, p]])
        own = lax.broadcasted_iota(jnp.int32, (MOE_CHUNK, LANES), 0) < cvalid_ref[0, p]
        for c in range(n_col):
            rows = chunk_rows(p, c)
            sorted_ref[rows, :] = jnp.where(own, out[:, c * LANES:(c + 1) * LANES],
                                            sorted_ref[rows, :])

    def run_chunks(ps):
        acts = [hidden(p) for p in ps]
        for p, act in zip(ps, acts):
            project(p, act)

    def pair(pi, carry):
        run_chunks([step * max_c + 2 * pi + half for half in range(2)])
        return carry

    n_chunks = nchunks_ref[0, step]
    lax.fori_loop(0, n_chunks // 2, pair, 0)

    @pl.when(n_chunks % 2 == 1)
    def _():
        run_chunks([step * max_c + n_chunks - 1])

    @pl.when(last_step)
    def _():
        gate2 = mod_ref[:, G2:G2 + D_MODEL]
        for s in range(n_sub):
            slot = s % FINAL_SLOTS
            for r in range(FINAL_ROWS):
                t = s * FINAL_ROWS + r
                weighted = [sorted_ref[column_tile(rows_ref[0, t]), :] * w_ref[0, t]
                            for rows_ref, w_ref in assignment_rows]
                yt_ref[r * F32_SUBLANES:(r + 1) * F32_SUBLANES, :] = functools.reduce(
                    jnp.add, weighted)
            x1_copy(s).wait()
            if s >= FINAL_SLOTS:
                out_copy(s - FINAL_SLOTS).wait()
            y = jnp.concatenate([yt_ref[pl.ds(c, FINAL_ROWS, stride=F32_SUBLANES), :]
                                 for c in range(n_col)], axis=1)
            x2 = xbuf_ref[slot] + gate2 * y
            obuf_ref[slot] = _rms(x2) * fg_ref[...]
            out_copy(s).start()
            if s + FINAL_SLOTS < n_sub:
                x1_copy(s + FINAL_SLOTS).start()
        for s in range(max(n_sub - FINAL_SLOTS, 0), n_sub):
            out_copy(s).wait()


def _moe(h2_tiles, tables, x1, mod3, final_g, w_gate, w_up, w_down, *, seq, mod_row0,
         blk=MOE_BLOCK):
    t = h2_tiles.shape[0] // F32_SUBLANES
    eg, n_steps = EXPERTS_PER_STEP, N_EXPERTS // EXPERTS_PER_STEP
    blocks_per_seq = max(seq // blk, 1)
    row = lambda b, g: (mod_row0 + (b // blocks_per_seq if mod_row0 else 0), 0, 0)
    smem = lambda n: pl.BlockSpec((None, 1, n), lambda b, g: (b, 0, 0), memory_space=pltpu.SMEM)
    final_buf = pltpu.VMEM((FINAL_SLOTS, FINAL_ROWS, D_MODEL), f32)
    return pl.pallas_call(
        functools.partial(_moe_kernel, blk=blk),
        grid=(t // blk, n_steps),
        in_specs=[smem(blk)] * 4 + [smem(n_steps * _max_chunks(blk))] * 3 + [
                  smem(n_steps),
                  pl.BlockSpec((blk * F32_SUBLANES, LANES), lambda b, g: (b, 0)),
                  pl.BlockSpec((eg, D_MODEL, D_EXPERT), lambda b, g: (g, 0, 0)),
                  pl.BlockSpec((eg, D_MODEL, D_EXPERT), lambda b, g: (g, 0, 0)),
                  pl.BlockSpec((eg, D_EXPERT, D_MODEL), lambda b, g: (g, 0, 0)),
                  pl.BlockSpec((None, 1, 6 * D_MODEL), row),
                  pl.BlockSpec((1, D_MODEL), lambda b, g: (0, 0)),
                  pl.BlockSpec(memory_space=pl.ANY)],
        out_specs=pl.BlockSpec(memory_space=pl.ANY),
        out_shape=jax.ShapeDtypeStruct((t, D_MODEL), f32),
        scratch_shapes=[pltpu.VMEM((D_MODEL // LANES * _sorted_rows(blk), LANES), f32),
                        pltpu.VMEM((FINAL_ROWS * F32_SUBLANES, LANES), f32),
                        final_buf, final_buf,
                        pltpu.SemaphoreType.DMA((FINAL_SLOTS,)),
                        pltpu.SemaphoreType.DMA((FINAL_SLOTS,))],
        compiler_params=_params("arbitrary", "arbitrary", vmem=MOE_VMEM_LIMIT_BYTES),
        name="moe",
    )(*tables, h2_tiles, w_gate, w_up, w_down, mod3, final_g, x1)


def _mixers(x, mod3, p, *, seq, mod_row0, ctx_kv, cast=()):
    if ctx_kv is not None:
        q, k, vt, u, bg = _inproj(x, mod3, p["norm1_g"], p["w_in"], seq=seq, mod_row0=mod_row0,
                                  rope_tabs=_rope_tables(seq))
        attn, *cast = _latent_attention(q, k, vt, ctx_kv[0], ctx_kv[1], p["sink"], seq=seq,
                                        cast=cast)
        kv_t = None
    else:
        q, k, kt, vt, u, bg = _inproj(x, mod3, p["norm1_g"], p["w_in"], seq=seq,
                                      mod_row0=mod_row0)
        attn = _context_attention(q, k, vt, p["sink"], seq=seq)
        kv_t = (kt, vt)
    x1, h2_tiles, route = _post(x, attn, u, bg, mod3, p, seq=seq, mod_row0=mod_row0)
    return x1, h2_tiles, route, kv_t, cast


def _experts(x1, h2_tiles, tables, mod3, p, *, seq, mod_row0):
    out = _moe(h2_tiles, tables, x1, mod3, p["final_g"], p["w_gate"], p["w_up"], p["w_down"],
               seq=seq, mod_row0=mod_row0)
    return out.reshape(-1, seq, D_MODEL)


def kernel(x_prompt, x_sample, cache_k, cache_v, c, c_ctx, w_mod, b_mod, norm1_g, w_in, conv_w,
           conv_b, sink, attn_out_g, conv_out_g, w_o, norm2_g, w_coarse, b_coarse, w_fine, b_fine,
           w_gate, w_up, w_down, final_g):
    batch, seq, _ = x_prompt.shape
    dec_batch, dec_seq, _ = x_sample.shape
    past = cache_k.shape[2]
    assert w_mod.shape[0] == 1 and 1 + dec_batch <= MOD_ROWS

    cvecs = jnp.concatenate([c_ctx[None], c, jnp.zeros((MOD_ROWS - 1 - dec_batch, D_MODEL), f32)])
    mod3 = _modulation(cvecs, w_mod[0], b_mod[0]).reshape(MOD_ROWS, 1, 6 * D_MODEL)

    pad = jnp.zeros((D_MODEL, LANES - N_EXPERTS - N_GROUPS), f32)
    mix_rows = np.concatenate([ATTN_HEAD_PERM, np.arange(ATTN_WIDTH, ATTN_WIDTH + CONV_WIDTH)])
    p = {
        "norm1_g": norm1_g, "w_in": w_in[0].astype(bf16), "conv_w": conv_w[0], "conv_b": conv_b,
        "sink": sink[0], "attn_out_g": attn_out_g[:, ATTN_HEAD_PERM], "conv_out_g": conv_out_g,
        "w_o": w_o[0][mix_rows].astype(bf16), "norm2_g": norm2_g,
        "w_router": _split_bf16(jnp.concatenate([w_fine[0], w_coarse[0], pad], axis=1)),
        "b_router": jnp.concatenate([b_fine[0], b_coarse[0], pad[0]])[None],
        "final_g": final_g[None],
    }

    x1_p, h2_p, route_p, kv_t, _ = _mixers(x_prompt.reshape(batch * seq, D_MODEL), mod3, p,
                                           seq=seq, mod_row0=0, ctx_kv=None)
    new_k, new_v = (a.reshape(batch, 1, N_KV_HEADS, HEAD_DIM, seq).transpose(0, 1, 4, 2, 3)
                    for a in kv_t)

    ctx_kv = (cache_k[:, 0].reshape(dec_batch, past, KV_WIDTH),
              cache_v[:, 0].reshape(dec_batch, past, KV_WIDTH))
    experts_f32 = (w_gate[0], w_up[0], w_down[0])
    ride = N_EXPERTS % (dec_batch * dec_seq // (LANES * LAT_ATTN_SUB_BLOCKS)) == 0
    x1_s, h2_s, route_s, _, experts_bf16 = _mixers(
        x_sample.reshape(dec_batch * dec_seq, D_MODEL), mod3, p, seq=dec_seq, mod_row0=1,
        ctx_kv=ctx_kv, cast=experts_f32 if ride else ())
    if not ride:
        experts_bf16 = [w.astype(bf16) for w in experts_f32]
    p = dict(p, w_gate=experts_bf16[0], w_up=experts_bf16[1], w_down=experts_bf16[2])

    n_blk_p = batch * seq // MOE_BLOCK
    tables = _dispatch_tables(jnp.concatenate([route_p, route_s], axis=1), MOE_BLOCK)
    y_prompt = _experts(x1_p, h2_p, [t[:n_blk_p] for t in tables], mod3, p, seq=seq, mod_row0=0)
    y_sample = _experts(x1_s, h2_s, [t[n_blk_p:] for t in tables], mod3, p, seq=dec_seq,
                        mod_row0=1)
    return y_prompt, y_sample, new_k, new_v
```

```python
import functools

import jax
import jax.numpy as jnp
import numpy as np
from jax import lax
from jax.experimental import pallas as pl
from jax.experimental.pallas import tpu as pltpu

D_MODEL = 1024
HEAD_DIM = 64
ATTN_WIDTH = 512
N_HEADS = 8
N_KV_HEADS = 2
Q_PER_KV = 4
KV_WIDTH = 128
CONV_WIDTH = 512
CONV_K = 3
WINDOW = 128
GRID_W = 64
ROPE_BASE = 10000.0
N_FREQ = 16
N_GROUPS = 4
EXPERTS_PER_GROUP = 8
N_EXPERTS = 32
TOP_K = 2
D_EXPERT = 256
IN_WIDTH = ATTN_WIDTH + 2 * KV_WIDTH + 3 * CONV_WIDTH
EPS = 1e-6
NEG = -1e30
SCALE = HEAD_DIM ** -0.5
LOG2_E = 1.4426950408889634

LANES = 128
F32_SUBLANES = 8
BF16_SUBLANES = 16
assert D_MODEL == F32_SUBLANES * LANES

INPROJ_ROWS = 512
INPROJ_SUB = 256
POST_SUB = 256
MOE_BLOCK = 2048
MOE_CHUNK = 160
CHUNKS_PER_TRIP = 4
DISPATCH_ROWS = 128
EXPERTS_PER_STEP = 4
FINAL_ROWS = 256
FINAL_SLOTS = 4
MOE_VMEM_LIMIT_BYTES = 60 * 1024 * 1024
VMEM_LIMIT_BYTES = 48 * 1024 * 1024

SH1, SC1, G1, SH2, SC2, G2 = (i * D_MODEL for i in range(6))
MOD_ROWS = 8

COARSE_LANE0 = N_EXPERTS

f32 = jnp.float32
bf16 = jnp.bfloat16


def _params(*semantics, vmem=VMEM_LIMIT_BYTES):
    return pltpu.CompilerParams(dimension_semantics=semantics, vmem_limit_bytes=vmem)


def _rms(x):
    return x * lax.rsqrt(jnp.mean(x * x, axis=-1, keepdims=True) + EPS)


def _dot(a, b):
    return jnp.dot(a, b, preferred_element_type=f32)


def _split_bf16(w):
    hi = w.astype(bf16)
    lo = (w - hi.astype(f32)).astype(bf16)
    return jnp.concatenate([hi, lo], axis=1)


def _mod_kernel(cv_ref, w_ref, b_ref, o_ref):
    a = cv_ref[...]
    a = a * jax.nn.sigmoid(a)
    o_ref[...] = jnp.dot(a, w_ref[...], precision=lax.Precision.HIGHEST,
                         preferred_element_type=f32) + b_ref[...]


def _modulation(cvecs, w_mod, b_mod):
    tn = 2 * D_MODEL
    return pl.pallas_call(
        _mod_kernel,
        grid=(6 * D_MODEL // tn,),
        in_specs=[pl.BlockSpec((MOD_ROWS, D_MODEL), lambda j: (0, 0)),
                  pl.BlockSpec((D_MODEL, tn), lambda j: (0, j)),
                  pl.BlockSpec((1, tn), lambda j: (0, j))],
        out_specs=pl.BlockSpec((MOD_ROWS, tn), lambda j: (0, j)),
        out_shape=jax.ShapeDtypeStruct((MOD_ROWS, 6 * D_MODEL), f32),
        compiler_params=_params("arbitrary"),
        name="mod",
    )(cvecs, w_mod, b_mod.reshape(1, -1))


def _swap_halves(x):
    lane = lax.broadcasted_iota(jnp.int32, x.shape, 1)
    up = pltpu.roll(x, LANES - N_FREQ, axis=1)
    dn = pltpu.roll(x, N_FREQ, axis=1)
    return jnp.where((lane % (2 * N_FREQ)) < N_FREQ, up, dn)


def _rope(x, cos, sin):
    parts = []
    for c in range(x.shape[1] // LANES):
        xc = x[:, c * LANES:(c + 1) * LANES]
        parts.append(xc * cos + _swap_halves(xc) * sin)
    return parts[0] if len(parts) == 1 else jnp.concatenate(parts, axis=1)


def _inproj_kernel(*refs, rope, seq):
    if rope:
        x_ref, mod_ref, g_ref, w_ref, cos_ref, sin_ref, q_ref, k_ref, vt_ref, u_ref, bg_ref = refs
    else:
        x_ref, mod_ref, g_ref, w_ref, q_ref, k_ref, kt_ref, vt_ref, u_ref, bg_ref = refs
    tm = x_ref.shape[0]
    subs = [slice(s * INPROJ_SUB, (s + 1) * INPROJ_SUB) for s in range(tm // INPROJ_SUB)]
    zs = []
    for rs in subs:
        h = _rms(x_ref[rs, :]) * g_ref[...]
        h = h * (1.0 + mod_ref[:, SC1:SC1 + D_MODEL]) + mod_ref[:, SH1:SH1 + D_MODEL]
        zs.append(_dot(h.astype(bf16), w_ref[...]))
    for rs, z in zip(subs, zs):
        o = 0
        q = z[:, o:o + ATTN_WIDTH]; o += ATTN_WIDTH
        k = z[:, o:o + KV_WIDTH]; o += KV_WIDTH
        v = z[:, o:o + KV_WIDTH]; o += KV_WIDTH
        bg = z[:, o:o + CONV_WIDTH]; o += CONV_WIDTH
        cg = z[:, o:o + CONV_WIDTH]; o += CONV_WIDTH
        xin = z[:, o:o + CONV_WIDTH]
        if rope:
            q = _rope(q, cos_ref[rs, :], sin_ref[rs, :])
            k = _rope(k, cos_ref[rs, :], sin_ref[rs, :])
        q_ref[rs, :] = q.astype(q_ref.dtype)
        k_ref[rs, :] = k.astype(k_ref.dtype)
        if rope:
            vt_ref[:, rs] = v.T.astype(vt_ref.dtype)
        else:
            for s in range(rs.start // seq, rs.stop // seq):
                rows = slice(s * seq - rs.start, (s + 1) * seq - rs.start)
                kt_ref[s] = k[rows].T
                vt_ref[s] = v[rows].T
        u_ref[rs, :] = (cg * xin).astype(u_ref.dtype)
        bg_ref[rs, :] = bg.astype(bg_ref.dtype)


def _inproj(x, mod3, norm_g, w_in, *, seq, mod_row0, rope_tabs=None, tm=INPROJ_ROWS):
    t = x.shape[0]
    tiles_per_seq = seq // tm
    assert rope_tabs is not None or INPROJ_SUB % seq == 0
    row = lambda i: (mod_row0 + (i // tiles_per_seq if mod_row0 else 0), 0, 0)
    in_specs = [pl.BlockSpec((tm, D_MODEL), lambda i: (i, 0)),
                pl.BlockSpec((None, 1, 6 * D_MODEL), row),
                pl.BlockSpec((1, D_MODEL), lambda i: (0, 0)),
                pl.BlockSpec((D_MODEL, IN_WIDTH), lambda i: (0, 0))]
    args = [x, mod3, norm_g, w_in]
    if rope_tabs is not None:
        in_specs += [pl.BlockSpec((tm, LANES), lambda i: (i % tiles_per_seq, 0))] * 2
        args += list(rope_tabs)
    rows = lambda w: (pl.BlockSpec((tm, w), lambda i: (i, 0)), jax.ShapeDtypeStruct((t, w), bf16))
    if rope_tabs is not None:
        transposed = [(pl.BlockSpec((None, KV_WIDTH, tm),
                                    lambda i: (i // tiles_per_seq, 0, i % tiles_per_seq)),
                       jax.ShapeDtypeStruct((t // seq, KV_WIDTH, seq), bf16))]
    else:
        transposed = [(pl.BlockSpec((tm // seq, KV_WIDTH, seq), lambda i: (i, 0, 0)),
                       jax.ShapeDtypeStruct((t // seq, KV_WIDTH, seq), f32))] * 2
    outs = [rows(ATTN_WIDTH), rows(KV_WIDTH)] + transposed + [rows(CONV_WIDTH), rows(CONV_WIDTH)]
    out_specs, out_shape = (list(z) for z in zip(*outs))
    return pl.pallas_call(
        functools.partial(_inproj_kernel, rope=rope_tabs is not None, seq=seq),
        grid=(t // tm,),
        in_specs=in_specs,
        out_specs=out_specs,
        out_shape=out_shape,
        compiler_params=_params("arbitrary"),
        name="inproj_rope" if rope_tabs is not None else "inproj",
    )(*args)


def _rope_tables(n):
    t = np.arange(n)
    inv = ROPE_BASE ** (-np.arange(N_FREQ, dtype=np.float32) / N_FREQ)
    rows = (t // GRID_W).astype(np.float32)
    cols = (t % GRID_W).astype(np.float32)
    d = np.arange(LANES) % HEAD_DIM
    pos = np.where(d[None, :] < HEAD_DIM // 2, rows[:, None], cols[:, None])
    ang = jnp.asarray(pos.astype(np.float32) * inv[d % N_FREQ][None, :])
    sign = np.where((d % (2 * N_FREQ)) < N_FREQ, -1.0, 1.0).astype(np.float32)
    return jnp.cos(ang), jnp.sin(ang) * sign[None, :]


def _attention_core(sink_ref, q_ref, o_ref, subs, shared, bq):
    n_col = Q_PER_KV * bq
    col_head = lax.broadcasted_iota(jnp.int32, (1, n_col), 1) // bq
    sink_rows = []
    for g in range(N_KV_HEADS):
        sink_row = jnp.full((1, n_col), sink_ref[g * Q_PER_KV], f32)
        for j in range(1, Q_PER_KV):
            sink_row = jnp.where(col_head == j, sink_ref[g * Q_PER_KV + j], sink_row)
        sink_rows.append(sink_row * LOG2_E)
    ones_rows = jnp.ones((BF16_SUBLANES, 1), bf16)

    def with_ones(vals_t, g):
        return jnp.concatenate(
            [vals_t[g * HEAD_DIM:(g + 1) * HEAD_DIM, :],
             jnp.broadcast_to(ones_rows, (BF16_SUBLANES, vals_t.shape[1]))], axis=0)

    shared_vals = None if shared is None else [with_ones(shared[1], g) for g in range(N_KV_HEADS)]
    qts = [(q_ref[i * bq:(i + 1) * bq, :].astype(f32) * (SCALE * LOG2_E)).T.astype(bf16)
           for i in range(len(subs))]

    def scores(i, g):
        lo, hi = g * HEAD_DIM, (g + 1) * HEAD_DIM
        heads = range(g * Q_PER_KV, (g + 1) * Q_PER_KV)
        qt_g = jnp.concatenate([qts[i][h * HEAD_DIM:(h + 1) * HEAD_DIM, :] for h in heads], axis=1)
        s_own = _dot(subs[i]["keys"][:, lo:hi], qt_g)
        if subs[i]["mask"] is not None:
            s_own = jnp.where(subs[i]["mask"], s_own, NEG)
        s_shared = None if shared is None else _dot(shared[0][:, lo:hi], qt_g)
        return s_own, s_shared

    def weighted_values(i, g, s_own, s_shared):
        m = jnp.maximum(jnp.max(s_own, axis=0, keepdims=True), sink_rows[g])
        if s_shared is not None:
            m = jnp.maximum(m, jnp.max(s_shared, axis=0, keepdims=True))
        o_aug = _dot(with_ones(subs[i]["vals_t"], g), jnp.exp2(s_own - m).astype(bf16))
        if s_shared is not None:
            o_aug = o_aug + _dot(shared_vals[g], jnp.exp2(s_shared - m).astype(bf16))
        den = o_aug[HEAD_DIM:HEAD_DIM + 1, :] + jnp.exp2(sink_rows[g] - m)
        return o_aug[:HEAD_DIM, :] / den

    passes = [(i, g) for i in range(len(subs)) for g in range(N_KV_HEADS)]
    out_t = {}
    pending = scores(*passes[0])
    for n, (i, g) in enumerate(passes):
        nxt = scores(*passes[n + 1]) if n + 1 < len(passes) else None
        out_t[i, g] = weighted_values(i, g, *pending)
        pending = nxt
    for i in range(len(subs)):
        blocks = [jnp.concatenate([out_t[i, g][:, j * bq:(j + 1) * bq]
                                   for g in range(N_KV_HEADS)], axis=0).T
                  for j in range(Q_PER_KV)]
        o_ref[i * bq:(i + 1) * bq, :] = jnp.concatenate(blocks, axis=1).astype(o_ref.dtype)


ATTN_HEAD_PERM = np.array([(Q_PER_KV * g + j) * HEAD_DIM + d for j in range(Q_PER_KV)
                           for g in range(N_KV_HEADS) for d in range(HEAD_DIM)])


def _permute_heads(a):
    blocks = a.reshape(N_KV_HEADS, Q_PER_KV, HEAD_DIM, *a.shape[1:])
    return blocks.swapaxes(0, 1).reshape(a.shape)


def _ctx_attn_kernel(sink_ref, q_ref, k_ref, vt_ref, o_ref, *, bq, seq):
    subs = []
    for s in range(q_ref.shape[0] // seq):
        keys = k_ref[s * seq:(s + 1) * seq, :]
        subs += [dict(keys=keys, vals_t=vt_ref[s].astype(bf16), mask=None)] * (seq // bq)
    _attention_core(sink_ref, q_ref, o_ref, subs, None, bq)


def _context_attention(q, k, vt, sink, *, seq, bq=128, seqs_per_step=4):
    t = q.shape[0]
    rows = seq * seqs_per_step
    return pl.pallas_call(
        functools.partial(_ctx_attn_kernel, bq=bq, seq=seq),
        grid=(t // rows,),
        in_specs=[pl.BlockSpec(memory_space=pltpu.SMEM),
                  pl.BlockSpec((rows, ATTN_WIDTH), lambda b: (b, 0)),
                  pl.BlockSpec((rows, KV_WIDTH), lambda b: (b, 0)),
                  pl.BlockSpec((seqs_per_step, KV_WIDTH, seq), lambda b: (b, 0, 0))],
        out_specs=pl.BlockSpec((rows, ATTN_WIDTH), lambda b: (b, 0)),
        out_shape=jax.ShapeDtypeStruct((t, ATTN_WIDTH), bf16),
        compiler_params=_params("arbitrary"),
        name="ctx_attn",
    )(sink, q, k, vt)


def _lat_attn_kernel(sink_ref, q_ref, k_ref, vt_ref, ck_ref, cv_ref, *rest, bq, seq, n_cast):
    o_ref = rest[n_cast]
    for src_ref, dst_ref in zip(rest[:n_cast], rest[n_cast + 1:]):
        dst_ref[...] = src_ref[...].astype(dst_ref.dtype)
    band = bq + 2 * WINDOW
    n_col = Q_PER_KV * bq
    delta = ((lax.broadcasted_iota(jnp.int32, (band, n_col), 1) & (bq - 1))
             - lax.broadcasted_iota(jnp.int32, (band, n_col), 0))
    n_sub = q_ref.shape[0] // bq
    subs = []
    for sb in range(n_sub):
        i = pl.program_id(1) * n_sub + sb
        start = pl.multiple_of(jnp.clip(i * bq - WINDOW, 0, seq - band), LANES)
        shifted = delta + (i * bq - start + WINDOW)
        subs.append(dict(keys=k_ref[pl.ds(start, band), :],
                         vals_t=vt_ref[:, pl.ds(start, band)],
                         mask=shifted.astype(jnp.uint32) <= 2 * WINDOW))
    shared = (ck_ref[...].astype(bf16), cv_ref[...].T.astype(bf16))
    _attention_core(sink_ref, q_ref, o_ref, subs, shared, bq)


LAT_ATTN_SUB_BLOCKS = 8


def _latent_attention(q, k, vt, ck, cv, sink, *, seq, cast=(), bq=128,
                      sub_blocks=LAT_ATTN_SUB_BLOCKS):
    nb, past = ck.shape[0], ck.shape[1]
    assert bq == LANES and N_KV_HEADS * HEAD_DIM == LANES
    q3, k3 = (a.reshape(nb, seq, a.shape[-1]) for a in (q, k))
    tq = bq * sub_blocks
    steps_per_seq = seq // tq
    n_steps = nb * steps_per_seq
    cast3 = [a.reshape((n_steps, -1) + a.shape[2:]) for a in cast]
    cast_specs = [pl.BlockSpec((None,) + a.shape[1:], lambda b, i: (b * steps_per_seq + i, 0, 0))
                  for a in cast3]
    out, *cast_out = pl.pallas_call(
        functools.partial(_lat_attn_kernel, bq=bq, seq=seq, n_cast=len(cast)),
        grid=(nb, steps_per_seq),
        in_specs=[pl.BlockSpec(memory_space=pltpu.SMEM),
                  pl.BlockSpec((None, tq, ATTN_WIDTH), lambda b, i: (b, i, 0)),
                  pl.BlockSpec((None, seq, KV_WIDTH), lambda b, i: (b, 0, 0)),
                  pl.BlockSpec((None, KV_WIDTH, seq), lambda b, i: (b, 0, 0)),
                  pl.BlockSpec((None, past, KV_WIDTH), lambda b, i: (b, 0, 0)),
                  pl.BlockSpec((None, past, KV_WIDTH), lambda b, i: (b, 0, 0))] + cast_specs,
        out_specs=[pl.BlockSpec((None, tq, ATTN_WIDTH), lambda b, i: (b, i, 0))] + cast_specs,
        out_shape=[jax.ShapeDtypeStruct((nb, seq, ATTN_WIDTH), bf16)]
        + [jax.ShapeDtypeStruct(a.shape, bf16) for a in cast3],
        compiler_params=_params("arbitrary", "arbitrary"),
        name="lat_attn",
    )(sink, q3, k3, vt, ck, cv, *cast3)
    return [out.reshape(nb * seq, ATTN_WIDTH)] + [o.reshape(a.shape) for o, a in zip(cast_out, cast)]


def _route(logits_t):
    r = logits_t.shape[1]
    big = jnp.float32(LANES)
    crow = lax.broadcasted_iota(jnp.int32, (F32_SUBLANES, r), 0)
    crow_f = crow.astype(f32)
    lc = jnp.where(crow < N_GROUPS, logits_t[COARSE_LANE0:COARSE_LANE0 + F32_SUBLANES, :], -jnp.inf)
    mc = jnp.max(lc, axis=0, keepdims=True)
    grp = jnp.min(jnp.where(lc == mc, crow_f, big), axis=0, keepdims=True)
    pg = 1.0 / jnp.sum(jnp.exp(lc - mc), axis=0, keepdims=True)
    erow = lax.broadcasted_iota(jnp.int32, (N_EXPERTS, r), 0).astype(f32)
    in_g = jnp.floor(erow * (1.0 / EXPERTS_PER_GROUP)) == grp
    fl = jnp.where(in_g, logits_t[0:N_EXPERTS, :], -jnp.inf)
    m1 = jnp.max(fl, axis=0, keepdims=True)
    i1 = jnp.min(jnp.where(fl == m1, erow, big), axis=0, keepdims=True)
    fl2 = jnp.where(erow == i1, -jnp.inf, fl)
    m2 = jnp.max(fl2, axis=0, keepdims=True)
    i2 = jnp.min(jnp.where(fl2 == m2, erow, big), axis=0, keepdims=True)
    e2 = jnp.exp(m2 - m1)
    p1 = pg / (1.0 + e2)
    p2 = pg * e2 / (1.0 + e2)
    packed = jnp.where(crow == 0, i1, jnp.where(crow == 1, i2, jnp.where(crow == 2, p1, p2)))
    return jnp.where(crow < 4, packed, 0.0)


def _post_kernel(x_ref, attn_ref, u_ref, up_ref, un_ref, bg_ref, mod_ref, ag_ref, cg_ref, cw_ref,
                 cb_ref, wo_ref, n2_ref, wr_ref, br_ref, x1_ref, h2_ref, route_ref, *, tm, seq):
    i = pl.program_id(0)
    u = u_ref[...].astype(f32)
    rows = lax.broadcasted_iota(jnp.int32, (tm, 1), 0)
    spos = (i * tm + rows) % seq
    u_dn = jnp.where(rows == 0, up_ref[...].astype(f32)[BF16_SUBLANES - 1:, :],
                     pltpu.roll(u, 1, axis=0))
    u_dn = jnp.where(spos == 0, 0.0, u_dn)
    u_up = jnp.where(rows == tm - 1, un_ref[...].astype(f32)[0:1, :], pltpu.roll(u, tm - 1, axis=0))
    u_up = jnp.where(spos == seq - 1, 0.0, u_up)
    y = u_dn * cw_ref[0:1, :] + u * cw_ref[1:2, :] + u_up * cw_ref[2:3, :] + cb_ref[...]
    conv = bg_ref[...].astype(f32) * y
    attn_n = (_rms(attn_ref[...].astype(f32)) * ag_ref[...]).astype(bf16)
    conv_n = (_rms(conv) * cg_ref[...]).astype(bf16)
    subs = [slice(s * POST_SUB, (s + 1) * POST_SUB) for s in range(tm // POST_SUB)]
    mixed = [_dot(attn_n[rs], wo_ref[0:ATTN_WIDTH, :]) + _dot(conv_n[rs], wo_ref[ATTN_WIDTH:, :])
             for rs in subs]
    logits = []
    for s, rs in enumerate(subs):
        x1 = x_ref[rs, :] + mod_ref[:, G1:G1 + D_MODEL] * mixed[s]
        x1_ref[rs, :] = x1
        h2 = _rms(x1) * n2_ref[...]
        h2 = h2 * (1.0 + mod_ref[:, SC2:SC2 + D_MODEL]) + mod_ref[:, SH2:SH2 + D_MODEL]
        for c in range(D_MODEL // LANES):
            h2_ref[pl.ds(rs.start * F32_SUBLANES + c, POST_SUB, stride=F32_SUBLANES), :] = (
                h2[:, c * LANES:(c + 1) * LANES])
        h2_hi = h2.astype(bf16)
        h2_lo = (h2 - h2_hi.astype(f32)).astype(bf16)
        both = _dot(h2_hi, wr_ref[...])
        logits.append(both[:, :LANES] + both[:, LANES:] + _dot(h2_lo, wr_ref[:, :LANES]))
    for s, rs in enumerate(subs):
        route_ref[:, rs] = _route((logits[s] + br_ref[...]).T)


def _post(x, attn, u, bg, mod3, p, *, seq, mod_row0, tm=4 * POST_SUB):
    t = x.shape[0]
    tiles_per_seq = max(seq // tm, 1)
    halo = BF16_SUBLANES
    n_halo = t // halo
    row = lambda i: (mod_row0 + (i // tiles_per_seq if mod_row0 else 0), 0, 0)
    tile = lambda w: pl.BlockSpec((tm, w), lambda i: (i, 0))
    full = lambda a: pl.BlockSpec(a.shape, lambda i: (0,) * a.ndim)
    small = [p["attn_out_g"], p["conv_out_g"], p["conv_w"], p["conv_b"], p["w_o"], p["norm2_g"],
             p["w_router"], p["b_router"]]
    return pl.pallas_call(
        functools.partial(_post_kernel, tm=tm, seq=seq),
        grid=(t // tm,),
        in_specs=[tile(D_MODEL), tile(ATTN_WIDTH), tile(CONV_WIDTH),
                  pl.BlockSpec((halo, CONV_WIDTH),
                               lambda i: (jnp.maximum(i * (tm // halo) - 1, 0), 0)),
                  pl.BlockSpec((halo, CONV_WIDTH),
                               lambda i: (jnp.minimum((i + 1) * (tm // halo), n_halo - 1), 0)),
                  tile(CONV_WIDTH),
                  pl.BlockSpec((None, 1, 6 * D_MODEL), row)] + [full(a) for a in small],
        out_specs=[tile(D_MODEL), pl.BlockSpec((tm * F32_SUBLANES, LANES), lambda i: (i, 0)),
                   pl.BlockSpec((F32_SUBLANES, tm), lambda i: (0, i))],
        out_shape=[jax.ShapeDtypeStruct((t, D_MODEL), f32),
                   jax.ShapeDtypeStruct((t * F32_SUBLANES, LANES), f32),
                   jax.ShapeDtypeStruct((F32_SUBLANES, t), f32)],
        compiler_params=_params("arbitrary"),
        name="post",
    )(x, attn, u, u, u, bg, mod3, *small)


def _max_chunks(blk):
    return -(-TOP_K * blk // MOE_CHUNK) + EXPERTS_PER_STEP


def _sorted_rows(blk):
    rows = TOP_K * blk + N_EXPERTS * (F32_SUBLANES - 1) + MOE_CHUNK
    assert rows % F32_SUBLANES == 0
    return rows + (F32_SUBLANES if rows // F32_SUBLANES % 2 == 0 else 0)


def _dispatch_tables(route, blk):
    nblk = route.shape[1] // blk
    n_slots = TOP_K * blk
    by_block = lambda a: a.reshape(TOP_K, nblk, blk).transpose(1, 0, 2)
    experts = by_block(route[0:TOP_K].astype(jnp.int32))
    weights = by_block(route[TOP_K:2 * TOP_K])
    ids = jnp.arange(N_EXPERTS, dtype=jnp.int32)
    onehot = experts.reshape(nblk, n_slots // LANES, LANES, 1) == ids
    earlier = jnp.tril(jnp.ones((LANES, LANES), bf16), -1)
    within = jnp.einsum("ij,bgje->bgie", earlier, onehot.astype(bf16),
                        preferred_element_type=f32)
    per_group = jnp.sum(onehot, axis=2, dtype=jnp.int32)
    before_group = jnp.cumsum(per_group, axis=1) - per_group
    counts = jnp.sum(per_group, axis=1)
    padded = (counts + F32_SUBLANES - 1) // F32_SUBLANES * F32_SUBLANES
    lo = jnp.cumsum(padded, axis=1) - padded
    row = within.astype(jnp.int32) + before_group[:, :, None, :] + lo[:, None, None, :]
    row = jnp.sum(jnp.where(onehot, row, 0), axis=-1).reshape(nblk, TOP_K, blk)

    eg, n_steps, max_c = EXPERTS_PER_STEP, N_EXPERTS // EXPERTS_PER_STEP, _max_chunks(blk)
    lo = lo.reshape(nblk, n_steps, eg)
    counts = counts.reshape(nblk, n_steps, eg)
    n_ch = (counts + MOE_CHUNK - 1) // MOE_CHUNK
    cum = jnp.cumsum(n_ch, axis=2)
    total = cum[..., -1]
    pos = jnp.arange(max_c, dtype=jnp.int32)
    j = jnp.sum(cum[:, :, None, :] <= pos[None, None, :, None], axis=-1, dtype=jnp.int32)
    j = jnp.minimum(j, eg - 1)
    hit = j[..., None] == jnp.arange(eg, dtype=jnp.int32)
    pick = lambda a: jnp.sum(jnp.where(hit, a[:, :, None, :], 0), axis=-1)
    done = (pos - pick(cum - n_ch)) * MOE_CHUNK
    live = pos < total[..., None]
    flat = lambda a: jnp.where(live, a, 0).reshape(nblk, 1, n_steps * max_c)
    return (row[:, 0:1], row[:, 1:2], weights[:, 0:1], weights[:, 1:2], flat(j),
            flat(pick(lo) + done), flat(jnp.clip(pick(counts) - done, 0, MOE_CHUNK)),
            total[:, None, :])


def _moe_kernel(row0_ref, row1_ref, w0_ref, w1_ref, cj_ref, cfirst_ref, cvalid_ref, nchunks_ref,
                h_ref, wg_ref, wu_ref, wd_ref, mod_ref, fg_ref, x1_hbm, out_hbm,
                sorted_ref, yt_ref, xbuf_ref, obuf_ref, sem_in, sem_out, *, blk):
    n_col = D_MODEL // LANES
    pitch = _sorted_rows(blk)
    blk_id, step = pl.program_id(0), pl.program_id(1)
    last_step = step == pl.num_programs(1) - 1
    max_c = _max_chunks(blk)
    n_sub = blk // FINAL_ROWS
    assignment_rows = (row0_ref, w0_ref), (row1_ref, w1_ref)

    def column_tile(row):
        return pl.ds(row, F32_SUBLANES, stride=pitch)

    def x1_copy(s):
        rows = pl.ds(blk_id * blk + s * FINAL_ROWS, FINAL_ROWS)
        slot = s % FINAL_SLOTS
        return pltpu.make_async_copy(x1_hbm.at[rows, :], xbuf_ref.at[slot], sem_in.at[slot])

    def out_copy(s):
        rows = pl.ds(blk_id * blk + s * FINAL_ROWS, FINAL_ROWS)
        slot = s % FINAL_SLOTS
        return pltpu.make_async_copy(obuf_ref.at[slot], out_hbm.at[rows, :], sem_out.at[slot])

    @pl.when(last_step)
    def _():
        for s in range(min(FINAL_SLOTS, n_sub)):
            x1_copy(s).start()

    @pl.when((blk_id == 0) & (step == 0))
    def _():
        zero_rows = F32_SUBLANES * LANES
        zeros = jnp.zeros((zero_rows, LANES), f32)

        def clear(i, carry):
            sorted_ref[pl.ds(pl.multiple_of(i * zero_rows, zero_rows), zero_rows), :] = zeros
            return carry

        lax.fori_loop(0, n_col * pitch // zero_rows, clear, 0)
        sorted_ref[n_col * pitch - zero_rows:, :] = zeros

    @pl.when(step == 0)
    def _():
        def dispatch(i, carry):
            first = pl.multiple_of(i * DISPATCH_ROWS * F32_SUBLANES, DISPATCH_ROWS * F32_SUBLANES)
            for r in range(DISPATCH_ROWS):
                tile = h_ref[pl.ds(first + r * F32_SUBLANES, F32_SUBLANES), :]
                for rows_ref, _ in assignment_rows:
                    sorted_ref[column_tile(rows_ref[0, i * DISPATCH_ROWS + r]), :] = tile
            return carry

        lax.fori_loop(0, blk // DISPATCH_ROWS, dispatch, 0)

    def chunk_rows(p, c):
        first = pl.multiple_of(cfirst_ref[0, p], F32_SUBLANES)
        return pl.ds(pl.multiple_of(c * pitch + first, F32_SUBLANES), MOE_CHUNK)

    def hidden(p):
        x = jnp.concatenate([sorted_ref[chunk_rows(p, c), :] for c in range(n_col)],
                            axis=1).astype(bf16)
        j = cj_ref[0, p]
        g = _dot(x, wg_ref[j])
        up = _dot(x, wu_ref[j])
        return ((g * jax.nn.sigmoid(g)) * up).astype(bf16)

    def project(p, act):
        out = _dot(act, wd_ref[cj_ref[0, p]])
        own = lax.broadcasted_iota(jnp.int32, (MOE_CHUNK, LANES), 0) < cvalid_ref[0, p]
        for c in range(n_col):
            rows = chunk_rows(p, c)
            sorted_ref[rows, :] = jnp.where(own, out[:, c * LANES:(c + 1) * LANES],
                                            sorted_ref[rows, :])

    def run_chunks(ps):
        acts = [hidden(p) for p in ps]
        for p, act in zip(ps, acts):
            project(p, act)

    def group(gi, carry):
        run_chunks([step * max_c + CHUNKS_PER_TRIP * gi + k for k in range(CHUNKS_PER_TRIP)])
        return carry

    n_chunks = nchunks_ref[0, step]
    lax.fori_loop(0, n_chunks // CHUNKS_PER_TRIP, group, 0)
    for left in range(1, CHUNKS_PER_TRIP):
        @pl.when(n_chunks % CHUNKS_PER_TRIP == left)
        def _():
            run_chunks([step * max_c + n_chunks - left + k for k in range(left)])

    @pl.when(last_step)
    def _():
        gate2 = mod_ref[:, G2:G2 + D_MODEL]
        for s in range(n_sub):
            slot = s % FINAL_SLOTS
            for r in range(FINAL_ROWS):
                t = s * FINAL_ROWS + r
                weighted = [sorted_ref[column_tile(rows_ref[0, t]), :] * w_ref[0, t]
                            for rows_ref, w_ref in assignment_rows]
                yt_ref[r * F32_SUBLANES:(r + 1) * F32_SUBLANES, :] = functools.reduce(
                    jnp.add, weighted)
            x1_copy(s).wait()
            if s >= FINAL_SLOTS:
                out_copy(s - FINAL_SLOTS).wait()
            y = jnp.concatenate([yt_ref[pl.ds(c, FINAL_ROWS, stride=F32_SUBLANES), :]
                                 for c in range(n_col)], axis=1)
            x2 = xbuf_ref[slot] + gate2 * y
            obuf_ref[slot] = _rms(x2) * fg_ref[...]
            out_copy(s).start()
            if s + FINAL_SLOTS < n_sub:
                x1_copy(s + FINAL_SLOTS).start()
        for s in range(max(n_sub - FINAL_SLOTS, 0), n_sub):
            out_copy(s).wait()


def _moe(h2_tiles, tables, x1, mod3, final_g, w_gate, w_up, w_down, *, seq, mod_row0,
         blk=MOE_BLOCK):
    t = h2_tiles.shape[0] // F32_SUBLANES
    eg, n_steps = EXPERTS_PER_STEP, N_EXPERTS // EXPERTS_PER_STEP
    blocks_per_seq = max(seq // blk, 1)
    row = lambda b, g: (mod_row0 + (b // blocks_per_seq if mod_row0 else 0), 0, 0)
    smem = lambda n: pl.BlockSpec((None, 1, n), lambda b, g: (b, 0, 0), memory_space=pltpu.SMEM)
    final_buf = pltpu.VMEM((FINAL_SLOTS, FINAL_ROWS, D_MODEL), f32)
    return pl.pallas_call(
        functools.partial(_moe_kernel, blk=blk),
        grid=(t // blk, n_steps),
        in_specs=[smem(blk)] * 4 + [smem(n_steps * _max_chunks(blk))] * 3 + [
                  smem(n_steps),
                  pl.BlockSpec((blk * F32_SUBLANES, LANES), lambda b, g: (b, 0)),
                  pl.BlockSpec((eg, D_MODEL, D_EXPERT), lambda b, g: (g, 0, 0)),
                  pl.BlockSpec((eg, D_MODEL, D_EXPERT), lambda b, g: (g, 0, 0)),
                  pl.BlockSpec((eg, D_EXPERT, D_MODEL), lambda b, g: (g, 0, 0)),
                  pl.BlockSpec((None, 1, 6 * D_MODEL), row),
                  pl.BlockSpec((1, D_MODEL), lambda b, g: (0, 0)),
                  pl.BlockSpec(memory_space=pl.ANY)],
        out_specs=pl.BlockSpec(memory_space=pl.ANY),
        out_shape=jax.ShapeDtypeStruct((t, D_MODEL), f32),
        scratch_shapes=[pltpu.VMEM((D_MODEL // LANES * _sorted_rows(blk), LANES), f32),
                        pltpu.VMEM((FINAL_ROWS * F32_SUBLANES, LANES), f32),
                        final_buf, final_buf,
                        pltpu.SemaphoreType.DMA((FINAL_SLOTS,)),
                        pltpu.SemaphoreType.DMA((FINAL_SLOTS,))],
        compiler_params=_params("arbitrary", "arbitrary", vmem=MOE_VMEM_LIMIT_BYTES),
        name="moe",
    )(*tables, h2_tiles, w_gate, w_up, w_down, mod3, final_g, x1)


def _mixers(x, mod3, p, *, seq, mod_row0, ctx_kv, cast=()):
    if ctx_kv is not None:
        q, k, vt, u, bg = _inproj(x, mod3, p["norm1_g"], p["w_in"], seq=seq, mod_row0=mod_row0,
                                  rope_tabs=_rope_tables(seq))
        attn, *cast = _latent_attention(q, k, vt, ctx_kv[0], ctx_kv[1], p["sink"], seq=seq,
                                        cast=cast)
        kv_t = None
    else:
        q, k, kt, vt, u, bg = _inproj(x, mod3, p["norm1_g"], p["w_in"], seq=seq,
                                      mod_row0=mod_row0)
        attn = _context_attention(q, k, vt, p["sink"], seq=seq)
        kv_t = (kt, vt)
    x1, h2_tiles, route = _post(x, attn, u, bg, mod3, p, seq=seq, mod_row0=mod_row0)
    return x1, h2_tiles, route, kv_t, cast


def _experts(x1, h2_tiles, tables, mod3, p, *, seq, mod_row0):
    out = _moe(h2_tiles, tables, x1, mod3, p["final_g"], p["w_gate"], p["w_up"], p["w_down"],
               seq=seq, mod_row0=mod_row0)
    return out.reshape(-1, seq, D_MODEL)


def kernel(x_prompt, x_sample, cache_k, cache_v, c, c_ctx, w_mod, b_mod, norm1_g, w_in, conv_w,
           conv_b, sink, attn_out_g, conv_out_g, w_o, norm2_g, w_coarse, b_coarse, w_fine, b_fine,
           w_gate, w_up, w_down, final_g):
    batch, seq, _ = x_prompt.shape
    dec_batch, dec_seq, _ = x_sample.shape
    past = cache_k.shape[2]
    assert w_mod.shape[0] == 1 and 1 + dec_batch <= MOD_ROWS

    cvecs = jnp.concatenate([c_ctx[None], c, jnp.zeros((MOD_ROWS - 1 - dec_batch, D_MODEL), f32)])
    mod3 = _modulation(cvecs, w_mod[0], b_mod[0]).reshape(MOD_ROWS, 1, 6 * D_MODEL)

    pad = jnp.zeros((D_MODEL, LANES - N_EXPERTS - N_GROUPS), f32)
    w_mix = jnp.concatenate([_permute_heads(w_o[0][:ATTN_WIDTH]), w_o[0][ATTN_WIDTH:]])
    p = {
        "norm1_g": norm1_g, "w_in": w_in[0].astype(bf16), "conv_w": conv_w[0], "conv_b": conv_b,
        "sink": sink[0], "attn_out_g": _permute_heads(attn_out_g[0])[None],
        "conv_out_g": conv_out_g, "w_o": w_mix.astype(bf16), "norm2_g": norm2_g,
        "w_router": _split_bf16(jnp.concatenate([w_fine[0], w_coarse[0], pad], axis=1)),
        "b_router": jnp.concatenate([b_fine[0], b_coarse[0], pad[0]])[None],
        "final_g": final_g[None],
    }

    x1_p, h2_p, route_p, kv_t, _ = _mixers(x_prompt.reshape(batch * seq, D_MODEL), mod3, p,
                                           seq=seq, mod_row0=0, ctx_kv=None)
    new_k, new_v = (a.reshape(batch, 1, N_KV_HEADS, HEAD_DIM, seq).transpose(0, 1, 4, 2, 3)
                    for a in kv_t)

    ctx_kv = (cache_k[:, 0].reshape(dec_batch, past, KV_WIDTH),
              cache_v[:, 0].reshape(dec_batch, past, KV_WIDTH))
    experts_f32 = (w_gate[0], w_up[0], w_down[0])
    ride = N_EXPERTS % (dec_batch * dec_seq // (LANES * LAT_ATTN_SUB_BLOCKS)) == 0
    x1_s, h2_s, route_s, _, experts_bf16 = _mixers(
        x_sample.reshape(dec_batch * dec_seq, D_MODEL), mod3, p, seq=dec_seq, mod_row0=1,
        ctx_kv=ctx_kv, cast=experts_f32 if ride else ())
    if not ride:
        experts_bf16 = [w.astype(bf16) for w in experts_f32]
    p = dict(p, w_gate=experts_bf16[0], w_up=experts_bf16[1], w_down=experts_bf16[2])

    n_blk_p = batch * seq // MOE_BLOCK
    tables = _dispatch_tables(jnp.concatenate([route_p, route_s], axis=1), MOE_BLOCK)
    y_prompt = _experts(x1_p, h2_p, [t[:n_blk_p] for t in tables], mod3, p, seq=seq, mod_row0=0)
    y_sample = _experts(x1_s, h2_s, [t[n_blk_p:] for t in tables], mod3, p, seq=dec_seq,
                        mod_row0=1)
    return y_prompt, y_sample, new_k, new_v
```

```python
import functools

import jax
import jax.numpy as jnp
import numpy as np
from jax import lax
from jax.experimental import pallas as pl
from jax.experimental.pallas import tpu as pltpu

D_MODEL = 1024
HEAD_DIM = 64
ATTN_WIDTH = 512
N_HEADS = 8
N_KV_HEADS = 2
Q_PER_KV = 4
KV_WIDTH = 128
CONV_WIDTH = 512
CONV_K = 3
WINDOW = 128
GRID_W = 64
ROPE_BASE = 10000.0
N_FREQ = 16
N_GROUPS = 4
EXPERTS_PER_GROUP = 8
N_EXPERTS = 32
TOP_K = 2
D_EXPERT = 256
IN_WIDTH = ATTN_WIDTH + 2 * KV_WIDTH + 3 * CONV_WIDTH
EPS = 1e-6
NEG = -1e30
SCALE = HEAD_DIM ** -0.5
LOG2_E = 1.4426950408889634

LANES = 128
F32_SUBLANES = 8
BF16_SUBLANES = 16
assert D_MODEL == F32_SUBLANES * LANES

INPROJ_ROWS = 512
INPROJ_SUB = 256
POST_SUB = 256
MOE_BLOCK = 2048
MOE_CHUNK = 160
CHUNKS_PER_TRIP = 4
DISPATCH_ROWS = 128
EXPERTS_PER_STEP = 4
WEIGHT_BUFFERS = 3
FINAL_ROWS = 256
FINAL_SLOTS = 4
MOE_VMEM_LIMIT_BYTES = 63 * 1024 * 1024
VMEM_LIMIT_BYTES = 48 * 1024 * 1024

SH1, SC1, G1, SH2, SC2, G2 = (i * D_MODEL for i in range(6))
MOD_ROWS = 8

COARSE_LANE0 = N_EXPERTS

f32 = jnp.float32
bf16 = jnp.bfloat16


def _params(*semantics, vmem=VMEM_LIMIT_BYTES):
    return pltpu.CompilerParams(dimension_semantics=semantics, vmem_limit_bytes=vmem)


def _rms(x):
    return x * lax.rsqrt(jnp.mean(x * x, axis=-1, keepdims=True) + EPS)


def _dot(a, b):
    return jnp.dot(a, b, preferred_element_type=f32)


def _split_bf16(w):
    hi = w.astype(bf16)
    lo = (w - hi.astype(f32)).astype(bf16)
    return jnp.concatenate([hi, lo], axis=1)


def _mod_kernel(cv_ref, w_ref, b_ref, o_ref):
    a = cv_ref[...]
    a = a * jax.nn.sigmoid(a)
    o_ref[...] = jnp.dot(a, w_ref[...], precision=lax.Precision.HIGHEST,
                         preferred_element_type=f32) + b_ref[...]


def _modulation(cvecs, w_mod, b_mod):
    tn = 2 * D_MODEL
    return pl.pallas_call(
        _mod_kernel,
        grid=(6 * D_MODEL // tn,),
        in_specs=[pl.BlockSpec((MOD_ROWS, D_MODEL), lambda j: (0, 0)),
                  pl.BlockSpec((D_MODEL, tn), lambda j: (0, j)),
                  pl.BlockSpec((1, tn), lambda j: (0, j))],
        out_specs=pl.BlockSpec((MOD_ROWS, tn), lambda j: (0, j)),
        out_shape=jax.ShapeDtypeStruct((MOD_ROWS, 6 * D_MODEL), f32),
        compiler_params=_params("arbitrary"),
        name="mod",
    )(cvecs, w_mod, b_mod.reshape(1, -1))


def _swap_halves(x):
    lane = lax.broadcasted_iota(jnp.int32, x.shape, 1)
    up = pltpu.roll(x, LANES - N_FREQ, axis=1)
    dn = pltpu.roll(x, N_FREQ, axis=1)
    return jnp.where((lane % (2 * N_FREQ)) < N_FREQ, up, dn)


def _rope(x, cos, sin):
    parts = []
    for c in range(x.shape[1] // LANES):
        xc = x[:, c * LANES:(c + 1) * LANES]
        parts.append(xc * cos + _swap_halves(xc) * sin)
    return parts[0] if len(parts) == 1 else jnp.concatenate(parts, axis=1)


def _inproj_kernel(*refs, rope, seq):
    if rope:
        x_ref, mod_ref, g_ref, w_ref, cos_ref, sin_ref, q_ref, k_ref, vt_ref, u_ref, bg_ref = refs
    else:
        x_ref, mod_ref, g_ref, w_ref, q_ref, k_ref, kt_ref, vt_ref, u_ref, bg_ref = refs
    tm = x_ref.shape[0]
    subs = [slice(s * INPROJ_SUB, (s + 1) * INPROJ_SUB) for s in range(tm // INPROJ_SUB)]
    zs = []
    for rs in subs:
        h = _rms(x_ref[rs, :]) * g_ref[...]
        h = h * (1.0 + mod_ref[:, SC1:SC1 + D_MODEL]) + mod_ref[:, SH1:SH1 + D_MODEL]
        zs.append(_dot(h.astype(bf16), w_ref[...]))
    for rs, z in zip(subs, zs):
        o = 0
        q = z[:, o:o + ATTN_WIDTH]; o += ATTN_WIDTH
        k = z[:, o:o + KV_WIDTH]; o += KV_WIDTH
        v = z[:, o:o + KV_WIDTH]; o += KV_WIDTH
        bg = z[:, o:o + CONV_WIDTH]; o += CONV_WIDTH
        cg = z[:, o:o + CONV_WIDTH]; o += CONV_WIDTH
        xin = z[:, o:o + CONV_WIDTH]
        if rope:
            q = _rope(q, cos_ref[rs, :], sin_ref[rs, :])
            k = _rope(k, cos_ref[rs, :], sin_ref[rs, :])
        q_ref[rs, :] = q.astype(q_ref.dtype)
        k_ref[rs, :] = k.astype(k_ref.dtype)
        if rope:
            vt_ref[:, rs] = v.T.astype(vt_ref.dtype)
        else:
            for s in range(rs.start // seq, rs.stop // seq):
                rows = slice(s * seq - rs.start, (s + 1) * seq - rs.start)
                kt_ref[s] = k[rows].T
                vt_ref[s] = v[rows].T
        u_ref[rs, :] = (cg * xin).astype(u_ref.dtype)
        bg_ref[rs, :] = bg.astype(bg_ref.dtype)


def _inproj(x, mod3, norm_g, w_in, *, seq, mod_row0, rope_tabs=None, tm=INPROJ_ROWS):
    t = x.shape[0]
    tiles_per_seq = seq // tm
    assert rope_tabs is not None or INPROJ_SUB % seq == 0
    row = lambda i: (mod_row0 + (i // tiles_per_seq if mod_row0 else 0), 0, 0)
    in_specs = [pl.BlockSpec((tm, D_MODEL), lambda i: (i, 0)),
                pl.BlockSpec((None, 1, 6 * D_MODEL), row),
                pl.BlockSpec((1, D_MODEL), lambda i: (0, 0)),
                pl.BlockSpec((D_MODEL, IN_WIDTH), lambda i: (0, 0))]
    args = [x, mod3, norm_g, w_in]
    if rope_tabs is not None:
        in_specs += [pl.BlockSpec((tm, LANES), lambda i: (i % tiles_per_seq, 0))] * 2
        args += list(rope_tabs)
    rows = lambda w: (pl.BlockSpec((tm, w), lambda i: (i, 0)), jax.ShapeDtypeStruct((t, w), bf16))
    if rope_tabs is not None:
        transposed = [(pl.BlockSpec((None, KV_WIDTH, tm),
                                    lambda i: (i // tiles_per_seq, 0, i % tiles_per_seq)),
                       jax.ShapeDtypeStruct((t // seq, KV_WIDTH, seq), bf16))]
    else:
        transposed = [(pl.BlockSpec((tm // seq, KV_WIDTH, seq), lambda i: (i, 0, 0)),
                       jax.ShapeDtypeStruct((t // seq, KV_WIDTH, seq), f32))] * 2
    outs = [rows(ATTN_WIDTH), rows(KV_WIDTH)] + transposed + [rows(CONV_WIDTH), rows(CONV_WIDTH)]
    out_specs, out_shape = (list(z) for z in zip(*outs))
    return pl.pallas_call(
        functools.partial(_inproj_kernel, rope=rope_tabs is not None, seq=seq),
        grid=(t // tm,),
        in_specs=in_specs,
        out_specs=out_specs,
        out_shape=out_shape,
        compiler_params=_params("arbitrary"),
        name="inproj_rope" if rope_tabs is not None else "inproj",
    )(*args)


def _rope_tables(n):
    t = np.arange(n)
    inv = ROPE_BASE ** (-np.arange(N_FREQ, dtype=np.float32) / N_FREQ)
    rows = (t // GRID_W).astype(np.float32)
    cols = (t % GRID_W).astype(np.float32)
    d = np.arange(LANES) % HEAD_DIM
    pos = np.where(d[None, :] < HEAD_DIM // 2, rows[:, None], cols[:, None])
    ang = jnp.asarray(pos.astype(np.float32) * inv[d % N_FREQ][None, :])
    sign = np.where((d % (2 * N_FREQ)) < N_FREQ, -1.0, 1.0).astype(np.float32)
    return jnp.cos(ang), jnp.sin(ang) * sign[None, :]


def _attention_core(sink_ref, q_ref, o_ref, subs, shared, bq):
    n_col = Q_PER_KV * bq
    col_head = lax.broadcasted_iota(jnp.int32, (1, n_col), 1) // bq
    sink_rows = []
    for g in range(N_KV_HEADS):
        sink_row = jnp.full((1, n_col), sink_ref[g * Q_PER_KV], f32)
        for j in range(1, Q_PER_KV):
            sink_row = jnp.where(col_head == j, sink_ref[g * Q_PER_KV + j], sink_row)
        sink_rows.append(sink_row * LOG2_E)
    ones_rows = jnp.ones((BF16_SUBLANES, 1), bf16)

    def with_ones(vals_t, g):
        return jnp.concatenate(
            [vals_t[g * HEAD_DIM:(g + 1) * HEAD_DIM, :],
             jnp.broadcast_to(ones_rows, (BF16_SUBLANES, vals_t.shape[1]))], axis=0)

    shared_vals = None if shared is None else [with_ones(shared[1], g) for g in range(N_KV_HEADS)]
    qts = [(q_ref[i * bq:(i + 1) * bq, :].astype(f32) * (SCALE * LOG2_E)).T.astype(bf16)
           for i in range(len(subs))]

    def scores(i, g):
        lo, hi = g * HEAD_DIM, (g + 1) * HEAD_DIM
        heads = range(g * Q_PER_KV, (g + 1) * Q_PER_KV)
        qt_g = jnp.concatenate([qts[i][h * HEAD_DIM:(h + 1) * HEAD_DIM, :] for h in heads], axis=1)
        s_own = _dot(subs[i]["keys"][:, lo:hi], qt_g)
        if subs[i]["mask"] is not None:
            s_own = jnp.where(subs[i]["mask"], s_own, NEG)
        s_shared = None if shared is None else _dot(shared[0][:, lo:hi], qt_g)
        return s_own, s_shared

    def weighted_values(i, g, s_own, s_shared):
        m = jnp.maximum(jnp.max(s_own, axis=0, keepdims=True), sink_rows[g])
        if s_shared is not None:
            m = jnp.maximum(m, jnp.max(s_shared, axis=0, keepdims=True))
        o_aug = _dot(with_ones(subs[i]["vals_t"], g), jnp.exp2(s_own - m).astype(bf16))
        if s_shared is not None:
            o_aug = o_aug + _dot(shared_vals[g], jnp.exp2(s_shared - m).astype(bf16))
        den = o_aug[HEAD_DIM:HEAD_DIM + 1, :] + jnp.exp2(sink_rows[g] - m)
        return o_aug[:HEAD_DIM, :] / den

    passes = [(i, g) for i in range(len(subs)) for g in range(N_KV_HEADS)]
    out_t = {}
    pending = scores(*passes[0])
    for n, (i, g) in enumerate(passes):
        nxt = scores(*passes[n + 1]) if n + 1 < len(passes) else None
        out_t[i, g] = weighted_values(i, g, *pending)
        pending = nxt
    for i in range(len(subs)):
        blocks = [jnp.concatenate([out_t[i, g][:, j * bq:(j + 1) * bq]
                                   for g in range(N_KV_HEADS)], axis=0).T
                  for j in range(Q_PER_KV)]
        o_ref[i * bq:(i + 1) * bq, :] = jnp.concatenate(blocks, axis=1).astype(o_ref.dtype)


ATTN_HEAD_PERM = np.array([(Q_PER_KV * g + j) * HEAD_DIM + d for j in range(Q_PER_KV)
                           for g in range(N_KV_HEADS) for d in range(HEAD_DIM)])


def _permute_heads(a):
    blocks = a.reshape(N_KV_HEADS, Q_PER_KV, HEAD_DIM, *a.shape[1:])
    return blocks.swapaxes(0, 1).reshape(a.shape)


def _ctx_attn_kernel(sink_ref, q_ref, k_ref, vt_ref, o_ref, *, bq, seq):
    subs = []
    for s in range(q_ref.shape[0] // seq):
        keys = k_ref[s * seq:(s + 1) * seq, :]
        subs += [dict(keys=keys, vals_t=vt_ref[s].astype(bf16), mask=None)] * (seq // bq)
    _attention_core(sink_ref, q_ref, o_ref, subs, None, bq)


def _context_attention(q, k, vt, sink, *, seq, bq=128, seqs_per_step=4):
    t = q.shape[0]
    rows = seq * seqs_per_step
    return pl.pallas_call(
        functools.partial(_ctx_attn_kernel, bq=bq, seq=seq),
        grid=(t // rows,),
        in_specs=[pl.BlockSpec(memory_space=pltpu.SMEM),
                  pl.BlockSpec((rows, ATTN_WIDTH), lambda b: (b, 0)),
                  pl.BlockSpec((rows, KV_WIDTH), lambda b: (b, 0)),
                  pl.BlockSpec((seqs_per_step, KV_WIDTH, seq), lambda b: (b, 0, 0))],
        out_specs=pl.BlockSpec((rows, ATTN_WIDTH), lambda b: (b, 0)),
        out_shape=jax.ShapeDtypeStruct((t, ATTN_WIDTH), bf16),
        compiler_params=_params("arbitrary"),
        name="ctx_attn",
    )(sink, q, k, vt)


def _lat_attn_kernel(sink_ref, q_ref, k_ref, vt_ref, ck_ref, cv_ref, *rest, bq, seq, n_cast):
    o_ref = rest[n_cast]
    for src_ref, dst_ref in zip(rest[:n_cast], rest[n_cast + 1:]):
        dst_ref[...] = src_ref[...].astype(dst_ref.dtype)
    band = bq + 2 * WINDOW
    n_col = Q_PER_KV * bq
    delta = ((lax.broadcasted_iota(jnp.int32, (band, n_col), 1) & (bq - 1))
             - lax.broadcasted_iota(jnp.int32, (band, n_col), 0))
    n_sub = q_ref.shape[0] // bq
    subs = []
    for sb in range(n_sub):
        i = pl.program_id(1) * n_sub + sb
        start = pl.multiple_of(jnp.clip(i * bq - WINDOW, 0, seq - band), LANES)
        shifted = delta + (i * bq - start + WINDOW)
        subs.append(dict(keys=k_ref[pl.ds(start, band), :],
                         vals_t=vt_ref[:, pl.ds(start, band)],
                         mask=shifted.astype(jnp.uint32) <= 2 * WINDOW))
    shared = (ck_ref[...].astype(bf16), cv_ref[...].T.astype(bf16))
    _attention_core(sink_ref, q_ref, o_ref, subs, shared, bq)


LAT_ATTN_SUB_BLOCKS = 8


def _latent_attention(q, k, vt, ck, cv, sink, *, seq, cast=(), bq=128,
                      sub_blocks=LAT_ATTN_SUB_BLOCKS):
    nb, past = ck.shape[0], ck.shape[1]
    assert bq == LANES and N_KV_HEADS * HEAD_DIM == LANES
    q3, k3 = (a.reshape(nb, seq, a.shape[-1]) for a in (q, k))
    tq = bq * sub_blocks
    steps_per_seq = seq // tq
    n_steps = nb * steps_per_seq
    cast3 = [a.reshape((n_steps, -1) + a.shape[2:]) for a in cast]
    cast_specs = [pl.BlockSpec((None,) + a.shape[1:], lambda b, i: (b * steps_per_seq + i, 0, 0))
                  for a in cast3]
    out, *cast_out = pl.pallas_call(
        functools.partial(_lat_attn_kernel, bq=bq, seq=seq, n_cast=len(cast)),
        grid=(nb, steps_per_seq),
        in_specs=[pl.BlockSpec(memory_space=pltpu.SMEM),
                  pl.BlockSpec((None, tq, ATTN_WIDTH), lambda b, i: (b, i, 0)),
                  pl.BlockSpec((None, seq, KV_WIDTH), lambda b, i: (b, 0, 0)),
                  pl.BlockSpec((None, KV_WIDTH, seq), lambda b, i: (b, 0, 0)),
                  pl.BlockSpec((None, past, KV_WIDTH), lambda b, i: (b, 0, 0)),
                  pl.BlockSpec((None, past, KV_WIDTH), lambda b, i: (b, 0, 0))] + cast_specs,
        out_specs=[pl.BlockSpec((None, tq, ATTN_WIDTH), lambda b, i: (b, i, 0))] + cast_specs,
        out_shape=[jax.ShapeDtypeStruct((nb, seq, ATTN_WIDTH), bf16)]
        + [jax.ShapeDtypeStruct(a.shape, bf16) for a in cast3],
        compiler_params=_params("arbitrary", "arbitrary"),
        name="lat_attn",
    )(sink, q3, k3, vt, ck, cv, *cast3)
    return [out.reshape(nb * seq, ATTN_WIDTH)] + [o.reshape(a.shape) for o, a in zip(cast_out, cast)]


def _route(logits_t):
    r = logits_t.shape[1]
    big = jnp.float32(LANES)
    crow = lax.broadcasted_iota(jnp.int32, (F32_SUBLANES, r), 0)
    crow_f = crow.astype(f32)
    lc = jnp.where(crow < N_GROUPS, logits_t[COARSE_LANE0:COARSE_LANE0 + F32_SUBLANES, :], -jnp.inf)
    mc = jnp.max(lc, axis=0, keepdims=True)
    grp = jnp.min(jnp.where(lc == mc, crow_f, big), axis=0, keepdims=True)
    pg = 1.0 / jnp.sum(jnp.exp(lc - mc), axis=0, keepdims=True)
    erow = lax.broadcasted_iota(jnp.int32, (N_EXPERTS, r), 0).astype(f32)
    in_g = jnp.floor(erow * (1.0 / EXPERTS_PER_GROUP)) == grp
    fl = jnp.where(in_g, logits_t[0:N_EXPERTS, :], -jnp.inf)
    m1 = jnp.max(fl, axis=0, keepdims=True)
    i1 = jnp.min(jnp.where(fl == m1, erow, big), axis=0, keepdims=True)
    fl2 = jnp.where(erow == i1, -jnp.inf, fl)
    m2 = jnp.max(fl2, axis=0, keepdims=True)
    i2 = jnp.min(jnp.where(fl2 == m2, erow, big), axis=0, keepdims=True)
    e2 = jnp.exp(m2 - m1)
    p1 = pg / (1.0 + e2)
    p2 = pg * e2 / (1.0 + e2)
    packed = jnp.where(crow == 0, i1, jnp.where(crow == 1, i2, jnp.where(crow == 2, p1, p2)))
    return jnp.where(crow < 4, packed, 0.0)


def _post_kernel(x_ref, attn_ref, u_ref, up_ref, un_ref, bg_ref, mod_ref, ag_ref, cg_ref, cw_ref,
                 cb_ref, wo_ref, n2_ref, wr_ref, br_ref, x1_ref, h2_ref, route_ref, *, tm, seq):
    i = pl.program_id(0)
    u = u_ref[...].astype(f32)
    rows = lax.broadcasted_iota(jnp.int32, (tm, 1), 0)
    spos = (i * tm + rows) % seq
    u_dn = jnp.where(rows == 0, up_ref[...].astype(f32)[BF16_SUBLANES - 1:, :],
                     pltpu.roll(u, 1, axis=0))
    u_dn = jnp.where(spos == 0, 0.0, u_dn)
    u_up = jnp.where(rows == tm - 1, un_ref[...].astype(f32)[0:1, :], pltpu.roll(u, tm - 1, axis=0))
    u_up = jnp.where(spos == seq - 1, 0.0, u_up)
    y = u_dn * cw_ref[0:1, :] + u * cw_ref[1:2, :] + u_up * cw_ref[2:3, :] + cb_ref[...]
    conv = bg_ref[...].astype(f32) * y
    attn_n = (_rms(attn_ref[...].astype(f32)) * ag_ref[...]).astype(bf16)
    conv_n = (_rms(conv) * cg_ref[...]).astype(bf16)
    subs = [slice(s * POST_SUB, (s + 1) * POST_SUB) for s in range(tm // POST_SUB)]
    mixed = [_dot(attn_n[rs], wo_ref[0:ATTN_WIDTH, :]) + _dot(conv_n[rs], wo_ref[ATTN_WIDTH:, :])
             for rs in subs]
    logits = []
    for s, rs in enumerate(subs):
        x1 = x_ref[rs, :] + mod_ref[:, G1:G1 + D_MODEL] * mixed[s]
        x1_ref[rs, :] = x1
        h2 = _rms(x1) * n2_ref[...]
        h2 = h2 * (1.0 + mod_ref[:, SC2:SC2 + D_MODEL]) + mod_ref[:, SH2:SH2 + D_MODEL]
        for c in range(D_MODEL // LANES):
            h2_ref[pl.ds(rs.start * F32_SUBLANES + c, POST_SUB, stride=F32_SUBLANES), :] = (
                h2[:, c * LANES:(c + 1) * LANES])
        h2_hi = h2.astype(bf16)
        h2_lo = (h2 - h2_hi.astype(f32)).astype(bf16)
        both = _dot(h2_hi, wr_ref[...])
        logits.append(both[:, :LANES] + both[:, LANES:] + _dot(h2_lo, wr_ref[:, :LANES]))
    for s, rs in enumerate(subs):
        route_ref[:, rs] = _route((logits[s] + br_ref[...]).T)


def _post(x, attn, u, bg, mod3, p, *, seq, mod_row0, tm=4 * POST_SUB):
    t = x.shape[0]
    tiles_per_seq = max(seq // tm, 1)
    halo = BF16_SUBLANES
    n_halo = t // halo
    row = lambda i: (mod_row0 + (i // tiles_per_seq if mod_row0 else 0), 0, 0)
    tile = lambda w: pl.BlockSpec((tm, w), lambda i: (i, 0))
    full = lambda a: pl.BlockSpec(a.shape, lambda i: (0,) * a.ndim)
    small = [p["attn_out_g"], p["conv_out_g"], p["conv_w"], p["conv_b"], p["w_o"], p["norm2_g"],
             p["w_router"], p["b_router"]]
    return pl.pallas_call(
        functools.partial(_post_kernel, tm=tm, seq=seq),
        grid=(t // tm,),
        in_specs=[tile(D_MODEL), tile(ATTN_WIDTH), tile(CONV_WIDTH),
                  pl.BlockSpec((halo, CONV_WIDTH),
                               lambda i: (jnp.maximum(i * (tm // halo) - 1, 0), 0)),
                  pl.BlockSpec((halo, CONV_WIDTH),
                               lambda i: (jnp.minimum((i + 1) * (tm // halo), n_halo - 1), 0)),
                  tile(CONV_WIDTH),
                  pl.BlockSpec((None, 1, 6 * D_MODEL), row)] + [full(a) for a in small],
        out_specs=[tile(D_MODEL), pl.BlockSpec((tm * F32_SUBLANES, LANES), lambda i: (i, 0)),
                   pl.BlockSpec((F32_SUBLANES, tm), lambda i: (0, i))],
        out_shape=[jax.ShapeDtypeStruct((t, D_MODEL), f32),
                   jax.ShapeDtypeStruct((t * F32_SUBLANES, LANES), f32),
                   jax.ShapeDtypeStruct((F32_SUBLANES, t), f32)],
        compiler_params=_params("arbitrary"),
        name="post",
    )(x, attn, u, u, u, bg, mod3, *small)


def _max_chunks(blk):
    return -(-TOP_K * blk // MOE_CHUNK) + EXPERTS_PER_STEP


def _sorted_rows(blk):
    rows = TOP_K * blk + N_EXPERTS * (F32_SUBLANES - 1) + MOE_CHUNK
    assert rows % F32_SUBLANES == 0
    return rows + (F32_SUBLANES if rows // F32_SUBLANES % 2 == 0 else 0)


def _dispatch_tables(route, blk):
    nblk = route.shape[1] // blk
    n_slots = TOP_K * blk
    by_block = lambda a: a.reshape(TOP_K, nblk, blk).transpose(1, 0, 2)
    experts = by_block(route[0:TOP_K].astype(jnp.int32))
    weights = by_block(route[TOP_K:2 * TOP_K])
    ids = jnp.arange(N_EXPERTS, dtype=jnp.int32)
    onehot = experts.reshape(nblk, n_slots // LANES, LANES, 1) == ids
    earlier = jnp.tril(jnp.ones((LANES, LANES), bf16), -1)
    within = jnp.einsum("ij,bgje->bgie", earlier, onehot.astype(bf16),
                        preferred_element_type=f32)
    per_group = jnp.sum(onehot, axis=2, dtype=jnp.int32)
    before_group = jnp.cumsum(per_group, axis=1) - per_group
    counts = jnp.sum(per_group, axis=1)
    padded = (counts + F32_SUBLANES - 1) // F32_SUBLANES * F32_SUBLANES
    lo = jnp.cumsum(padded, axis=1) - padded
    row = within.astype(jnp.int32) + before_group[:, :, None, :] + lo[:, None, None, :]
    row = jnp.sum(jnp.where(onehot, row, 0), axis=-1).reshape(nblk, TOP_K, blk)

    eg, n_steps, max_c = EXPERTS_PER_STEP, N_EXPERTS // EXPERTS_PER_STEP, _max_chunks(blk)
    lo = lo.reshape(nblk, n_steps, eg)
    counts = counts.reshape(nblk, n_steps, eg)
    n_ch = (counts + MOE_CHUNK - 1) // MOE_CHUNK
    cum = jnp.cumsum(n_ch, axis=2)
    total = cum[..., -1]
    pos = jnp.arange(max_c, dtype=jnp.int32)
    j = jnp.sum(cum[:, :, None, :] <= pos[None, None, :, None], axis=-1, dtype=jnp.int32)
    j = jnp.minimum(j, eg - 1)
    hit = j[..., None] == jnp.arange(eg, dtype=jnp.int32)
    pick = lambda a: jnp.sum(jnp.where(hit, a[:, :, None, :], 0), axis=-1)
    done = (pos - pick(cum - n_ch)) * MOE_CHUNK
    live = pos < total[..., None]
    flat = lambda a: jnp.where(live, a, 0).reshape(nblk, 1, n_steps * max_c)
    return (row[:, 0:1], row[:, 1:2], weights[:, 0:1], weights[:, 1:2], flat(j),
            flat(pick(lo) + done), flat(jnp.clip(pick(counts) - done, 0, MOE_CHUNK)),
            total[:, None, :])


def _moe_kernel(row0_ref, row1_ref, w0_ref, w1_ref, cj_ref, cfirst_ref, cvalid_ref, nchunks_ref,
                h_ref, mod_ref, fg_ref, wg_hbm, wu_hbm, wd_hbm, x1_hbm, out_hbm,
                sorted_ref, yt_ref, wg_ref, wu_ref, wd_ref, xbuf_ref, obuf_ref, sem_w, sem_in,
                sem_out, *, blk):
    n_col = D_MODEL // LANES
    pitch = _sorted_rows(blk)
    blk_id, step = pl.program_id(0), pl.program_id(1)
    n_steps = pl.num_programs(1)
    last_step = step == n_steps - 1
    max_c = _max_chunks(blk)
    n_sub = blk // FINAL_ROWS
    assignment_rows = (row0_ref, w0_ref), (row1_ref, w1_ref)
    run = blk_id * n_steps + step
    ahead = WEIGHT_BUFFERS - 1

    def weight_copies(r):
        slot = r % WEIGHT_BUFFERS
        experts = pl.ds(r % n_steps * EXPERTS_PER_STEP, EXPERTS_PER_STEP)
        pairs = (wg_hbm, wg_ref), (wu_hbm, wu_ref), (wd_hbm, wd_ref)
        return [pltpu.make_async_copy(hbm.at[experts], buf.at[slot], sem_w.at[k, slot])
                for k, (hbm, buf) in enumerate(pairs)]

    @pl.when(run == 0)
    def _():
        for r in range(ahead):
            for copy in weight_copies(r):
                copy.start()

    @pl.when(run + ahead < pl.num_programs(0) * n_steps)
    def _():
        for copy in weight_copies(run + ahead):
            copy.start()

    def column_tile(row):
        return pl.ds(row, F32_SUBLANES, stride=pitch)

    def x1_copy(s):
        rows = pl.ds(blk_id * blk + s * FINAL_ROWS, FINAL_ROWS)
        slot = s % FINAL_SLOTS
        return pltpu.make_async_copy(x1_hbm.at[rows, :], xbuf_ref.at[slot], sem_in.at[slot])

    def out_copy(s):
        rows = pl.ds(blk_id * blk + s * FINAL_ROWS, FINAL_ROWS)
        slot = s % FINAL_SLOTS
        return pltpu.make_async_copy(obuf_ref.at[slot], out_hbm.at[rows, :], sem_out.at[slot])

    @pl.when(last_step)
    def _():
        for s in range(min(FINAL_SLOTS, n_sub)):
            x1_copy(s).start()

    @pl.when((blk_id == 0) & (step == 0))
    def _():
        zero_rows = F32_SUBLANES * LANES
        zeros = jnp.zeros((zero_rows, LANES), f32)

        def clear(i, carry):
            sorted_ref[pl.ds(pl.multiple_of(i * zero_rows, zero_rows), zero_rows), :] = zeros
            return carry

        lax.fori_loop(0, n_col * pitch // zero_rows, clear, 0)
        sorted_ref[n_col * pitch - zero_rows:, :] = zeros

    @pl.when(step == 0)
    def _():
        def dispatch(i, carry):
            first = pl.multiple_of(i * DISPATCH_ROWS * F32_SUBLANES, DISPATCH_ROWS * F32_SUBLANES)
            for r in range(DISPATCH_ROWS):
                tile = h_ref[pl.ds(first + r * F32_SUBLANES, F32_SUBLANES), :]
                for rows_ref, _ in assignment_rows:
                    sorted_ref[column_tile(rows_ref[0, i * DISPATCH_ROWS + r]), :] = tile
            return carry

        lax.fori_loop(0, blk // DISPATCH_ROWS, dispatch, 0)

    def chunk_rows(p, c):
        first = pl.multiple_of(cfirst_ref[0, p], F32_SUBLANES)
        return pl.ds(pl.multiple_of(c * pitch + first, F32_SUBLANES), MOE_CHUNK)

    def hidden(p):
        x = jnp.concatenate([sorted_ref[chunk_rows(p, c), :] for c in range(n_col)],
                            axis=1).astype(bf16)
        j = cj_ref[0, p]
        g = _dot(x, wg_ref[run % WEIGHT_BUFFERS, j])
        up = _dot(x, wu_ref[run % WEIGHT_BUFFERS, j])
        return ((g * jax.nn.sigmoid(g)) * up).astype(bf16)

    def project(p, act):
        out = _dot(act, wd_ref[run % WEIGHT_BUFFERS, cj_ref[0, p]])
        own = lax.broadcasted_iota(jnp.int32, (MOE_CHUNK, LANES), 0) < cvalid_ref[0, p]
        for c in range(n_col):
            rows = chunk_rows(p, c)
            sorted_ref[rows, :] = jnp.where(own, out[:, c * LANES:(c + 1) * LANES],
                                            sorted_ref[rows, :])

    def run_chunks(ps):
        acts = [hidden(p) for p in ps]
        for p, act in zip(ps, acts):
            project(p, act)

    def group(gi, carry):
        run_chunks([step * max_c + CHUNKS_PER_TRIP * gi + k for k in range(CHUNKS_PER_TRIP)])
        return carry

    for copy in weight_copies(run):
        copy.wait()
    n_chunks = nchunks_ref[0, step]
    lax.fori_loop(0, n_chunks // CHUNKS_PER_TRIP, group, 0)
    for left in range(1, CHUNKS_PER_TRIP):
        @pl.when(n_chunks % CHUNKS_PER_TRIP == left)
        def _():
            run_chunks([step * max_c + n_chunks - left + k for k in range(left)])

    @pl.when(last_step)
    def _():
        gate2 = mod_ref[:, G2:G2 + D_MODEL]
        for s in range(n_sub):
            slot = s % FINAL_SLOTS
            for r in range(FINAL_ROWS):
                t = s * FINAL_ROWS + r
                weighted = [sorted_ref[column_tile(rows_ref[0, t]), :] * w_ref[0, t]
                            for rows_ref, w_ref in assignment_rows]
                yt_ref[r * F32_SUBLANES:(r + 1) * F32_SUBLANES, :] = functools.reduce(
                    jnp.add, weighted)
            x1_copy(s).wait()
            if s >= FINAL_SLOTS:
                out_copy(s - FINAL_SLOTS).wait()
            y = jnp.concatenate([yt_ref[pl.ds(c, FINAL_ROWS, stride=F32_SUBLANES), :]
                                 for c in range(n_col)], axis=1)
            x2 = xbuf_ref[slot] + gate2 * y
            obuf_ref[slot] = _rms(x2) * fg_ref[...]
            out_copy(s).start()
            if s + FINAL_SLOTS < n_sub:
                x1_copy(s + FINAL_SLOTS).start()
        for s in range(max(n_sub - FINAL_SLOTS, 0), n_sub):
            out_copy(s).wait()


def _moe(h2_tiles, tables, x1, mod3, final_g, w_gate, w_up, w_down, *, seq, mod_row0,
         blk=MOE_BLOCK):
    t = h2_tiles.shape[0] // F32_SUBLANES
    eg, n_steps = EXPERTS_PER_STEP, N_EXPERTS // EXPERTS_PER_STEP
    blocks_per_seq = max(seq // blk, 1)
    row = lambda b, g: (mod_row0 + (b // blocks_per_seq if mod_row0 else 0), 0, 0)
    smem = lambda n: pl.BlockSpec((None, 1, n), lambda b, g: (b, 0, 0), memory_space=pltpu.SMEM)
    final_buf = pltpu.VMEM((FINAL_SLOTS, FINAL_ROWS, D_MODEL), f32)
    assert t // blk * n_steps >= WEIGHT_BUFFERS - 1
    weight_buf = lambda w: pltpu.VMEM((WEIGHT_BUFFERS, eg) + w.shape[1:], w.dtype)
    return pl.pallas_call(
        functools.partial(_moe_kernel, blk=blk),
        grid=(t // blk, n_steps),
        in_specs=[smem(blk)] * 4 + [smem(n_steps * _max_chunks(blk))] * 3 + [
                  smem(n_steps),
                  pl.BlockSpec((blk * F32_SUBLANES, LANES), lambda b, g: (b, 0)),
                  pl.BlockSpec((None, 1, 6 * D_MODEL), row),
                  pl.BlockSpec((1, D_MODEL), lambda b, g: (0, 0))]
        + [pl.BlockSpec(memory_space=pl.ANY)] * 4,
        out_specs=pl.BlockSpec(memory_space=pl.ANY),
        out_shape=jax.ShapeDtypeStruct((t, D_MODEL), f32),
        scratch_shapes=[pltpu.VMEM((D_MODEL // LANES * _sorted_rows(blk), LANES), f32),
                        pltpu.VMEM((FINAL_ROWS * F32_SUBLANES, LANES), f32),
                        weight_buf(w_gate), weight_buf(w_up), weight_buf(w_down),
                        final_buf, final_buf,
                        pltpu.SemaphoreType.DMA((3, WEIGHT_BUFFERS)),
                        pltpu.SemaphoreType.DMA((FINAL_SLOTS,)),
                        pltpu.SemaphoreType.DMA((FINAL_SLOTS,))],
        compiler_params=_params("arbitrary", "arbitrary", vmem=MOE_VMEM_LIMIT_BYTES),
        name="moe",
    )(*tables, h2_tiles, mod3, final_g, w_gate, w_up, w_down, x1)


def _mixers(x, mod3, p, *, seq, mod_row0, ctx_kv, cast=()):
    if ctx_kv is not None:
        q, k, vt, u, bg = _inproj(x, mod3, p["norm1_g"], p["w_in"], seq=seq, mod_row0=mod_row0,
                                  rope_tabs=_rope_tables(seq))
        attn, *cast = _latent_attention(q, k, vt, ctx_kv[0], ctx_kv[1], p["sink"], seq=seq,
                                        cast=cast)
        kv_t = None
    else:
        q, k, kt, vt, u, bg = _inproj(x, mod3, p["norm1_g"], p["w_in"], seq=seq,
                                      mod_row0=mod_row0)
        attn = _context_attention(q, k, vt, p["sink"], seq=seq)
        kv_t = (kt, vt)
    x1, h2_tiles, route = _post(x, attn, u, bg, mod3, p, seq=seq, mod_row0=mod_row0)
    return x1, h2_tiles, route, kv_t, cast


def _experts(x1, h2_tiles, tables, mod3, p, *, seq, mod_row0):
    out = _moe(h2_tiles, tables, x1, mod3, p["final_g"], p["w_gate"], p["w_up"], p["w_down"],
               seq=seq, mod_row0=mod_row0)
    return out.reshape(-1, seq, D_MODEL)


def kernel(x_prompt, x_sample, cache_k, cache_v, c, c_ctx, w_mod, b_mod, norm1_g, w_in, conv_w,
           conv_b, sink, attn_out_g, conv_out_g, w_o, norm2_g, w_coarse, b_coarse, w_fine, b_fine,
           w_gate, w_up, w_down, final_g):
    batch, seq, _ = x_prompt.shape
    dec_batch, dec_seq, _ = x_sample.shape
    past = cache_k.shape[2]
    assert w_mod.shape[0] == 1 and 1 + dec_batch <= MOD_ROWS

    cvecs = jnp.concatenate([c_ctx[None], c, jnp.zeros((MOD_ROWS - 1 - dec_batch, D_MODEL), f32)])
    mod3 = _modulation(cvecs, w_mod[0], b_mod[0]).reshape(MOD_ROWS, 1, 6 * D_MODEL)

    pad = jnp.zeros((D_MODEL, LANES - N_EXPERTS - N_GROUPS), f32)
    w_mix = jnp.concatenate([_permute_heads(w_o[0][:ATTN_WIDTH]), w_o[0][ATTN_WIDTH:]])
    p = {
        "norm1_g": norm1_g, "w_in": w_in[0].astype(bf16), "conv_w": conv_w[0], "conv_b": conv_b,
        "sink": sink[0], "attn_out_g": _permute_heads(attn_out_g[0])[None],
        "conv_out_g": conv_out_g, "w_o": w_mix.astype(bf16), "norm2_g": norm2_g,
        "w_router": _split_bf16(jnp.concatenate([w_fine[0], w_coarse[0], pad], axis=1)),
        "b_router": jnp.concatenate([b_fine[0], b_coarse[0], pad[0]])[None],
        "final_g": final_g[None],
    }

    x1_p, h2_p, route_p, kv_t, _ = _mixers(x_prompt.reshape(batch * seq, D_MODEL), mod3, p,
                                           seq=seq, mod_row0=0, ctx_kv=None)
    new_k, new_v = (a.reshape(batch, 1, N_KV_HEADS, HEAD_DIM, seq).transpose(0, 1, 4, 2, 3)
                    for a in kv_t)

    ctx_kv = (cache_k[:, 0].reshape(dec_batch, past, KV_WIDTH),
              cache_v[:, 0].reshape(dec_batch, past, KV_WIDTH))
    experts_f32 = (w_gate[0], w_up[0], w_down[0])
    ride = N_EXPERTS % (dec_batch * dec_seq // (LANES * LAT_ATTN_SUB_BLOCKS)) == 0
    x1_s, h2_s, route_s, _, experts_bf16 = _mixers(
        x_sample.reshape(dec_batch * dec_seq, D_MODEL), mod3, p, seq=dec_seq, mod_row0=1,
        ctx_kv=ctx_kv, cast=experts_f32 if ride else ())
    if not ride:
        experts_bf16 = [w.astype(bf16) for w in experts_f32]
    p = dict(p, w_gate=experts_bf16[0], w_up=experts_bf16[1], w_down=experts_bf16[2])

    n_blk_p = batch * seq // MOE_BLOCK
    tables = _dispatch_tables(jnp.concatenate([route_p, route_s], axis=1), MOE_BLOCK)
    y_prompt = _experts(x1_p, h2_p, [t[:n_blk_p] for t in tables], mod3, p, seq=seq, mod_row0=0)
    y_sample = _experts(x1_s, h2_s, [t[n_blk_p:] for t in tables], mod3, p, seq=dec_seq,
                        mod_row0=1)
    return y_prompt, y_sample, new_k, new_v
```

```python
import functools

import jax
import jax.numpy as jnp
import numpy as np
from jax import lax
from jax.experimental import pallas as pl
from jax.experimental.pallas import tpu as pltpu

D_MODEL = 1024
HEAD_DIM = 64
ATTN_WIDTH = 512
N_HEADS = 8
N_KV_HEADS = 2
Q_PER_KV = 4
KV_WIDTH = 128
CONV_WIDTH = 512
CONV_K = 3
WINDOW = 128
GRID_W = 64
ROPE_BASE = 10000.0
N_FREQ = 16
N_GROUPS = 4
EXPERTS_PER_GROUP = 8
N_EXPERTS = 32
TOP_K = 2
D_EXPERT = 256
IN_WIDTH = ATTN_WIDTH + 2 * KV_WIDTH + 3 * CONV_WIDTH
EPS = 1e-6
NEG = -1e30
SCALE = HEAD_DIM ** -0.5
LOG2_E = 1.4426950408889634

LANES = 128
F32_SUBLANES = 8
BF16_SUBLANES = 16
assert D_MODEL == F32_SUBLANES * LANES

INPROJ_ROWS = 1024
INPROJ_SUB = 256
POST_SUB = 256
MOE_BLOCK = 2048
MOE_CHUNK = 160
CHUNKS_PER_TRIP = 4
DISPATCH_ROWS = 128
EXPERTS_PER_STEP = 4
WEIGHT_BUFFERS = 3
FINAL_ROWS = 256
FINAL_SLOTS = 4
MOE_VMEM_LIMIT_BYTES = 60 * 1024 * 1024
VMEM_LIMIT_BYTES = 48 * 1024 * 1024

SH1, SC1, G1, SH2, SC2, G2 = (i * D_MODEL for i in range(6))
MOD_ROWS = 8

COARSE_LANE0 = N_EXPERTS

f32 = jnp.float32
bf16 = jnp.bfloat16


def _params(*semantics, vmem=VMEM_LIMIT_BYTES):
    return pltpu.CompilerParams(dimension_semantics=semantics, vmem_limit_bytes=vmem)


def _rms(x):
    return x * lax.rsqrt(jnp.mean(x * x, axis=-1, keepdims=True) + EPS)


def _dot(a, b):
    return jnp.dot(a, b, preferred_element_type=f32)


def _split_bf16(w):
    hi = w.astype(bf16)
    lo = (w - hi.astype(f32)).astype(bf16)
    return jnp.concatenate([hi, lo], axis=1)


def _mod_kernel(cv_ref, w_ref, b_ref, o_ref):
    a = cv_ref[...]
    a = a * jax.nn.sigmoid(a)
    o_ref[...] = jnp.dot(a, w_ref[...], precision=lax.Precision.HIGHEST,
                         preferred_element_type=f32) + b_ref[...]


def _modulation(cvecs, w_mod, b_mod):
    tn = 2 * D_MODEL
    return pl.pallas_call(
        _mod_kernel,
        grid=(6 * D_MODEL // tn,),
        in_specs=[pl.BlockSpec((MOD_ROWS, D_MODEL), lambda j: (0, 0)),
                  pl.BlockSpec((D_MODEL, tn), lambda j: (0, j)),
                  pl.BlockSpec((1, tn), lambda j: (0, j))],
        out_specs=pl.BlockSpec((MOD_ROWS, tn), lambda j: (0, j)),
        out_shape=jax.ShapeDtypeStruct((MOD_ROWS, 6 * D_MODEL), f32),
        compiler_params=_params("arbitrary"),
        name="mod",
    )(cvecs, w_mod, b_mod.reshape(1, -1))


def _swap_halves(x):
    lane = lax.broadcasted_iota(jnp.int32, x.shape, 1)
    up = pltpu.roll(x, LANES - N_FREQ, axis=1)
    dn = pltpu.roll(x, N_FREQ, axis=1)
    return jnp.where((lane % (2 * N_FREQ)) < N_FREQ, up, dn)


def _rope(x, cos, sin):
    parts = []
    for c in range(x.shape[1] // LANES):
        xc = x[:, c * LANES:(c + 1) * LANES]
        parts.append(xc * cos + _swap_halves(xc) * sin)
    return parts[0] if len(parts) == 1 else jnp.concatenate(parts, axis=1)


def _inproj_kernel(*refs, rope, seq):
    if rope:
        x_ref, mod_ref, g_ref, w_ref, cos_ref, sin_ref, q_ref, k_ref, vt_ref, u_ref, bg_ref = refs
    else:
        x_ref, mod_ref, g_ref, w_ref, q_ref, k_ref, kt_ref, vt_ref, u_ref, bg_ref = refs
    tm = x_ref.shape[0]
    subs = [slice(s * INPROJ_SUB, (s + 1) * INPROJ_SUB) for s in range(tm // INPROJ_SUB)]
    zs = []
    for rs in subs:
        h = _rms(x_ref[rs, :]) * g_ref[...]
        h = h * (1.0 + mod_ref[:, SC1:SC1 + D_MODEL]) + mod_ref[:, SH1:SH1 + D_MODEL]
        zs.append(_dot(h.astype(bf16), w_ref[...]))
    for rs, z in zip(subs, zs):
        o = 0
        q = z[:, o:o + ATTN_WIDTH]; o += ATTN_WIDTH
        k = z[:, o:o + KV_WIDTH]; o += KV_WIDTH
        v = z[:, o:o + KV_WIDTH]; o += KV_WIDTH
        bg = z[:, o:o + CONV_WIDTH]; o += CONV_WIDTH
        cg = z[:, o:o + CONV_WIDTH]; o += CONV_WIDTH
        xin = z[:, o:o + CONV_WIDTH]
        if rope:
            q = _rope(q, cos_ref[rs, :], sin_ref[rs, :])
            k = _rope(k, cos_ref[rs, :], sin_ref[rs, :])
        q_ref[rs, :] = q.astype(q_ref.dtype)
        k_ref[rs, :] = k.astype(k_ref.dtype)
        if rope:
            vt_ref[:, rs] = v.T.astype(vt_ref.dtype)
        else:
            for s in range(rs.start // seq, rs.stop // seq):
                rows = slice(s * seq - rs.start, (s + 1) * seq - rs.start)
                kt_ref[s] = k[rows].T
                vt_ref[s] = v[rows].T
        u_ref[rs, :] = (cg * xin).astype(u_ref.dtype)
        bg_ref[rs, :] = bg.astype(bg_ref.dtype)


def _inproj(x, mod3, norm_g, w_in, *, seq, mod_row0, rope_tabs=None, tm=INPROJ_ROWS):
    t = x.shape[0]
    tiles_per_seq = seq // tm
    assert rope_tabs is not None or INPROJ_SUB % seq == 0
    row = lambda i: (mod_row0 + (i // tiles_per_seq if mod_row0 else 0), 0, 0)
    in_specs = [pl.BlockSpec((tm, D_MODEL), lambda i: (i, 0)),
                pl.BlockSpec((None, 1, 6 * D_MODEL), row),
                pl.BlockSpec((1, D_MODEL), lambda i: (0, 0)),
                pl.BlockSpec((D_MODEL, IN_WIDTH), lambda i: (0, 0))]
    args = [x, mod3, norm_g, w_in]
    if rope_tabs is not None:
        in_specs += [pl.BlockSpec((tm, LANES), lambda i: (i % tiles_per_seq, 0))] * 2
        args += list(rope_tabs)
    rows = lambda w: (pl.BlockSpec((tm, w), lambda i: (i, 0)), jax.ShapeDtypeStruct((t, w), bf16))
    if rope_tabs is not None:
        transposed = [(pl.BlockSpec((None, KV_WIDTH, tm),
                                    lambda i: (i // tiles_per_seq, 0, i % tiles_per_seq)),
                       jax.ShapeDtypeStruct((t // seq, KV_WIDTH, seq), bf16))]
    else:
        transposed = [(pl.BlockSpec((tm // seq, KV_WIDTH, seq), lambda i: (i, 0, 0)),
                       jax.ShapeDtypeStruct((t // seq, KV_WIDTH, seq), f32))] * 2
    outs = [rows(ATTN_WIDTH), rows(KV_WIDTH)] + transposed + [rows(CONV_WIDTH), rows(CONV_WIDTH)]
    out_specs, out_shape = (list(z) for z in zip(*outs))
    return pl.pallas_call(
        functools.partial(_inproj_kernel, rope=rope_tabs is not None, seq=seq),
        grid=(t // tm,),
        in_specs=in_specs,
        out_specs=out_specs,
        out_shape=out_shape,
        compiler_params=_params("arbitrary"),
        name="inproj_rope" if rope_tabs is not None else "inproj",
    )(*args)


def _rope_tables(n):
    t = np.arange(n)
    inv = ROPE_BASE ** (-np.arange(N_FREQ, dtype=np.float32) / N_FREQ)
    rows = (t // GRID_W).astype(np.float32)
    cols = (t % GRID_W).astype(np.float32)
    d = np.arange(LANES) % HEAD_DIM
    pos = np.where(d[None, :] < HEAD_DIM // 2, rows[:, None], cols[:, None])
    ang = jnp.asarray(pos.astype(np.float32) * inv[d % N_FREQ][None, :])
    sign = np.where((d % (2 * N_FREQ)) < N_FREQ, -1.0, 1.0).astype(np.float32)
    return jnp.cos(ang), jnp.sin(ang) * sign[None, :]


def _attention_core(sink_ref, q_ref, o_ref, subs, shared, bq):
    n_col = Q_PER_KV * bq
    col_head = lax.broadcasted_iota(jnp.int32, (1, n_col), 1) // bq
    sink_rows = []
    for g in range(N_KV_HEADS):
        sink_row = jnp.full((1, n_col), sink_ref[g * Q_PER_KV], f32)
        for j in range(1, Q_PER_KV):
            sink_row = jnp.where(col_head == j, sink_ref[g * Q_PER_KV + j], sink_row)
        sink_rows.append(sink_row * LOG2_E)
    ones_rows = jnp.ones((BF16_SUBLANES, 1), bf16)

    def with_ones(vals_t, g):
        return jnp.concatenate(
            [vals_t[g * HEAD_DIM:(g + 1) * HEAD_DIM, :],
             jnp.broadcast_to(ones_rows, (BF16_SUBLANES, vals_t.shape[1]))], axis=0)

    shared_vals = None if shared is None else [with_ones(shared[1], g) for g in range(N_KV_HEADS)]
    qts = [(q_ref[i * bq:(i + 1) * bq, :].astype(f32) * (SCALE * LOG2_E)).T.astype(bf16)
           for i in range(len(subs))]

    def scores(i, g):
        lo, hi = g * HEAD_DIM, (g + 1) * HEAD_DIM
        heads = range(g * Q_PER_KV, (g + 1) * Q_PER_KV)
        qt_g = jnp.concatenate([qts[i][h * HEAD_DIM:(h + 1) * HEAD_DIM, :] for h in heads], axis=1)
        s_own = _dot(subs[i]["keys"][:, lo:hi], qt_g)
        if subs[i]["mask"] is not None:
            s_own = jnp.where(subs[i]["mask"], s_own, NEG)
        s_shared = None if shared is None else _dot(shared[0][:, lo:hi], qt_g)
        return s_own, s_shared

    def weighted_values(i, g, s_own, s_shared):
        m = jnp.maximum(jnp.max(s_own, axis=0, keepdims=True), sink_rows[g])
        if s_shared is not None:
            m = jnp.maximum(m, jnp.max(s_shared, axis=0, keepdims=True))
        o_aug = _dot(with_ones(subs[i]["vals_t"], g), jnp.exp2(s_own - m).astype(bf16))
        if s_shared is not None:
            o_aug = o_aug + _dot(shared_vals[g], jnp.exp2(s_shared - m).astype(bf16))
        den = o_aug[HEAD_DIM:HEAD_DIM + 1, :] + jnp.exp2(sink_rows[g] - m)
        return o_aug[:HEAD_DIM, :] / den

    passes = [(i, g) for i in range(len(subs)) for g in range(N_KV_HEADS)]
    out_t = {}
    pending = scores(*passes[0])
    for n, (i, g) in enumerate(passes):
        nxt = scores(*passes[n + 1]) if n + 1 < len(passes) else None
        out_t[i, g] = weighted_values(i, g, *pending)
        pending = nxt
    for i in range(len(subs)):
        blocks = [jnp.concatenate([out_t[i, g][:, j * bq:(j + 1) * bq]
                                   for g in range(N_KV_HEADS)], axis=0).T
                  for j in range(Q_PER_KV)]
        o_ref[i * bq:(i + 1) * bq, :] = jnp.concatenate(blocks, axis=1).astype(o_ref.dtype)


ATTN_HEAD_PERM = np.array([(Q_PER_KV * g + j) * HEAD_DIM + d for j in range(Q_PER_KV)
                           for g in range(N_KV_HEADS) for d in range(HEAD_DIM)])


def _permute_heads(a):
    blocks = a.reshape(N_KV_HEADS, Q_PER_KV, HEAD_DIM, *a.shape[1:])
    return blocks.swapaxes(0, 1).reshape(a.shape)


def _ctx_attn_kernel(sink_ref, q_ref, k_ref, vt_ref, o_ref, *, bq, seq):
    subs = []
    for s in range(q_ref.shape[0] // seq):
        keys = k_ref[s * seq:(s + 1) * seq, :]
        subs += [dict(keys=keys, vals_t=vt_ref[s].astype(bf16), mask=None)] * (seq // bq)
    _attention_core(sink_ref, q_ref, o_ref, subs, None, bq)


def _context_attention(q, k, vt, sink, *, seq, bq=128, seqs_per_step=4):
    t = q.shape[0]
    rows = seq * seqs_per_step
    return pl.pallas_call(
        functools.partial(_ctx_attn_kernel, bq=bq, seq=seq),
        grid=(t // rows,),
        in_specs=[pl.BlockSpec(memory_space=pltpu.SMEM),
                  pl.BlockSpec((rows, ATTN_WIDTH), lambda b: (b, 0)),
                  pl.BlockSpec((rows, KV_WIDTH), lambda b: (b, 0)),
                  pl.BlockSpec((seqs_per_step, KV_WIDTH, seq), lambda b: (b, 0, 0))],
        out_specs=pl.BlockSpec((rows, ATTN_WIDTH), lambda b: (b, 0)),
        out_shape=jax.ShapeDtypeStruct((t, ATTN_WIDTH), bf16),
        compiler_params=_params("arbitrary"),
        name="ctx_attn",
    )(sink, q, k, vt)


def _lat_attn_kernel(sink_ref, q_ref, k_ref, vt_ref, ck_ref, cv_ref, *rest, bq, seq, n_cast):
    o_ref = rest[n_cast]
    for src_ref, dst_ref in zip(rest[:n_cast], rest[n_cast + 1:]):
        dst_ref[...] = src_ref[...].astype(dst_ref.dtype)
    band = bq + 2 * WINDOW
    n_col = Q_PER_KV * bq
    delta = ((lax.broadcasted_iota(jnp.int32, (band, n_col), 1) & (bq - 1))
             - lax.broadcasted_iota(jnp.int32, (band, n_col), 0))
    n_sub = q_ref.shape[0] // bq
    subs = []
    for sb in range(n_sub):
        i = pl.program_id(1) * n_sub + sb
        start = pl.multiple_of(jnp.clip(i * bq - WINDOW, 0, seq - band), LANES)
        shifted = delta + (i * bq - start + WINDOW)
        subs.append(dict(keys=k_ref[pl.ds(start, band), :],
                         vals_t=vt_ref[:, pl.ds(start, band)],
                         mask=shifted.astype(jnp.uint32) <= 2 * WINDOW))
    shared = (ck_ref[...].astype(bf16), cv_ref[...].T.astype(bf16))
    _attention_core(sink_ref, q_ref, o_ref, subs, shared, bq)


LAT_ATTN_SUB_BLOCKS = 8


def _latent_attention(q, k, vt, ck, cv, sink, *, seq, cast=(), bq=128,
                      sub_blocks=LAT_ATTN_SUB_BLOCKS):
    nb, past = ck.shape[0], ck.shape[1]
    assert bq == LANES and N_KV_HEADS * HEAD_DIM == LANES
    q3, k3 = (a.reshape(nb, seq, a.shape[-1]) for a in (q, k))
    tq = bq * sub_blocks
    steps_per_seq = seq // tq
    n_steps = nb * steps_per_seq
    cast3 = [a.reshape((n_steps, -1) + a.shape[2:]) for a in cast]
    cast_specs = [pl.BlockSpec((None,) + a.shape[1:], lambda b, i: (b * steps_per_seq + i, 0, 0))
                  for a in cast3]
    out, *cast_out = pl.pallas_call(
        functools.partial(_lat_attn_kernel, bq=bq, seq=seq, n_cast=len(cast)),
        grid=(nb, steps_per_seq),
        in_specs=[pl.BlockSpec(memory_space=pltpu.SMEM),
                  pl.BlockSpec((None, tq, ATTN_WIDTH), lambda b, i: (b, i, 0)),
                  pl.BlockSpec((None, seq, KV_WIDTH), lambda b, i: (b, 0, 0)),
                  pl.BlockSpec((None, KV_WIDTH, seq), lambda b, i: (b, 0, 0)),
                  pl.BlockSpec((None, past, KV_WIDTH), lambda b, i: (b, 0, 0)),
                  pl.BlockSpec((None, past, KV_WIDTH), lambda b, i: (b, 0, 0))] + cast_specs,
        out_specs=[pl.BlockSpec((None, tq, ATTN_WIDTH), lambda b, i: (b, i, 0))] + cast_specs,
        out_shape=[jax.ShapeDtypeStruct((nb, seq, ATTN_WIDTH), bf16)]
        + [jax.ShapeDtypeStruct(a.shape, bf16) for a in cast3],
        compiler_params=_params("arbitrary", "arbitrary"),
        name="lat_attn",
    )(sink, q3, k3, vt, ck, cv, *cast3)
    return [out.reshape(nb * seq, ATTN_WIDTH)] + [o.reshape(a.shape) for o, a in zip(cast_out, cast)]


def _route(logits_t):
    r = logits_t.shape[1]
    big = jnp.float32(LANES)
    crow = lax.broadcasted_iota(jnp.int32, (F32_SUBLANES, r), 0)
    crow_f = crow.astype(f32)
    lc = jnp.where(crow < N_GROUPS, logits_t[COARSE_LANE0:COARSE_LANE0 + F32_SUBLANES, :], -jnp.inf)
    mc = jnp.max(lc, axis=0, keepdims=True)
    grp = jnp.min(jnp.where(lc == mc, crow_f, big), axis=0, keepdims=True)
    pg = 1.0 / jnp.sum(jnp.exp(lc - mc), axis=0, keepdims=True)
    erow = lax.broadcasted_iota(jnp.int32, (N_EXPERTS, r), 0).astype(f32)
    in_g = jnp.floor(erow * (1.0 / EXPERTS_PER_GROUP)) == grp
    fl = jnp.where(in_g, logits_t[0:N_EXPERTS, :], -jnp.inf)
    m1 = jnp.max(fl, axis=0, keepdims=True)
    i1 = jnp.min(jnp.where(fl == m1, erow, big), axis=0, keepdims=True)
    fl2 = jnp.where(erow == i1, -jnp.inf, fl)
    m2 = jnp.max(fl2, axis=0, keepdims=True)
    i2 = jnp.min(jnp.where(fl2 == m2, erow, big), axis=0, keepdims=True)
    e2 = jnp.exp(m2 - m1)
    p1 = pg / (1.0 + e2)
    p2 = pg * e2 / (1.0 + e2)
    packed = jnp.where(crow == 0, i1, jnp.where(crow == 1, i2, jnp.where(crow == 2, p1, p2)))
    return jnp.where(crow < 4, packed, 0.0)


def _post_kernel(x_ref, attn_ref, u_ref, up_ref, un_ref, bg_ref, mod_ref, ag_ref, cg_ref, cw_ref,
                 cb_ref, wo_ref, n2_ref, wr_ref, br_ref, x1_ref, h2_ref, route_ref, *, tm, seq):
    i = pl.program_id(0)
    u = u_ref[...].astype(f32)
    rows = lax.broadcasted_iota(jnp.int32, (tm, 1), 0)
    spos = (i * tm + rows) % seq
    u_dn = jnp.where(rows == 0, up_ref[...].astype(f32)[BF16_SUBLANES - 1:, :],
                     pltpu.roll(u, 1, axis=0))
    u_dn = jnp.where(spos == 0, 0.0, u_dn)
    u_up = jnp.where(rows == tm - 1, un_ref[...].astype(f32)[0:1, :], pltpu.roll(u, tm - 1, axis=0))
    u_up = jnp.where(spos == seq - 1, 0.0, u_up)
    y = u_dn * cw_ref[0:1, :] + u * cw_ref[1:2, :] + u_up * cw_ref[2:3, :] + cb_ref[...]
    conv = bg_ref[...].astype(f32) * y
    attn_n = (_rms(attn_ref[...].astype(f32)) * ag_ref[...]).astype(bf16)
    conv_n = (_rms(conv) * cg_ref[...]).astype(bf16)
    subs = [slice(s * POST_SUB, (s + 1) * POST_SUB) for s in range(tm // POST_SUB)]
    mixed = [_dot(attn_n[rs], wo_ref[0:ATTN_WIDTH, :]) + _dot(conv_n[rs], wo_ref[ATTN_WIDTH:, :])
             for rs in subs]
    logits = []
    for s, rs in enumerate(subs):
        x1 = x_ref[rs, :] + mod_ref[:, G1:G1 + D_MODEL] * mixed[s]
        x1_ref[rs, :] = x1
        h2 = _rms(x1) * n2_ref[...]
        h2 = h2 * (1.0 + mod_ref[:, SC2:SC2 + D_MODEL]) + mod_ref[:, SH2:SH2 + D_MODEL]
        for c in range(D_MODEL // LANES):
            h2_ref[pl.ds(rs.start * F32_SUBLANES + c, POST_SUB, stride=F32_SUBLANES), :] = (
                h2[:, c * LANES:(c + 1) * LANES])
        h2_hi = h2.astype(bf16)
        h2_lo = (h2 - h2_hi.astype(f32)).astype(bf16)
        both = _dot(h2_hi, wr_ref[...])
        logits.append(both[:, :LANES] + both[:, LANES:] + _dot(h2_lo, wr_ref[:, :LANES]))
    for s, rs in enumerate(subs):
        route_ref[:, rs] = _route((logits[s] + br_ref[...]).T)


def _post(x, attn, u, bg, mod3, p, *, seq, mod_row0, tm=4 * POST_SUB):
    t = x.shape[0]
    tiles_per_seq = max(seq // tm, 1)
    halo = BF16_SUBLANES
    n_halo = t // halo
    row = lambda i: (mod_row0 + (i // tiles_per_seq if mod_row0 else 0), 0, 0)
    tile = lambda w: pl.BlockSpec((tm, w), lambda i: (i, 0))
    full = lambda a: pl.BlockSpec(a.shape, lambda i: (0,) * a.ndim)
    small = [p["attn_out_g"], p["conv_out_g"], p["conv_w"], p["conv_b"], p["w_o"], p["norm2_g"],
             p["w_router"], p["b_router"]]
    return pl.pallas_call(
        functools.partial(_post_kernel, tm=tm, seq=seq),
        grid=(t // tm,),
        in_specs=[tile(D_MODEL), tile(ATTN_WIDTH), tile(CONV_WIDTH),
                  pl.BlockSpec((halo, CONV_WIDTH),
                               lambda i: (jnp.maximum(i * (tm // halo) - 1, 0), 0)),
                  pl.BlockSpec((halo, CONV_WIDTH),
                               lambda i: (jnp.minimum((i + 1) * (tm // halo), n_halo - 1), 0)),
                  tile(CONV_WIDTH),
                  pl.BlockSpec((None, 1, 6 * D_MODEL), row)] + [full(a) for a in small],
        out_specs=[tile(D_MODEL), pl.BlockSpec((tm * F32_SUBLANES, LANES), lambda i: (i, 0)),
                   pl.BlockSpec((F32_SUBLANES, tm), lambda i: (0, i))],
        out_shape=[jax.ShapeDtypeStruct((t, D_MODEL), f32),
                   jax.ShapeDtypeStruct((t * F32_SUBLANES, LANES), f32),
                   jax.ShapeDtypeStruct((F32_SUBLANES, t), f32)],
        compiler_params=_params("arbitrary"),
        name="post",
    )(x, attn, u, u, u, bg, mod3, *small)


def _max_chunks(blk):
    return -(-TOP_K * blk // MOE_CHUNK) + EXPERTS_PER_STEP


def _sorted_rows(blk):
    rows = TOP_K * blk + N_EXPERTS * (F32_SUBLANES - 1) + MOE_CHUNK
    assert rows % F32_SUBLANES == 0
    return rows + (F32_SUBLANES if rows // F32_SUBLANES % 2 == 0 else 0)


def _dispatch_tables(route, blk):
    nblk = route.shape[1] // blk
    n_slots = TOP_K * blk
    by_block = lambda a: a.reshape(TOP_K, nblk, blk).transpose(1, 0, 2)
    experts = by_block(route[0:TOP_K].astype(jnp.int32))
    weights = by_block(route[TOP_K:2 * TOP_K])
    ids = jnp.arange(N_EXPERTS, dtype=jnp.int32)
    onehot = experts.reshape(nblk, n_slots // LANES, LANES, 1) == ids
    earlier = jnp.tril(jnp.ones((LANES, LANES), bf16), -1)
    within = jnp.einsum("ij,bgje->bgie", earlier, onehot.astype(bf16),
                        preferred_element_type=f32)
    per_group = jnp.sum(onehot, axis=2, dtype=jnp.int32)
    before_group = jnp.cumsum(per_group, axis=1) - per_group
    counts = jnp.sum(per_group, axis=1)
    padded = (counts + F32_SUBLANES - 1) // F32_SUBLANES * F32_SUBLANES
    lo = jnp.cumsum(padded, axis=1) - padded
    row = within.astype(jnp.int32) + before_group[:, :, None, :] + lo[:, None, None, :]
    row = jnp.sum(jnp.where(onehot, row, 0), axis=-1).reshape(nblk, TOP_K, blk)

    eg, n_steps, max_c = EXPERTS_PER_STEP, N_EXPERTS // EXPERTS_PER_STEP, _max_chunks(blk)
    lo = lo.reshape(nblk, n_steps, eg)
    counts = counts.reshape(nblk, n_steps, eg)
    n_ch = (counts + MOE_CHUNK - 1) // MOE_CHUNK
    cum = jnp.cumsum(n_ch, axis=2)
    total = cum[..., -1]
    pos = jnp.arange(max_c, dtype=jnp.int32)
    j = jnp.sum(cum[:, :, None, :] <= pos[None, None, :, None], axis=-1, dtype=jnp.int32)
    j = jnp.minimum(j, eg - 1)
    hit = j[..., None] == jnp.arange(eg, dtype=jnp.int32)
    pick = lambda a: jnp.sum(jnp.where(hit, a[:, :, None, :], 0), axis=-1)
    done = (pos - pick(cum - n_ch)) * MOE_CHUNK
    live = pos < total[..., None]
    flat = lambda a: jnp.where(live, a, 0).reshape(nblk, 1, n_steps * max_c)
    return (row[:, 0:1], row[:, 1:2], weights[:, 0:1], weights[:, 1:2], flat(j),
            flat(pick(lo) + done), flat(jnp.clip(pick(counts) - done, 0, MOE_CHUNK)),
            total[:, None, :])


def _moe_kernel(row0_ref, row1_ref, w0_ref, w1_ref, cj_ref, cfirst_ref, cvalid_ref, nchunks_ref,
                mod_ref, fg_ref, h_hbm, wg_hbm, wu_hbm, wd_hbm, x1_hbm, out_hbm,
                sorted_ref, yt_ref, h_ref, wg_ref, wu_ref, wd_ref, xbuf_ref, obuf_ref, sem_h, sem_w,
                sem_in, sem_out, *, blk):
    n_col = D_MODEL // LANES
    pitch = _sorted_rows(blk)
    blk_id, step = pl.program_id(0), pl.program_id(1)
    n_steps = pl.num_programs(1)
    last_step = step == n_steps - 1
    max_c = _max_chunks(blk)
    n_sub = blk // FINAL_ROWS
    assignment_rows = (row0_ref, w0_ref), (row1_ref, w1_ref)
    run = blk_id * n_steps + step
    ahead = WEIGHT_BUFFERS - 1

    def weight_copies(r):
        slot = r % WEIGHT_BUFFERS
        experts = pl.ds(r % n_steps * EXPERTS_PER_STEP, EXPERTS_PER_STEP)
        pairs = (wg_hbm, wg_ref), (wu_hbm, wu_ref), (wd_hbm, wd_ref)
        return [pltpu.make_async_copy(hbm.at[experts], buf.at[slot], sem_w.at[k, slot])
                for k, (hbm, buf) in enumerate(pairs)]

    def h_copy(b):
        tiles = pl.ds(b * (blk * F32_SUBLANES), blk * F32_SUBLANES)
        return pltpu.make_async_copy(h_hbm.at[tiles, :], h_ref, sem_h.at[0])

    @pl.when(run == 0)
    def _():
        h_copy(0).start()
        for r in range(ahead):
            for copy in weight_copies(r):
                copy.start()

    @pl.when(run + ahead < pl.num_programs(0) * n_steps)
    def _():
        for copy in weight_copies(run + ahead):
            copy.start()

    def column_tile(row):
        return pl.ds(row, F32_SUBLANES, stride=pitch)

    def x1_copy(s):
        rows = pl.ds(blk_id * blk + s * FINAL_ROWS, FINAL_ROWS)
        slot = s % FINAL_SLOTS
        return pltpu.make_async_copy(x1_hbm.at[rows, :], xbuf_ref.at[slot], sem_in.at[slot])

    def out_copy(s):
        rows = pl.ds(blk_id * blk + s * FINAL_ROWS, FINAL_ROWS)
        slot = s % FINAL_SLOTS
        return pltpu.make_async_copy(obuf_ref.at[slot], out_hbm.at[rows, :], sem_out.at[slot])

    @pl.when(last_step)
    def _():
        for s in range(min(FINAL_SLOTS, n_sub)):
            x1_copy(s).start()

    @pl.when((blk_id == 0) & (step == 0))
    def _():
        zero_rows = F32_SUBLANES * LANES
        zeros = jnp.zeros((zero_rows, LANES), f32)

        def clear(i, carry):
            sorted_ref[pl.ds(pl.multiple_of(i * zero_rows, zero_rows), zero_rows), :] = zeros
            return carry

        lax.fori_loop(0, n_col * pitch // zero_rows, clear, 0)
        sorted_ref[n_col * pitch - zero_rows:, :] = zeros

    @pl.when(step == 0)
    def _():
        h_copy(blk_id).wait()

        def dispatch(i, carry):
            first = pl.multiple_of(i * DISPATCH_ROWS * F32_SUBLANES, DISPATCH_ROWS * F32_SUBLANES)
            for r in range(DISPATCH_ROWS):
                tile = h_ref[pl.ds(first + r * F32_SUBLANES, F32_SUBLANES), :]
                for rows_ref, _ in assignment_rows:
                    sorted_ref[column_tile(rows_ref[0, i * DISPATCH_ROWS + r]), :] = tile
            return carry

        lax.fori_loop(0, blk // DISPATCH_ROWS, dispatch, 0)

        @pl.when(blk_id + 1 < pl.num_programs(0))
        def _():
            h_copy(blk_id + 1).start()

    def chunk_rows(p, c):
        first = pl.multiple_of(cfirst_ref[0, p], F32_SUBLANES)
        return pl.ds(pl.multiple_of(c * pitch + first, F32_SUBLANES), MOE_CHUNK)

    def hidden(p):
        x = jnp.concatenate([sorted_ref[chunk_rows(p, c), :] for c in range(n_col)],
                            axis=1).astype(bf16)
        j = cj_ref[0, p]
        g = _dot(x, wg_ref[run % WEIGHT_BUFFERS, j])
        up = _dot(x, wu_ref[run % WEIGHT_BUFFERS, j])
        return ((g * jax.nn.sigmoid(g)) * up).astype(bf16)

    def project(p, act):
        out = _dot(act, wd_ref[run % WEIGHT_BUFFERS, cj_ref[0, p]])
        own = lax.broadcasted_iota(jnp.int32, (MOE_CHUNK, LANES), 0) < cvalid_ref[0, p]
        for c in range(n_col):
            rows = chunk_rows(p, c)
            sorted_ref[rows, :] = jnp.where(own, out[:, c * LANES:(c + 1) * LANES],
                                            sorted_ref[rows, :])

    def run_chunks(ps):
        acts = [hidden(p) for p in ps]
        for p, act in zip(ps, acts):
            project(p, act)

    def group(gi, carry):
        run_chunks([step * max_c + CHUNKS_PER_TRIP * gi + k for k in range(CHUNKS_PER_TRIP)])
        return carry

    for copy in weight_copies(run):
        copy.wait()
    n_chunks = nchunks_ref[0, step]
    lax.fori_loop(0, n_chunks // CHUNKS_PER_TRIP, group, 0)
    for left in range(1, CHUNKS_PER_TRIP):
        @pl.when(n_chunks % CHUNKS_PER_TRIP == left)
        def _():
            run_chunks([step * max_c + n_chunks - left + k for k in range(left)])

    @pl.when(last_step)
    def _():
        gate2 = mod_ref[:, G2:G2 + D_MODEL]
        for s in range(n_sub):
            slot = s % FINAL_SLOTS
            for r in range(FINAL_ROWS):
                t = s * FINAL_ROWS + r
                weighted = [sorted_ref[column_tile(rows_ref[0, t]), :] * w_ref[0, t]
                            for rows_ref, w_ref in assignment_rows]
                yt_ref[r * F32_SUBLANES:(r + 1) * F32_SUBLANES, :] = functools.reduce(
                    jnp.add, weighted)
            x1_copy(s).wait()
            if s >= FINAL_SLOTS:
                out_copy(s - FINAL_SLOTS).wait()
            y = jnp.concatenate([yt_ref[pl.ds(c, FINAL_ROWS, stride=F32_SUBLANES), :]
                                 for c in range(n_col)], axis=1)
            x2 = xbuf_ref[slot] + gate2 * y
            obuf_ref[slot] = _rms(x2) * fg_ref[...]
            out_copy(s).start()
            if s + FINAL_SLOTS < n_sub:
                x1_copy(s + FINAL_SLOTS).start()
        for s in range(max(n_sub - FINAL_SLOTS, 0), n_sub):
            out_copy(s).wait()


def _moe(h2_tiles, tables, x1, mod3, final_g, w_gate, w_up, w_down, *, seq, mod_row0,
         blk=MOE_BLOCK):
    t = h2_tiles.shape[0] // F32_SUBLANES
    eg, n_steps = EXPERTS_PER_STEP, N_EXPERTS // EXPERTS_PER_STEP
    blocks_per_seq = max(seq // blk, 1)
    row = lambda b, g: (mod_row0 + (b // blocks_per_seq if mod_row0 else 0), 0, 0)
    smem = lambda n: pl.BlockSpec((None, 1, n), lambda b, g: (b, 0, 0), memory_space=pltpu.SMEM)
    final_buf = pltpu.VMEM((FINAL_SLOTS, FINAL_ROWS, D_MODEL), f32)
    assert t // blk * n_steps >= WEIGHT_BUFFERS - 1
    weight_buf = lambda w: pltpu.VMEM((WEIGHT_BUFFERS, eg) + w.shape[1:], w.dtype)
    return pl.pallas_call(
        functools.partial(_moe_kernel, blk=blk),
        grid=(t // blk, n_steps),
        in_specs=[smem(blk)] * 4 + [smem(n_steps * _max_chunks(blk))] * 3 + [
                  smem(n_steps),
                  pl.BlockSpec((None, 1, 6 * D_MODEL), row),
                  pl.BlockSpec((1, D_MODEL), lambda b, g: (0, 0))]
        + [pl.BlockSpec(memory_space=pl.ANY)] * 5,
        out_specs=pl.BlockSpec(memory_space=pl.ANY),
        out_shape=jax.ShapeDtypeStruct((t, D_MODEL), f32),
        scratch_shapes=[pltpu.VMEM((D_MODEL // LANES * _sorted_rows(blk), LANES), f32),
                        pltpu.VMEM((FINAL_ROWS * F32_SUBLANES, LANES), f32),
                        pltpu.VMEM((blk * F32_SUBLANES, LANES), f32),
                        weight_buf(w_gate), weight_buf(w_up), weight_buf(w_down),
                        final_buf, final_buf,
                        pltpu.SemaphoreType.DMA((1,)),
                        pltpu.SemaphoreType.DMA((3, WEIGHT_BUFFERS)),
                        pltpu.SemaphoreType.DMA((FINAL_SLOTS,)),
                        pltpu.SemaphoreType.DMA((FINAL_SLOTS,))],
        compiler_params=_params("arbitrary", "arbitrary", vmem=MOE_VMEM_LIMIT_BYTES),
        name="moe",
    )(*tables, mod3, final_g, h2_tiles, w_gate, w_up, w_down, x1)


def _mixers(x, mod3, p, *, seq, mod_row0, ctx_kv, cast=()):
    if ctx_kv is not None:
        q, k, vt, u, bg = _inproj(x, mod3, p["norm1_g"], p["w_in"], seq=seq, mod_row0=mod_row0,
                                  rope_tabs=_rope_tables(seq))
        attn, *cast = _latent_attention(q, k, vt, ctx_kv[0], ctx_kv[1], p["sink"], seq=seq,
                                        cast=cast)
        kv_t = None
    else:
        q, k, kt, vt, u, bg = _inproj(x, mod3, p["norm1_g"], p["w_in"], seq=seq,
                                      mod_row0=mod_row0)
        attn = _context_attention(q, k, vt, p["sink"], seq=seq)
        kv_t = (kt, vt)
    x1, h2_tiles, route = _post(x, attn, u, bg, mod3, p, seq=seq, mod_row0=mod_row0)
    return x1, h2_tiles, route, kv_t, cast


def _experts(x1, h2_tiles, tables, mod3, p, *, seq, mod_row0):
    out = _moe(h2_tiles, tables, x1, mod3, p["final_g"], p["w_gate"], p["w_up"], p["w_down"],
               seq=seq, mod_row0=mod_row0)
    return out.reshape(-1, seq, D_MODEL)


def kernel(x_prompt, x_sample, cache_k, cache_v, c, c_ctx, w_mod, b_mod, norm1_g, w_in, conv_w,
           conv_b, sink, attn_out_g, conv_out_g, w_o, norm2_g, w_coarse, b_coarse, w_fine, b_fine,
           w_gate, w_up, w_down, final_g):
    batch, seq, _ = x_prompt.shape
    dec_batch, dec_seq, _ = x_sample.shape
    past = cache_k.shape[2]
    assert w_mod.shape[0] == 1 and 1 + dec_batch <= MOD_ROWS

    cvecs = jnp.concatenate([c_ctx[None], c, jnp.zeros((MOD_ROWS - 1 - dec_batch, D_MODEL), f32)])
    mod3 = _modulation(cvecs, w_mod[0], b_mod[0]).reshape(MOD_ROWS, 1, 6 * D_MODEL)

    pad = jnp.zeros((D_MODEL, LANES - N_EXPERTS - N_GROUPS), f32)
    w_mix = jnp.concatenate([_permute_heads(w_o[0][:ATTN_WIDTH]), w_o[0][ATTN_WIDTH:]])
    p = {
        "norm1_g": norm1_g, "w_in": w_in[0].astype(bf16), "conv_w": conv_w[0], "conv_b": conv_b,
        "sink": sink[0], "attn_out_g": _permute_heads(attn_out_g[0])[None],
        "conv_out_g": conv_out_g, "w_o": w_mix.astype(bf16), "norm2_g": norm2_g,
        "w_router": _split_bf16(jnp.concatenate([w_fine[0], w_coarse[0], pad], axis=1)),
        "b_router": jnp.concatenate([b_fine[0], b_coarse[0], pad[0]])[None],
        "final_g": final_g[None],
    }

    x1_p, h2_p, route_p, kv_t, _ = _mixers(x_prompt.reshape(batch * seq, D_MODEL), mod3, p,
                                           seq=seq, mod_row0=0, ctx_kv=None)
    new_k, new_v = (a.reshape(batch, 1, N_KV_HEADS, HEAD_DIM, seq).transpose(0, 1, 4, 2, 3)
                    for a in kv_t)

    ctx_kv = (cache_k[:, 0].reshape(dec_batch, past, KV_WIDTH),
              cache_v[:, 0].reshape(dec_batch, past, KV_WIDTH))
    experts_f32 = (w_gate[0], w_up[0], w_down[0])
    ride = N_EXPERTS % (dec_batch * dec_seq // (LANES * LAT_ATTN_SUB_BLOCKS)) == 0
    x1_s, h2_s, route_s, _, experts_bf16 = _mixers(
        x_sample.reshape(dec_batch * dec_seq, D_MODEL), mod3, p, seq=dec_seq, mod_row0=1,
        ctx_kv=ctx_kv, cast=experts_f32 if ride else ())
    if not ride:
        experts_bf16 = [w.astype(bf16) for w in experts_f32]
    p = dict(p, w_gate=experts_bf16[0], w_up=experts_bf16[1], w_down=experts_bf16[2])

    n_blk_p = batch * seq // MOE_BLOCK
    tables = _dispatch_tables(jnp.concatenate([route_p, route_s], axis=1), MOE_BLOCK)
    y_prompt = _experts(x1_p, h2_p, [t[:n_blk_p] for t in tables], mod3, p, seq=seq, mod_row0=0)
    y_sample = _experts(x1_s, h2_s, [t[n_blk_p:] for t in tables], mod3, p, seq=dec_seq,
                        mod_row0=1)
    return y_prompt, y_sample, new_k, new_v
```

```python
import functools

import jax
import jax.numpy as jnp
import numpy as np
from jax import lax
from jax.experimental import pallas as pl
from jax.experimental.pallas import tpu as pltpu

D_MODEL = 1024
HEAD_DIM = 64
ATTN_WIDTH = 512
N_HEADS = 8
N_KV_HEADS = 2
Q_PER_KV = 4
KV_WIDTH = 128
CONV_WIDTH = 512
CONV_K = 3
WINDOW = 128
GRID_W = 64
ROPE_BASE = 10000.0
N_FREQ = 16
N_GROUPS = 4
EXPERTS_PER_GROUP = 8
N_EXPERTS = 32
TOP_K = 2
D_EXPERT = 256
IN_WIDTH = ATTN_WIDTH + 2 * KV_WIDTH + 3 * CONV_WIDTH
EPS = 1e-6
NEG = -1e30
SCALE = HEAD_DIM ** -0.5
LOG2_E = 1.4426950408889634

LANES = 128
F32_SUBLANES = 8
BF16_SUBLANES = 16
assert D_MODEL == F32_SUBLANES * LANES

INPROJ_ROWS = 1024
INPROJ_SUB = 512
POST_SUB = 256
MOE_BLOCK = 2048
MOE_CHUNK = 160
CHUNKS_PER_TRIP = 4
DISPATCH_ROWS = 128
EXPERTS_PER_STEP = 4
WEIGHT_BUFFERS = 3
FINAL_ROWS = 256
FINAL_SLOTS = 4
MOE_VMEM_LIMIT_BYTES = 60 * 1024 * 1024
VMEM_LIMIT_BYTES = 48 * 1024 * 1024

SH1, SC1, G1, SH2, SC2, G2 = (i * D_MODEL for i in range(6))
MOD_ROWS = 8

COARSE_LANE0 = N_EXPERTS

f32 = jnp.float32
bf16 = jnp.bfloat16


def _params(*semantics, vmem=VMEM_LIMIT_BYTES):
    return pltpu.CompilerParams(dimension_semantics=semantics, vmem_limit_bytes=vmem)


def _rms(x):
    return x * lax.rsqrt(jnp.mean(x * x, axis=-1, keepdims=True) + EPS)


def _dot(a, b):
    return jnp.dot(a, b, preferred_element_type=f32)


def _split_bf16(w):
    hi = w.astype(bf16)
    lo = (w - hi.astype(f32)).astype(bf16)
    return jnp.concatenate([hi, lo], axis=1)


def _mod_kernel(cv_ref, w_ref, b_ref, o_ref):
    a = cv_ref[...]
    a = a * jax.nn.sigmoid(a)
    o_ref[...] = jnp.dot(a, w_ref[...], precision=lax.Precision.HIGHEST,
                         preferred_element_type=f32) + b_ref[...]


def _modulation(cvecs, w_mod, b_mod):
    tn = 2 * D_MODEL
    return pl.pallas_call(
        _mod_kernel,
        grid=(6 * D_MODEL // tn,),
        in_specs=[pl.BlockSpec((MOD_ROWS, D_MODEL), lambda j: (0, 0)),
                  pl.BlockSpec((D_MODEL, tn), lambda j: (0, j)),
                  pl.BlockSpec((1, tn), lambda j: (0, j))],
        out_specs=pl.BlockSpec((MOD_ROWS, tn), lambda j: (0, j)),
        out_shape=jax.ShapeDtypeStruct((MOD_ROWS, 6 * D_MODEL), f32),
        compiler_params=_params("arbitrary"),
        name="mod",
    )(cvecs, w_mod, b_mod.reshape(1, -1))


def _swap_halves(x):
    lane = lax.broadcasted_iota(jnp.int32, x.shape, 1)
    up = pltpu.roll(x, LANES - N_FREQ, axis=1)
    dn = pltpu.roll(x, N_FREQ, axis=1)
    return jnp.where((lane % (2 * N_FREQ)) < N_FREQ, up, dn)


def _rope(x, cos, sin):
    parts = []
    for c in range(x.shape[1] // LANES):
        xc = x[:, c * LANES:(c + 1) * LANES]
        parts.append(xc * cos + _swap_halves(xc) * sin)
    return parts[0] if len(parts) == 1 else jnp.concatenate(parts, axis=1)


def _inproj_kernel(*refs, rope, seq):
    if rope:
        x_ref, mod_ref, g_ref, w_ref, cos_ref, sin_ref, q_ref, k_ref, vt_ref, u_ref, bg_ref = refs
    else:
        x_ref, mod_ref, g_ref, w_ref, q_ref, k_ref, kt_ref, vt_ref, u_ref, bg_ref = refs
    tm = x_ref.shape[0]
    subs = [slice(s * INPROJ_SUB, (s + 1) * INPROJ_SUB) for s in range(tm // INPROJ_SUB)]
    zs = []
    for rs in subs:
        h = _rms(x_ref[rs, :]) * g_ref[...]
        h = h * (1.0 + mod_ref[:, SC1:SC1 + D_MODEL]) + mod_ref[:, SH1:SH1 + D_MODEL]
        zs.append(_dot(h.astype(bf16), w_ref[...]))
    for rs, z in zip(subs, zs):
        o = 0
        q = z[:, o:o + ATTN_WIDTH]; o += ATTN_WIDTH
        k = z[:, o:o + KV_WIDTH]; o += KV_WIDTH
        v = z[:, o:o + KV_WIDTH]; o += KV_WIDTH
        bg = z[:, o:o + CONV_WIDTH]; o += CONV_WIDTH
        cg = z[:, o:o + CONV_WIDTH]; o += CONV_WIDTH
        xin = z[:, o:o + CONV_WIDTH]
        if rope:
            q = _rope(q, cos_ref[rs, :], sin_ref[rs, :])
            k = _rope(k, cos_ref[rs, :], sin_ref[rs, :])
        q_ref[rs, :] = q.astype(q_ref.dtype)
        k_ref[rs, :] = k.astype(k_ref.dtype)
        if rope:
            vt_ref[:, rs] = v.T.astype(vt_ref.dtype)
        else:
            for s in range(rs.start // seq, rs.stop // seq):
                rows = slice(s * seq - rs.start, (s + 1) * seq - rs.start)
                kt_ref[s] = k[rows].T
                vt_ref[s] = v[rows].T
        u_ref[rs, :] = (cg * xin).astype(u_ref.dtype)
        bg_ref[rs, :] = bg.astype(bg_ref.dtype)


def _inproj(x, mod3, norm_g, w_in, *, seq, mod_row0, rope_tabs=None, tm=INPROJ_ROWS):
    t = x.shape[0]
    tiles_per_seq = seq // tm
    assert rope_tabs is not None or INPROJ_SUB % seq == 0
    row = lambda i: (mod_row0 + (i // tiles_per_seq if mod_row0 else 0), 0, 0)
    in_specs = [pl.BlockSpec((tm, D_MODEL), lambda i: (i, 0)),
                pl.BlockSpec((None, 1, 6 * D_MODEL), row),
                pl.BlockSpec((1, D_MODEL), lambda i: (0, 0)),
                pl.BlockSpec((D_MODEL, IN_WIDTH), lambda i: (0, 0))]
    args = [x, mod3, norm_g, w_in]
    if rope_tabs is not None:
        in_specs += [pl.BlockSpec((tm, LANES), lambda i: (i % tiles_per_seq, 0))] * 2
        args += list(rope_tabs)
    rows = lambda w: (pl.BlockSpec((tm, w), lambda i: (i, 0)), jax.ShapeDtypeStruct((t, w), bf16))
    if rope_tabs is not None:
        transposed = [(pl.BlockSpec((None, KV_WIDTH, tm),
                                    lambda i: (i // tiles_per_seq, 0, i % tiles_per_seq)),
                       jax.ShapeDtypeStruct((t // seq, KV_WIDTH, seq), bf16))]
    else:
        transposed = [(pl.BlockSpec((tm // seq, KV_WIDTH, seq), lambda i: (i, 0, 0)),
                       jax.ShapeDtypeStruct((t // seq, KV_WIDTH, seq), f32))] * 2
    outs = [rows(ATTN_WIDTH), rows(KV_WIDTH)] + transposed + [rows(CONV_WIDTH), rows(CONV_WIDTH)]
    out_specs, out_shape = (list(z) for z in zip(*outs))
    return pl.pallas_call(
        functools.partial(_inproj_kernel, rope=rope_tabs is not None, seq=seq),
        grid=(t // tm,),
        in_specs=in_specs,
        out_specs=out_specs,
        out_shape=out_shape,
        compiler_params=_params("arbitrary"),
        name="inproj_rope" if rope_tabs is not None else "inproj",
    )(*args)


def _rope_tables(n):
    t = np.arange(n)
    inv = ROPE_BASE ** (-np.arange(N_FREQ, dtype=np.float32) / N_FREQ)
    rows = (t // GRID_W).astype(np.float32)
    cols = (t % GRID_W).astype(np.float32)
    d = np.arange(LANES) % HEAD_DIM
    pos = np.where(d[None, :] < HEAD_DIM // 2, rows[:, None], cols[:, None])
    ang = jnp.asarray(pos.astype(np.float32) * inv[d % N_FREQ][None, :])
    sign = np.where((d % (2 * N_FREQ)) < N_FREQ, -1.0, 1.0).astype(np.float32)
    return jnp.cos(ang), jnp.sin(ang) * sign[None, :]


def _attention_core(sink_ref, q_ref, o_ref, subs, shared, bq):
    n_col = Q_PER_KV * bq
    col_head = lax.broadcasted_iota(jnp.int32, (1, n_col), 1) // bq
    sink_rows = []
    for g in range(N_KV_HEADS):
        sink_row = jnp.full((1, n_col), sink_ref[g * Q_PER_KV], f32)
        for j in range(1, Q_PER_KV):
            sink_row = jnp.where(col_head == j, sink_ref[g * Q_PER_KV + j], sink_row)
        sink_rows.append(sink_row * LOG2_E)
    ones_rows = jnp.ones((BF16_SUBLANES, 1), bf16)

    def with_ones(vals_t, g):
        return jnp.concatenate(
            [vals_t[g * HEAD_DIM:(g + 1) * HEAD_DIM, :],
             jnp.broadcast_to(ones_rows, (BF16_SUBLANES, vals_t.shape[1]))], axis=0)

    shared_vals = None if shared is None else [with_ones(shared[1], g) for g in range(N_KV_HEADS)]
    qts = [(q_ref[i * bq:(i + 1) * bq, :].astype(f32) * (SCALE * LOG2_E)).T.astype(bf16)
           for i in range(len(subs))]

    def scores(i, g):
        lo, hi = g * HEAD_DIM, (g + 1) * HEAD_DIM
        heads = range(g * Q_PER_KV, (g + 1) * Q_PER_KV)
        qt_g = jnp.concatenate([qts[i][h * HEAD_DIM:(h + 1) * HEAD_DIM, :] for h in heads], axis=1)
        s_own = _dot(subs[i]["keys"][:, lo:hi], qt_g)
        if subs[i]["mask"] is not None:
            s_own = jnp.where(subs[i]["mask"], s_own, NEG)
        s_shared = None if shared is None else _dot(shared[0][:, lo:hi], qt_g)
        return s_own, s_shared

    def weighted_values(i, g, s_own, s_shared):
        m = jnp.maximum(jnp.max(s_own, axis=0, keepdims=True), sink_rows[g])
        if s_shared is not None:
            m = jnp.maximum(m, jnp.max(s_shared, axis=0, keepdims=True))
        o_aug = _dot(with_ones(subs[i]["vals_t"], g), jnp.exp2(s_own - m).astype(bf16))
        if s_shared is not None:
            o_aug = o_aug + _dot(shared_vals[g], jnp.exp2(s_shared - m).astype(bf16))
        den = o_aug[HEAD_DIM:HEAD_DIM + 1, :] + jnp.exp2(sink_rows[g] - m)
        return o_aug[:HEAD_DIM, :] / den

    passes = [(i, g) for i in range(len(subs)) for g in range(N_KV_HEADS)]
    out_t = {}
    pending = scores(*passes[0])
    for n, (i, g) in enumerate(passes):
        nxt = scores(*passes[n + 1]) if n + 1 < len(passes) else None
        out_t[i, g] = weighted_values(i, g, *pending)
        pending = nxt
    for i in range(len(subs)):
        blocks = [jnp.concatenate([out_t[i, g][:, j * bq:(j + 1) * bq]
                                   for g in range(N_KV_HEADS)], axis=0).T
                  for j in range(Q_PER_KV)]
        o_ref[i * bq:(i + 1) * bq, :] = jnp.concatenate(blocks, axis=1).astype(o_ref.dtype)


ATTN_HEAD_PERM = np.array([(Q_PER_KV * g + j) * HEAD_DIM + d for j in range(Q_PER_KV)
                           for g in range(N_KV_HEADS) for d in range(HEAD_DIM)])


def _permute_heads(a):
    blocks = a.reshape(N_KV_HEADS, Q_PER_KV, HEAD_DIM, *a.shape[1:])
    return blocks.swapaxes(0, 1).reshape(a.shape)


def _ctx_attn_kernel(sink_ref, q_ref, k_ref, vt_ref, o_ref, *, bq, seq):
    subs = []
    for s in range(q_ref.shape[0] // seq):
        keys = k_ref[s * seq:(s + 1) * seq, :]
        subs += [dict(keys=keys, vals_t=vt_ref[s].astype(bf16), mask=None)] * (seq // bq)
    _attention_core(sink_ref, q_ref, o_ref, subs, None, bq)


def _context_attention(q, k, vt, sink, *, seq, bq=128, seqs_per_step=4):
    t = q.shape[0]
    rows = seq * seqs_per_step
    return pl.pallas_call(
        functools.partial(_ctx_attn_kernel, bq=bq, seq=seq),
        grid=(t // rows,),
        in_specs=[pl.BlockSpec(memory_space=pltpu.SMEM),
                  pl.BlockSpec((rows, ATTN_WIDTH), lambda b: (b, 0)),
                  pl.BlockSpec((rows, KV_WIDTH), lambda b: (b, 0)),
                  pl.BlockSpec((seqs_per_step, KV_WIDTH, seq), lambda b: (b, 0, 0))],
        out_specs=pl.BlockSpec((rows, ATTN_WIDTH), lambda b: (b, 0)),
        out_shape=jax.ShapeDtypeStruct((t, ATTN_WIDTH), bf16),
        compiler_params=_params("arbitrary"),
        name="ctx_attn",
    )(sink, q, k, vt)


def _lat_attn_kernel(sink_ref, q_ref, k_ref, vt_ref, ck_ref, cv_ref, *rest, bq, seq, n_cast):
    o_ref = rest[n_cast]
    for src_ref, dst_ref in zip(rest[:n_cast], rest[n_cast + 1:]):
        dst_ref[...] = src_ref[...].astype(dst_ref.dtype)
    band = bq + 2 * WINDOW
    n_col = Q_PER_KV * bq
    delta = ((lax.broadcasted_iota(jnp.int32, (band, n_col), 1) & (bq - 1))
             - lax.broadcasted_iota(jnp.int32, (band, n_col), 0))
    n_sub = q_ref.shape[0] // bq
    subs = []
    for sb in range(n_sub):
        i = pl.program_id(1) * n_sub + sb
        start = pl.multiple_of(jnp.clip(i * bq - WINDOW, 0, seq - band), LANES)
        shifted = delta + (i * bq - start + WINDOW)
        subs.append(dict(keys=k_ref[pl.ds(start, band), :],
                         vals_t=vt_ref[:, pl.ds(start, band)],
                         mask=shifted.astype(jnp.uint32) <= 2 * WINDOW))
    shared = (ck_ref[...].astype(bf16), cv_ref[...].T.astype(bf16))
    _attention_core(sink_ref, q_ref, o_ref, subs, shared, bq)


LAT_ATTN_SUB_BLOCKS = 8


def _latent_attention(q, k, vt, ck, cv, sink, *, seq, cast=(), bq=128,
                      sub_blocks=LAT_ATTN_SUB_BLOCKS):
    nb, past = ck.shape[0], ck.shape[1]
    assert bq == LANES and N_KV_HEADS * HEAD_DIM == LANES
    q3, k3 = (a.reshape(nb, seq, a.shape[-1]) for a in (q, k))
    tq = bq * sub_blocks
    steps_per_seq = seq // tq
    n_steps = nb * steps_per_seq
    cast3 = [a.reshape((n_steps, -1) + a.shape[2:]) for a in cast]
    cast_specs = [pl.BlockSpec((None,) + a.shape[1:], lambda b, i: (b * steps_per_seq + i, 0, 0))
                  for a in cast3]
    out, *cast_out = pl.pallas_call(
        functools.partial(_lat_attn_kernel, bq=bq, seq=seq, n_cast=len(cast)),
        grid=(nb, steps_per_seq),
        in_specs=[pl.BlockSpec(memory_space=pltpu.SMEM),
                  pl.BlockSpec((None, tq, ATTN_WIDTH), lambda b, i: (b, i, 0)),
                  pl.BlockSpec((None, seq, KV_WIDTH), lambda b, i: (b, 0, 0)),
                  pl.BlockSpec((None, KV_WIDTH, seq), lambda b, i: (b, 0, 0)),
                  pl.BlockSpec((None, past, KV_WIDTH), lambda b, i: (b, 0, 0)),
                  pl.BlockSpec((None, past, KV_WIDTH), lambda b, i: (b, 0, 0))] + cast_specs,
        out_specs=[pl.BlockSpec((None, tq, ATTN_WIDTH), lambda b, i: (b, i, 0))] + cast_specs,
        out_shape=[jax.ShapeDtypeStruct((nb, seq, ATTN_WIDTH), bf16)]
        + [jax.ShapeDtypeStruct(a.shape, bf16) for a in cast3],
        compiler_params=_params("arbitrary", "arbitrary"),
        name="lat_attn",
    )(sink, q3, k3, vt, ck, cv, *cast3)
    return [out.reshape(nb * seq, ATTN_WIDTH)] + [o.reshape(a.shape) for o, a in zip(cast_out, cast)]


def _route(logits_t):
    r = logits_t.shape[1]
    big = jnp.float32(LANES)
    crow = lax.broadcasted_iota(jnp.int32, (F32_SUBLANES, r), 0)
    crow_f = crow.astype(f32)
    lc = jnp.where(crow < N_GROUPS, logits_t[COARSE_LANE0:COARSE_LANE0 + F32_SUBLANES, :], -jnp.inf)
    mc = jnp.max(lc, axis=0, keepdims=True)
    grp = jnp.min(jnp.where(lc == mc, crow_f, big), axis=0, keepdims=True)
    pg = 1.0 / jnp.sum(jnp.exp(lc - mc), axis=0, keepdims=True)
    erow = lax.broadcasted_iota(jnp.int32, (N_EXPERTS, r), 0).astype(f32)
    in_g = jnp.floor(erow * (1.0 / EXPERTS_PER_GROUP)) == grp
    fl = jnp.where(in_g, logits_t[0:N_EXPERTS, :], -jnp.inf)
    m1 = jnp.max(fl, axis=0, keepdims=True)
    i1 = jnp.min(jnp.where(fl == m1, erow, big), axis=0, keepdims=True)
    fl2 = jnp.where(erow == i1, -jnp.inf, fl)
    m2 = jnp.max(fl2, axis=0, keepdims=True)
    i2 = jnp.min(jnp.where(fl2 == m2, erow, big), axis=0, keepdims=True)
    e2 = jnp.exp(m2 - m1)
    p1 = pg / (1.0 + e2)
    p2 = pg * e2 / (1.0 + e2)
    packed = jnp.where(crow == 0, i1, jnp.where(crow == 1, i2, jnp.where(crow == 2, p1, p2)))
    return jnp.where(crow < 4, packed, 0.0)


def _post_kernel(x_ref, attn_ref, u_ref, up_ref, un_ref, bg_ref, mod_ref, ag_ref, cg_ref, cw_ref,
                 cb_ref, wo_ref, n2_ref, wr_ref, br_ref, x1_ref, h2_ref, route_ref, *, tm, seq):
    i = pl.program_id(0)
    u = u_ref[...].astype(f32)
    rows = lax.broadcasted_iota(jnp.int32, (tm, 1), 0)
    spos = (i * tm + rows) % seq
    u_dn = jnp.where(rows == 0, up_ref[...].astype(f32)[BF16_SUBLANES - 1:, :],
                     pltpu.roll(u, 1, axis=0))
    u_dn = jnp.where(spos == 0, 0.0, u_dn)
    u_up = jnp.where(rows == tm - 1, un_ref[...].astype(f32)[0:1, :], pltpu.roll(u, tm - 1, axis=0))
    u_up = jnp.where(spos == seq - 1, 0.0, u_up)
    y = u_dn * cw_ref[0:1, :] + u * cw_ref[1:2, :] + u_up * cw_ref[2:3, :] + cb_ref[...]
    conv = bg_ref[...].astype(f32) * y
    attn_n = (_rms(attn_ref[...].astype(f32)) * ag_ref[...]).astype(bf16)
    conv_n = (_rms(conv) * cg_ref[...]).astype(bf16)
    subs = [slice(s * POST_SUB, (s + 1) * POST_SUB) for s in range(tm // POST_SUB)]
    mixed = [_dot(attn_n[rs], wo_ref[0:ATTN_WIDTH, :]) + _dot(conv_n[rs], wo_ref[ATTN_WIDTH:, :])
             for rs in subs]
    logits = []
    for s, rs in enumerate(subs):
        x1 = x_ref[rs, :] + mod_ref[:, G1:G1 + D_MODEL] * mixed[s]
        x1_ref[rs, :] = x1
        h2 = _rms(x1) * n2_ref[...]
        h2 = h2 * (1.0 + mod_ref[:, SC2:SC2 + D_MODEL]) + mod_ref[:, SH2:SH2 + D_MODEL]
        for c in range(D_MODEL // LANES):
            h2_ref[pl.ds(rs.start * F32_SUBLANES + c, POST_SUB, stride=F32_SUBLANES), :] = (
                h2[:, c * LANES:(c + 1) * LANES])
        h2_hi = h2.astype(bf16)
        h2_lo = (h2 - h2_hi.astype(f32)).astype(bf16)
        both = _dot(h2_hi, wr_ref[...])
        logits.append(both[:, :LANES] + both[:, LANES:] + _dot(h2_lo, wr_ref[:, :LANES]))
    for s, rs in enumerate(subs):
        route_ref[:, rs] = _route((logits[s] + br_ref[...]).T)


def _post(x, attn, u, bg, mod3, p, *, seq, mod_row0, tm=4 * POST_SUB):
    t = x.shape[0]
    tiles_per_seq = max(seq // tm, 1)
    halo = BF16_SUBLANES
    n_halo = t // halo
    row = lambda i: (mod_row0 + (i // tiles_per_seq if mod_row0 else 0), 0, 0)
    tile = lambda w: pl.BlockSpec((tm, w), lambda i: (i, 0))
    full = lambda a: pl.BlockSpec(a.shape, lambda i: (0,) * a.ndim)
    small = [p["attn_out_g"], p["conv_out_g"], p["conv_w"], p["conv_b"], p["w_o"], p["norm2_g"],
             p["w_router"], p["b_router"]]
    return pl.pallas_call(
        functools.partial(_post_kernel, tm=tm, seq=seq),
        grid=(t // tm,),
        in_specs=[tile(D_MODEL), tile(ATTN_WIDTH), tile(CONV_WIDTH),
                  pl.BlockSpec((halo, CONV_WIDTH),
                               lambda i: (jnp.maximum(i * (tm // halo) - 1, 0), 0)),
                  pl.BlockSpec((halo, CONV_WIDTH),
                               lambda i: (jnp.minimum((i + 1) * (tm // halo), n_halo - 1), 0)),
                  tile(CONV_WIDTH),
                  pl.BlockSpec((None, 1, 6 * D_MODEL), row)] + [full(a) for a in small],
        out_specs=[tile(D_MODEL), pl.BlockSpec((tm * F32_SUBLANES, LANES), lambda i: (i, 0)),
                   pl.BlockSpec((F32_SUBLANES, tm), lambda i: (0, i))],
        out_shape=[jax.ShapeDtypeStruct((t, D_MODEL), f32),
                   jax.ShapeDtypeStruct((t * F32_SUBLANES, LANES), f32),
                   jax.ShapeDtypeStruct((F32_SUBLANES, t), f32)],
        compiler_params=_params("arbitrary"),
        name="post",
    )(x, attn, u, u, u, bg, mod3, *small)


def _max_chunks(blk):
    return -(-TOP_K * blk // MOE_CHUNK) + EXPERTS_PER_STEP


def _sorted_rows(blk):
    rows = TOP_K * blk + N_EXPERTS * (F32_SUBLANES - 1) + MOE_CHUNK
    assert rows % F32_SUBLANES == 0
    return rows + (F32_SUBLANES if rows // F32_SUBLANES % 2 == 0 else 0)


def _dispatch_tables(route, blk):
    nblk = route.shape[1] // blk
    n_slots = TOP_K * blk
    by_block = lambda a: a.reshape(TOP_K, nblk, blk).transpose(1, 0, 2)
    experts = by_block(route[0:TOP_K].astype(jnp.int32))
    weights = by_block(route[TOP_K:2 * TOP_K])
    ids = jnp.arange(N_EXPERTS, dtype=jnp.int32)
    onehot = experts.reshape(nblk, n_slots // LANES, LANES, 1) == ids
    earlier = jnp.tril(jnp.ones((LANES, LANES), bf16), -1)
    within = jnp.einsum("ij,bgje->bgie", earlier, onehot.astype(bf16),
                        preferred_element_type=f32)
    per_group = jnp.sum(onehot, axis=2, dtype=jnp.int32)
    before_group = jnp.cumsum(per_group, axis=1) - per_group
    counts = jnp.sum(per_group, axis=1)
    padded = (counts + F32_SUBLANES - 1) // F32_SUBLANES * F32_SUBLANES
    lo = jnp.cumsum(padded, axis=1) - padded
    row = within.astype(jnp.int32) + before_group[:, :, None, :] + lo[:, None, None, :]
    row = jnp.sum(jnp.where(onehot, row, 0), axis=-1).reshape(nblk, TOP_K, blk)

    eg, n_steps, max_c = EXPERTS_PER_STEP, N_EXPERTS // EXPERTS_PER_STEP, _max_chunks(blk)
    lo = lo.reshape(nblk, n_steps, eg)
    counts = counts.reshape(nblk, n_steps, eg)
    n_ch = (counts + MOE_CHUNK - 1) // MOE_CHUNK
    cum = jnp.cumsum(n_ch, axis=2)
    total = cum[..., -1]
    pos = jnp.arange(max_c, dtype=jnp.int32)
    j = jnp.sum(cum[:, :, None, :] <= pos[None, None, :, None], axis=-1, dtype=jnp.int32)
    j = jnp.minimum(j, eg - 1)
    hit = j[..., None] == jnp.arange(eg, dtype=jnp.int32)
    pick = lambda a: jnp.sum(jnp.where(hit, a[:, :, None, :], 0), axis=-1)
    done = (pos - pick(cum - n_ch)) * MOE_CHUNK
    live = pos < total[..., None]
    flat = lambda a: jnp.where(live, a, 0).reshape(nblk, 1, n_steps * max_c)
    return (row[:, 0:1], row[:, 1:2], weights[:, 0:1], weights[:, 1:2], flat(j),
            flat(pick(lo) + done), flat(jnp.clip(pick(counts) - done, 0, MOE_CHUNK)),
            total[:, None, :])


def _moe_kernel(row0_ref, row1_ref, w0_ref, w1_ref, cj_ref, cfirst_ref, cvalid_ref, nchunks_ref,
                mod_ref, fg_ref, h_hbm, wg_hbm, wu_hbm, wd_hbm, x1_hbm, out_hbm,
                sorted_ref, yt_ref, h_ref, wg_ref, wu_ref, wd_ref, xbuf_ref, obuf_ref, sem_h, sem_w,
                sem_in, sem_out, *, blk):
    n_col = D_MODEL // LANES
    pitch = _sorted_rows(blk)
    blk_id, step = pl.program_id(0), pl.program_id(1)
    n_steps = pl.num_programs(1)
    last_step = step == n_steps - 1
    max_c = _max_chunks(blk)
    n_sub = blk // FINAL_ROWS
    assignment_rows = (row0_ref, w0_ref), (row1_ref, w1_ref)
    run = blk_id * n_steps + step
    ahead = WEIGHT_BUFFERS - 1

    def weight_copies(r):
        slot = r % WEIGHT_BUFFERS
        experts = pl.ds(r % n_steps * EXPERTS_PER_STEP, EXPERTS_PER_STEP)
        pairs = (wg_hbm, wg_ref), (wu_hbm, wu_ref), (wd_hbm, wd_ref)
        return [pltpu.make_async_copy(hbm.at[experts], buf.at[slot], sem_w.at[k, slot])
                for k, (hbm, buf) in enumerate(pairs)]

    def h_copy(b):
        tiles = pl.ds(b * (blk * F32_SUBLANES), blk * F32_SUBLANES)
        return pltpu.make_async_copy(h_hbm.at[tiles, :], h_ref, sem_h.at[0])

    @pl.when(run == 0)
    def _():
        h_copy(0).start()
        for r in range(ahead):
            for copy in weight_copies(r):
                copy.start()

    @pl.when(run + ahead < pl.num_programs(0) * n_steps)
    def _():
        for copy in weight_copies(run + ahead):
            copy.start()

    def column_tile(row):
        return pl.ds(row, F32_SUBLANES, stride=pitch)

    def x1_copy(s):
        rows = pl.ds(blk_id * blk + s * FINAL_ROWS, FINAL_ROWS)
        slot = s % FINAL_SLOTS
        return pltpu.make_async_copy(x1_hbm.at[rows, :], xbuf_ref.at[slot], sem_in.at[slot])

    def out_copy(s):
        rows = pl.ds(blk_id * blk + s * FINAL_ROWS, FINAL_ROWS)
        slot = s % FINAL_SLOTS
        return pltpu.make_async_copy(obuf_ref.at[slot], out_hbm.at[rows, :], sem_out.at[slot])

    @pl.when(last_step)
    def _():
        for s in range(min(FINAL_SLOTS, n_sub)):
            x1_copy(s).start()

    @pl.when((blk_id == 0) & (step == 0))
    def _():
        zero_rows = F32_SUBLANES * LANES
        zeros = jnp.zeros((zero_rows, LANES), f32)

        def clear(i, carry):
            sorted_ref[pl.ds(pl.multiple_of(i * zero_rows, zero_rows), zero_rows), :] = zeros
            return carry

        lax.fori_loop(0, n_col * pitch // zero_rows, clear, 0)
        sorted_ref[n_col * pitch - zero_rows:, :] = zeros

    @pl.when(step == 0)
    def _():
        h_copy(blk_id).wait()

        def dispatch(i, carry):
            first = pl.multiple_of(i * DISPATCH_ROWS * F32_SUBLANES, DISPATCH_ROWS * F32_SUBLANES)
            for r in range(DISPATCH_ROWS):
                tile = h_ref[pl.ds(first + r * F32_SUBLANES, F32_SUBLANES), :]
                for rows_ref, _ in assignment_rows:
                    sorted_ref[column_tile(rows_ref[0, i * DISPATCH_ROWS + r]), :] = tile
            return carry

        lax.fori_loop(0, blk // DISPATCH_ROWS, dispatch, 0)

        @pl.when(blk_id + 1 < pl.num_programs(0))
        def _():
            h_copy(blk_id + 1).start()

    def chunk_rows(p, c):
        first = pl.multiple_of(cfirst_ref[0, p], F32_SUBLANES)
        return pl.ds(pl.multiple_of(c * pitch + first, F32_SUBLANES), MOE_CHUNK)

    def hidden(p):
        x = jnp.concatenate([sorted_ref[chunk_rows(p, c), :] for c in range(n_col)],
                            axis=1).astype(bf16)
        j = cj_ref[0, p]
        g = _dot(x, wg_ref[run % WEIGHT_BUFFERS, j])
        up = _dot(x, wu_ref[run % WEIGHT_BUFFERS, j])
        return ((g * jax.nn.sigmoid(g)) * up).astype(bf16)

    def project(p, act):
        out = _dot(act, wd_ref[run % WEIGHT_BUFFERS, cj_ref[0, p]])
        own = lax.broadcasted_iota(jnp.int32, (MOE_CHUNK, LANES), 0) < cvalid_ref[0, p]
        for c in range(n_col):
            rows = chunk_rows(p, c)
            sorted_ref[rows, :] = jnp.where(own, out[:, c * LANES:(c + 1) * LANES],
                                            sorted_ref[rows, :])

    def run_chunks(ps):
        acts = [hidden(p) for p in ps]
        for p, act in zip(ps, acts):
            project(p, act)

    def group(gi, carry):
        run_chunks([step * max_c + CHUNKS_PER_TRIP * gi + k for k in range(CHUNKS_PER_TRIP)])
        return carry

    for copy in weight_copies(run):
        copy.wait()
    n_chunks = nchunks_ref[0, step]
    lax.fori_loop(0, n_chunks // CHUNKS_PER_TRIP, group, 0)
    for left in range(1, CHUNKS_PER_TRIP):
        @pl.when(n_chunks % CHUNKS_PER_TRIP == left)
        def _():
            run_chunks([step * max_c + n_chunks - left + k for k in range(left)])

    @pl.when(last_step)
    def _():
        gate2 = mod_ref[:, G2:G2 + D_MODEL]
        for s in range(n_sub):
            slot = s % FINAL_SLOTS
            for r in range(FINAL_ROWS):
                t = s * FINAL_ROWS + r
                weighted = [sorted_ref[column_tile(rows_ref[0, t]), :] * w_ref[0, t]
                            for rows_ref, w_ref in assignment_rows]
                yt_ref[r * F32_SUBLANES:(r + 1) * F32_SUBLANES, :] = functools.reduce(
                    jnp.add, weighted)
            x1_copy(s).wait()
            if s >= FINAL_SLOTS:
                out_copy(s - FINAL_SLOTS).wait()
            y = jnp.concatenate([yt_ref[pl.ds(c, FINAL_ROWS, stride=F32_SUBLANES), :]
                                 for c in range(n_col)], axis=1)
            x2 = xbuf_ref[slot] + gate2 * y
            obuf_ref[slot] = _rms(x2) * fg_ref[...]
            out_copy(s).start()
            if s + FINAL_SLOTS < n_sub:
                x1_copy(s + FINAL_SLOTS).start()
        for s in range(max(n_sub - FINAL_SLOTS, 0), n_sub):
            out_copy(s).wait()


def _moe(h2_tiles, tables, x1, mod3, final_g, w_gate, w_up, w_down, *, seq, mod_row0,
         blk=MOE_BLOCK):
    t = h2_tiles.shape[0] // F32_SUBLANES
    eg, n_steps = EXPERTS_PER_STEP, N_EXPERTS // EXPERTS_PER_STEP
    blocks_per_seq = max(seq // blk, 1)
    row = lambda b, g: (mod_row0 + (b // blocks_per_seq if mod_row0 else 0), 0, 0)
    smem = lambda n: pl.BlockSpec((None, 1, n), lambda b, g: (b, 0, 0), memory_space=pltpu.SMEM)
    final_buf = pltpu.VMEM((FINAL_SLOTS, FINAL_ROWS, D_MODEL), f32)
    assert t // blk * n_steps >= WEIGHT_BUFFERS - 1
    weight_buf = lambda w: pltpu.VMEM((WEIGHT_BUFFERS, eg) + w.shape[1:], w.dtype)
    return pl.pallas_call(
        functools.partial(_moe_kernel, blk=blk),
        grid=(t // blk, n_steps),
        in_specs=[smem(blk)] * 4 + [smem(n_steps * _max_chunks(blk))] * 3 + [
                  smem(n_steps),
                  pl.BlockSpec((None, 1, 6 * D_MODEL), row),
                  pl.BlockSpec((1, D_MODEL), lambda b, g: (0, 0))]
        + [pl.BlockSpec(memory_space=pl.ANY)] * 5,
        out_specs=pl.BlockSpec(memory_space=pl.ANY),
        out_shape=jax.ShapeDtypeStruct((t, D_MODEL), f32),
        scratch_shapes=[pltpu.VMEM((D_MODEL // LANES * _sorted_rows(blk), LANES), f32),
                        pltpu.VMEM((FINAL_ROWS * F32_SUBLANES, LANES), f32),
                        pltpu.VMEM((blk * F32_SUBLANES, LANES), f32),
                        weight_buf(w_gate), weight_buf(w_up), weight_buf(w_down),
                        final_buf, final_buf,
                        pltpu.SemaphoreType.DMA((1,)),
                        pltpu.SemaphoreType.DMA((3, WEIGHT_BUFFERS)),
                        pltpu.SemaphoreType.DMA((FINAL_SLOTS,)),
                        pltpu.SemaphoreType.DMA((FINAL_SLOTS,))],
        compiler_params=_params("arbitrary", "arbitrary", vmem=MOE_VMEM_LIMIT_BYTES),
        name="moe",
    )(*tables, mod3, final_g, h2_tiles, w_gate, w_up, w_down, x1)


def _mixers(x, mod3, p, *, seq, mod_row0, ctx_kv, cast=()):
    if ctx_kv is not None:
        q, k, vt, u, bg = _inproj(x, mod3, p["norm1_g"], p["w_in"], seq=seq, mod_row0=mod_row0,
                                  rope_tabs=_rope_tables(seq))
        attn, *cast = _latent_attention(q, k, vt, ctx_kv[0], ctx_kv[1], p["sink"], seq=seq,
                                        cast=cast)
        kv_t = None
    else:
        q, k, kt, vt, u, bg = _inproj(x, mod3, p["norm1_g"], p["w_in"], seq=seq,
                                      mod_row0=mod_row0)
        attn = _context_attention(q, k, vt, p["sink"], seq=seq)
        kv_t = (kt, vt)
    x1, h2_tiles, route = _post(x, attn, u, bg, mod3, p, seq=seq, mod_row0=mod_row0)
    return x1, h2_tiles, route, kv_t, cast


def _experts(x1, h2_tiles, tables, mod3, p, *, seq, mod_row0):
    out = _moe(h2_tiles, tables, x1, mod3, p["final_g"], p["w_gate"], p["w_up"], p["w_down"],
               seq=seq, mod_row0=mod_row0)
    return out.reshape(-1, seq, D_MODEL)


def kernel(x_prompt, x_sample, cache_k, cache_v, c, c_ctx, w_mod, b_mod, norm1_g, w_in, conv_w,
           conv_b, sink, attn_out_g, conv_out_g, w_o, norm2_g, w_coarse, b_coarse, w_fine, b_fine,
           w_gate, w_up, w_down, final_g):
    batch, seq, _ = x_prompt.shape
    dec_batch, dec_seq, _ = x_sample.shape
    past = cache_k.shape[2]
    assert w_mod.shape[0] == 1 and 1 + dec_batch <= MOD_ROWS

    cvecs = jnp.concatenate([c_ctx[None], c, jnp.zeros((MOD_ROWS - 1 - dec_batch, D_MODEL), f32)])
    mod3 = _modulation(cvecs, w_mod[0], b_mod[0]).reshape(MOD_ROWS, 1, 6 * D_MODEL)

    pad = jnp.zeros((D_MODEL, LANES - N_EXPERTS - N_GROUPS), f32)
    w_mix = jnp.concatenate([_permute_heads(w_o[0][:ATTN_WIDTH]), w_o[0][ATTN_WIDTH:]])
    p = {
        "norm1_g": norm1_g, "w_in": w_in[0].astype(bf16), "conv_w": conv_w[0], "conv_b": conv_b,
        "sink": sink[0], "attn_out_g": _permute_heads(attn_out_g[0])[None],
        "conv_out_g": conv_out_g, "w_o": w_mix.astype(bf16), "norm2_g": norm2_g,
        "w_router": _split_bf16(jnp.concatenate([w_fine[0], w_coarse[0], pad], axis=1)),
        "b_router": jnp.concatenate([b_fine[0], b_coarse[0], pad[0]])[None],
        "final_g": final_g[None],
    }

    x1_p, h2_p, route_p, kv_t, _ = _mixers(x_prompt.reshape(batch * seq, D_MODEL), mod3, p,
                                           seq=seq, mod_row0=0, ctx_kv=None)
    new_k, new_v = (a.reshape(batch, 1, N_KV_HEADS, HEAD_DIM, seq).transpose(0, 1, 4, 2, 3)
                    for a in kv_t)

    ctx_kv = (cache_k[:, 0].reshape(dec_batch, past, KV_WIDTH),
              cache_v[:, 0].reshape(dec_batch, past, KV_WIDTH))
    experts_f32 = (w_gate[0], w_up[0], w_down[0])
    ride = N_EXPERTS % (dec_batch * dec_seq // (LANES * LAT_ATTN_SUB_BLOCKS)) == 0
    x1_s, h2_s, route_s, _, experts_bf16 = _mixers(
        x_sample.reshape(dec_batch * dec_seq, D_MODEL), mod3, p, seq=dec_seq, mod_row0=1,
        ctx_kv=ctx_kv, cast=experts_f32 if ride else ())
    if not ride:
        experts_bf16 = [w.astype(bf16) for w in experts_f32]
    p = dict(p, w_gate=experts_bf16[0], w_up=experts_bf16[1], w_down=experts_bf16[2])

    n_blk_p = batch * seq // MOE_BLOCK
    tables = _dispatch_tables(jnp.concatenate([route_p, route_s], axis=1), MOE_BLOCK)
    y_prompt = _experts(x1_p, h2_p, [t[:n_blk_p] for t in tables], mod3, p, seq=seq, mod_row0=0)
    y_sample = _experts(x1_s, h2_s, [t[n_blk_p:] for t in tables], mod3, p, seq=dec_seq,
                        mod_row0=1)
    return y_prompt, y_sample, new_k, new_v
```

```python
import functools

import jax
import jax.numpy as jnp
import numpy as np
from jax import lax
from jax.experimental import pallas as pl
from jax.experimental.pallas import tpu as pltpu

D_MODEL = 1024
HEAD_DIM = 64
ATTN_WIDTH = 512
N_HEADS = 8
N_KV_HEADS = 2
Q_PER_KV = 4
KV_WIDTH = 128
CONV_WIDTH = 512
CONV_K = 3
WINDOW = 128
GRID_W = 64
ROPE_BASE = 10000.0
N_FREQ = 16
N_GROUPS = 4
EXPERTS_PER_GROUP = 8
N_EXPERTS = 32
TOP_K = 2
D_EXPERT = 256
IN_WIDTH = ATTN_WIDTH + 2 * KV_WIDTH + 3 * CONV_WIDTH
EPS = 1e-6
NEG = -1e30
SCALE = HEAD_DIM ** -0.5
LOG2_E = 1.4426950408889634

LANES = 128
F32_SUBLANES = 8
BF16_SUBLANES = 16
assert D_MODEL == F32_SUBLANES * LANES

INPROJ_ROWS = 1024
INPROJ_SUB = 512
ATTN_PASS_HEADS = 4
POST_SUB = 256
MOE_BLOCK = 2048
MOE_CHUNK = 160
CHUNKS_PER_TRIP = 4
DISPATCH_ROWS = 128
EXPERTS_PER_STEP = 4
WEIGHT_BUFFERS = 3
FINAL_ROWS = 256
FINAL_SLOTS = 4
MOE_VMEM_LIMIT_BYTES = 60 * 1024 * 1024
VMEM_LIMIT_BYTES = 48 * 1024 * 1024

SH1, SC1, G1, SH2, SC2, G2 = (i * D_MODEL for i in range(6))
MOD_ROWS = 8

COARSE_LANE0 = N_EXPERTS

f32 = jnp.float32
bf16 = jnp.bfloat16


def _params(*semantics, vmem=VMEM_LIMIT_BYTES):
    return pltpu.CompilerParams(dimension_semantics=semantics, vmem_limit_bytes=vmem)


def _rms(x):
    return x * lax.rsqrt(jnp.mean(x * x, axis=-1, keepdims=True) + EPS)


def _dot(a, b):
    return jnp.dot(a, b, preferred_element_type=f32)


def _split_bf16(w):
    hi = w.astype(bf16)
    lo = (w - hi.astype(f32)).astype(bf16)
    return jnp.concatenate([hi, lo], axis=1)


def _mod_kernel(cv_ref, w_ref, b_ref, o_ref):
    a = cv_ref[...]
    a = a * jax.nn.sigmoid(a)
    o_ref[...] = jnp.dot(a, w_ref[...], precision=lax.Precision.HIGHEST,
                         preferred_element_type=f32) + b_ref[...]


def _modulation(cvecs, w_mod, b_mod):
    tn = 2 * D_MODEL
    return pl.pallas_call(
        _mod_kernel,
        grid=(6 * D_MODEL // tn,),
        in_specs=[pl.BlockSpec((MOD_ROWS, D_MODEL), lambda j: (0, 0)),
                  pl.BlockSpec((D_MODEL, tn), lambda j: (0, j)),
                  pl.BlockSpec((1, tn), lambda j: (0, j))],
        out_specs=pl.BlockSpec((MOD_ROWS, tn), lambda j: (0, j)),
        out_shape=jax.ShapeDtypeStruct((MOD_ROWS, 6 * D_MODEL), f32),
        compiler_params=_params("arbitrary"),
        name="mod",
    )(cvecs, w_mod, b_mod.reshape(1, -1))


def _swap_halves(x):
    lane = lax.broadcasted_iota(jnp.int32, x.shape, 1)
    up = pltpu.roll(x, LANES - N_FREQ, axis=1)
    dn = pltpu.roll(x, N_FREQ, axis=1)
    return jnp.where((lane % (2 * N_FREQ)) < N_FREQ, up, dn)


def _rope(x, cos, sin):
    parts = []
    for c in range(x.shape[1] // LANES):
        xc = x[:, c * LANES:(c + 1) * LANES]
        parts.append(xc * cos + _swap_halves(xc) * sin)
    return parts[0] if len(parts) == 1 else jnp.concatenate(parts, axis=1)


def _inproj_kernel(*refs, rope, seq):
    if rope:
        x_ref, mod_ref, g_ref, w_ref, cos_ref, sin_ref, q_ref, k_ref, vt_ref, u_ref, bg_ref = refs
    else:
        x_ref, mod_ref, g_ref, w_ref, q_ref, k_ref, kt_ref, vt_ref, u_ref, bg_ref = refs
    tm = x_ref.shape[0]
    subs = [slice(s * INPROJ_SUB, (s + 1) * INPROJ_SUB) for s in range(tm // INPROJ_SUB)]
    zs = []
    for rs in subs:
        h = _rms(x_ref[rs, :]) * g_ref[...]
        h = h * (1.0 + mod_ref[:, SC1:SC1 + D_MODEL]) + mod_ref[:, SH1:SH1 + D_MODEL]
        zs.append(_dot(h.astype(bf16), w_ref[...]))
    for rs, z in zip(subs, zs):
        o = 0
        q = z[:, o:o + ATTN_WIDTH]; o += ATTN_WIDTH
        k = z[:, o:o + KV_WIDTH]; o += KV_WIDTH
        v = z[:, o:o + KV_WIDTH]; o += KV_WIDTH
        bg = z[:, o:o + CONV_WIDTH]; o += CONV_WIDTH
        cg = z[:, o:o + CONV_WIDTH]; o += CONV_WIDTH
        xin = z[:, o:o + CONV_WIDTH]
        if rope:
            q = _rope(q, cos_ref[rs, :], sin_ref[rs, :])
            k = _rope(k, cos_ref[rs, :], sin_ref[rs, :])
        q_ref[rs, :] = q.astype(q_ref.dtype)
        k_ref[rs, :] = k.astype(k_ref.dtype)
        if rope:
            vt_ref[:, rs] = v.T.astype(vt_ref.dtype)
        else:
            for s in range(rs.start // seq, rs.stop // seq):
                rows = slice(s * seq - rs.start, (s + 1) * seq - rs.start)
                kt_ref[s] = k[rows].T
                vt_ref[s] = v[rows].T
        u_ref[rs, :] = (cg * xin).astype(u_ref.dtype)
        bg_ref[rs, :] = bg.astype(bg_ref.dtype)


def _inproj(x, mod3, norm_g, w_in, *, seq, mod_row0, rope_tabs=None, tm=INPROJ_ROWS):
    t = x.shape[0]
    tiles_per_seq = seq // tm
    assert rope_tabs is not None or INPROJ_SUB % seq == 0
    row = lambda i: (mod_row0 + (i // tiles_per_seq if mod_row0 else 0), 0, 0)
    in_specs = [pl.BlockSpec((tm, D_MODEL), lambda i: (i, 0)),
                pl.BlockSpec((None, 1, 6 * D_MODEL), row),
                pl.BlockSpec((1, D_MODEL), lambda i: (0, 0)),
                pl.BlockSpec((D_MODEL, IN_WIDTH), lambda i: (0, 0))]
    args = [x, mod3, norm_g, w_in]
    if rope_tabs is not None:
        in_specs += [pl.BlockSpec((tm, LANES), lambda i: (i % tiles_per_seq, 0))] * 2
        args += list(rope_tabs)
    rows = lambda w: (pl.BlockSpec((tm, w), lambda i: (i, 0)), jax.ShapeDtypeStruct((t, w), bf16))
    if rope_tabs is not None:
        transposed = [(pl.BlockSpec((None, KV_WIDTH, tm),
                                    lambda i: (i // tiles_per_seq, 0, i % tiles_per_seq)),
                       jax.ShapeDtypeStruct((t // seq, KV_WIDTH, seq), bf16))]
    else:
        transposed = [(pl.BlockSpec((tm // seq, KV_WIDTH, seq), lambda i: (i, 0, 0)),
                       jax.ShapeDtypeStruct((t // seq, KV_WIDTH, seq), f32))] * 2
    outs = [rows(ATTN_WIDTH), rows(KV_WIDTH)] + transposed + [rows(CONV_WIDTH), rows(CONV_WIDTH)]
    out_specs, out_shape = (list(z) for z in zip(*outs))
    return pl.pallas_call(
        functools.partial(_inproj_kernel, rope=rope_tabs is not None, seq=seq),
        grid=(t // tm,),
        in_specs=in_specs,
        out_specs=out_specs,
        out_shape=out_shape,
        compiler_params=_params("arbitrary"),
        name="inproj_rope" if rope_tabs is not None else "inproj",
    )(*args)


def _rope_tables(n):
    t = np.arange(n)
    inv = ROPE_BASE ** (-np.arange(N_FREQ, dtype=np.float32) / N_FREQ)
    rows = (t // GRID_W).astype(np.float32)
    cols = (t % GRID_W).astype(np.float32)
    d = np.arange(LANES) % HEAD_DIM
    pos = np.where(d[None, :] < HEAD_DIM // 2, rows[:, None], cols[:, None])
    ang = jnp.asarray(pos.astype(np.float32) * inv[d % N_FREQ][None, :])
    sign = np.where((d % (2 * N_FREQ)) < N_FREQ, -1.0, 1.0).astype(np.float32)
    return jnp.cos(ang), jnp.sin(ang) * sign[None, :]


def _attention_core(sink_ref, q_ref, o_ref, subs, shared, bq):
    n_col = Q_PER_KV * bq
    col_head = lax.broadcasted_iota(jnp.int32, (1, n_col), 1) // bq
    sink_rows = []
    for g in range(N_KV_HEADS):
        sink_row = jnp.full((1, n_col), sink_ref[g * Q_PER_KV], f32)
        for j in range(1, Q_PER_KV):
            sink_row = jnp.where(col_head == j, sink_ref[g * Q_PER_KV + j], sink_row)
        sink_rows.append(sink_row * LOG2_E)
    ones_rows = jnp.ones((BF16_SUBLANES, 1), bf16)

    def with_ones(vals_t, g):
        return jnp.concatenate(
            [vals_t[g * HEAD_DIM:(g + 1) * HEAD_DIM, :],
             jnp.broadcast_to(ones_rows, (BF16_SUBLANES, vals_t.shape[1]))], axis=0)

    shared_vals = None if shared is None else [with_ones(shared[1], g) for g in range(N_KV_HEADS)]
    qts, masks = {}, {}

    def qt(i):
        if i not in qts:
            qts[i] = (q_ref[i * bq:(i + 1) * bq, :].astype(f32) * (SCALE * LOG2_E)).T.astype(bf16)
        return qts[i]

    def mask(i):
        if i not in masks:
            masks[i] = subs[i]["mask"]() if subs[i]["mask"] is not None else None
        return masks[i]

    pass_cols = ATTN_PASS_HEADS * bq

    def scores(i, g, cb):
        lo, hi = g * HEAD_DIM, (g + 1) * HEAD_DIM
        first = g * Q_PER_KV + cb * ATTN_PASS_HEADS
        qt_g = jnp.concatenate([qt(i)[h * HEAD_DIM:(h + 1) * HEAD_DIM, :]
                                for h in range(first, first + ATTN_PASS_HEADS)], axis=1)
        s_own = _dot(subs[i]["keys"][:, lo:hi], qt_g)
        if mask(i) is not None:
            s_own = jnp.where(mask(i)[:, cb * pass_cols:(cb + 1) * pass_cols], s_own, NEG)
        s_shared = None if shared is None else _dot(shared[0][:, lo:hi], qt_g)
        return s_own, s_shared

    def weighted_values(i, g, cb, s_own, s_shared):
        sink_row = sink_rows[g][:, cb * pass_cols:(cb + 1) * pass_cols]
        m = jnp.maximum(jnp.max(s_own, axis=0, keepdims=True), sink_row)
        if s_shared is not None:
            m = jnp.maximum(m, jnp.max(s_shared, axis=0, keepdims=True))
        o_aug = _dot(with_ones(subs[i]["vals_t"], g), jnp.exp2(s_own - m).astype(bf16))
        if s_shared is not None:
            o_aug = o_aug + _dot(shared_vals[g], jnp.exp2(s_shared - m).astype(bf16))
        den = o_aug[HEAD_DIM:HEAD_DIM + 1, :] + jnp.exp2(sink_row - m)
        return o_aug[:HEAD_DIM, :] / den

    passes = [(i, g, cb) for i in range(len(subs)) for g in range(N_KV_HEADS)
              for cb in range(Q_PER_KV // ATTN_PASS_HEADS)]
    parts = {}
    pending = scores(*passes[0])
    for n, key in enumerate(passes):
        nxt = scores(*passes[n + 1]) if n + 1 < len(passes) else None
        parts[key] = weighted_values(*key, *pending)
        pending = nxt
    out_t = {(i, g): jnp.concatenate([parts[i, g, cb]
                                      for cb in range(Q_PER_KV // ATTN_PASS_HEADS)], axis=1)
             for i in range(len(subs)) for g in range(N_KV_HEADS)}
    for i in range(len(subs)):
        blocks = [jnp.concatenate([out_t[i, g][:, j * bq:(j + 1) * bq]
                                   for g in range(N_KV_HEADS)], axis=0).T
                  for j in range(Q_PER_KV)]
        o_ref[i * bq:(i + 1) * bq, :] = jnp.concatenate(blocks, axis=1).astype(o_ref.dtype)


ATTN_HEAD_PERM = np.array([(Q_PER_KV * g + j) * HEAD_DIM + d for j in range(Q_PER_KV)
                           for g in range(N_KV_HEADS) for d in range(HEAD_DIM)])


def _permute_heads(a):
    blocks = a.reshape(N_KV_HEADS, Q_PER_KV, HEAD_DIM, *a.shape[1:])
    return blocks.swapaxes(0, 1).reshape(a.shape)


def _ctx_attn_kernel(sink_ref, q_ref, k_ref, vt_ref, o_ref, *, bq, seq):
    subs = []
    for s in range(q_ref.shape[0] // seq):
        keys = k_ref[s * seq:(s + 1) * seq, :]
        subs += [dict(keys=keys, vals_t=vt_ref[s].astype(bf16), mask=None)] * (seq // bq)
    _attention_core(sink_ref, q_ref, o_ref, subs, None, bq)


def _context_attention(q, k, vt, sink, *, seq, bq=128, seqs_per_step=4):
    t = q.shape[0]
    rows = seq * seqs_per_step
    return pl.pallas_call(
        functools.partial(_ctx_attn_kernel, bq=bq, seq=seq),
        grid=(t // rows,),
        in_specs=[pl.BlockSpec(memory_space=pltpu.SMEM),
                  pl.BlockSpec((rows, ATTN_WIDTH), lambda b: (b, 0)),
                  pl.BlockSpec((rows, KV_WIDTH), lambda b: (b, 0)),
                  pl.BlockSpec((seqs_per_step, KV_WIDTH, seq), lambda b: (b, 0, 0))],
        out_specs=pl.BlockSpec((rows, ATTN_WIDTH), lambda b: (b, 0)),
        out_shape=jax.ShapeDtypeStruct((t, ATTN_WIDTH), bf16),
        compiler_params=_params("arbitrary"),
        name="ctx_attn",
    )(sink, q, k, vt)


def _lat_attn_kernel(sink_ref, q_ref, k_ref, vt_ref, ck_ref, cv_ref, *rest, bq, seq, n_cast):
    o_ref = rest[n_cast]
    for src_ref, dst_ref in zip(rest[:n_cast], rest[n_cast + 1:]):
        dst_ref[...] = src_ref[...].astype(dst_ref.dtype)
    band = bq + 2 * WINDOW
    n_col = Q_PER_KV * bq
    delta = ((lax.broadcasted_iota(jnp.int32, (band, n_col), 1) & (bq - 1))
             - lax.broadcasted_iota(jnp.int32, (band, n_col), 0))
    n_sub = q_ref.shape[0] // bq
    subs = []
    for sb in range(n_sub):
        i = pl.program_id(1) * n_sub + sb
        start = pl.multiple_of(jnp.clip(i * bq - WINDOW, 0, seq - band), LANES)
        offset = i * bq - start + WINDOW
        subs.append(dict(keys=k_ref[pl.ds(start, band), :],
                         vals_t=vt_ref[:, pl.ds(start, band)],
                         mask=functools.partial(
                             lambda off: (delta + off).astype(jnp.uint32) <= 2 * WINDOW, offset)))
    shared = (ck_ref[...].astype(bf16), cv_ref[...].T.astype(bf16))
    _attention_core(sink_ref, q_ref, o_ref, subs, shared, bq)


LAT_ATTN_SUB_BLOCKS = 8


def _latent_attention(q, k, vt, ck, cv, sink, *, seq, cast=(), bq=128,
                      sub_blocks=LAT_ATTN_SUB_BLOCKS):
    nb, past = ck.shape[0], ck.shape[1]
    assert bq == LANES and N_KV_HEADS * HEAD_DIM == LANES
    q3, k3 = (a.reshape(nb, seq, a.shape[-1]) for a in (q, k))
    tq = bq * sub_blocks
    steps_per_seq = seq // tq
    n_steps = nb * steps_per_seq
    cast3 = [a.reshape((n_steps, -1) + a.shape[2:]) for a in cast]
    cast_specs = [pl.BlockSpec((None,) + a.shape[1:], lambda b, i: (b * steps_per_seq + i, 0, 0))
                  for a in cast3]
    out, *cast_out = pl.pallas_call(
        functools.partial(_lat_attn_kernel, bq=bq, seq=seq, n_cast=len(cast)),
        grid=(nb, steps_per_seq),
        in_specs=[pl.BlockSpec(memory_space=pltpu.SMEM),
                  pl.BlockSpec((None, tq, ATTN_WIDTH), lambda b, i: (b, i, 0)),
                  pl.BlockSpec((None, seq, KV_WIDTH), lambda b, i: (b, 0, 0)),
                  pl.BlockSpec((None, KV_WIDTH, seq), lambda b, i: (b, 0, 0)),
                  pl.BlockSpec((None, past, KV_WIDTH), lambda b, i: (b, 0, 0)),
                  pl.BlockSpec((None, past, KV_WIDTH), lambda b, i: (b, 0, 0))] + cast_specs,
        out_specs=[pl.BlockSpec((None, tq, ATTN_WIDTH), lambda b, i: (b, i, 0))] + cast_specs,
        out_shape=[jax.ShapeDtypeStruct((nb, seq, ATTN_WIDTH), bf16)]
        + [jax.ShapeDtypeStruct(a.shape, bf16) for a in cast3],
        compiler_params=_params("arbitrary", "arbitrary"),
        name="lat_attn",
    )(sink, q3, k3, vt, ck, cv, *cast3)
    return [out.reshape(nb * seq, ATTN_WIDTH)] + [o.reshape(a.shape) for o, a in zip(cast_out, cast)]


def _route(logits_t):
    r = logits_t.shape[1]
    big = jnp.float32(LANES)
    crow = lax.broadcasted_iota(jnp.int32, (F32_SUBLANES, r), 0)
    crow_f = crow.astype(f32)
    lc = jnp.where(crow < N_GROUPS, logits_t[COARSE_LANE0:COARSE_LANE0 + F32_SUBLANES, :], -jnp.inf)
    mc = jnp.max(lc, axis=0, keepdims=True)
    grp = jnp.min(jnp.where(lc == mc, crow_f, big), axis=0, keepdims=True)
    pg = 1.0 / jnp.sum(jnp.exp(lc - mc), axis=0, keepdims=True)
    erow = lax.broadcasted_iota(jnp.int32, (N_EXPERTS, r), 0).astype(f32)
    in_g = jnp.floor(erow * (1.0 / EXPERTS_PER_GROUP)) == grp
    fl = jnp.where(in_g, logits_t[0:N_EXPERTS, :], -jnp.inf)
    m1 = jnp.max(fl, axis=0, keepdims=True)
    i1 = jnp.min(jnp.where(fl == m1, erow, big), axis=0, keepdims=True)
    fl2 = jnp.where(erow == i1, -jnp.inf, fl)
    m2 = jnp.max(fl2, axis=0, keepdims=True)
    i2 = jnp.min(jnp.where(fl2 == m2, erow, big), axis=0, keepdims=True)
    e2 = jnp.exp(m2 - m1)
    p1 = pg / (1.0 + e2)
    p2 = pg * e2 / (1.0 + e2)
    packed = jnp.where(crow == 0, i1, jnp.where(crow == 1, i2, jnp.where(crow == 2, p1, p2)))
    return jnp.where(crow < 4, packed, 0.0)


def _post_kernel(x_ref, attn_ref, u_ref, up_ref, un_ref, bg_ref, mod_ref, ag_ref, cg_ref, cw_ref,
                 cb_ref, wo_ref, n2_ref, wr_ref, br_ref, x1_ref, h2_ref, route_ref, *, tm, seq):
    i = pl.program_id(0)
    u = u_ref[...].astype(f32)
    rows = lax.broadcasted_iota(jnp.int32, (tm, 1), 0)
    spos = (i * tm + rows) % seq
    u_dn = jnp.where(rows == 0, up_ref[...].astype(f32)[BF16_SUBLANES - 1:, :],
                     pltpu.roll(u, 1, axis=0))
    u_dn = jnp.where(spos == 0, 0.0, u_dn)
    u_up = jnp.where(rows == tm - 1, un_ref[...].astype(f32)[0:1, :], pltpu.roll(u, tm - 1, axis=0))
    u_up = jnp.where(spos == seq - 1, 0.0, u_up)
    y = u_dn * cw_ref[0:1, :] + u * cw_ref[1:2, :] + u_up * cw_ref[2:3, :] + cb_ref[...]
    conv = bg_ref[...].astype(f32) * y
    attn_n = (_rms(attn_ref[...].astype(f32)) * ag_ref[...]).astype(bf16)
    conv_n = (_rms(conv) * cg_ref[...]).astype(bf16)
    subs = [slice(s * POST_SUB, (s + 1) * POST_SUB) for s in range(tm // POST_SUB)]
    mixed = [_dot(attn_n[rs], wo_ref[0:ATTN_WIDTH, :]) + _dot(conv_n[rs], wo_ref[ATTN_WIDTH:, :])
             for rs in subs]
    logits = []
    for s, rs in enumerate(subs):
        x1 = x_ref[rs, :] + mod_ref[:, G1:G1 + D_MODEL] * mixed[s]
        x1_ref[rs, :] = x1
        h2 = _rms(x1) * n2_ref[...]
        h2 = h2 * (1.0 + mod_ref[:, SC2:SC2 + D_MODEL]) + mod_ref[:, SH2:SH2 + D_MODEL]
        for c in range(D_MODEL // LANES):
            h2_ref[pl.ds(rs.start * F32_SUBLANES + c, POST_SUB, stride=F32_SUBLANES), :] = (
                h2[:, c * LANES:(c + 1) * LANES])
        h2_hi = h2.astype(bf16)
        h2_lo = (h2 - h2_hi.astype(f32)).astype(bf16)
        both = _dot(h2_hi, wr_ref[...])
        logits.append(both[:, :LANES] + both[:, LANES:] + _dot(h2_lo, wr_ref[:, :LANES]))
    for s, rs in enumerate(subs):
        route_ref[:, rs] = _route((logits[s] + br_ref[...]).T)


def _post(x, attn, u, bg, mod3, p, *, seq, mod_row0, tm=4 * POST_SUB):
    t = x.shape[0]
    tiles_per_seq = max(seq // tm, 1)
    halo = BF16_SUBLANES
    n_halo = t // halo
    row = lambda i: (mod_row0 + (i // tiles_per_seq if mod_row0 else 0), 0, 0)
    tile = lambda w: pl.BlockSpec((tm, w), lambda i: (i, 0))
    full = lambda a: pl.BlockSpec(a.shape, lambda i: (0,) * a.ndim)
    small = [p["attn_out_g"], p["conv_out_g"], p["conv_w"], p["conv_b"], p["w_o"], p["norm2_g"],
             p["w_router"], p["b_router"]]
    return pl.pallas_call(
        functools.partial(_post_kernel, tm=tm, seq=seq),
        grid=(t // tm,),
        in_specs=[tile(D_MODEL), tile(ATTN_WIDTH), tile(CONV_WIDTH),
                  pl.BlockSpec((halo, CONV_WIDTH),
                               lambda i: (jnp.maximum(i * (tm // halo) - 1, 0), 0)),
                  pl.BlockSpec((halo, CONV_WIDTH),
                               lambda i: (jnp.minimum((i + 1) * (tm // halo), n_halo - 1), 0)),
                  tile(CONV_WIDTH),
                  pl.BlockSpec((None, 1, 6 * D_MODEL), row)] + [full(a) for a in small],
        out_specs=[tile(D_MODEL), pl.BlockSpec((tm * F32_SUBLANES, LANES), lambda i: (i, 0)),
                   pl.BlockSpec((F32_SUBLANES, tm), lambda i: (0, i))],
        out_shape=[jax.ShapeDtypeStruct((t, D_MODEL), f32),
                   jax.ShapeDtypeStruct((t * F32_SUBLANES, LANES), f32),
                   jax.ShapeDtypeStruct((F32_SUBLANES, t), f32)],
        compiler_params=_params("arbitrary"),
        name="post",
    )(x, attn, u, u, u, bg, mod3, *small)


def _max_chunks(blk):
    return -(-TOP_K * blk // MOE_CHUNK) + EXPERTS_PER_STEP


def _sorted_rows(blk):
    rows = TOP_K * blk + N_EXPERTS * (F32_SUBLANES - 1) + MOE_CHUNK
    assert rows % F32_SUBLANES == 0
    return rows + (F32_SUBLANES if rows // F32_SUBLANES % 2 == 0 else 0)


def _dispatch_tables(route, blk):
    nblk = route.shape[1] // blk
    n_slots = TOP_K * blk
    by_block = lambda a: a.reshape(TOP_K, nblk, blk).transpose(1, 0, 2)
    experts = by_block(route[0:TOP_K].astype(jnp.int32))
    weights = by_block(route[TOP_K:2 * TOP_K])
    ids = jnp.arange(N_EXPERTS, dtype=jnp.int32)
    onehot = experts.reshape(nblk, n_slots // LANES, LANES, 1) == ids
    earlier = jnp.tril(jnp.ones((LANES, LANES), bf16), -1)
    within = jnp.einsum("ij,bgje->bgie", earlier, onehot.astype(bf16),
                        preferred_element_type=f32)
    per_group = jnp.sum(onehot, axis=2, dtype=jnp.int32)
    before_group = jnp.cumsum(per_group, axis=1) - per_group
    counts = jnp.sum(per_group, axis=1)
    padded = (counts + F32_SUBLANES - 1) // F32_SUBLANES * F32_SUBLANES
    lo = jnp.cumsum(padded, axis=1) - padded
    row = within.astype(jnp.int32) + before_group[:, :, None, :] + lo[:, None, None, :]
    row = jnp.sum(jnp.where(onehot, row, 0), axis=-1).reshape(nblk, TOP_K, blk)

    eg, n_steps, max_c = EXPERTS_PER_STEP, N_EXPERTS // EXPERTS_PER_STEP, _max_chunks(blk)
    lo = lo.reshape(nblk, n_steps, eg)
    counts = counts.reshape(nblk, n_steps, eg)
    n_ch = (counts + MOE_CHUNK - 1) // MOE_CHUNK
    cum = jnp.cumsum(n_ch, axis=2)
    total = cum[..., -1]
    pos = jnp.arange(max_c, dtype=jnp.int32)
    j = jnp.sum(cum[:, :, None, :] <= pos[None, None, :, None], axis=-1, dtype=jnp.int32)
    j = jnp.minimum(j, eg - 1)
    hit = j[..., None] == jnp.arange(eg, dtype=jnp.int32)
    pick = lambda a: jnp.sum(jnp.where(hit, a[:, :, None, :], 0), axis=-1)
    done = (pos - pick(cum - n_ch)) * MOE_CHUNK
    live = pos < total[..., None]
    flat = lambda a: jnp.where(live, a, 0).reshape(nblk, 1, n_steps * max_c)
    return (row[:, 0:1], row[:, 1:2], weights[:, 0:1], weights[:, 1:2], flat(j),
            flat(pick(lo) + done), flat(jnp.clip(pick(counts) - done, 0, MOE_CHUNK)),
            total[:, None, :])


def _moe_kernel(row0_ref, row1_ref, w0_ref, w1_ref, cj_ref, cfirst_ref, cvalid_ref, nchunks_ref,
                mod_ref, fg_ref, h_hbm, wg_hbm, wu_hbm, wd_hbm, x1_hbm, out_hbm,
                sorted_ref, yt_ref, h_ref, wg_ref, wu_ref, wd_ref, xbuf_ref, obuf_ref, sem_h, sem_w,
                sem_in, sem_out, *, blk):
    n_col = D_MODEL // LANES
    pitch = _sorted_rows(blk)
    blk_id, step = pl.program_id(0), pl.program_id(1)
    n_steps = pl.num_programs(1)
    last_step = step == n_steps - 1
    max_c = _max_chunks(blk)
    n_sub = blk // FINAL_ROWS
    assignment_rows = (row0_ref, w0_ref), (row1_ref, w1_ref)
    run = blk_id * n_steps + step
    ahead = WEIGHT_BUFFERS - 1

    def weight_copies(r):
        slot = r % WEIGHT_BUFFERS
        experts = pl.ds(r % n_steps * EXPERTS_PER_STEP, EXPERTS_PER_STEP)
        pairs = (wg_hbm, wg_ref), (wu_hbm, wu_ref), (wd_hbm, wd_ref)
        return [pltpu.make_async_copy(hbm.at[experts], buf.at[slot], sem_w.at[k, slot])
                for k, (hbm, buf) in enumerate(pairs)]

    def h_copy(b):
        tiles = pl.ds(b * (blk * F32_SUBLANES), blk * F32_SUBLANES)
        return pltpu.make_async_copy(h_hbm.at[tiles, :], h_ref, sem_h.at[0])

    @pl.when(run == 0)
    def _():
        h_copy(0).start()
        for r in range(ahead):
            for copy in weight_copies(r):
                copy.start()

    @pl.when(run + ahead < pl.num_programs(0) * n_steps)
    def _():
        for copy in weight_copies(run + ahead):
            copy.start()

    def column_tile(row):
        return pl.ds(row, F32_SUBLANES, stride=pitch)

    def x1_copy(s):
        rows = pl.ds(blk_id * blk + s * FINAL_ROWS, FINAL_ROWS)
        slot = s % FINAL_SLOTS
        return pltpu.make_async_copy(x1_hbm.at[rows, :], xbuf_ref.at[slot], sem_in.at[slot])

    def out_copy(s):
        rows = pl.ds(blk_id * blk + s * FINAL_ROWS, FINAL_ROWS)
        slot = s % FINAL_SLOTS
        return pltpu.make_async_copy(obuf_ref.at[slot], out_hbm.at[rows, :], sem_out.at[slot])

    @pl.when(last_step)
    def _():
        for s in range(min(FINAL_SLOTS, n_sub)):
            x1_copy(s).start()

    @pl.when((blk_id == 0) & (step == 0))
    def _():
        zero_rows = F32_SUBLANES * LANES
        zeros = jnp.zeros((zero_rows, LANES), f32)

        def clear(i, carry):
            sorted_ref[pl.ds(pl.multiple_of(i * zero_rows, zero_rows), zero_rows), :] = zeros
            return carry

        lax.fori_loop(0, n_col * pitch // zero_rows, clear, 0)
        sorted_ref[n_col * pitch - zero_rows:, :] = zeros

    @pl.when(step == 0)
    def _():
        h_copy(blk_id).wait()

        def dispatch(i, carry):
            first = pl.multiple_of(i * DISPATCH_ROWS * F32_SUBLANES, DISPATCH_ROWS * F32_SUBLANES)
            for r in range(DISPATCH_ROWS):
                tile = h_ref[pl.ds(first + r * F32_SUBLANES, F32_SUBLANES), :]
                for rows_ref, _ in assignment_rows:
                    sorted_ref[column_tile(rows_ref[0, i * DISPATCH_ROWS + r]), :] = tile
            return carry

        lax.fori_loop(0, blk // DISPATCH_ROWS, dispatch, 0)

        @pl.when(blk_id + 1 < pl.num_programs(0))
        def _():
            h_copy(blk_id + 1).start()

    def chunk_rows(p, c):
        first = pl.multiple_of(cfirst_ref[0, p], F32_SUBLANES)
        return pl.ds(pl.multiple_of(c * pitch + first, F32_SUBLANES), MOE_CHUNK)

    def hidden(p):
        x = jnp.concatenate([sorted_ref[chunk_rows(p, c), :] for c in range(n_col)],
                            axis=1).astype(bf16)
        j = cj_ref[0, p]
        g = _dot(x, wg_ref[run % WEIGHT_BUFFERS, j])
        up = _dot(x, wu_ref[run % WEIGHT_BUFFERS, j])
        return ((g * jax.nn.sigmoid(g)) * up).astype(bf16)

    def project(p, act):
        out = _dot(act, wd_ref[run % WEIGHT_BUFFERS, cj_ref[0, p]])
        own = lax.broadcasted_iota(jnp.int32, (MOE_CHUNK, LANES), 0) < cvalid_ref[0, p]
        for c in range(n_col):
            rows = chunk_rows(p, c)
            sorted_ref[rows, :] = jnp.where(own, out[:, c * LANES:(c + 1) * LANES],
                                            sorted_ref[rows, :])

    def run_chunks(ps):
        acts = [hidden(p) for p in ps]
        for p, act in zip(ps, acts):
            project(p, act)

    def group(gi, carry):
        run_chunks([step * max_c + CHUNKS_PER_TRIP * gi + k for k in range(CHUNKS_PER_TRIP)])
        return carry

    for copy in weight_copies(run):
        copy.wait()
    n_chunks = nchunks_ref[0, step]
    lax.fori_loop(0, n_chunks // CHUNKS_PER_TRIP, group, 0)
    for left in range(1, CHUNKS_PER_TRIP):
        @pl.when(n_chunks % CHUNKS_PER_TRIP == left)
        def _():
            run_chunks([step * max_c + n_chunks - left + k for k in range(left)])

    @pl.when(last_step)
    def _():
        gate2 = mod_ref[:, G2:G2 + D_MODEL]
        for s in range(n_sub):
            slot = s % FINAL_SLOTS
            for r in range(FINAL_ROWS):
                t = s * FINAL_ROWS + r
                weighted = [sorted_ref[column_tile(rows_ref[0, t]), :] * w_ref[0, t]
                            for rows_ref, w_ref in assignment_rows]
                yt_ref[r * F32_SUBLANES:(r + 1) * F32_SUBLANES, :] = functools.reduce(
                    jnp.add, weighted)
            x1_copy(s).wait()
            if s >= FINAL_SLOTS:
                out_copy(s - FINAL_SLOTS).wait()
            y = jnp.concatenate([yt_ref[pl.ds(c, FINAL_ROWS, stride=F32_SUBLANES), :]
                                 for c in range(n_col)], axis=1)
            x2 = xbuf_ref[slot] + gate2 * y
            obuf_ref[slot] = _rms(x2) * fg_ref[...]
            out_copy(s).start()
            if s + FINAL_SLOTS < n_sub:
                x1_copy(s + FINAL_SLOTS).start()
        for s in range(max(n_sub - FINAL_SLOTS, 0), n_sub):
            out_copy(s).wait()


def _moe(h2_tiles, tables, x1, mod3, final_g, w_gate, w_up, w_down, *, seq, mod_row0,
         blk=MOE_BLOCK):
    t = h2_tiles.shape[0] // F32_SUBLANES
    eg, n_steps = EXPERTS_PER_STEP, N_EXPERTS // EXPERTS_PER_STEP
    blocks_per_seq = max(seq // blk, 1)
    row = lambda b, g: (mod_row0 + (b // blocks_per_seq if mod_row0 else 0), 0, 0)
    smem = lambda n: pl.BlockSpec((None, 1, n), lambda b, g: (b, 0, 0), memory_space=pltpu.SMEM)
    final_buf = pltpu.VMEM((FINAL_SLOTS, FINAL_ROWS, D_MODEL), f32)
    assert t // blk * n_steps >= WEIGHT_BUFFERS - 1
    weight_buf = lambda w: pltpu.VMEM((WEIGHT_BUFFERS, eg) + w.shape[1:], w.dtype)
    return pl.pallas_call(
        functools.partial(_moe_kernel, blk=blk),
        grid=(t // blk, n_steps),
        in_specs=[smem(blk)] * 4 + [smem(n_steps * _max_chunks(blk))] * 3 + [
                  smem(n_steps),
                  pl.BlockSpec((None, 1, 6 * D_MODEL), row),
                  pl.BlockSpec((1, D_MODEL), lambda b, g: (0, 0))]
        + [pl.BlockSpec(memory_space=pl.ANY)] * 5,
        out_specs=pl.BlockSpec(memory_space=pl.ANY),
        out_shape=jax.ShapeDtypeStruct((t, D_MODEL), f32),
        scratch_shapes=[pltpu.VMEM((D_MODEL // LANES * _sorted_rows(blk), LANES), f32),
                        pltpu.VMEM((FINAL_ROWS * F32_SUBLANES, LANES), f32),
                        pltpu.VMEM((blk * F32_SUBLANES, LANES), f32),
                        weight_buf(w_gate), weight_buf(w_up), weight_buf(w_down),
                        final_buf, final_buf,
                        pltpu.SemaphoreType.DMA((1,)),
                        pltpu.SemaphoreType.DMA((3, WEIGHT_BUFFERS)),
                        pltpu.SemaphoreType.DMA((FINAL_SLOTS,)),
                        pltpu.SemaphoreType.DMA((FINAL_SLOTS,))],
        compiler_params=_params("arbitrary", "arbitrary", vmem=MOE_VMEM_LIMIT_BYTES),
        name="moe",
    )(*tables, mod3, final_g, h2_tiles, w_gate, w_up, w_down, x1)


def _mixers(x, mod3, p, *, seq, mod_row0, ctx_kv, cast=()):
    if ctx_kv is not None:
        q, k, vt, u, bg = _inproj(x, mod3, p["norm1_g"], p["w_in"], seq=seq, mod_row0=mod_row0,
                                  rope_tabs=_rope_tables(seq))
        attn, *cast = _latent_attention(q, k, vt, ctx_kv[0], ctx_kv[1], p["sink"], seq=seq,
                                        cast=cast)
        kv_t = None
    else:
        q, k, kt, vt, u, bg = _inproj(x, mod3, p["norm1_g"], p["w_in"], seq=seq,
                                      mod_row0=mod_row0)
        attn = _context_attention(q, k, vt, p["sink"], seq=seq)
        kv_t = (kt, vt)
    x1, h2_tiles, route = _post(x, attn, u, bg, mod3, p, seq=seq, mod_row0=mod_row0)
    return x1, h2_tiles, route, kv_t, cast


def _experts(x1, h2_tiles, tables, mod3, p, *, seq, mod_row0):
    out = _moe(h2_tiles, tables, x1, mod3, p["final_g"], p["w_gate"], p["w_up"], p["w_down"],
               seq=seq, mod_row0=mod_row0)
    return out.reshape(-1, seq, D_MODEL)


def kernel(x_prompt, x_sample, cache_k, cache_v, c, c_ctx, w_mod, b_mod, norm1_g, w_in, conv_w,
           conv_b, sink, attn_out_g, conv_out_g, w_o, norm2_g, w_coarse, b_coarse, w_fine, b_fine,
           w_gate, w_up, w_down, final_g):
    batch, seq, _ = x_prompt.shape
    dec_batch, dec_seq, _ = x_sample.shape
    past = cache_k.shape[2]
    assert w_mod.shape[0] == 1 and 1 + dec_batch <= MOD_ROWS

    cvecs = jnp.concatenate([c_ctx[None], c, jnp.zeros((MOD_ROWS - 1 - dec_batch, D_MODEL), f32)])
    mod3 = _modulation(cvecs, w_mod[0], b_mod[0]).reshape(MOD_ROWS, 1, 6 * D_MODEL)

    pad = jnp.zeros((D_MODEL, LANES - N_EXPERTS - N_GROUPS), f32)
    w_mix = jnp.concatenate([_permute_heads(w_o[0][:ATTN_WIDTH]), w_o[0][ATTN_WIDTH:]])
    p = {
        "norm1_g": norm1_g, "w_in": w_in[0].astype(bf16), "conv_w": conv_w[0], "conv_b": conv_b,
        "sink": sink[0], "attn_out_g": _permute_heads(attn_out_g[0])[None],
        "conv_out_g": conv_out_g, "w_o": w_mix.astype(bf16), "norm2_g": norm2_g,
        "w_router": _split_bf16(jnp.concatenate([w_fine[0], w_coarse[0], pad], axis=1)),
        "b_router": jnp.concatenate([b_fine[0], b_coarse[0], pad[0]])[None],
        "final_g": final_g[None],
    }

    x1_p, h2_p, route_p, kv_t, _ = _mixers(x_prompt.reshape(batch * seq, D_MODEL), mod3, p,
                                           seq=seq, mod_row0=0, ctx_kv=None)
    new_k, new_v = (a.reshape(batch, 1, N_KV_HEADS, HEAD_DIM, seq).transpose(0, 1, 4, 2, 3)
                    for a in kv_t)

    ctx_kv = (cache_k[:, 0].reshape(dec_batch, past, KV_WIDTH),
              cache_v[:, 0].reshape(dec_batch, past, KV_WIDTH))
    experts_f32 = (w_gate[0], w_up[0], w_down[0])
    ride = N_EXPERTS % (dec_batch * dec_seq // (LANES * LAT_ATTN_SUB_BLOCKS)) == 0
    x1_s, h2_s, route_s, _, experts_bf16 = _mixers(
        x_sample.reshape(dec_batch * dec_seq, D_MODEL), mod3, p, seq=dec_seq, mod_row0=1,
        ctx_kv=ctx_kv, cast=experts_f32 if ride else ())
    if not ride:
        experts_bf16 = [w.astype(bf16) for w in experts_f32]
    p = dict(p, w_gate=experts_bf16[0], w_up=experts_bf16[1], w_down=experts_bf16[2])

    n_blk_p = batch * seq // MOE_BLOCK
    tables = _dispatch_tables(jnp.concatenate([route_p, route_s], axis=1), MOE_BLOCK)
    y_prompt = _experts(x1_p, h2_p, [t[:n_blk_p] for t in tables], mod3, p, seq=seq, mod_row0=0)
    y_sample = _experts(x1_s, h2_s, [t[n_blk_p:] for t in tables], mod3, p, seq=dec_seq,
                        mod_row0=1)
    return y_prompt, y_sample, new_k, new_v
```
